```python
import jax, jax.numpy as jnp
from jax import lax
import numpy as np

D_MODEL = 2048
BATCH = 8
SEQ = 4096
DEPTH = 2

N_MIXERS = 2
N_HEADS = 16
HEAD_DIM = D_MODEL // N_HEADS
Q_BLOCK = 128
CONV_WIDTH = 3
D_FF = 4 * D_MODEL
N_MOD = 6
RMS_EPS = 1e-6
FORGET_BIAS_CENTER = 3.0

kernel_name = 'fox_shortconv_hybrid_adaln'


def rmsnorm(x, gain):
    xf = x.astype(jnp.float32)
    inv = lax.rsqrt(jnp.mean(xf * xf, axis=-1, keepdims=True) + RMS_EPS)
    return (xf * inv).astype(x.dtype) * gain


def modulate(h, shift, scale):
    return h * (1.0 + scale[:, None, :]) + shift[:, None, :]


def fox_attention(q, k, v, log_f):
    B, H, S, Dh = q.shape
    nb = S // Q_BLOCK
    F = jnp.cumsum(log_f.astype(jnp.float32), axis=-1)
    q_blocks = q.reshape(B, H, nb, Q_BLOCK, Dh).transpose(2, 0, 1, 3, 4)
    F_blocks = F.reshape(B, H, nb, Q_BLOCK).transpose(2, 0, 1, 3)
    k_pos = jnp.arange(S)
    scale = HEAD_DIM ** -0.5

    def one_block(args):
        blk, q_i, F_i = args
        s = jnp.einsum('bhqd,bhkd->bhqk', q_i, k).astype(jnp.float32) * scale
        s = s + F_i[..., :, None] - F[..., None, :]
        q_pos = blk * Q_BLOCK + jnp.arange(Q_BLOCK)
        causal = k_pos[None, :] <= q_pos[:, None]
        s = jnp.where(causal, s, -jnp.inf)
        p = jax.nn.softmax(s, axis=-1)
        return jnp.einsum('bhqk,bhkd->bhqd', p.astype(v.dtype), v)

    out = lax.map(one_block, (jnp.arange(nb), q_blocks, F_blocks))
    return out.transpose(1, 2, 0, 3, 4).reshape(B, H, S, Dh)


def fox_mixer(h, w_in, b_f, w_out):
    B, S, D = h.shape
    proj = h @ w_in
    q, k, v, f_logit = jnp.split(proj, [D, 2 * D, 3 * D], axis=-1)
    to_heads = lambda t: t.reshape(B, S, N_HEADS, HEAD_DIM).transpose(0, 2, 1, 3)
    log_f = jax.nn.log_sigmoid((f_logit + b_f).astype(jnp.float32)).transpose(0, 2, 1)
    o = fox_attention(to_heads(q), to_heads(k), to_heads(v), log_f)
    o = o.transpose(0, 2, 1, 3).reshape(B, S, D)
    return o @ w_out


def short_conv_mixer(h, w_in, conv_w, w_out):
    S = h.shape[1]
    proj = h @ w_in
    b_gate, c_gate, u = jnp.split(proj, 3, axis=-1)
    u = c_gate * u
    u_pad = jnp.pad(u, ((0, 0), (CONV_WIDTH - 1, 0), (0, 0)))
    y = sum(conv_w[tap] * u_pad[:, tap:tap + S, :] for tap in range(CONV_WIDTH))
    return (b_gate * y) @ w_out


def sq_relu_mlp(h, w_up, w_down):
    return jnp.square(jax.nn.relu(h @ w_up)) @ w_down


def _fwd_setup_inputs(seed: int = 0) -> dict:
    key = jax.random.key(seed)
    ks = jax.random.split(key, 16)
    D = D_MODEL
    n_fox = (DEPTH + 1) // 2
    n_conv = DEPTH // 2
    nrm = lambda k, shape, fan_in, mult=1.0: jax.random.normal(k, shape, jnp.float32) * (mult * fan_in ** -0.5)
    return {
        'x': jax.random.normal(ks[0], (BATCH, SEQ, D), jnp.float32),
        'c': jax.random.normal(ks[1], (BATCH, D), jnp.float32),
        'ada_w': nrm(ks[2], (DEPTH, D, N_MOD * D), D, 0.5),
        'ada_b': 0.01 * jax.random.normal(ks[3], (DEPTH, N_MOD * D), jnp.float32),
        'norm_mix': 1.0 + 0.02 * jax.random.normal(ks[4], (DEPTH, D), jnp.float32),
        'norm_mlp': 1.0 + 0.02 * jax.random.normal(ks[5], (DEPTH, D), jnp.float32),
        'fox_w_in': nrm(ks[6], (n_fox, D, 3 * D + N_HEADS), D),
        'fox_b_f': FORGET_BIAS_CENTER + 0.1 * jax.random.normal(ks[7], (n_fox, N_HEADS), jnp.float32),
        'fox_w_out': nrm(ks[8], (n_fox, D, D), D),
        'conv_w_in': nrm(ks[9], (n_conv, D, 3 * D), D),
        'conv_w': nrm(ks[10], (n_conv, CONV_WIDTH, D), CONV_WIDTH),
        'conv_w_out': nrm(ks[11], (n_conv, D, D), D),
        'mlp_w_up': nrm(ks[12], (DEPTH, D, D_FF), D),
        'mlp_w_down': nrm(ks[13], (DEPTH, D_FF, D), D_FF),
        'final_norm': 1.0 + 0.02 * jax.random.normal(ks[14], (D,), jnp.float32),
    }


def _fwd_reference(x, c, ada_w, ada_b, norm_mix, norm_mlp, fox_w_in, fox_b_f, fox_w_out,
              conv_w_in, conv_w, conv_w_out, mlp_w_up, mlp_w_down, final_norm):
    c_act = jax.nn.silu(c)
    for i in range(DEPTH):
        mod = c_act @ ada_w[i] + ada_b[i]
        sh_mix, sc_mix, g_mix, sh_mlp, sc_mlp, g_mlp = jnp.split(mod, N_MOD, axis=-1)
        h = modulate(rmsnorm(x, norm_mix[i]), sh_mix, sc_mix)
        j = i // N_MIXERS
        if i % N_MIXERS == 0:
            mix = fox_mixer(h, fox_w_in[j], fox_b_f[j], fox_w_out[j])
        else:
            mix = short_conv_mixer(h, conv_w_in[j], conv_w[j], conv_w_out[j])
        x = x + g_mix[:, None, :] * mix
        h = modulate(rmsnorm(x, norm_mlp[i]), sh_mlp, sc_mlp)
        x = x + g_mlp[:, None, :] * sq_relu_mlp(h, mlp_w_up[i], mlp_w_down[i])
    return rmsnorm(x, final_norm)


import jax as _jax
import jax.numpy as _jnp

TWIN_FORMAT = 'train_step'
FWD_PARAMS = ['x', 'c', 'ada_w', 'ada_b', 'norm_mix', 'norm_mlp', 'fox_w_in', 'fox_b_f', 'fox_w_out', 'conv_w_in', 'conv_w', 'conv_w_out', 'mlp_w_up', 'mlp_w_down', 'final_norm']
TWIN_WEIGHTS = ['ada_w', 'ada_b', 'norm_mix', 'norm_mlp', 'fox_w_in', 'fox_b_f', 'fox_w_out', 'conv_w_in', 'conv_w', 'conv_w_out', 'mlp_w_up', 'mlp_w_down', 'final_norm']
TWIN_DIFF_INPUT = 'x'
TWIN_INPUTS = ['x', 'c', 'ada_w', 'ada_b', 'norm_mix', 'norm_mlp', 'fox_w_in', 'fox_b_f', 'fox_w_out', 'conv_w_in', 'conv_w', 'conv_w_out', 'mlp_w_up', 'mlp_w_down', 'final_norm', 'loss_target', 'm_ada_w', 'm_ada_b', 'm_norm_mix', 'm_norm_mlp', 'm_fox_w_in', 'm_fox_b_f', 'm_fox_w_out', 'm_conv_w_in', 'm_conv_w', 'm_conv_w_out', 'm_mlp_w_up', 'm_mlp_w_down', 'm_final_norm', 'v_ada_w', 'v_ada_b', 'v_norm_mix', 'v_norm_mlp', 'v_fox_w_in', 'v_fox_b_f', 'v_fox_w_out', 'v_conv_w_in', 'v_conv_w', 'v_conv_w_out', 'v_mlp_w_up', 'v_mlp_w_down', 'v_final_norm']
TWIN_OUTPUTS = ['loss', 'grad_x', 'grad_ada_w', 'grad_ada_b', 'grad_norm_mix', 'grad_norm_mlp', 'grad_fox_w_in', 'grad_fox_b_f', 'grad_fox_w_out', 'grad_conv_w_in', 'grad_conv_w', 'grad_conv_w_out', 'grad_mlp_w_up', 'grad_mlp_w_down', 'grad_final_norm', 'delta_ada_w', 'delta_ada_b', 'delta_norm_mix', 'delta_norm_mlp', 'delta_fox_w_in', 'delta_fox_b_f', 'delta_fox_w_out', 'delta_conv_w_in', 'delta_conv_w', 'delta_conv_w_out', 'delta_mlp_w_up', 'delta_mlp_w_down', 'delta_final_norm', 'new_m_ada_w', 'new_m_ada_b', 'new_m_norm_mix', 'new_m_norm_mlp', 'new_m_fox_w_in', 'new_m_fox_b_f', 'new_m_fox_w_out', 'new_m_conv_w_in', 'new_m_conv_w', 'new_m_conv_w_out', 'new_m_mlp_w_up', 'new_m_mlp_w_down', 'new_m_final_norm', 'new_v_ada_w', 'new_v_ada_b', 'new_v_norm_mix', 'new_v_norm_mlp', 'new_v_fox_w_in', 'new_v_fox_b_f', 'new_v_fox_w_out', 'new_v_conv_w_in', 'new_v_conv_w', 'new_v_conv_w_out', 'new_v_mlp_w_up', 'new_v_mlp_w_down', 'new_v_final_norm']
TWIN_LEAF_KINDS = {'loss': 'loss', 'grad_x': 'grad_x', 'grad_ada_w': 'grad_w', 'grad_ada_b': 'grad_w', 'grad_norm_mix': 'grad_w', 'grad_norm_mlp': 'grad_w', 'grad_fox_w_in': 'grad_w', 'grad_fox_b_f': 'grad_w', 'grad_fox_w_out': 'grad_w', 'grad_conv_w_in': 'grad_w', 'grad_conv_w': 'grad_w', 'grad_conv_w_out': 'grad_w', 'grad_mlp_w_up': 'grad_w', 'grad_mlp_w_down': 'grad_w', 'grad_final_norm': 'grad_w', 'delta_ada_w': 'delta_w', 'delta_ada_b': 'delta_w', 'delta_norm_mix': 'delta_w', 'delta_norm_mlp': 'delta_w', 'delta_fox_w_in': 'delta_w', 'delta_fox_b_f': 'delta_w', 'delta_fox_w_out': 'delta_w', 'delta_conv_w_in': 'delta_w', 'delta_conv_w': 'delta_w', 'delta_conv_w_out': 'delta_w', 'delta_mlp_w_up': 'delta_w', 'delta_mlp_w_down': 'delta_w', 'delta_final_norm': 'delta_w', 'new_m_ada_w': 'new_m', 'new_m_ada_b': 'new_m', 'new_m_norm_mix': 'new_m', 'new_m_norm_mlp': 'new_m', 'new_m_fox_w_in': 'new_m', 'new_m_fox_b_f': 'new_m', 'new_m_fox_w_out': 'new_m', 'new_m_conv_w_in': 'new_m', 'new_m_conv_w': 'new_m', 'new_m_conv_w_out': 'new_m', 'new_m_mlp_w_up': 'new_m', 'new_m_mlp_w_down': 'new_m', 'new_m_final_norm': 'new_m', 'new_v_ada_w': 'new_v', 'new_v_ada_b': 'new_v', 'new_v_norm_mix': 'new_v', 'new_v_norm_mlp': 'new_v', 'new_v_fox_w_in': 'new_v', 'new_v_fox_b_f': 'new_v', 'new_v_fox_w_out': 'new_v', 'new_v_conv_w_in': 'new_v', 'new_v_conv_w': 'new_v', 'new_v_conv_w_out': 'new_v', 'new_v_mlp_w_up': 'new_v', 'new_v_mlp_w_down': 'new_v', 'new_v_final_norm': 'new_v'}


def _forward(args):
    return _fwd_reference(*[args[k] for k in FWD_PARAMS])


def _output_shape():
    def fwd():
        inp = _fwd_setup_inputs(0)
        return _fwd_reference(*[inp[k] for k in FWD_PARAMS])
    out = _jax.eval_shape(fwd)
    return out.shape, out.dtype

N_MICROBATCH = 1
ADAM_LR = 0.001
ADAM_B1 = 0.9
ADAM_B2 = 0.999
ADAM_EPS = 1e-08
ADAM_WD = 0.01
ADAM_STEP = 10
PER_EXAMPLE_BATCH_AXIS = {'x': 0, 'c': 0, 'loss_target': 0}
SHARED_INPUTS = []
_WEIGHT_DTYPES = {'ada_w': _jnp.float32, 'ada_b': _jnp.float32, 'norm_mix': _jnp.float32, 'norm_mlp': _jnp.float32, 'fox_w_in': _jnp.float32, 'fox_b_f': _jnp.float32, 'fox_w_out': _jnp.float32, 'conv_w_in': _jnp.float32, 'conv_w': _jnp.float32, 'conv_w_out': _jnp.float32, 'mlp_w_up': _jnp.float32, 'mlp_w_down': _jnp.float32, 'final_norm': _jnp.float32}
MOMENT_SCALE = {'ada_w': 4.350286e-02, 'ada_b': 7.849969e-02, 'norm_mix': 3.708520e-02, 'norm_mlp': 4.037901e-02, 'fox_w_in': 1.123044e-02, 'fox_b_f': 6.633701e-02, 'fox_w_out': 1.386466e-02, 'conv_w_in': 2.960626e-02, 'conv_w': 2.964673e-02, 'conv_w_out': 2.961503e-02, 'mlp_w_up': 2.036720e-02, 'mlp_w_down': 3.685130e-02, 'final_norm': 1.609033e+01}


def _to_microbatches(a, axis):
    t = _jnp.moveaxis(a, axis, 0)
    t = t.reshape((N_MICROBATCH, t.shape[0] // N_MICROBATCH) + t.shape[1:])
    return _jnp.moveaxis(t, 1, axis + 1)


def setup_inputs(seed: int = 0) -> dict:
    inp = _fwd_setup_inputs(seed)
    key = _jax.random.fold_in(_jax.random.key(seed), 7919)
    shape, _ = _output_shape()
    out = dict(inp)
    out["loss_target"] = _jax.random.normal(_jax.random.fold_in(key, 0), shape, _jnp.float32)
    for i, name in enumerate(TWIN_WEIGHTS):
        w = inp[name].astype(_jnp.float32)
        if MOMENT_SCALE is None:
            s = _jnp.sqrt(_jnp.mean(_jnp.square(w)) + 1e-30)
        else:
            s = MOMENT_SCALE[name]
        km, kv = _jax.random.split(_jax.random.fold_in(key, i + 1))
        out[name] = w
        out["m_" + name] = s * _jax.random.normal(km, w.shape, _jnp.float32)
        out["v_" + name] = (s * s) * _jax.random.uniform(kv, w.shape, _jnp.float32, 0.5, 1.5)
    if N_MICROBATCH > 1:
        for name, axis in PER_EXAMPLE_BATCH_AXIS.items():
            out[name] = _to_microbatches(out[name], axis)
    return {'x': out['x'], 'c': out['c'], 'ada_w': out['ada_w'], 'ada_b': out['ada_b'], 'norm_mix': out['norm_mix'], 'norm_mlp': out['norm_mlp'], 'fox_w_in': out['fox_w_in'], 'fox_b_f': out['fox_b_f'], 'fox_w_out': out['fox_w_out'], 'conv_w_in': out['conv_w_in'], 'conv_w': out['conv_w'], 'conv_w_out': out['conv_w_out'], 'mlp_w_up': out['mlp_w_up'], 'mlp_w_down': out['mlp_w_down'], 'final_norm': out['final_norm'], 'loss_target': out['loss_target'], 'm_ada_w': out['m_ada_w'], 'm_ada_b': out['m_ada_b'], 'm_norm_mix': out['m_norm_mix'], 'm_norm_mlp': out['m_norm_mlp'], 'm_fox_w_in': out['m_fox_w_in'], 'm_fox_b_f': out['m_fox_b_f'], 'm_fox_w_out': out['m_fox_w_out'], 'm_conv_w_in': out['m_conv_w_in'], 'm_conv_w': out['m_conv_w'], 'm_conv_w_out': out['m_conv_w_out'], 'm_mlp_w_up': out['m_mlp_w_up'], 'm_mlp_w_down': out['m_mlp_w_down'], 'm_final_norm': out['m_final_norm'], 'v_ada_w': out['v_ada_w'], 'v_ada_b': out['v_ada_b'], 'v_norm_mix': out['v_norm_mix'], 'v_norm_mlp': out['v_norm_mlp'], 'v_fox_w_in': out['v_fox_w_in'], 'v_fox_b_f': out['v_fox_b_f'], 'v_fox_w_out': out['v_fox_w_out'], 'v_conv_w_in': out['v_conv_w_in'], 'v_conv_w': out['v_conv_w'], 'v_conv_w_out': out['v_conv_w_out'], 'v_mlp_w_up': out['v_mlp_w_up'], 'v_mlp_w_down': out['v_mlp_w_down'], 'v_final_norm': out['v_final_norm']}


def _loss(weights, diff, rest, loss_target):
    with _jax.named_scope("forward"):
        args = {**rest, TWIN_DIFF_INPUT: diff, **{k: w.astype(_WEIGHT_DTYPES[k]) for k, w in weights.items()}}
        y = _forward(args)
    with _jax.named_scope("loss_head"):
        err = _jnp.square(y.astype(_jnp.float32) - loss_target)
        return 0.5 * _jnp.sum(_jnp.mean(err, axis=-1)) if err.ndim else 0.5 * err


def _adamw(w, g, m, v):
    m = ADAM_B1 * m + (1.0 - ADAM_B1) * g
    v = ADAM_B2 * v + (1.0 - ADAM_B2) * _jnp.square(g)
    m_hat = m / (1.0 - ADAM_B1 ** ADAM_STEP)
    v_hat = v / (1.0 - ADAM_B2 ** ADAM_STEP)
    delta = -ADAM_LR * (m_hat / (_jnp.sqrt(v_hat) + ADAM_EPS) + ADAM_WD * w)
    return delta, m, v


def reference(x, c, ada_w, ada_b, norm_mix, norm_mlp, fox_w_in, fox_b_f, fox_w_out, conv_w_in, conv_w, conv_w_out, mlp_w_up, mlp_w_down, final_norm, loss_target, m_ada_w, m_ada_b, m_norm_mix, m_norm_mlp, m_fox_w_in, m_fox_b_f, m_fox_w_out, m_conv_w_in, m_conv_w, m_conv_w_out, m_mlp_w_up, m_mlp_w_down, m_final_norm, v_ada_w, v_ada_b, v_norm_mix, v_norm_mlp, v_fox_w_in, v_fox_b_f, v_fox_w_out, v_conv_w_in, v_conv_w, v_conv_w_out, v_mlp_w_up, v_mlp_w_down, v_final_norm):
    given = dict(x=x, c=c, ada_w=ada_w, ada_b=ada_b, norm_mix=norm_mix, norm_mlp=norm_mlp, fox_w_in=fox_w_in, fox_b_f=fox_b_f, fox_w_out=fox_w_out, conv_w_in=conv_w_in, conv_w=conv_w, conv_w_out=conv_w_out, mlp_w_up=mlp_w_up, mlp_w_down=mlp_w_down, final_norm=final_norm, loss_target=loss_target, m_ada_w=m_ada_w, m_ada_b=m_ada_b, m_norm_mix=m_norm_mix, m_norm_mlp=m_norm_mlp, m_fox_w_in=m_fox_w_in, m_fox_b_f=m_fox_b_f, m_fox_w_out=m_fox_w_out, m_conv_w_in=m_conv_w_in, m_conv_w=m_conv_w, m_conv_w_out=m_conv_w_out, m_mlp_w_up=m_mlp_w_up, m_mlp_w_down=m_mlp_w_down, m_final_norm=m_final_norm, v_ada_w=v_ada_w, v_ada_b=v_ada_b, v_norm_mix=v_norm_mix, v_norm_mlp=v_norm_mlp, v_fox_w_in=v_fox_w_in, v_fox_b_f=v_fox_b_f, v_fox_w_out=v_fox_w_out, v_conv_w_in=v_conv_w_in, v_conv_w=v_conv_w, v_conv_w_out=v_conv_w_out, v_mlp_w_up=v_mlp_w_up, v_mlp_w_down=v_mlp_w_down, v_final_norm=v_final_norm)
    weights = {n: given[n] for n in TWIN_WEIGHTS}
    shared = {n: given[n] for n in SHARED_INPUTS}
    per_example = {n: given[n] for n in ['x', 'c']}
    grad_fn = _jax.value_and_grad(_loss, argnums=(0, 1))

    def one_microbatch(ex, loss_target):
        ex = dict(ex)
        diff = ex.pop(TWIN_DIFF_INPUT)
        return grad_fn(weights, diff, {**shared, **ex}, loss_target)

    if N_MICROBATCH == 1:
        loss, (grad_w, grad_x) = one_microbatch(per_example, given["loss_target"])
    else:
        def body(carry, xs):
            loss_sum, grad_sum = carry
            l_k, (gw_k, gx_k) = one_microbatch(xs[0], xs[1])
            with _jax.named_scope("update"):
                return (loss_sum + l_k, _jax.tree.map(_jnp.add, grad_sum, gw_k)), gx_k

        init = (_jnp.zeros((), _jnp.float32), _jax.tree.map(_jnp.zeros_like, weights))
        (loss, grad_w), grad_x = _jax.lax.scan(body, init, (per_example, given["loss_target"]))
    with _jax.named_scope("update"):
        delta_w, new_m, new_v = {}, {}, {}
        for n in TWIN_WEIGHTS:
            delta_w[n], new_m[n], new_v[n] = _adamw(weights[n], grad_w[n], given["m_" + n], given["v_" + n])
    return (loss, grad_x, *[grad_w[n] for n in TWIN_WEIGHTS], *[delta_w[n] for n in TWIN_WEIGHTS],
            *[new_m[n] for n in TWIN_WEIGHTS], *[new_v[n] for n in TWIN_WEIGHTS])
```

```python
import functools
import math

import jax
import jax.numpy as jnp
from jax import lax
from jax.experimental import pallas as pl
from jax.experimental.pallas import tpu as pltpu

F32 = jnp.float32
BF16 = jnp.bfloat16
MESH = pl.DeviceIdType.MESH
NDEV = 8
HEAD_DIM = 128
LANES = 128
CONV_WIDTH = 3
RMS_EPS = 1e-6
ADAM_LR, ADAM_B1, ADAM_B2, ADAM_EPS, ADAM_WD, ADAM_STEP = 0.001, 0.9, 0.999, 1e-08, 0.01, 10
NEG = -1e30
V7X_VMEM_BYTES = 64 * 1024 * 1024
VMEM_HEADROOM = 12 * 1024 * 1024
HBM = pl.BlockSpec(memory_space=pltpu.HBM)
HIGHEST = lax.Precision.HIGHEST


def _nbytes(shape, dtype):
    return math.prod(shape) * jnp.dtype(dtype).itemsize


def _params(semantics, block_bytes, temp_bytes=0):
    limit = min(2 * block_bytes + temp_bytes + VMEM_HEADROOM, V7X_VMEM_BYTES - 4 * 1024 * 1024)
    return pltpu.CompilerParams(dimension_semantics=semantics, vmem_limit_bytes=int(limit))


def _my_index():
    return lax.axis_index("x") * 4 + lax.axis_index("y") * 2 + lax.axis_index("c")


def _peer(r):
    x, y, c = lax.axis_index("x"), lax.axis_index("y"), lax.axis_index("c")
    px = 1 - x if (r >> 2) & 1 else x
    py = 1 - y if (r >> 1) & 1 else y
    pc = 1 - c if r & 1 else c
    return (px, py, pc), px * 4 + py * 2 + pc


def _exchange(arrays, name, scatter):
    n = len(arrays)

    def body(*refs):
        ins, outs = refs[:n], refs[n:2 * n]
        send_sems, recv_sems, local_sems = refs[2 * n:]
        me = _my_index()
        local = []
        for a in range(n):
            src = ins[a].at[me] if scatter else ins[a]
            local.append(pltpu.make_async_copy(src, outs[a].at[me], local_sems.at[a]))
            local[-1].start()
        sends = []
        for r in range(1, NDEV):
            peer, pidx = _peer(r)
            for a in range(n):
                src = ins[a].at[pidx] if scatter else ins[a]
                cp = pltpu.make_async_remote_copy(
                    src_ref=src, dst_ref=outs[a].at[me],
                    send_sem=send_sems.at[a * (NDEV - 1) + r - 1], recv_sem=recv_sems.at[a * (NDEV - 1) + r - 1],
                    device_id=peer, device_id_type=MESH)
                cp.start()
                sends.append(cp)
        for r in range(1, NDEV):
            peer, pidx = _peer(r)
            for a in range(n):
                src = ins[a].at[pidx] if scatter else ins[a]
                pltpu.make_async_remote_copy(
                    src_ref=src, dst_ref=outs[a].at[pidx],
                    send_sem=send_sems.at[a * (NDEV - 1) + r - 1], recv_sem=recv_sems.at[a * (NDEV - 1) + r - 1],
                    device_id=peer, device_id_type=MESH).wait_recv()
        for cp in sends:
            cp.wait_send()
        for cp in local:
            cp.wait()

    out_shape = [jax.ShapeDtypeStruct(a.shape if scatter else (NDEV,) + a.shape, a.dtype) for a in arrays]
    return pl.pallas_call(
        body, name=name, out_shape=out_shape, in_specs=[HBM] * n, out_specs=[HBM] * n,
        scratch_shapes=[pltpu.SemaphoreType.DMA((n * (NDEV - 1),)), pltpu.SemaphoreType.DMA((n * (NDEV - 1),)),
                        pltpu.SemaphoreType.DMA((n,))],
    )(*arrays)


def _all_gather(arrays, name):
    return _exchange(arrays, name, scatter=False)


def _all_to_all(arrays, name):
    return _exchange(arrays, name, scatter=True)


def _matmul(a, b, *, mode, name, out_dtypes, tm, tn, tk, epilogue=None, extras=(), a_pre=None,
            b_shards=False, out_shards=False, precision=None):
    if mode == "tn":
        K, M = a.shape
    else:
        M, K = a.shape
    if b_shards:
        shard_cols = b.shape[2]
        N = b.shape[1] if mode == "nt" else shard_cols * NDEV
        assert b.shape[0] == NDEV and (tk if mode == "nt" else tn) == shard_cols
    else:
        N = b.shape[0] if mode == "nt" else b.shape[1]
    tm, tn, tk = min(tm, M), min(tn, N), min(tk, K)
    assert M % tm == 0 and N % tn == 0 and K % tk == 0, (name, M, N, K, tm, tn, tk)
    nm, nn, nk = M // tm, N // tn, K // tk
    n_out, n_ext = len(out_dtypes), len(extras)
    contract = {"nn": ((1,), (0,)), "nt": ((1,), (1,)), "tn": ((0,), (0,))}[mode]

    def body(*refs):
        a_ref, b_ref = refs[:2]
        ext_refs = refs[2:2 + n_ext]
        out_refs = refs[2 + n_ext:2 + n_ext + n_out]
        acc_ref = refs[2 + n_ext + n_out] if nk > 1 else None
        av, bv = a_ref[...], b_ref[...]
        if a_pre is not None:
            av = a_pre(av)
        if precision is None:
            av, bv = av.astype(BF16), bv.astype(BF16)
        part = lax.dot_general(av, bv, (contract, ((), ())), preferred_element_type=F32, precision=precision)

        def finish(acc):
            vals = (acc,) if epilogue is None else epilogue(acc, *[r[...] for r in ext_refs])
            for r, v in zip(out_refs, vals):
                r[...] = v.astype(r.dtype)

        if nk == 1:
            finish(part)
        else:
            k = pl.program_id(2)

            @pl.when(k == 0)
            def _():
                acc_ref[...] = part

            @pl.when(k > 0)
            def _():
                acc_ref[...] += part

            @pl.when(k == nk - 1)
            def _():
                finish(acc_ref[...])

    if mode == "tn":
        a_spec = pl.BlockSpec((tk, tm), lambda i, j, k: (k, i))
    else:
        a_spec = pl.BlockSpec((tm, tk), lambda i, j, k: (i, k))
    if b_shards and mode == "nt":
        b_spec, b_block = pl.BlockSpec((None, tn, tk), lambda i, j, k: (k, j, 0)), (tn, tk)
    elif b_shards:
        b_spec, b_block = pl.BlockSpec((None, tk, tn), lambda i, j, k: (j, k, 0)), (tk, tn)
    elif mode == "nt":
        b_spec, b_block = pl.BlockSpec((tn, tk), lambda i, j, k: (j, k)), (tn, tk)
    else:
        b_spec, b_block = pl.BlockSpec((tk, tn), lambda i, j, k: (k, j)), (tk, tn)
    in_specs, block_bytes = [a_spec, b_spec], _nbytes((tm, tk), a.dtype) + _nbytes(b_block, b.dtype)
    for arr, kind in extras:
        if kind == "tile":
            assert arr.shape == (M, N), (name, arr.shape)
            in_specs.append(pl.BlockSpec((tm, tn), lambda i, j, k: (i, j)))
            block_bytes += _nbytes((tm, tn), arr.dtype)
        else:
            assert arr.shape == (1, N), (name, arr.shape)
            in_specs.append(pl.BlockSpec((1, tn), lambda i, j, k: (0, j)))
    if out_shards:
        assert n_out == 1 and tn * NDEV == N
        out_shape = [jax.ShapeDtypeStruct((NDEV, M, tn), out_dtypes[0])]
        out_specs = [pl.BlockSpec((None, tm, tn), lambda i, j, k: (j, i, 0))]
    else:
        out_shape = [jax.ShapeDtypeStruct((M, N), d) for d in out_dtypes]
        out_specs = [pl.BlockSpec((tm, tn), lambda i, j, k: (i, j)) for _ in out_dtypes]
    block_bytes += sum(_nbytes((tm, tn), d) for d in out_dtypes)
    scratch = [pltpu.VMEM((tm, tn), F32)] if nk > 1 else []
    outs = pl.pallas_call(
        body, name=name, grid=(nm, nn, nk), in_specs=in_specs, out_specs=out_specs, out_shape=out_shape,
        scratch_shapes=scratch,
        compiler_params=_params(("parallel", "parallel", "arbitrary"), block_bytes, 2 * tm * tn * 4),
    )(a, b, *[arr for arr, _ in extras])
    return outs[0] if n_out == 1 else outs


def _rowwise(fn, tiled, smalls, out_tiles, out_sums, *, name, ts=256):
    S = tiled[0].shape[0]
    ts = min(ts, S)
    assert S % ts == 0
    nt, ns, no, na = len(tiled), len(smalls), len(out_tiles), len(out_sums)

    def body(*refs):
        t_refs, s_refs = refs[:nt], refs[nt:nt + ns]
        o_refs, a_refs = refs[nt + ns:nt + ns + no], refs[nt + ns + no:]
        tile_vals, sum_vals = fn([r[...] for r in t_refs], [r[...] for r in s_refs])
        for r, v in zip(o_refs, tile_vals):
            r[...] = v.astype(r.dtype)

        @pl.when(pl.program_id(0) == 0)
        def _():
            for r in a_refs:
                r[...] = jnp.zeros_like(r)

        for r, v in zip(a_refs, sum_vals):
            r[...] += v

    in_specs = [pl.BlockSpec((ts, t.shape[1]), lambda i: (i, 0)) for t in tiled]
    in_specs += [pl.BlockSpec(s.shape, lambda i: (0, 0)) for s in smalls]
    out_specs = [pl.BlockSpec((ts, w), lambda i: (i, 0)) for w, _ in out_tiles]
    out_specs += [pl.BlockSpec((1, w), lambda i: (0, 0)) for w in out_sums]
    out_shape = [jax.ShapeDtypeStruct((S, w), d) for w, d in out_tiles]
    out_shape += [jax.ShapeDtypeStruct((1, w), F32) for w in out_sums]
    block_bytes = sum(_nbytes((ts, t.shape[1]), t.dtype) for t in tiled) + sum(_nbytes((ts, w), d) for w, d in out_tiles)
    width = max(t.shape[1] for t in tiled)
    outs = pl.pallas_call(
        body, name=name, grid=(S // ts,), in_specs=in_specs, out_specs=out_specs, out_shape=out_shape,
        compiler_params=_params(("arbitrary",), block_bytes, 6 * ts * width * 4),
    )(*tiled, *smalls)
    return outs[:no], outs[no:]


def _colsum(v):
    return jnp.sum(v, axis=0, keepdims=True)


def _rms_mod_fwd(x, gain, shift, scale, name):
    def fn(tiles, smalls):
        (xv,), (g, sh, sc) = tiles, smalls
        inv = lax.rsqrt(jnp.mean(xv * xv, axis=-1, keepdims=True) + RMS_EPS)
        h = (xv * inv) * g * (1.0 + sc) + sh
        return (h, inv), ()

    D = x.shape[1]
    (h, inv), _ = _rowwise(fn, [x], [gain, shift, scale], [(D, BF16), (1, F32)], [], name=name)
    return h, inv


def _gate_bwd(dx, y, gate, name):
    def fn(tiles, smalls):
        (dxv, yv), (g,) = tiles, smalls
        return (dxv * g,), (_colsum(dxv * yv),)

    D = dx.shape[1]
    (dy,), (dgate,) = _rowwise(fn, [dx, y], [gate], [(D, BF16)], [D], name=name)
    return dy, dgate


def _rms_mod_bwd(dh, x, inv, dx_res, gain, scale, name):
    def fn(tiles, smalls):
        (dhv, xv, iv, dres), (g, sc) = tiles, smalls
        dhv = dhv.astype(F32)
        xhat = xv * iv
        dr = dhv * (1.0 + sc)
        dxhat = dr * g
        dxn = iv * (dxhat - xhat * jnp.mean(dxhat * xhat, axis=-1, keepdims=True))
        return (dres + dxn,), (_colsum(dhv), _colsum(dhv * (xhat * g)), _colsum(dr * xhat))

    D = x.shape[1]
    (dx,), (dsh, dsc, dgain) = _rowwise(fn, [dh, x, inv, dx_res], [gain, scale], [(D, F32)], [D, D, D], name=name)
    return dx, dsh, dsc, dgain


def _final_loss_bwd(x, target, gain, name):
    D = x.shape[1]

    def fn(tiles, smalls):
        (xv, tv), (g,) = tiles, smalls
        inv = lax.rsqrt(jnp.mean(xv * xv, axis=-1, keepdims=True) + RMS_EPS)
        xhat = xv * inv
        err = xhat * g - tv
        loss = 0.5 * jnp.sum(jnp.mean(err * err, axis=-1, keepdims=True), axis=0, keepdims=True)
        dout = err * (1.0 / D)
        dxhat = dout * g
        dxv = inv * (dxhat - xhat * jnp.mean(dxhat * xhat, axis=-1, keepdims=True))
        return (dxv,), (_colsum(dout * xhat), jnp.broadcast_to(loss, (1, LANES)))

    (dx,), (dgain, loss) = _rowwise(fn, [x, target], [gain], [(D, F32)], [D, LANES], name=name)
    return dx, dgain, loss


SCAN_BLOCK = 256


def _triangle(n, lower):
    r = lax.broadcasted_iota(jnp.int32, (n, n), 0)
    c = lax.broadcasted_iota(jnp.int32, (n, n), 1)
    return (r >= c if lower else r <= c).astype(F32)


def _forget_cumsum(logits, bias, name):
    S = logits.shape[0]
    blk = min(SCAN_BLOCK, S)
    nb = S // blk

    def body(z_ref, b_ref, f_ref):
        z = z_ref[...] + b_ref[...]
        f_ref[...] = jnp.minimum(z, 0.0) - jnp.log(1.0 + jnp.exp(-jnp.abs(z)))
        tri = _triangle(blk, lower=True)

        def step(i, carry):
            off = pl.multiple_of(i * blk, blk)
            cs = jnp.dot(tri, f_ref[pl.ds(off, blk), :], preferred_element_type=F32, precision=HIGHEST) + carry
            f_ref[pl.ds(off, blk), :] = cs
            return cs[blk - 1:blk, :]

        lax.fori_loop(0, nb, step, jnp.zeros((1, LANES), F32))

    return pl.pallas_call(body, name=name, out_shape=jax.ShapeDtypeStruct((S, LANES), F32))(logits, bias)


def _forget_bwd(dfk, logits, bias, name):
    S = logits.shape[0]
    blk = min(SCAN_BLOCK, S)
    nb = S // blk

    def body(d_ref, z_ref, b_ref, o_ref, db_ref):
        tri = _triangle(blk, lower=False)

        def step(t, carry):
            off = pl.multiple_of((nb - 1 - t) * blk, blk)
            cs = jnp.dot(tri, d_ref[pl.ds(off, blk), :], preferred_element_type=F32, precision=HIGHEST) + carry
            o_ref[pl.ds(off, blk), :] = cs
            return cs[0:1, :]

        lax.fori_loop(0, nb, step, jnp.zeros((1, LANES), F32))
        z = z_ref[...] + b_ref[...]
        dz = -o_ref[...] / (1.0 + jnp.exp(z))
        o_ref[...] = dz
        db_ref[...] = _colsum(dz)

    return pl.pallas_call(
        body, name=name,
        out_shape=(jax.ShapeDtypeStruct((S, LANES), F32), jax.ShapeDtypeStruct((1, LANES), F32)),
    )(dfk, logits, bias)


ATTN_BLOCK = 512
_NT = (((1,), (1,)), ((), ()))


def _attn_specs(S, H, tb):
    q_blk = lambda part: pl.BlockSpec((tb, HEAD_DIM), lambda h, i: (i, part * H + h))
    q_all = lambda part: pl.BlockSpec((S, HEAD_DIM), lambda h, i: (0, part * H + h))
    col_blk = pl.BlockSpec((None, tb, 1), lambda h, i: (h, i, 0))
    row_all = pl.BlockSpec((None, 1, S), lambda h, i: (h, 0, 0))
    return q_blk, q_all, col_blk, row_all


def _attn_params(S, tb):
    return _params(("parallel", "parallel"), 4 * S * HEAD_DIM * 2, 10 * tb * tb * 4)


def _attn_fwd(qkv, f_col, f_row, name):
    S, H = qkv.shape[0], qkv.shape[1] // (3 * HEAD_DIM)
    tb = min(ATTN_BLOCK, S)
    scale = HEAD_DIM ** -0.5
    q_blk, q_all, col_blk, row_all = _attn_specs(S, H, tb)

    def body(q_ref, k_ref, v_ref, fc_ref, fr_ref, o_ref, lse_ref):
        i = pl.program_id(1)
        q, fc = q_ref[...], fc_ref[...]

        def step(j, carry, diagonal):
            m, l, acc = carry
            off = pl.multiple_of(j * tb, tb)
            k, v = k_ref[pl.ds(off, tb), :], v_ref[pl.ds(off, tb), :]
            s = lax.dot_general(q, k, _NT, preferred_element_type=F32) * scale + (fc - fr_ref[:, pl.ds(off, tb)])
            if diagonal:
                row = lax.broadcasted_iota(jnp.int32, (tb, tb), 0)
                col = lax.broadcasted_iota(jnp.int32, (tb, tb), 1)
                s = jnp.where(col <= row, s, NEG)
            m_new = jnp.maximum(m, jnp.max(s, axis=-1, keepdims=True))
            p = jnp.exp(s - m_new)
            alpha = jnp.exp(m - m_new)
            l = alpha * l + jnp.sum(p, axis=-1, keepdims=True)
            acc = alpha * acc + jnp.dot(p.astype(BF16), v, preferred_element_type=F32)
            return m_new, l, acc

        init = (jnp.full((tb, 1), NEG, F32), jnp.zeros((tb, 1), F32), jnp.zeros((tb, HEAD_DIM), F32))
        carry = lax.fori_loop(0, i, lambda j, c: step(j, c, False), init)
        m, l, acc = step(i, carry, True)
        o_ref[...] = (acc / l).astype(o_ref.dtype)
        lse_ref[...] = m + jnp.log(l)

    return pl.pallas_call(
        body, name=name, grid=(H, S // tb),
        in_specs=[q_blk(0), q_all(1), q_all(2), col_blk, row_all],
        out_specs=[pl.BlockSpec((tb, HEAD_DIM), lambda h, i: (i, h)), col_blk],
        out_shape=[jax.ShapeDtypeStruct((S, H * HEAD_DIM), BF16), jax.ShapeDtypeStruct((H, S, 1), F32)],
        compiler_params=_attn_params(S, tb),
    )(qkv, qkv, qkv, f_col, f_row)


def _attn_bwd_q(qkv, do, f_col, f_row, lse_col, name):
    S, H = qkv.shape[0], qkv.shape[1] // (3 * HEAD_DIM)
    tb = min(ATTN_BLOCK, S)
    nq = S // tb
    scale = HEAD_DIM ** -0.5
    q_blk, q_all, col_blk, row_all = _attn_specs(S, H, tb)
    head_blk = pl.BlockSpec((tb, HEAD_DIM), lambda h, i: (i, h))

    def body(q_ref, k_ref, v_ref, do_ref, fc_ref, fr_ref, lse_ref, dq_ref, delta_ref, p_buf, dp_buf):
        i = pl.program_id(1)
        q, do, fc, lse = q_ref[...], do_ref[...], fc_ref[...], lse_ref[...]

        def scores(j, delta, diagonal):
            off = pl.multiple_of(j * tb, tb)
            k, v = k_ref[pl.ds(off, tb), :], v_ref[pl.ds(off, tb), :]
            s = lax.dot_general(q, k, _NT, preferred_element_type=F32) * scale + (fc - fr_ref[:, pl.ds(off, tb)])
            if diagonal:
                row = lax.broadcasted_iota(jnp.int32, (tb, tb), 0)
                col = lax.broadcasted_iota(jnp.int32, (tb, tb), 1)
                s = jnp.where(col <= row, s, NEG)
            p = jnp.exp(s - lse)
            dp = lax.dot_general(do, v, _NT, preferred_element_type=F32)
            p_buf[j] = p
            dp_buf[j] = dp
            return delta + jnp.sum(p * dp, axis=-1, keepdims=True)

        delta = lax.fori_loop(0, i, lambda j, c: scores(j, c, False), jnp.zeros((tb, 1), F32))
        delta = scores(i, delta, True)
        delta_ref[...] = delta

        def grad(j, dq):
            off = pl.multiple_of(j * tb, tb)
            ds = p_buf[j] * (dp_buf[j] - delta)
            return dq + jnp.dot(ds.astype(BF16), k_ref[pl.ds(off, tb), :], preferred_element_type=F32)

        dq = lax.fori_loop(0, i + 1, grad, jnp.zeros((tb, HEAD_DIM), F32))
        dq_ref[...] = (dq * scale).astype(dq_ref.dtype)

    return pl.pallas_call(
        body, name=name, grid=(H, nq),
        in_specs=[q_blk(0), q_all(1), q_all(2), head_blk, col_blk, row_all, col_blk],
        out_specs=[head_blk, col_blk],
        out_shape=[jax.ShapeDtypeStruct((S, H * HEAD_DIM), BF16), jax.ShapeDtypeStruct((H, S, 1), F32)],
        scratch_shapes=[pltpu.VMEM((nq, tb, tb), F32), pltpu.VMEM((nq, tb, tb), F32)],
        compiler_params=_params(("parallel", "parallel"), 4 * S * HEAD_DIM * 2, 2 * nq * tb * tb * 4 + 10 * tb * tb * 4),
    )(qkv, qkv, qkv, do, f_col, f_row, lse_col)


def _attn_bwd_kv(qkv, do, f_col, f_row, lse_row, delta_row, name):
    S, H = qkv.shape[0], qkv.shape[1] // (3 * HEAD_DIM)
    tb = min(ATTN_BLOCK, S)
    nq = S // tb
    scale = HEAD_DIM ** -0.5
    q_blk, q_all, col_blk, row_all = _attn_specs(S, H, tb)
    head_blk = pl.BlockSpec((tb, HEAD_DIM), lambda h, i: (i, h))
    head_all = pl.BlockSpec((S, HEAD_DIM), lambda h, i: (0, h))

    def body(k_ref, v_ref, q_ref, do_ref, fc_ref, fr_ref, lse_ref, delta_ref, dk_ref, dv_ref, dfk_ref):
        j = pl.program_id(1)
        k, v, fck = k_ref[...], v_ref[...], fc_ref[...]

        def step(i, carry, diagonal):
            dk, dv, dfk = carry
            off = pl.multiple_of(i * tb, tb)
            q, do = q_ref[pl.ds(off, tb), :], do_ref[pl.ds(off, tb), :]
            st = lax.dot_general(k, q, _NT, preferred_element_type=F32) * scale + (fr_ref[:, pl.ds(off, tb)] - fck)
            if diagonal:
                row = lax.broadcasted_iota(jnp.int32, (tb, tb), 0)
                col = lax.broadcasted_iota(jnp.int32, (tb, tb), 1)
                st = jnp.where(col >= row, st, NEG)
            pt = jnp.exp(st - lse_ref[:, pl.ds(off, tb)])
            dv = dv + jnp.dot(pt.astype(BF16), do, preferred_element_type=F32)
            dpt = lax.dot_general(v, do, _NT, preferred_element_type=F32)
            dst = pt * (dpt - delta_ref[:, pl.ds(off, tb)])
            dk = dk + jnp.dot(dst.astype(BF16), q, preferred_element_type=F32)
            return dk, dv, dfk + jnp.sum(dst, axis=-1, keepdims=True)

        zeros = jnp.zeros((tb, HEAD_DIM), F32)
        carry = step(j, (zeros, zeros, jnp.zeros((tb, 1), F32)), True)
        dk, dv, dfk = lax.fori_loop(j + 1, nq, lambda i, c: step(i, c, False), carry)
        dk_ref[...] = (dk * scale).astype(dk_ref.dtype)
        dv_ref[...] = dv.astype(dv_ref.dtype)
        dfk_ref[...] = dfk

    return pl.pallas_call(
        body, name=name, grid=(H, nq),
        in_specs=[q_blk(1), q_blk(2), q_all(0), head_all, col_blk, row_all, row_all, row_all],
        out_specs=[head_blk, head_blk, col_blk],
        out_shape=[jax.ShapeDtypeStruct((S, H * HEAD_DIM), BF16), jax.ShapeDtypeStruct((S, H * HEAD_DIM), BF16),
                   jax.ShapeDtypeStruct((H, S, 1), F32)],
        compiler_params=_attn_params(S, tb),
    )(qkv, qkv, qkv, do, f_col, f_row, lse_row, delta_row)


CONV_TILE = 128


def _shift_down(v, n):
    row = lax.broadcasted_iota(jnp.int32, v.shape, 0)
    return jnp.where(row >= n, pltpu.roll(v, n, 0), 0.0)


def _shift_up(v, n):
    S = v.shape[0]
    row = lax.broadcasted_iota(jnp.int32, v.shape, 0)
    return jnp.where(row < S - n, pltpu.roll(v, S - n, 0), 0.0)


def _conv_specs(S, D, tc):
    nb = D // tc
    part = lambda p: pl.BlockSpec((S, tc), lambda j: (0, p * nb + j))
    return part, pl.BlockSpec((S, tc), lambda j: (0, j)), pl.BlockSpec((8, tc), lambda j: (0, j))


def _conv_fwd(proj, conv_w8, name):
    S, D = proj.shape[0], proj.shape[1] // 3
    tc = min(CONV_TILE, D)
    part, chan, taps = _conv_specs(S, D, tc)

    def body(b_ref, c_ref, u_ref, w_ref, z_ref):
        cu = c_ref[...].astype(F32) * u_ref[...].astype(F32)
        w = w_ref[...]
        y = w[0:1, :] * _shift_down(cu, 2) + w[1:2, :] * _shift_down(cu, 1) + w[2:3, :] * cu
        z_ref[...] = (b_ref[...].astype(F32) * y).astype(z_ref.dtype)

    return pl.pallas_call(
        body, name=name, grid=(D // tc,), in_specs=[part(0), part(1), part(2), taps], out_specs=chan,
        out_shape=jax.ShapeDtypeStruct((S, D), BF16),
        compiler_params=_params(("parallel",), 3 * _nbytes((S, tc), proj.dtype) + S * tc * 2, 6 * S * tc * 4),
    )(proj, proj, proj, conv_w8)


def _conv_bwd(proj, dz, conv_w8, name):
    S, D = proj.shape[0], proj.shape[1] // 3
    tc = min(CONV_TILE, D)
    part, chan, taps = _conv_specs(S, D, tc)

    def body(b_ref, c_ref, u_ref, dz_ref, w_ref, db_ref, dc_ref, du_ref, dw_ref):
        cv, uv = c_ref[...].astype(F32), u_ref[...].astype(F32)
        dzv, w = dz_ref[...].astype(F32), w_ref[...]
        cu = cv * uv
        cu1, cu2 = _shift_down(cu, 1), _shift_down(cu, 2)
        y = w[0:1, :] * cu2 + w[1:2, :] * cu1 + w[2:3, :] * cu
        db_ref[...] = (dzv * y).astype(db_ref.dtype)
        dy = dzv * b_ref[...].astype(F32)
        dcu = w[2:3, :] * dy + w[1:2, :] * _shift_up(dy, 1) + w[0:1, :] * _shift_up(dy, 2)
        dc_ref[...] = (dcu * uv).astype(dc_ref.dtype)
        du_ref[...] = (dcu * cv).astype(du_ref.dtype)
        dw_ref[...] = jnp.concatenate(
            [_colsum(dy * cu2), _colsum(dy * cu1), _colsum(dy * cu), jnp.zeros((8 - CONV_WIDTH, tc), F32)], axis=0)

    return pl.pallas_call(
        body, name=name, grid=(D // tc,), in_specs=[part(0), part(1), part(2), chan, taps],
        out_specs=[chan, chan, chan, taps],
        out_shape=[jax.ShapeDtypeStruct((S, D), BF16)] * 3 + [jax.ShapeDtypeStruct((8, D), F32)],
        compiler_params=_params(("parallel",), 3 * _nbytes((S, tc), proj.dtype) + _nbytes((S, tc), dz.dtype)
                                + 3 * S * tc * 2, 10 * S * tc * 4),
    )(proj, proj, proj, dz, conv_w8)


def _adamw(w, m, v, parts, name):
    R, C = w.shape
    P = parts.shape[0]
    row_bytes = C * (12 + 16 + P * parts.dtype.itemsize)
    tr = R if R * row_bytes <= (4 << 20) else max(8, ((4 << 20) // row_bytes) // 8 * 8)
    while R % tr:
        tr -= 8
    c1, c2 = 1.0 - ADAM_B1 ** ADAM_STEP, 1.0 - ADAM_B2 ** ADAM_STEP

    def body(w_ref, m_ref, v_ref, p_ref, g_ref, d_ref, nm_ref, nv_ref):
        g = p_ref[0].astype(F32)
        for p in range(1, P):
            g = g + p_ref[p].astype(F32)
        nm = ADAM_B1 * m_ref[...] + (1.0 - ADAM_B1) * g
        nv = ADAM_B2 * v_ref[...] + (1.0 - ADAM_B2) * (g * g)
        g_ref[...] = g
        nm_ref[...] = nm
        nv_ref[...] = nv
        d_ref[...] = -ADAM_LR * ((nm / c1) / (jnp.sqrt(nv / c2) + ADAM_EPS) + ADAM_WD * w_ref[...])

    blk = pl.BlockSpec((tr, C), lambda i: (i, 0))
    return pl.pallas_call(
        body, name=name, grid=(R // tr,), in_specs=[blk, blk, blk, pl.BlockSpec((P, tr, C), lambda i: (0, i, 0))],
        out_specs=[blk] * 4, out_shape=[jax.ShapeDtypeStruct((R, C), F32)] * 4,
        compiler_params=_params(("parallel",), tr * row_bytes),
    )(w, m, v, parts)


def _silu(v):
    return v / (1.0 + jnp.exp(-v))


def _pad_rows(a, rows):
    return jnp.pad(a, ((0, rows - a.shape[0]), (0, 0)))


def _pad_cols(a, cols):
    return jnp.pad(a, ((0, 0), (0, cols - a.shape[1])))


def kernel(x, c, ada_w, ada_b, norm_mix, norm_mlp, fox_w_in, fox_b_f, fox_w_out, conv_w_in, conv_w, conv_w_out, mlp_w_up, mlp_w_down, final_norm, loss_target, m_ada_w, m_ada_b, m_norm_mix, m_norm_mlp, m_fox_w_in, m_fox_b_f, m_fox_w_out, m_conv_w_in, m_conv_w, m_conv_w_out, m_mlp_w_up, m_mlp_w_down, m_final_norm, v_ada_w, v_ada_b, v_norm_mix, v_norm_mlp, v_fox_w_in, v_fox_b_f, v_fox_w_out, v_conv_w_in, v_conv_w, v_conv_w_out, v_mlp_w_up, v_mlp_w_down, v_final_norm):
    S, D = x.shape[1], x.shape[2]
    H = D // HEAD_DIM
    FF = mlp_w_up.shape[2] * NDEV
    depth = ada_w.shape[0]
    n_mod = 6
    assert depth == 2 and fox_w_in.shape[0] == 1 and conv_w_in.shape[0] == 1 and H <= LANES
    me = _my_index()
    x0, target = x[0], loss_target[0]
    row = lambda vec: vec.reshape(1, -1)

    c_all = _all_gather([c], "gather_cond")[0].reshape(NDEV, D)
    ncol = ada_w.shape[2]
    ada_b_mine = lax.dynamic_slice_in_dim(ada_b, me * ncol, ncol, axis=1)
    mod_cols = jnp.stack([
        _matmul(c_all, ada_w[i], mode="nn", name=f"ada_fwd_{i}", out_dtypes=[F32], tm=NDEV, tn=ncol // 2, tk=D,
                a_pre=_silu, precision=HIGHEST, epilogue=lambda acc, b: (acc + b,), extras=[(ada_b_mine[i:i + 1], "row")])
        for i in range(depth)])
    mod_all = _all_gather([mod_cols], "gather_mod")[0]
    mod = lax.dynamic_index_in_dim(mod_all, me, axis=2, keepdims=False)
    mod = mod.transpose(1, 0, 2).reshape(depth, n_mod, 1, D)
    sh_mix, sc_mix, g_mix, sh_mlp, sc_mlp, g_mlp = (mod[:, k] for k in range(n_mod))

    gathered = _all_gather(
        [fox_w_in[0].astype(BF16), fox_w_out[0].astype(BF16), conv_w_in[0].astype(BF16), conv_w[0],
         conv_w_out[0].astype(BF16), mlp_w_up[0].astype(BF16), mlp_w_up[1].astype(BF16),
         mlp_w_down[0].astype(BF16), mlp_w_down[1].astype(BF16)], "gather_weights")
    fox_in = gathered[0].transpose(1, 0, 2).reshape(D, 3 * D + H)
    w_qkv, w_f = fox_in[:, :3 * D], _pad_cols(fox_in[:, 3 * D:], LANES)
    w_fox_out = gathered[1].reshape(D, D)
    w_conv_in = gathered[2]
    w_taps = _pad_rows(gathered[3].transpose(1, 0, 2).reshape(CONV_WIDTH, D), 8)
    w_conv_out = gathered[4].reshape(D, D)
    w_up = [gathered[5], gathered[6]]
    w_down = [gathered[7].reshape(FF, D), gathered[8].reshape(FF, D)]
    b_f = _pad_cols(fox_b_f, LANES)

    def residual(acc, x_in, gate):
        return (x_in + gate * acc, acc)

    def mlp_fwd(i, x_in):
        h, inv = _rms_mod_fwd(x_in, row(norm_mlp[i]), sh_mlp[i], sc_mlp[i], f"mlp_norm_{i}")
        r, a = _matmul(h, w_up[i], mode="nn", name=f"mlp_up_{i}", out_dtypes=[BF16, BF16], tm=1024, tn=FF // NDEV, tk=D,
                       b_shards=True, epilogue=lambda acc: (jnp.maximum(acc, 0.0), jnp.square(jnp.maximum(acc, 0.0))))
        x_out, y = _matmul(a, w_down[i], mode="nn", name=f"mlp_down_{i}", out_dtypes=[F32, F32], tm=512, tn=1024, tk=1024,
                           epilogue=residual, extras=[(x_in, "tile"), (g_mlp[i], "row")])
        return x_out, (x_in, h, inv, r, a, y)

    def mlp_bwd(i, dx, saved):
        x_in, h, inv, r, a, y = saved
        dy, dgate = _gate_bwd(dx, y, g_mlp[i], f"mlp_gate_bwd_{i}")
        du = _matmul(dy, w_down[i], mode="nt", name=f"mlp_down_bwd_{i}", out_dtypes=[BF16], tm=1024, tn=1024, tk=D,
                     epilogue=lambda acc, rv: (acc * (2.0 * rv.astype(F32)),), extras=[(r, "tile")])
        d_down = _matmul(a, dy, mode="tn", name=f"mlp_down_wgrad_{i}", out_dtypes=[BF16], tm=1024, tn=1024, tk=1024)
        dh = _matmul(du, w_up[i], mode="nt", name=f"mlp_up_bwd_{i}", out_dtypes=[F32], tm=1024, tn=1024, tk=FF // NDEV,
                     b_shards=True)
        d_up = _matmul(h, du, mode="tn", name=f"mlp_up_wgrad_{i}", out_dtypes=[BF16], tm=1024, tn=FF // NDEV, tk=1024,
                       out_shards=True)
        dx, dsh, dsc, dgain = _rms_mod_bwd(dh, x_in, inv, dx, row(norm_mlp[i]), sc_mlp[i], f"mlp_norm_bwd_{i}")
        return dx, (dsh, dsc, dgate, dgain), d_up, d_down.reshape(NDEV, FF // NDEV, D)

    h0, inv0 = _rms_mod_fwd(x0, row(norm_mix[0]), sh_mix[0], sc_mix[0], "fox_norm")
    qkv = _matmul(h0, w_qkv, mode="nn", name="fox_qkv", out_dtypes=[BF16], tm=1024, tn=1024, tk=D)
    f_logit = _matmul(h0, w_f, mode="nn", name="fox_forget_logits", out_dtypes=[F32], tm=1024, tn=LANES, tk=D)
    f_cum = _forget_cumsum(f_logit, b_f, "fox_forget_cumsum")
    f_heads = f_cum[:, :H].T
    f_col, f_row = f_heads.reshape(H, S, 1), f_heads.reshape(H, 1, S)
    o, lse = _attn_fwd(qkv, f_col, f_row, "fox_attention")
    x1, mix0 = _matmul(o, w_fox_out, mode="nn", name="fox_out", out_dtypes=[F32, F32], tm=512, tn=1024, tk=D,
                       epilogue=residual, extras=[(x0, "tile"), (g_mix[0], "row")])
    x2, mlp0 = mlp_fwd(0, x1)

    h1, inv1 = _rms_mod_fwd(x2, row(norm_mix[1]), sh_mix[1], sc_mix[1], "conv_norm")
    proj = _matmul(h1, w_conv_in, mode="nn", name="conv_in", out_dtypes=[F32], tm=1024, tn=3 * D // NDEV, tk=D,
                   b_shards=True)
    z = _conv_fwd(proj, w_taps, "conv_mix")
    x3, mix1 = _matmul(z, w_conv_out, mode="nn", name="conv_out", out_dtypes=[F32, F32], tm=512, tn=1024, tk=D,
                       epilogue=residual, extras=[(x2, "tile"), (g_mix[1], "row")])
    x4, mlp1 = mlp_fwd(1, x3)

    dx, d_final, loss_lanes = _final_loss_bwd(x4, target, row(final_norm), "loss_head")
    loss = lax.psum(loss_lanes[0, 0], ("x", "y", "c"))

    dx, dmod_mlp1, d_up1, d_down1 = mlp_bwd(1, dx, mlp1)
    dmix, dg_mix1 = _gate_bwd(dx, mix1, g_mix[1], "conv_gate_bwd")
    dz = _matmul(dmix, w_conv_out, mode="nt", name="conv_out_bwd", out_dtypes=[F32], tm=1024, tn=1024, tk=D)
    d_conv_out = _matmul(z, dmix, mode="tn", name="conv_out_wgrad", out_dtypes=[BF16], tm=1024, tn=1024, tk=1024)
    db, dc, du, d_taps = _conv_bwd(proj, dz, w_taps, "conv_mix_bwd")
    dproj = jnp.concatenate([db, dc, du], axis=1)
    dh1 = _matmul(dproj, w_conv_in, mode="nt", name="conv_in_bwd", out_dtypes=[F32], tm=1024, tn=1024, tk=3 * D // NDEV,
                  b_shards=True)
    d_conv_in = _matmul(h1, dproj, mode="tn", name="conv_in_wgrad", out_dtypes=[BF16], tm=1024, tn=3 * D // NDEV, tk=1024,
                        out_shards=True)
    dx, dsh1, dsc1, dgain_mix1 = _rms_mod_bwd(dh1, x2, inv1, dx, row(norm_mix[1]), sc_mix[1], "conv_norm_bwd")

    dx, dmod_mlp0, d_up0, d_down0 = mlp_bwd(0, dx, mlp0)
    dmix, dg_mix0 = _gate_bwd(dx, mix0, g_mix[0], "fox_gate_bwd")
    do = _matmul(dmix, w_fox_out, mode="nt", name="fox_out_bwd", out_dtypes=[BF16], tm=1024, tn=1024, tk=D)
    d_fox_out = _matmul(o, dmix, mode="tn", name="fox_out_wgrad", out_dtypes=[BF16], tm=1024, tn=1024, tk=1024)
    dq, delta = _attn_bwd_q(qkv, do, f_col, f_row, lse, "fox_attention_bwd_q")
    dk, dv, dfk = _attn_bwd_kv(qkv, do, f_col, f_row, lse.reshape(H, 1, S), delta.reshape(H, 1, S), "fox_attention_bwd_kv")
    dqkv = jnp.concatenate([dq, dk, dv], axis=1)
    dfk_lanes = _pad_cols(dfk.reshape(H, S).T, LANES)
    df_logit, db_f = _forget_bwd(dfk_lanes, f_logit, b_f, "fox_forget_bwd")
    dh0_f = _matmul(df_logit, w_f, mode="nt", name="fox_forget_logits_bwd", out_dtypes=[F32], tm=1024, tn=1024, tk=LANES)
    dh0 = _matmul(dqkv, w_qkv, mode="nt", name="fox_qkv_bwd", out_dtypes=[F32], tm=1024, tn=1024, tk=1024,
                  epilogue=lambda acc, extra: (acc + extra,), extras=[(dh0_f, "tile")])
    d_qkv = _matmul(h0, dqkv, mode="tn", name="fox_qkv_wgrad", out_dtypes=[BF16], tm=1024, tn=1024, tk=1024)
    d_f = _matmul(h0, df_logit, mode="tn", name="fox_forget_wgrad", out_dtypes=[BF16], tm=1024, tn=LANES, tk=1024)
    dx, dsh0, dsc0, dgain_mix0 = _rms_mod_bwd(dh0, x0, inv0, dx, row(norm_mix[0]), sc_mix[0], "fox_norm_bwd")
    grad_x = dx.reshape(1, S, D)

    dmod = jnp.concatenate([
        jnp.concatenate([dsh0, dsc0, dg_mix0, dmod_mlp0[0], dmod_mlp0[1], dmod_mlp0[2]], axis=1),
        jnp.concatenate([dsh1, dsc1, dg_mix1, dmod_mlp1[0], dmod_mlp1[1], dmod_mlp1[2]], axis=1)], axis=0)
    small_sizes = [depth * n_mod * D, depth * D, depth * D, H, D]
    n_small = sum(small_sizes)
    n_rows = -(-n_small // (8 * LANES)) * 8

    def pack(parts):
        flat = jnp.concatenate([p.reshape(-1) for p in parts])
        return jnp.pad(flat, (0, n_rows * LANES - n_small)).reshape(n_rows, LANES)

    def unpack(packed, shapes):
        flat, out, at = packed.reshape(-1), [], 0
        for size, shape in zip(small_sizes, shapes):
            out.append(flat[at:at + size].reshape(shape))
            at += size
        return out

    small_partial = pack([dmod, jnp.concatenate([dgain_mix0, dgain_mix1], axis=0),
                          jnp.concatenate([dmod_mlp0[3], dmod_mlp1[3]], axis=0), db_f[0, :H], d_final])
    small_parts = _all_gather([small_partial], "gather_small_grads")[0]
    small_shapes = [ada_b.shape, norm_mix.shape, norm_mlp.shape, fox_b_f.shape, final_norm.shape]
    small_out = _adamw(pack([ada_b, norm_mix, norm_mlp, fox_b_f, final_norm]),
                       pack([m_ada_b, m_norm_mix, m_norm_mlp, m_fox_b_f, m_final_norm]),
                       pack([v_ada_b, v_norm_mix, v_norm_mlp, v_fox_b_f, v_final_norm]), small_parts, "adamw_small")
    small_out = [unpack(t, small_shapes) for t in small_out]

    dmod_all = small_parts.reshape(NDEV, -1)[:, :depth * n_mod * D].reshape(NDEV, depth, n_mod * D)
    dmod_mine = lax.dynamic_slice_in_dim(dmod_all, me * ncol, ncol, axis=2)
    d_ada = jnp.stack([
        _matmul(c_all, dmod_mine[:, i], mode="tn", name=f"ada_wgrad_{i}", out_dtypes=[F32], tm=1024, tn=ncol // 2, tk=NDEV,
                a_pre=_silu, precision=HIGHEST) for i in range(depth)])

    d_fox_in = jnp.concatenate([d_qkv, d_f[:, :H]], axis=1).reshape(D, NDEV, -1).transpose(1, 0, 2)
    d_taps_split = d_taps[:CONV_WIDTH].reshape(CONV_WIDTH, NDEV, -1).transpose(1, 0, 2)
    received = _all_to_all(
        [d_fox_in, d_fox_out.reshape(NDEV, D // NDEV, D), d_conv_in, d_taps_split, d_conv_out.reshape(NDEV, D // NDEV, D),
         jnp.stack([d_up0, d_up1], axis=1), jnp.stack([d_down0, d_down1], axis=1)], "scatter_weight_grads")

    def update(w, m, v, parts, name):
        shape = w.shape
        flat = lambda t: t.reshape(-1, shape[-1])
        outs = _adamw(flat(w), flat(m), flat(v), parts.reshape(parts.shape[0], -1, shape[-1]), name)
        return [t.reshape(shape) for t in outs]

    big = [
        update(ada_w, m_ada_w, v_ada_w, d_ada[None], "adamw_ada_w"),
        update(fox_w_in, m_fox_w_in, v_fox_w_in, received[0], "adamw_fox_w_in"),
        update(fox_w_out, m_fox_w_out, v_fox_w_out, received[1], "adamw_fox_w_out"),
        update(conv_w_in, m_conv_w_in, v_conv_w_in, received[2], "adamw_conv_w_in"),
        update(conv_w, m_conv_w, v_conv_w, received[3], "adamw_conv_w"),
        update(conv_w_out, m_conv_w_out, v_conv_w_out, received[4], "adamw_conv_w_out"),
        update(mlp_w_up, m_mlp_w_up, v_mlp_w_up, received[5], "adamw_mlp_w_up"),
        update(mlp_w_down, m_mlp_w_down, v_mlp_w_down, received[6], "adamw_mlp_w_down"),
    ]
    outputs = [loss, grad_x]
    for kind in range(4):
        sm = small_out[kind]
        outputs += [big[0][kind], sm[0], sm[1], sm[2], big[1][kind], sm[3], big[2][kind], big[3][kind], big[4][kind],
                    big[5][kind], big[6][kind], big[7][kind], sm[4]]
    return tuple(outputs)
```

```python
import functools
import math

import jax
import jax.numpy as jnp
from jax import lax
from jax.experimental import pallas as pl
from jax.experimental.pallas import tpu as pltpu

F32 = jnp.float32
BF16 = jnp.bfloat16
MESH = pl.DeviceIdType.MESH
NDEV = 8
HEAD_DIM = 128
LANES = 128
CONV_WIDTH = 3
RMS_EPS = 1e-6
ADAM_LR, ADAM_B1, ADAM_B2, ADAM_EPS, ADAM_WD, ADAM_STEP = 0.001, 0.9, 0.999, 1e-08, 0.01, 10
NEG = -1e30
V7X_VMEM_BYTES = 64 * 1024 * 1024
VMEM_HEADROOM = 12 * 1024 * 1024
HBM = pl.BlockSpec(memory_space=pltpu.HBM)
HIGHEST = lax.Precision.HIGHEST


def _nbytes(shape, dtype):
    return math.prod(shape) * jnp.dtype(dtype).itemsize


def _params(semantics, block_bytes, temp_bytes=0):
    limit = min(2 * block_bytes + temp_bytes + VMEM_HEADROOM, V7X_VMEM_BYTES - 4 * 1024 * 1024)
    return pltpu.CompilerParams(dimension_semantics=semantics, vmem_limit_bytes=int(limit))


def _my_index():
    return lax.axis_index("x") * 4 + lax.axis_index("y") * 2 + lax.axis_index("c")


def _peer(r):
    x, y, c = lax.axis_index("x"), lax.axis_index("y"), lax.axis_index("c")
    px = 1 - x if (r >> 2) & 1 else x
    py = 1 - y if (r >> 1) & 1 else y
    pc = 1 - c if r & 1 else c
    return (px, py, pc), px * 4 + py * 2 + pc


def _exchange(arrays, name, scatter):
    n = len(arrays)

    def body(*refs):
        ins, outs = refs[:n], refs[n:2 * n]
        send_sems, recv_sems, local_sems = refs[2 * n:]
        me = _my_index()
        local = []
        for a in range(n):
            src = ins[a].at[me] if scatter else ins[a]
            local.append(pltpu.make_async_copy(src, outs[a].at[me], local_sems.at[a]))
            local[-1].start()
        sends = []
        for r in range(1, NDEV):
            peer, pidx = _peer(r)
            for a in range(n):
                src = ins[a].at[pidx] if scatter else ins[a]
                cp = pltpu.make_async_remote_copy(
                    src_ref=src, dst_ref=outs[a].at[me],
                    send_sem=send_sems.at[a * (NDEV - 1) + r - 1], recv_sem=recv_sems.at[a * (NDEV - 1) + r - 1],
                    device_id=peer, device_id_type=MESH)
                cp.start()
                sends.append(cp)
        for r in range(1, NDEV):
            peer, pidx = _peer(r)
            for a in range(n):
                src = ins[a].at[pidx] if scatter else ins[a]
                pltpu.make_async_remote_copy(
                    src_ref=src, dst_ref=outs[a].at[pidx],
                    send_sem=send_sems.at[a * (NDEV - 1) + r - 1], recv_sem=recv_sems.at[a * (NDEV - 1) + r - 1],
                    device_id=peer, device_id_type=MESH).wait_recv()
        for cp in sends:
            cp.wait_send()
        for cp in local:
            cp.wait()

    out_shape = [jax.ShapeDtypeStruct(a.shape if scatter else (NDEV,) + a.shape, a.dtype) for a in arrays]
    return pl.pallas_call(
        body, name=name, out_shape=out_shape, in_specs=[HBM] * n, out_specs=[HBM] * n,
        scratch_shapes=[pltpu.SemaphoreType.DMA((n * (NDEV - 1),)), pltpu.SemaphoreType.DMA((n * (NDEV - 1),)),
                        pltpu.SemaphoreType.DMA((n,))],
    )(*arrays)


def _all_gather(arrays, name):
    return _exchange(arrays, name, scatter=False)


SEM = pl.BlockSpec(memory_space=pltpu.SEMAPHORE)
ANY = pl.BlockSpec(memory_space=pl.ANY)
DATAFLOW = pltpu.SideEffectType.DATAFLOW_SIDE_EFFECTING
TOKEN_SHAPE = (8, LANES)


def _exchange_start(arrays, name, scatter, after):
    n = len(arrays)
    n_sems = n * (NDEV - 1)

    def body(*refs):
        ins, lands = refs[:n], refs[n:2 * n]
        send_sems, recv_sems = refs[2 * n + 1], refs[2 * n + 2]
        token, local_sems = refs[2 * n + 3 + 2 * n], refs[2 * n + 4 + 2 * n]
        me = _my_index()
        local = []
        for a in range(n):
            src = ins[a].at[me] if scatter else ins[a]
            local.append(pltpu.make_async_copy(src, lands[a].at[me], local_sems.at[a]))
            local[-1].start()
        for r in range(1, NDEV):
            peer, pidx = _peer(r)
            for a in range(n):
                src = ins[a].at[pidx] if scatter else ins[a]
                pltpu.make_async_remote_copy(
                    src_ref=src, dst_ref=lands[a].at[me],
                    send_sem=send_sems.at[a * (NDEV - 1) + r - 1], recv_sem=recv_sems.at[a * (NDEV - 1) + r - 1],
                    device_id=peer, device_id_type=MESH).start()
        for cp in local:
            cp.wait()
        token[...] = jnp.zeros(TOKEN_SHAPE, F32)

    land_shapes = [a.shape if scatter else (NDEV,) + a.shape for a in arrays]
    srcs = [pltpu.with_memory_space_constraint(a, pltpu.HBM) for a in arrays]
    lands = [pltpu.with_memory_space_constraint(lax.empty(s, a.dtype), pltpu.HBM) for s, a in zip(land_shapes, arrays)]
    outs = pl.pallas_call(
        body, name=name,
        out_shape=(pltpu.SemaphoreType.DMA((n_sems,)), pltpu.SemaphoreType.DMA((n_sems,)),
                   *[pltpu.HBM(a.shape, a.dtype) for a in arrays], *[pltpu.HBM(s, a.dtype) for s, a in zip(land_shapes, arrays)],
                   jax.ShapeDtypeStruct(TOKEN_SHAPE, F32)),
        in_specs=[HBM] * (2 * n) + [ANY],
        out_specs=(SEM, SEM, *[HBM] * (2 * n), pl.BlockSpec(memory_space=pltpu.VMEM)),
        input_output_aliases={i: 2 + i for i in range(2 * n)},
        scratch_shapes=[pltpu.SemaphoreType.DMA((n,))],
        compiler_params=pltpu.CompilerParams(has_side_effects=DATAFLOW),
    )(*srcs, *lands, after)
    return (scatter, outs[0], outs[1], outs[2:2 + n], outs[2 + n:2 + 2 * n]), outs[-1]


def _exchange_wait(handle, name, after):
    scatter, send_sems_in, recv_sems_in, srcs, lands = handle
    n = len(srcs)

    def body(*refs):
        src_refs, land_refs = refs[:n], refs[n:2 * n]
        send_sems, recv_sems = refs[2 * n], refs[2 * n + 1]
        for r in range(1, NDEV):
            peer, pidx = _peer(r)
            for a in range(n):
                src = src_refs[a].at[pidx] if scatter else src_refs[a]
                cp = pltpu.make_async_remote_copy(
                    src_ref=src, dst_ref=land_refs[a].at[pidx],
                    send_sem=send_sems.at[a * (NDEV - 1) + r - 1], recv_sem=recv_sems.at[a * (NDEV - 1) + r - 1],
                    device_id=peer, device_id_type=MESH)
                cp.wait_send()
                cp.wait_recv()

    outs = pl.pallas_call(
        body, name=name,
        out_shape=tuple(pltpu.HBM(t.shape, t.dtype) for t in (*srcs, *lands)),
        in_specs=[HBM] * (2 * n) + [SEM, SEM, ANY], out_specs=tuple([HBM] * (2 * n)),
        input_output_aliases={i: i for i in range(2 * n)},
        compiler_params=pltpu.CompilerParams(has_side_effects=DATAFLOW),
    )(*srcs, *lands, send_sems_in, recv_sems_in, after)
    return list(outs[n:])


def _matmul(a, b, *, mode, name, out_dtypes, tm, tn, tk, epilogue=None, extras=(), a_pre=None,
            b_shards=False, out_shards=False, precision=None):
    if mode == "tn":
        K, M = a.shape
    else:
        M, K = a.shape
    if b_shards:
        shard_cols = b.shape[2]
        N = b.shape[1] if mode == "nt" else shard_cols * NDEV
        assert b.shape[0] == NDEV and (tk if mode == "nt" else tn) == shard_cols
    else:
        N = b.shape[0] if mode == "nt" else b.shape[1]
    tm, tn, tk = min(tm, M), min(tn, N), min(tk, K)
    assert M % tm == 0 and N % tn == 0 and K % tk == 0, (name, M, N, K, tm, tn, tk)
    nm, nn, nk = M // tm, N // tn, K // tk
    n_out, n_ext = len(out_dtypes), len(extras)
    contract = {"nn": ((1,), (0,)), "nt": ((1,), (1,)), "tn": ((0,), (0,))}[mode]

    def body(*refs):
        a_ref, b_ref = refs[:2]
        ext_refs = refs[2:2 + n_ext]
        out_refs = refs[2 + n_ext:2 + n_ext + n_out]
        acc_ref = refs[2 + n_ext + n_out] if nk > 1 else None
        av, bv = a_ref[...], b_ref[...]
        if a_pre is not None:
            av = a_pre(av)
        if precision is None:
            av, bv = av.astype(BF16), bv.astype(BF16)
        part = lax.dot_general(av, bv, (contract, ((), ())), preferred_element_type=F32, precision=precision)

        def finish(acc):
            vals = (acc,) if epilogue is None else epilogue(acc, *[r[...] for r in ext_refs])
            for r, v in zip(out_refs, vals):
                r[...] = v.astype(r.dtype)

        if nk == 1:
            finish(part)
        else:
            k = pl.program_id(2)

            @pl.when(k == 0)
            def _():
                acc_ref[...] = part

            @pl.when(k > 0)
            def _():
                acc_ref[...] += part

            @pl.when(k == nk - 1)
            def _():
                finish(acc_ref[...])

    if mode == "tn":
        a_spec = pl.BlockSpec((tk, tm), lambda i, j, k: (k, i))
    else:
        a_spec = pl.BlockSpec((tm, tk), lambda i, j, k: (i, k))
    if b_shards and mode == "nt":
        b_spec, b_block = pl.BlockSpec((None, tn, tk), lambda i, j, k: (k, j, 0)), (tn, tk)
    elif b_shards:
        b_spec, b_block = pl.BlockSpec((None, tk, tn), lambda i, j, k: (j, k, 0)), (tk, tn)
    elif mode == "nt":
        b_spec, b_block = pl.BlockSpec((tn, tk), lambda i, j, k: (j, k)), (tn, tk)
    else:
        b_spec, b_block = pl.BlockSpec((tk, tn), lambda i, j, k: (k, j)), (tk, tn)
    in_specs, block_bytes = [a_spec, b_spec], _nbytes((tm, tk), a.dtype) + _nbytes(b_block, b.dtype)
    for arr, kind in extras:
        if kind == "tile":
            assert arr.shape == (M, N), (name, arr.shape)
            in_specs.append(pl.BlockSpec((tm, tn), lambda i, j, k: (i, j)))
            block_bytes += _nbytes((tm, tn), arr.dtype)
        else:
            assert arr.shape == (1, N), (name, arr.shape)
            in_specs.append(pl.BlockSpec((1, tn), lambda i, j, k: (0, j)))
    if out_shards:
        assert n_out == 1 and tn * NDEV == N
        out_shape = [jax.ShapeDtypeStruct((NDEV, M, tn), out_dtypes[0])]
        out_specs = [pl.BlockSpec((None, tm, tn), lambda i, j, k: (j, i, 0))]
    else:
        out_shape = [jax.ShapeDtypeStruct((M, N), d) for d in out_dtypes]
        out_specs = [pl.BlockSpec((tm, tn), lambda i, j, k: (i, j)) for _ in out_dtypes]
    block_bytes += sum(_nbytes((tm, tn), d) for d in out_dtypes)
    scratch = [pltpu.VMEM((tm, tn), F32)] if nk > 1 else []
    outs = pl.pallas_call(
        body, name=name, grid=(nm, nn, nk), in_specs=in_specs, out_specs=out_specs, out_shape=out_shape,
        scratch_shapes=scratch,
        compiler_params=_params(("parallel", "parallel", "arbitrary"), block_bytes, 2 * tm * tn * 4),
    )(a, b, *[arr for arr, _ in extras])
    return outs[0] if n_out == 1 else outs


def _rowwise(fn, tiled, smalls, out_tiles, out_sums, *, name, ts=256):
    S = tiled[0].shape[0]
    ts = min(ts, S)
    assert S % ts == 0
    nt, ns, no, na = len(tiled), len(smalls), len(out_tiles), len(out_sums)

    def body(*refs):
        t_refs, s_refs = refs[:nt], refs[nt:nt + ns]
        o_refs, a_refs = refs[nt + ns:nt + ns + no], refs[nt + ns + no:]
        tile_vals, sum_vals = fn([r[...] for r in t_refs], [r[...] for r in s_refs])
        for r, v in zip(o_refs, tile_vals):
            r[...] = v.astype(r.dtype)

        @pl.when(pl.program_id(0) == 0)
        def _():
            for r in a_refs:
                r[...] = jnp.zeros_like(r)

        for r, v in zip(a_refs, sum_vals):
            r[...] += v

    in_specs = [pl.BlockSpec((ts, t.shape[1]), lambda i: (i, 0)) for t in tiled]
    in_specs += [pl.BlockSpec(s.shape, lambda i: (0, 0)) for s in smalls]
    out_specs = [pl.BlockSpec((ts, w), lambda i: (i, 0)) for w, _ in out_tiles]
    out_specs += [pl.BlockSpec((1, w), lambda i: (0, 0)) for w in out_sums]
    out_shape = [jax.ShapeDtypeStruct((S, w), d) for w, d in out_tiles]
    out_shape += [jax.ShapeDtypeStruct((1, w), F32) for w in out_sums]
    block_bytes = sum(_nbytes((ts, t.shape[1]), t.dtype) for t in tiled) + sum(_nbytes((ts, w), d) for w, d in out_tiles)
    width = max(t.shape[1] for t in tiled)
    outs = pl.pallas_call(
        body, name=name, grid=(S // ts,), in_specs=in_specs, out_specs=out_specs, out_shape=out_shape,
        compiler_params=_params(("arbitrary",), block_bytes, 6 * ts * width * 4),
    )(*tiled, *smalls)
    return outs[:no], outs[no:]


def _colsum(v):
    return jnp.sum(v, axis=0, keepdims=True)


def _rms_mod_fwd(x, gain, shift, scale, name):
    def fn(tiles, smalls):
        (xv,), (g, sh, sc) = tiles, smalls
        inv = lax.rsqrt(jnp.mean(xv * xv, axis=-1, keepdims=True) + RMS_EPS)
        h = (xv * inv) * g * (1.0 + sc) + sh
        return (h, inv), ()

    D = x.shape[1]
    (h, inv), _ = _rowwise(fn, [x], [gain, shift, scale], [(D, BF16), (1, F32)], [], name=name)
    return h, inv


def _gate_bwd(dx, y, gate, name):
    def fn(tiles, smalls):
        (dxv, yv), (g,) = tiles, smalls
        return (dxv * g,), (_colsum(dxv * yv),)

    D = dx.shape[1]
    (dy,), (dgate,) = _rowwise(fn, [dx, y], [gate], [(D, BF16)], [D], name=name)
    return dy, dgate


def _rms_mod_bwd(dh, x, inv, dx_res, gain, scale, name):
    def fn(tiles, smalls):
        (dhv, xv, iv, dres), (g, sc) = tiles, smalls
        dhv = dhv.astype(F32)
        xhat = xv * iv
        dr = dhv * (1.0 + sc)
        dxhat = dr * g
        dxn = iv * (dxhat - xhat * jnp.mean(dxhat * xhat, axis=-1, keepdims=True))
        return (dres + dxn,), (_colsum(dhv), _colsum(dhv * (xhat * g)), _colsum(dr * xhat))

    D = x.shape[1]
    (dx,), (dsh, dsc, dgain) = _rowwise(fn, [dh, x, inv, dx_res], [gain, scale], [(D, F32)], [D, D, D], name=name)
    return dx, dsh, dsc, dgain


def _final_loss_bwd(x, target, gain, name):
    D = x.shape[1]

    def fn(tiles, smalls):
        (xv, tv), (g,) = tiles, smalls
        inv = lax.rsqrt(jnp.mean(xv * xv, axis=-1, keepdims=True) + RMS_EPS)
        xhat = xv * inv
        err = xhat * g - tv
        loss = 0.5 * jnp.sum(jnp.mean(err * err, axis=-1, keepdims=True), axis=0, keepdims=True)
        dout = err * (1.0 / D)
        dxhat = dout * g
        dxv = inv * (dxhat - xhat * jnp.mean(dxhat * xhat, axis=-1, keepdims=True))
        return (dxv,), (_colsum(dout * xhat), jnp.broadcast_to(loss, (1, LANES)))

    (dx,), (dgain, loss) = _rowwise(fn, [x, target], [gain], [(D, F32)], [D, LANES], name=name)
    return dx, dgain, loss


SCAN_BLOCK = 256


def _triangle(n, lower):
    r = lax.broadcasted_iota(jnp.int32, (n, n), 0)
    c = lax.broadcasted_iota(jnp.int32, (n, n), 1)
    return (r >= c if lower else r <= c).astype(F32)


def _forget_cumsum(logits, bias, name):
    S = logits.shape[0]
    blk = min(SCAN_BLOCK, S)
    nb = S // blk

    def body(z_ref, b_ref, f_ref):
        z = z_ref[...] + b_ref[...]
        f_ref[...] = jnp.minimum(z, 0.0) - jnp.log(1.0 + jnp.exp(-jnp.abs(z)))
        tri = _triangle(blk, lower=True)

        def step(i, carry):
            off = pl.multiple_of(i * blk, blk)
            cs = jnp.dot(tri, f_ref[pl.ds(off, blk), :], preferred_element_type=F32, precision=HIGHEST) + carry
            f_ref[pl.ds(off, blk), :] = cs
            return cs[blk - 1:blk, :]

        lax.fori_loop(0, nb, step, jnp.zeros((1, LANES), F32))

    return pl.pallas_call(body, name=name, out_shape=jax.ShapeDtypeStruct((S, LANES), F32))(logits, bias)


def _forget_bwd(dfk, logits, bias, name):
    S = logits.shape[0]
    blk = min(SCAN_BLOCK, S)
    nb = S // blk

    def body(d_ref, z_ref, b_ref, o_ref, db_ref):
        tri = _triangle(blk, lower=False)

        def step(t, carry):
            off = pl.multiple_of((nb - 1 - t) * blk, blk)
            cs = jnp.dot(tri, d_ref[pl.ds(off, blk), :], preferred_element_type=F32, precision=HIGHEST) + carry
            o_ref[pl.ds(off, blk), :] = cs
            return cs[0:1, :]

        lax.fori_loop(0, nb, step, jnp.zeros((1, LANES), F32))
        z = z_ref[...] + b_ref[...]
        dz = -o_ref[...] / (1.0 + jnp.exp(z))
        o_ref[...] = dz
        db_ref[...] = _colsum(dz)

    return pl.pallas_call(
        body, name=name,
        out_shape=(jax.ShapeDtypeStruct((S, LANES), F32), jax.ShapeDtypeStruct((1, LANES), F32)),
    )(dfk, logits, bias)


ATTN_BLOCK = 512
_NT = (((1,), (1,)), ((), ()))


def _attn_specs(S, H, tb):
    q_blk = lambda part: pl.BlockSpec((tb, HEAD_DIM), lambda h, i: (i, part * H + h))
    q_all = lambda part: pl.BlockSpec((S, HEAD_DIM), lambda h, i: (0, part * H + h))
    col_blk = pl.BlockSpec((None, tb, 1), lambda h, i: (h, i, 0))
    row_all = pl.BlockSpec((None, 1, S), lambda h, i: (h, 0, 0))
    return q_blk, q_all, col_blk, row_all


def _attn_params(S, tb):
    return _params(("parallel", "parallel"), 4 * S * HEAD_DIM * 2, 10 * tb * tb * 4)


def _attn_fwd(qkv, f_col, f_row, name):
    S, H = qkv.shape[0], qkv.shape[1] // (3 * HEAD_DIM)
    tb = min(ATTN_BLOCK, S)
    scale = HEAD_DIM ** -0.5
    q_blk, q_all, col_blk, row_all = _attn_specs(S, H, tb)

    def body(q_ref, k_ref, v_ref, fc_ref, fr_ref, o_ref, lse_ref):
        i = pl.program_id(1)
        q, fc = q_ref[...], fc_ref[...]

        def step(j, carry, diagonal):
            m, l, acc = carry
            off = pl.multiple_of(j * tb, tb)
            k, v = k_ref[pl.ds(off, tb), :], v_ref[pl.ds(off, tb), :]
            s = lax.dot_general(q, k, _NT, preferred_element_type=F32) * scale + (fc - fr_ref[:, pl.ds(off, tb)])
            if diagonal:
                row = lax.broadcasted_iota(jnp.int32, (tb, tb), 0)
                col = lax.broadcasted_iota(jnp.int32, (tb, tb), 1)
                s = jnp.where(col <= row, s, NEG)
            m_new = jnp.maximum(m, jnp.max(s, axis=-1, keepdims=True))
            p = jnp.exp(s - m_new)
            alpha = jnp.exp(m - m_new)
            l = alpha * l + jnp.sum(p, axis=-1, keepdims=True)
            acc = alpha * acc + jnp.dot(p.astype(BF16), v, preferred_element_type=F32)
            return m_new, l, acc

        init = (jnp.full((tb, 1), NEG, F32), jnp.zeros((tb, 1), F32), jnp.zeros((tb, HEAD_DIM), F32))
        carry = lax.fori_loop(0, i, lambda j, c: step(j, c, False), init)
        m, l, acc = step(i, carry, True)
        o_ref[...] = (acc / l).astype(o_ref.dtype)
        lse_ref[...] = m + jnp.log(l)

    return pl.pallas_call(
        body, name=name, grid=(H, S // tb),
        in_specs=[q_blk(0), q_all(1), q_all(2), col_blk, row_all],
        out_specs=[pl.BlockSpec((tb, HEAD_DIM), lambda h, i: (i, h)), col_blk],
        out_shape=[jax.ShapeDtypeStruct((S, H * HEAD_DIM), BF16), jax.ShapeDtypeStruct((H, S, 1), F32)],
        compiler_params=_attn_params(S, tb),
    )(qkv, qkv, qkv, f_col, f_row)


def _attn_bwd_q(qkv, do, f_col, f_row, lse_col, name):
    S, H = qkv.shape[0], qkv.shape[1] // (3 * HEAD_DIM)
    tb = min(ATTN_BLOCK, S)
    nq = S // tb
    scale = HEAD_DIM ** -0.5
    q_blk, q_all, col_blk, row_all = _attn_specs(S, H, tb)
    head_blk = pl.BlockSpec((tb, HEAD_DIM), lambda h, i: (i, h))

    def body(q_ref, k_ref, v_ref, do_ref, fc_ref, fr_ref, lse_ref, dq_ref, delta_ref, p_buf, dp_buf):
        i = pl.program_id(1)
        q, do, fc, lse = q_ref[...], do_ref[...], fc_ref[...], lse_ref[...]

        def scores(j, delta, diagonal):
            off = pl.multiple_of(j * tb, tb)
            k, v = k_ref[pl.ds(off, tb), :], v_ref[pl.ds(off, tb), :]
            s = lax.dot_general(q, k, _NT, preferred_element_type=F32) * scale + (fc - fr_ref[:, pl.ds(off, tb)])
            if diagonal:
                row = lax.broadcasted_iota(jnp.int32, (tb, tb), 0)
                col = lax.broadcasted_iota(jnp.int32, (tb, tb), 1)
                s = jnp.where(col <= row, s, NEG)
            p = jnp.exp(s - lse)
            dp = lax.dot_general(do, v, _NT, preferred_element_type=F32)
            p_buf[j] = p
            dp_buf[j] = dp
            return delta + jnp.sum(p * dp, axis=-1, keepdims=True)

        delta = lax.fori_loop(0, i, lambda j, c: scores(j, c, False), jnp.zeros((tb, 1), F32))
        delta = scores(i, delta, True)
        delta_ref[...] = delta

        def grad(j, dq):
            off = pl.multiple_of(j * tb, tb)
            ds = p_buf[j] * (dp_buf[j] - delta)
            return dq + jnp.dot(ds.astype(BF16), k_ref[pl.ds(off, tb), :], preferred_element_type=F32)

        dq = lax.fori_loop(0, i + 1, grad, jnp.zeros((tb, HEAD_DIM), F32))
        dq_ref[...] = (dq * scale).astype(dq_ref.dtype)

    return pl.pallas_call(
        body, name=name, grid=(H, nq),
        in_specs=[q_blk(0), q_all(1), q_all(2), head_blk, col_blk, row_all, col_blk],
        out_specs=[head_blk, col_blk],
        out_shape=[jax.ShapeDtypeStruct((S, H * HEAD_DIM), BF16), jax.ShapeDtypeStruct((H, S, 1), F32)],
        scratch_shapes=[pltpu.VMEM((nq, tb, tb), F32), pltpu.VMEM((nq, tb, tb), F32)],
        compiler_params=_params(("parallel", "parallel"), 4 * S * HEAD_DIM * 2, 2 * nq * tb * tb * 4 + 10 * tb * tb * 4),
    )(qkv, qkv, qkv, do, f_col, f_row, lse_col)


def _attn_bwd_kv(qkv, do, f_col, f_row, lse_row, delta_row, name):
    S, H = qkv.shape[0], qkv.shape[1] // (3 * HEAD_DIM)
    tb = min(ATTN_BLOCK, S)
    nq = S // tb
    scale = HEAD_DIM ** -0.5
    q_blk, q_all, col_blk, row_all = _attn_specs(S, H, tb)
    head_blk = pl.BlockSpec((tb, HEAD_DIM), lambda h, i: (i, h))
    head_all = pl.BlockSpec((S, HEAD_DIM), lambda h, i: (0, h))

    def body(k_ref, v_ref, q_ref, do_ref, fc_ref, fr_ref, lse_ref, delta_ref, dk_ref, dv_ref, dfk_ref):
        j = pl.program_id(1)
        k, v, fck = k_ref[...], v_ref[...], fc_ref[...]

        def step(i, carry, diagonal):
            dk, dv, dfk = carry
            off = pl.multiple_of(i * tb, tb)
            q, do = q_ref[pl.ds(off, tb), :], do_ref[pl.ds(off, tb), :]
            st = lax.dot_general(k, q, _NT, preferred_element_type=F32) * scale + (fr_ref[:, pl.ds(off, tb)] - fck)
            if diagonal:
                row = lax.broadcasted_iota(jnp.int32, (tb, tb), 0)
                col = lax.broadcasted_iota(jnp.int32, (tb, tb), 1)
                st = jnp.where(col >= row, st, NEG)
            pt = jnp.exp(st - lse_ref[:, pl.ds(off, tb)])
            dv = dv + jnp.dot(pt.astype(BF16), do, preferred_element_type=F32)
            dpt = lax.dot_general(v, do, _NT, preferred_element_type=F32)
            dst = pt * (dpt - delta_ref[:, pl.ds(off, tb)])
            dk = dk + jnp.dot(dst.astype(BF16), q, preferred_element_type=F32)
            return dk, dv, dfk + jnp.sum(dst, axis=-1, keepdims=True)

        zeros = jnp.zeros((tb, HEAD_DIM), F32)
        carry = step(j, (zeros, zeros, jnp.zeros((tb, 1), F32)), True)
        dk, dv, dfk = lax.fori_loop(j + 1, nq, lambda i, c: step(i, c, False), carry)
        dk_ref[...] = (dk * scale).astype(dk_ref.dtype)
        dv_ref[...] = dv.astype(dv_ref.dtype)
        dfk_ref[...] = dfk

    return pl.pallas_call(
        body, name=name, grid=(H, nq),
        in_specs=[q_blk(1), q_blk(2), q_all(0), head_all, col_blk, row_all, row_all, row_all],
        out_specs=[head_blk, head_blk, col_blk],
        out_shape=[jax.ShapeDtypeStruct((S, H * HEAD_DIM), BF16), jax.ShapeDtypeStruct((S, H * HEAD_DIM), BF16),
                   jax.ShapeDtypeStruct((H, S, 1), F32)],
        compiler_params=_attn_params(S, tb),
    )(qkv, qkv, qkv, do, f_col, f_row, lse_row, delta_row)


CONV_TILE = 128


def _shift_down(v, n):
    row = lax.broadcasted_iota(jnp.int32, v.shape, 0)
    return jnp.where(row >= n, pltpu.roll(v, n, 0), 0.0)


def _shift_up(v, n):
    S = v.shape[0]
    row = lax.broadcasted_iota(jnp.int32, v.shape, 0)
    return jnp.where(row < S - n, pltpu.roll(v, S - n, 0), 0.0)


def _conv_specs(S, D, tc):
    nb = D // tc
    part = lambda p: pl.BlockSpec((S, tc), lambda j: (0, p * nb + j))
    return part, pl.BlockSpec((S, tc), lambda j: (0, j)), pl.BlockSpec((8, tc), lambda j: (0, j))


def _conv_fwd(proj, conv_w8, name):
    S, D = proj.shape[0], proj.shape[1] // 3
    tc = min(CONV_TILE, D)
    part, chan, taps = _conv_specs(S, D, tc)

    def body(b_ref, c_ref, u_ref, w_ref, z_ref):
        cu = c_ref[...].astype(F32) * u_ref[...].astype(F32)
        w = w_ref[...]
        y = w[0:1, :] * _shift_down(cu, 2) + w[1:2, :] * _shift_down(cu, 1) + w[2:3, :] * cu
        z_ref[...] = (b_ref[...].astype(F32) * y).astype(z_ref.dtype)

    return pl.pallas_call(
        body, name=name, grid=(D // tc,), in_specs=[part(0), part(1), part(2), taps], out_specs=chan,
        out_shape=jax.ShapeDtypeStruct((S, D), BF16),
        compiler_params=_params(("parallel",), 3 * _nbytes((S, tc), proj.dtype) + S * tc * 2, 6 * S * tc * 4),
    )(proj, proj, proj, conv_w8)


def _conv_bwd(proj, dz, conv_w8, name):
    S, D = proj.shape[0], proj.shape[1] // 3
    tc = min(CONV_TILE, D)
    part, chan, taps = _conv_specs(S, D, tc)

    def body(b_ref, c_ref, u_ref, dz_ref, w_ref, db_ref, dc_ref, du_ref, dw_ref):
        cv, uv = c_ref[...].astype(F32), u_ref[...].astype(F32)
        dzv, w = dz_ref[...].astype(F32), w_ref[...]
        cu = cv * uv
        cu1, cu2 = _shift_down(cu, 1), _shift_down(cu, 2)
        y = w[0:1, :] * cu2 + w[1:2, :] * cu1 + w[2:3, :] * cu
        db_ref[...] = (dzv * y).astype(db_ref.dtype)
        dy = dzv * b_ref[...].astype(F32)
        dcu = w[2:3, :] * dy + w[1:2, :] * _shift_up(dy, 1) + w[0:1, :] * _shift_up(dy, 2)
        dc_ref[...] = (dcu * uv).astype(dc_ref.dtype)
        du_ref[...] = (dcu * cv).astype(du_ref.dtype)
        dw_ref[...] = jnp.concatenate(
            [_colsum(dy * cu2), _colsum(dy * cu1), _colsum(dy * cu), jnp.zeros((8 - CONV_WIDTH, tc), F32)], axis=0)

    return pl.pallas_call(
        body, name=name, grid=(D // tc,), in_specs=[part(0), part(1), part(2), chan, taps],
        out_specs=[chan, chan, chan, taps],
        out_shape=[jax.ShapeDtypeStruct((S, D), BF16)] * 3 + [jax.ShapeDtypeStruct((8, D), F32)],
        compiler_params=_params(("parallel",), 3 * _nbytes((S, tc), proj.dtype) + _nbytes((S, tc), dz.dtype)
                                + 3 * S * tc * 2, 10 * S * tc * 4),
    )(proj, proj, proj, dz, conv_w8)


def _adamw(w, m, v, parts, name, layer=0, prev=None):
    L, R, C = w.shape
    P = parts.shape[0]
    assert parts.shape[1:] == (R, C), (name, parts.shape, w.shape)
    row_bytes = C * (12 + 16 + P * parts.dtype.itemsize)
    tr = R if R * row_bytes <= (4 << 20) else max(8, ((4 << 20) // row_bytes) // 8 * 8)
    while R % tr:
        tr -= 8
    c1, c2 = 1.0 - ADAM_B1 ** ADAM_STEP, 1.0 - ADAM_B2 ** ADAM_STEP

    def body(w_ref, m_ref, v_ref, p_ref, *rest):
        g_ref, d_ref, nm_ref, nv_ref = rest[-4:]
        g = p_ref[0].astype(F32)
        for p in range(1, P):
            g = g + p_ref[p].astype(F32)
        nm = ADAM_B1 * m_ref[...] + (1.0 - ADAM_B1) * g
        nv = ADAM_B2 * v_ref[...] + (1.0 - ADAM_B2) * (g * g)
        g_ref[...] = g
        nm_ref[...] = nm
        nv_ref[...] = nv
        d_ref[...] = -ADAM_LR * ((nm / c1) / (jnp.sqrt(nv / c2) + ADAM_EPS) + ADAM_WD * w_ref[...])

    blk = pl.BlockSpec((None, tr, C), lambda i: (layer, i, 0))
    prev = [] if prev is None else list(prev)
    return pl.pallas_call(
        body, name=name, grid=(R // tr,),
        in_specs=[blk, blk, blk, pl.BlockSpec((P, tr, C), lambda i: (0, i, 0))] + [ANY] * len(prev),
        out_specs=[blk] * 4, out_shape=[jax.ShapeDtypeStruct((L, R, C), F32)] * 4,
        input_output_aliases={4 + k: k for k in range(len(prev))},
        compiler_params=_params(("parallel",), tr * row_bytes),
    )(w, m, v, parts, *prev)


def _silu(v):
    return v / (1.0 + jnp.exp(-v))


def _pad_rows(a, rows):
    return jnp.pad(a, ((0, rows - a.shape[0]), (0, 0)))


def _pad_cols(a, cols):
    return jnp.pad(a, ((0, 0), (0, cols - a.shape[1])))


def kernel(x, c, ada_w, ada_b, norm_mix, norm_mlp, fox_w_in, fox_b_f, fox_w_out, conv_w_in, conv_w, conv_w_out, mlp_w_up, mlp_w_down, final_norm, loss_target, m_ada_w, m_ada_b, m_norm_mix, m_norm_mlp, m_fox_w_in, m_fox_b_f, m_fox_w_out, m_conv_w_in, m_conv_w, m_conv_w_out, m_mlp_w_up, m_mlp_w_down, m_final_norm, v_ada_w, v_ada_b, v_norm_mix, v_norm_mlp, v_fox_w_in, v_fox_b_f, v_fox_w_out, v_conv_w_in, v_conv_w, v_conv_w_out, v_mlp_w_up, v_mlp_w_down, v_final_norm):
    S, D = x.shape[1], x.shape[2]
    H = D // HEAD_DIM
    FF = mlp_w_up.shape[2] * NDEV
    depth = ada_w.shape[0]
    n_mod = 6
    assert depth == 2 and fox_w_in.shape[0] == 1 and conv_w_in.shape[0] == 1 and H <= LANES
    me = _my_index()
    x0, target = x[0], loss_target[0]
    row = lambda vec: vec.reshape(1, -1)

    def tied(vec, token):
        return vec + token[0, 0]

    bf = lambda w: w.astype(BF16)
    fox_handle, token = _exchange_start([bf(fox_w_in[0]), bf(fox_w_out[0])], "gather_fox_start", False, c)
    mlp_handles = [None, None]
    mlp_handles[0], token = _exchange_start([bf(mlp_w_up[0]), bf(mlp_w_down[0])], "gather_mlp0_start", False, token)
    conv_handle, token = _exchange_start([bf(conv_w_in[0]), conv_w[0], bf(conv_w_out[0])], "gather_conv_start", False, token)
    mlp_handles[1], token = _exchange_start([bf(mlp_w_up[1]), bf(mlp_w_down[1])], "gather_mlp1_start", False, token)

    c_all = _all_gather([tied(c, token)], "gather_cond")[0].reshape(NDEV, D)
    ncol = ada_w.shape[2]
    ada_b_mine = lax.dynamic_slice_in_dim(ada_b, me * ncol, ncol, axis=1)
    mod_cols = jnp.stack([
        _matmul(c_all, ada_w[i], mode="nn", name=f"ada_fwd_{i}", out_dtypes=[F32], tm=NDEV, tn=ncol // 2, tk=D,
                a_pre=_silu, precision=HIGHEST, epilogue=lambda acc, b: (acc + b,), extras=[(ada_b_mine[i:i + 1], "row")])
        for i in range(depth)])
    mod_all = _all_gather([mod_cols], "gather_mod")[0]
    mod = lax.dynamic_index_in_dim(mod_all, me, axis=2, keepdims=False)
    mod = mod.transpose(1, 0, 2).reshape(depth, n_mod, 1, D)
    sh_mix, sc_mix, g_mix, sh_mlp, sc_mlp, g_mlp = (mod[:, k] for k in range(n_mod))

    b_f = _pad_cols(fox_b_f, LANES)

    def residual(acc, x_in, gate):
        return (x_in + gate * acc, acc)

    def mlp_fwd(i, x_in):
        h, inv = _rms_mod_fwd(x_in, row(norm_mlp[i]), sh_mlp[i], sc_mlp[i], f"mlp_norm_{i}")
        w_up, w_down = _exchange_wait(mlp_handles[i], f"gather_mlp{i}_wait", h)
        w_down = w_down.reshape(FF, D)
        r, a = _matmul(h, w_up, mode="nn", name=f"mlp_up_{i}", out_dtypes=[BF16, BF16], tm=1024, tn=FF // NDEV, tk=D,
                       b_shards=True, epilogue=lambda acc: (jnp.maximum(acc, 0.0), jnp.square(jnp.maximum(acc, 0.0))))
        x_out, y = _matmul(a, w_down, mode="nn", name=f"mlp_down_{i}", out_dtypes=[F32, F32], tm=512, tn=1024, tk=1024,
                           epilogue=residual, extras=[(x_in, "tile"), (g_mlp[i], "row")])
        return x_out, (x_in, h, inv, r, a, y, w_up, w_down)

    def mlp_bwd(i, dx, saved, gate):
        x_in, h, inv, r, a, y, w_up, w_down = saved
        dy, dgate = _gate_bwd(dx, y, gate, f"mlp_gate_bwd_{i}")
        du = _matmul(dy, w_down, mode="nt", name=f"mlp_down_bwd_{i}", out_dtypes=[BF16], tm=1024, tn=1024, tk=D,
                     epilogue=lambda acc, rv: (acc * (2.0 * rv.astype(F32)),), extras=[(r, "tile")])
        d_down = _matmul(a, dy, mode="tn", name=f"mlp_down_wgrad_{i}", out_dtypes=[BF16], tm=1024, tn=1024, tk=1024)
        dh = _matmul(du, w_up, mode="nt", name=f"mlp_up_bwd_{i}", out_dtypes=[F32], tm=1024, tn=1024, tk=FF // NDEV,
                     b_shards=True)
        d_up = _matmul(h, du, mode="tn", name=f"mlp_up_wgrad_{i}", out_dtypes=[BF16], tm=1024, tn=FF // NDEV, tk=1024,
                       out_shards=True)
        dx, dsh, dsc, dgain = _rms_mod_bwd(dh, x_in, inv, dx, row(norm_mlp[i]), sc_mlp[i], f"mlp_norm_bwd_{i}")
        handle, token = _exchange_start([d_up, d_down.reshape(NDEV, FF // NDEV, D)], f"scatter_mlp{i}_start", True, dx)
        return dx, (dsh, dsc, dgate, dgain), handle, token

    h0, inv0 = _rms_mod_fwd(x0, row(norm_mix[0]), sh_mix[0], sc_mix[0], "fox_norm")
    fox_in, w_fox_out = _exchange_wait(fox_handle, "gather_fox_wait", h0)
    fox_in = fox_in.transpose(1, 0, 2).reshape(D, 3 * D + H)
    w_qkv, w_f = fox_in[:, :3 * D], _pad_cols(fox_in[:, 3 * D:], LANES)
    w_fox_out = w_fox_out.reshape(D, D)
    qkv = _matmul(h0, w_qkv, mode="nn", name="fox_qkv", out_dtypes=[BF16], tm=1024, tn=1024, tk=D)
    f_logit = _matmul(h0, w_f, mode="nn", name="fox_forget_logits", out_dtypes=[F32], tm=1024, tn=LANES, tk=D)
    f_cum = _forget_cumsum(f_logit, b_f, "fox_forget_cumsum")
    f_heads = f_cum[:, :H].T
    f_col, f_row = f_heads.reshape(H, S, 1), f_heads.reshape(H, 1, S)
    o, lse = _attn_fwd(qkv, f_col, f_row, "fox_attention")
    x1, mix0 = _matmul(o, w_fox_out, mode="nn", name="fox_out", out_dtypes=[F32, F32], tm=512, tn=1024, tk=D,
                       epilogue=residual, extras=[(x0, "tile"), (g_mix[0], "row")])
    x2, mlp0 = mlp_fwd(0, x1)

    h1, inv1 = _rms_mod_fwd(x2, row(norm_mix[1]), sh_mix[1], sc_mix[1], "conv_norm")
    w_conv_in, w_taps, w_conv_out = _exchange_wait(conv_handle, "gather_conv_wait", h1)
    w_taps = _pad_rows(w_taps.transpose(1, 0, 2).reshape(CONV_WIDTH, D), 8)
    w_conv_out = w_conv_out.reshape(D, D)
    proj = _matmul(h1, w_conv_in, mode="nn", name="conv_in", out_dtypes=[F32], tm=1024, tn=3 * D // NDEV, tk=D,
                   b_shards=True)
    z = _conv_fwd(proj, w_taps, "conv_mix")
    x3, mix1 = _matmul(z, w_conv_out, mode="nn", name="conv_out", out_dtypes=[F32, F32], tm=512, tn=1024, tk=D,
                       epilogue=residual, extras=[(x2, "tile"), (g_mix[1], "row")])
    x4, mlp1 = mlp_fwd(1, x3)

    dx, d_final, loss_lanes = _final_loss_bwd(x4, target, row(final_norm), "loss_head")
    loss = lax.psum(loss_lanes[0, 0], ("x", "y", "c"))

    dx, dmod_mlp1, mlp1_scatter, token = mlp_bwd(1, dx, mlp1, g_mlp[1])
    dmix, dg_mix1 = _gate_bwd(dx, mix1, tied(g_mix[1], token), "conv_gate_bwd")
    dz = _matmul(dmix, w_conv_out, mode="nt", name="conv_out_bwd", out_dtypes=[F32], tm=1024, tn=1024, tk=D)
    d_conv_out = _matmul(z, dmix, mode="tn", name="conv_out_wgrad", out_dtypes=[BF16], tm=1024, tn=1024, tk=1024)
    db, dc, du, d_taps = _conv_bwd(proj, dz, w_taps, "conv_mix_bwd")
    dproj = jnp.concatenate([db, dc, du], axis=1)
    dh1 = _matmul(dproj, w_conv_in, mode="nt", name="conv_in_bwd", out_dtypes=[F32], tm=1024, tn=1024, tk=3 * D // NDEV,
                  b_shards=True)
    d_conv_in = _matmul(h1, dproj, mode="tn", name="conv_in_wgrad", out_dtypes=[BF16], tm=1024, tn=3 * D // NDEV, tk=1024,
                        out_shards=True)
    dx, dsh1, dsc1, dgain_mix1 = _rms_mod_bwd(dh1, x2, inv1, dx, row(norm_mix[1]), sc_mix[1], "conv_norm_bwd")
    d_taps_split = d_taps[:CONV_WIDTH].reshape(CONV_WIDTH, NDEV, -1).transpose(1, 0, 2)
    conv_scatter, token = _exchange_start([d_conv_in, d_taps_split, d_conv_out.reshape(NDEV, D // NDEV, D)],
                                          "scatter_conv_start", True, dx)

    dx, dmod_mlp0, mlp0_scatter, token = mlp_bwd(0, dx, mlp0, tied(g_mlp[0], token))
    dmix, dg_mix0 = _gate_bwd(dx, mix0, tied(g_mix[0], token), "fox_gate_bwd")
    do = _matmul(dmix, w_fox_out, mode="nt", name="fox_out_bwd", out_dtypes=[BF16], tm=1024, tn=1024, tk=D)
    d_fox_out = _matmul(o, dmix, mode="tn", name="fox_out_wgrad", out_dtypes=[BF16], tm=1024, tn=1024, tk=1024)
    dq, delta = _attn_bwd_q(qkv, do, f_col, f_row, lse, "fox_attention_bwd_q")
    dk, dv, dfk = _attn_bwd_kv(qkv, do, f_col, f_row, lse.reshape(H, 1, S), delta.reshape(H, 1, S), "fox_attention_bwd_kv")
    dqkv = jnp.concatenate([dq, dk, dv], axis=1)
    dfk_lanes = _pad_cols(dfk.reshape(H, S).T, LANES)
    df_logit, db_f = _forget_bwd(dfk_lanes, f_logit, b_f, "fox_forget_bwd")
    dh0_f = _matmul(df_logit, w_f, mode="nt", name="fox_forget_logits_bwd", out_dtypes=[F32], tm=1024, tn=1024, tk=LANES)
    dh0 = _matmul(dqkv, w_qkv, mode="nt", name="fox_qkv_bwd", out_dtypes=[F32], tm=1024, tn=1024, tk=1024,
                  epilogue=lambda acc, extra: (acc + extra,), extras=[(dh0_f, "tile")])
    d_qkv = _matmul(h0, dqkv, mode="tn", name="fox_qkv_wgrad", out_dtypes=[BF16], tm=1024, tn=1024, tk=1024)
    d_f = _matmul(h0, df_logit, mode="tn", name="fox_forget_wgrad", out_dtypes=[BF16], tm=1024, tn=LANES, tk=1024)
    dx, dsh0, dsc0, dgain_mix0 = _rms_mod_bwd(dh0, x0, inv0, dx, row(norm_mix[0]), sc_mix[0], "fox_norm_bwd")
    grad_x = dx.reshape(1, S, D)
    d_fox_in = jnp.concatenate([d_qkv, d_f[:, :H]], axis=1).reshape(D, NDEV, -1).transpose(1, 0, 2)
    fox_scatter, token = _exchange_start([d_fox_in, d_fox_out.reshape(NDEV, D // NDEV, D)], "scatter_fox_start", True, dx)

    dmod = jnp.concatenate([
        jnp.concatenate([dsh0, dsc0, dg_mix0, dmod_mlp0[0], dmod_mlp0[1], dmod_mlp0[2]], axis=1),
        jnp.concatenate([dsh1, dsc1, dg_mix1, dmod_mlp1[0], dmod_mlp1[1], dmod_mlp1[2]], axis=1)], axis=0)
    small_sizes = [depth * n_mod * D, depth * D, depth * D, H, D]
    n_small = sum(small_sizes)
    n_rows = -(-n_small // (8 * LANES)) * 8

    def pack(parts):
        flat = jnp.concatenate([p.reshape(-1) for p in parts])
        return jnp.pad(flat, (0, n_rows * LANES - n_small)).reshape(n_rows, LANES)

    def unpack(packed, shapes):
        flat, out, at = packed.reshape(-1), [], 0
        for size, shape in zip(small_sizes, shapes):
            out.append(flat[at:at + size].reshape(shape))
            at += size
        return out

    small_partial = pack([dmod, jnp.concatenate([dgain_mix0, dgain_mix1], axis=0),
                          jnp.concatenate([dmod_mlp0[3], dmod_mlp1[3]], axis=0), db_f[0, :H], d_final])
    small_parts = _all_gather([tied(small_partial, token)], "gather_small_grads")[0]
    small_shapes = [ada_b.shape, norm_mix.shape, norm_mlp.shape, fox_b_f.shape, final_norm.shape]
    small_out = _adamw(pack([ada_b, norm_mix, norm_mlp, fox_b_f, final_norm])[None],
                       pack([m_ada_b, m_norm_mix, m_norm_mlp, m_fox_b_f, m_final_norm])[None],
                       pack([v_ada_b, v_norm_mix, v_norm_mlp, v_fox_b_f, v_final_norm])[None], small_parts, "adamw_small")
    small_out = [unpack(t, small_shapes) for t in small_out]

    dmod_all = small_parts.reshape(NDEV, -1)[:, :depth * n_mod * D].reshape(NDEV, depth, n_mod * D)
    dmod_mine = lax.dynamic_slice_in_dim(dmod_all, me * ncol, ncol, axis=2)
    ada_out = None
    for i in range(depth):
        d_ada = _matmul(c_all, dmod_mine[:, i], mode="tn", name=f"ada_wgrad_{i}", out_dtypes=[F32], tm=1024, tn=ncol // 2,
                        tk=NDEV, a_pre=_silu, precision=HIGHEST)
        ada_out = _adamw(ada_w, m_ada_w, v_ada_w, d_ada[None], f"adamw_ada_w_{i}", layer=i, prev=ada_out)

    up1, down1 = _exchange_wait(mlp1_scatter, "scatter_mlp1_wait", ada_out[0])
    up_out = _adamw(mlp_w_up, m_mlp_w_up, v_mlp_w_up, up1, "adamw_mlp_w_up_1", layer=1)
    down_out = _adamw(mlp_w_down, m_mlp_w_down, v_mlp_w_down, down1, "adamw_mlp_w_down_1", layer=1)
    cin, taps, cout = _exchange_wait(conv_scatter, "scatter_conv_wait", down_out[0])
    conv_in_out = _adamw(conv_w_in, m_conv_w_in, v_conv_w_in, cin, "adamw_conv_w_in")
    conv_w_res = _adamw(conv_w, m_conv_w, v_conv_w, taps, "adamw_conv_w")
    conv_out_out = _adamw(conv_w_out, m_conv_w_out, v_conv_w_out, cout, "adamw_conv_w_out")
    up0, down0 = _exchange_wait(mlp0_scatter, "scatter_mlp0_wait", conv_out_out[0])
    up_out = _adamw(mlp_w_up, m_mlp_w_up, v_mlp_w_up, up0, "adamw_mlp_w_up_0", layer=0, prev=up_out)
    down_out = _adamw(mlp_w_down, m_mlp_w_down, v_mlp_w_down, down0, "adamw_mlp_w_down_0", layer=0, prev=down_out)
    fin, fout = _exchange_wait(fox_scatter, "scatter_fox_wait", down_out[0])
    fox_in_out = _adamw(fox_w_in, m_fox_w_in, v_fox_w_in, fin, "adamw_fox_w_in")
    fox_out_out = _adamw(fox_w_out, m_fox_w_out, v_fox_w_out, fout, "adamw_fox_w_out")

    outputs = [loss, grad_x]
    for kind in range(4):
        sm = small_out[kind]
        outputs += [ada_out[kind], sm[0], sm[1], sm[2], fox_in_out[kind], sm[3], fox_out_out[kind], conv_in_out[kind],
                    conv_w_res[kind], conv_out_out[kind], up_out[kind], down_out[kind], sm[4]]
    return tuple(outputs)
```

```python
import functools
import math

import jax
import jax.numpy as jnp
from jax import lax
from jax.experimental import pallas as pl
from jax.experimental.pallas import tpu as pltpu

F32 = jnp.float32
BF16 = jnp.bfloat16
MESH = pl.DeviceIdType.MESH
NDEV = 8
HEAD_DIM = 128
LANES = 128
CONV_WIDTH = 3
RMS_EPS = 1e-6
ADAM_LR, ADAM_B1, ADAM_B2, ADAM_EPS, ADAM_WD, ADAM_STEP = 0.001, 0.9, 0.999, 1e-08, 0.01, 10
NEG = -1e30
V7X_VMEM_BYTES = 64 * 1024 * 1024
VMEM_HEADROOM = 12 * 1024 * 1024
HBM = pl.BlockSpec(memory_space=pltpu.HBM)
HIGHEST = lax.Precision.HIGHEST


def _nbytes(shape, dtype):
    return math.prod(shape) * jnp.dtype(dtype).itemsize


def _params(semantics, block_bytes, temp_bytes=0):
    limit = min(2 * block_bytes + temp_bytes + VMEM_HEADROOM, V7X_VMEM_BYTES - 4 * 1024 * 1024)
    return pltpu.CompilerParams(dimension_semantics=semantics, vmem_limit_bytes=int(limit))


def _my_index():
    return lax.axis_index("x") * 4 + lax.axis_index("y") * 2 + lax.axis_index("c")


def _peer(r):
    x, y, c = lax.axis_index("x"), lax.axis_index("y"), lax.axis_index("c")
    px = 1 - x if (r >> 2) & 1 else x
    py = 1 - y if (r >> 1) & 1 else y
    pc = 1 - c if r & 1 else c
    return (px, py, pc), px * 4 + py * 2 + pc


def _exchange(arrays, name, scatter, after=None):
    n = len(arrays)
    after = [] if after is None else list(after)

    def body(*refs):
        ins, outs = refs[:n], refs[n + len(after):2 * n + len(after)]
        send_sems, recv_sems, local_sems = refs[2 * n + len(after):]
        me = _my_index()
        local = []
        for a in range(n):
            src = ins[a].at[me] if scatter else ins[a]
            local.append(pltpu.make_async_copy(src, outs[a].at[me], local_sems.at[a]))
            local[-1].start()
        sends = []
        for r in range(1, NDEV):
            peer, pidx = _peer(r)
            for a in range(n):
                src = ins[a].at[pidx] if scatter else ins[a]
                cp = pltpu.make_async_remote_copy(
                    src_ref=src, dst_ref=outs[a].at[me],
                    send_sem=send_sems.at[a * (NDEV - 1) + r - 1], recv_sem=recv_sems.at[a * (NDEV - 1) + r - 1],
                    device_id=peer, device_id_type=MESH)
                cp.start()
                sends.append(cp)
        for r in range(1, NDEV):
            peer, pidx = _peer(r)
            for a in range(n):
                src = ins[a].at[pidx] if scatter else ins[a]
                pltpu.make_async_remote_copy(
                    src_ref=src, dst_ref=outs[a].at[pidx],
                    send_sem=send_sems.at[a * (NDEV - 1) + r - 1], recv_sem=recv_sems.at[a * (NDEV - 1) + r - 1],
                    device_id=peer, device_id_type=MESH).wait_recv()
        for cp in sends:
            cp.wait_send()
        for cp in local:
            cp.wait()

    out_shape = [jax.ShapeDtypeStruct(a.shape if scatter else (NDEV,) + a.shape, a.dtype) for a in arrays]
    return pl.pallas_call(
        body, name=name, out_shape=out_shape, in_specs=[HBM] * n + [ANY] * len(after), out_specs=[HBM] * n,
        scratch_shapes=[pltpu.SemaphoreType.DMA((n * (NDEV - 1),)), pltpu.SemaphoreType.DMA((n * (NDEV - 1),)),
                        pltpu.SemaphoreType.DMA((n,))],
    )(*arrays, *after)


def _all_gather(arrays, name, after=None):
    return _exchange(arrays, name, scatter=False, after=after)


SEM = pl.BlockSpec(memory_space=pltpu.SEMAPHORE)
ANY = pl.BlockSpec(memory_space=pl.ANY)
DATAFLOW = pltpu.SideEffectType.DATAFLOW_SIDE_EFFECTING
TOKEN_SHAPE = (8, LANES)


def _exchange_start(arrays, name, scatter, after):
    n = len(arrays)
    n_sems = n * (NDEV - 1)

    def body(*refs):
        ins, lands = refs[:n], refs[n:2 * n]
        send_sems, recv_sems = refs[2 * n + 1], refs[2 * n + 2]
        token = refs[2 * n + 3 + 2 * n]
        me = _my_index()
        for r in range(1, NDEV):
            peer, pidx = _peer(r)
            for a in range(n):
                src = ins[a].at[pidx] if scatter else ins[a]
                pltpu.make_async_remote_copy(
                    src_ref=src, dst_ref=lands[a].at[me],
                    send_sem=send_sems.at[a * (NDEV - 1) + r - 1], recv_sem=recv_sems.at[a * (NDEV - 1) + r - 1],
                    device_id=peer, device_id_type=MESH).start()
        token[...] = jnp.zeros(TOKEN_SHAPE, F32)

    land_shapes = [a.shape if scatter else (NDEV,) + a.shape for a in arrays]
    srcs = [pltpu.with_memory_space_constraint(a, pltpu.HBM) for a in arrays]
    lands = [pltpu.with_memory_space_constraint(lax.empty(s, a.dtype), pltpu.HBM) for s, a in zip(land_shapes, arrays)]
    outs = pl.pallas_call(
        body, name=name,
        out_shape=(pltpu.SemaphoreType.DMA((n_sems,)), pltpu.SemaphoreType.DMA((n_sems,)),
                   *[pltpu.HBM(a.shape, a.dtype) for a in arrays], *[pltpu.HBM(s, a.dtype) for s, a in zip(land_shapes, arrays)],
                   jax.ShapeDtypeStruct(TOKEN_SHAPE, F32)),
        in_specs=[HBM] * (2 * n) + [ANY],
        out_specs=(SEM, SEM, *[HBM] * (2 * n), pl.BlockSpec(memory_space=pltpu.VMEM)),
        input_output_aliases={i: 2 + i for i in range(2 * n)},
        compiler_params=pltpu.CompilerParams(has_side_effects=DATAFLOW),
    )(*srcs, *lands, after)
    return (scatter, outs[0], outs[1], outs[2:2 + n], outs[2 + n:2 + 2 * n]), outs[-1]


def _exchange_wait(handle, name, after):
    scatter, send_sems_in, recv_sems_in, srcs, lands = handle
    n = len(srcs)

    def body(*refs):
        src_refs, land_refs = refs[:n], refs[n:2 * n]
        send_sems, recv_sems = refs[2 * n], refs[2 * n + 1]
        for r in range(1, NDEV):
            peer, pidx = _peer(r)
            for a in range(n):
                src = src_refs[a].at[pidx] if scatter else src_refs[a]
                cp = pltpu.make_async_remote_copy(
                    src_ref=src, dst_ref=land_refs[a].at[pidx],
                    send_sem=send_sems.at[a * (NDEV - 1) + r - 1], recv_sem=recv_sems.at[a * (NDEV - 1) + r - 1],
                    device_id=peer, device_id_type=MESH)
                cp.wait_send()
                cp.wait_recv()

    outs = pl.pallas_call(
        body, name=name,
        out_shape=tuple(pltpu.HBM(t.shape, t.dtype) for t in (*srcs, *lands)),
        in_specs=[HBM] * (2 * n) + [SEM, SEM, ANY], out_specs=tuple([HBM] * (2 * n)),
        input_output_aliases={i: i for i in range(2 * n)},
        compiler_params=pltpu.CompilerParams(has_side_effects=DATAFLOW),
    )(*srcs, *lands, send_sems_in, recv_sems_in, after)
    me = _my_index()
    mine = [lax.dynamic_index_in_dim(s, me, 0, keepdims=False) if scatter else s for s in outs[:n]]
    return [lax.dynamic_update_index_in_dim(land, own, me, 0) for land, own in zip(outs[n:], mine)]


def _matmul(a, b, *, mode, name, out_dtypes, tm, tn, tk, epilogue=None, extras=(), a_pre=None,
            b_shards=False, out_shards=False, precision=None, after=()):
    after = list(after)
    n_after = len(after)
    if mode == "tn":
        K, M = a.shape
    else:
        M, K = a.shape
    if b_shards:
        shard_cols = b.shape[2]
        N = b.shape[1] if mode == "nt" else shard_cols * NDEV
        assert b.shape[0] == NDEV and (tk if mode == "nt" else tn) == shard_cols
    else:
        N = b.shape[0] if mode == "nt" else b.shape[1]
    tm, tn, tk = min(tm, M), min(tn, N), min(tk, K)
    assert M % tm == 0 and N % tn == 0 and K % tk == 0, (name, M, N, K, tm, tn, tk)
    nm, nn, nk = M // tm, N // tn, K // tk
    n_out, n_ext = len(out_dtypes), len(extras)
    contract = {"nn": ((1,), (0,)), "nt": ((1,), (1,)), "tn": ((0,), (0,))}[mode]

    def body(*refs):
        a_ref, b_ref = refs[:2]
        ext_refs = refs[2:2 + n_ext]
        out_refs = refs[2 + n_ext + n_after:2 + n_ext + n_after + n_out]
        acc_ref = refs[2 + n_ext + n_after + n_out] if nk > 1 else None
        av, bv = a_ref[...], b_ref[...]
        if a_pre is not None:
            av = a_pre(av)
        if precision is None:
            av, bv = av.astype(BF16), bv.astype(BF16)
        part = lax.dot_general(av, bv, (contract, ((), ())), preferred_element_type=F32, precision=precision)

        def finish(acc):
            vals = (acc,) if epilogue is None else epilogue(acc, *[r[...] for r in ext_refs])
            for r, v in zip(out_refs, vals):
                r[...] = v.astype(r.dtype)

        if nk == 1:
            finish(part)
        else:
            k = pl.program_id(2)

            @pl.when(k == 0)
            def _():
                acc_ref[...] = part

            @pl.when(k > 0)
            def _():
                acc_ref[...] += part

            @pl.when(k == nk - 1)
            def _():
                finish(acc_ref[...])

    if mode == "tn":
        a_spec = pl.BlockSpec((tk, tm), lambda i, j, k: (k, i))
    else:
        a_spec = pl.BlockSpec((tm, tk), lambda i, j, k: (i, k))
    if b_shards and mode == "nt":
        b_spec, b_block = pl.BlockSpec((None, tn, tk), lambda i, j, k: (k, j, 0)), (tn, tk)
    elif b_shards:
        b_spec, b_block = pl.BlockSpec((None, tk, tn), lambda i, j, k: (j, k, 0)), (tk, tn)
    elif mode == "nt":
        b_spec, b_block = pl.BlockSpec((tn, tk), lambda i, j, k: (j, k)), (tn, tk)
    else:
        b_spec, b_block = pl.BlockSpec((tk, tn), lambda i, j, k: (k, j)), (tk, tn)
    in_specs, block_bytes = [a_spec, b_spec], _nbytes((tm, tk), a.dtype) + _nbytes(b_block, b.dtype)
    for arr, kind in extras:
        if kind == "tile":
            assert arr.shape == (M, N), (name, arr.shape)
            in_specs.append(pl.BlockSpec((tm, tn), lambda i, j, k: (i, j)))
            block_bytes += _nbytes((tm, tn), arr.dtype)
        else:
            assert arr.shape == (1, N), (name, arr.shape)
            in_specs.append(pl.BlockSpec((1, tn), lambda i, j, k: (0, j)))
    in_specs += [ANY] * n_after
    if out_shards:
        assert n_out == 1 and tn * NDEV == N
        out_shape = [jax.ShapeDtypeStruct((NDEV, M, tn), out_dtypes[0])]
        out_specs = [pl.BlockSpec((None, tm, tn), lambda i, j, k: (j, i, 0))]
    else:
        out_shape = [jax.ShapeDtypeStruct((M, N), d) for d in out_dtypes]
        out_specs = [pl.BlockSpec((tm, tn), lambda i, j, k: (i, j)) for _ in out_dtypes]
    block_bytes += sum(_nbytes((tm, tn), d) for d in out_dtypes)
    scratch = [pltpu.VMEM((tm, tn), F32)] if nk > 1 else []
    outs = pl.pallas_call(
        body, name=name, grid=(nm, nn, nk), in_specs=in_specs, out_specs=out_specs, out_shape=out_shape,
        scratch_shapes=scratch,
        compiler_params=_params(("parallel", "parallel", "arbitrary"), block_bytes, 2 * tm * tn * 4),
    )(a, b, *[arr for arr, _ in extras], *after)
    return outs[0] if n_out == 1 else outs


def _rowwise(fn, tiled, smalls, out_tiles, out_sums, *, name, ts=256):
    S = tiled[0].shape[0]
    ts = min(ts, S)
    assert S % ts == 0
    nt, ns, no, na = len(tiled), len(smalls), len(out_tiles), len(out_sums)

    def body(*refs):
        t_refs, s_refs = refs[:nt], refs[nt:nt + ns]
        o_refs, a_refs = refs[nt + ns:nt + ns + no], refs[nt + ns + no:]
        tile_vals, sum_vals = fn([r[...] for r in t_refs], [r[...] for r in s_refs])
        for r, v in zip(o_refs, tile_vals):
            r[...] = v.astype(r.dtype)

        @pl.when(pl.program_id(0) == 0)
        def _():
            for r in a_refs:
                r[...] = jnp.zeros_like(r)

        for r, v in zip(a_refs, sum_vals):
            r[...] += v

    in_specs = [pl.BlockSpec((ts, t.shape[1]), lambda i: (i, 0)) for t in tiled]
    in_specs += [pl.BlockSpec(s.shape, lambda i: (0, 0)) for s in smalls]
    out_specs = [pl.BlockSpec((ts, w), lambda i: (i, 0)) for w, _ in out_tiles]
    out_specs += [pl.BlockSpec((1, w), lambda i: (0, 0)) for w in out_sums]
    out_shape = [jax.ShapeDtypeStruct((S, w), d) for w, d in out_tiles]
    out_shape += [jax.ShapeDtypeStruct((1, w), F32) for w in out_sums]
    block_bytes = sum(_nbytes((ts, t.shape[1]), t.dtype) for t in tiled) + sum(_nbytes((ts, w), d) for w, d in out_tiles)
    width = max(t.shape[1] for t in tiled)
    outs = pl.pallas_call(
        body, name=name, grid=(S // ts,), in_specs=in_specs, out_specs=out_specs, out_shape=out_shape,
        compiler_params=_params(("arbitrary",), block_bytes, 6 * ts * width * 4),
    )(*tiled, *smalls)
    return outs[:no], outs[no:]


def _colsum(v):
    return jnp.sum(v, axis=0, keepdims=True)


def _rms_mod_fwd(x, gain, shift, scale, name):
    def fn(tiles, smalls):
        (xv,), (g, sh, sc) = tiles, smalls
        inv = lax.rsqrt(jnp.mean(xv * xv, axis=-1, keepdims=True) + RMS_EPS)
        h = (xv * inv) * g * (1.0 + sc) + sh
        return (h, inv), ()

    D = x.shape[1]
    (h, inv), _ = _rowwise(fn, [x], [gain, shift, scale], [(D, BF16), (1, F32)], [], name=name)
    return h, inv


def _gate_bwd(dx, y, gate, name):
    def fn(tiles, smalls):
        (dxv, yv), (g,) = tiles, smalls
        return (dxv * g,), (_colsum(dxv * yv),)

    D = dx.shape[1]
    (dy,), (dgate,) = _rowwise(fn, [dx, y], [gate], [(D, BF16)], [D], name=name)
    return dy, dgate


def _rms_mod_bwd(dh, x, inv, dx_res, gain, scale, name):
    def fn(tiles, smalls):
        (dhv, xv, iv, dres), (g, sc) = tiles, smalls
        dhv = dhv.astype(F32)
        xhat = xv * iv
        dr = dhv * (1.0 + sc)
        dxhat = dr * g
        dxn = iv * (dxhat - xhat * jnp.mean(dxhat * xhat, axis=-1, keepdims=True))
        return (dres + dxn,), (_colsum(dhv), _colsum(dhv * (xhat * g)), _colsum(dr * xhat))

    D = x.shape[1]
    (dx,), (dsh, dsc, dgain) = _rowwise(fn, [dh, x, inv, dx_res], [gain, scale], [(D, F32)], [D, D, D], name=name)
    return dx, dsh, dsc, dgain


def _final_loss_bwd(x, target, gain, name):
    D = x.shape[1]

    def fn(tiles, smalls):
        (xv, tv), (g,) = tiles, smalls
        inv = lax.rsqrt(jnp.mean(xv * xv, axis=-1, keepdims=True) + RMS_EPS)
        xhat = xv * inv
        err = xhat * g - tv
        loss = 0.5 * jnp.sum(jnp.mean(err * err, axis=-1, keepdims=True), axis=0, keepdims=True)
        dout = err * (1.0 / D)
        dxhat = dout * g
        dxv = inv * (dxhat - xhat * jnp.mean(dxhat * xhat, axis=-1, keepdims=True))
        return (dxv,), (_colsum(dout * xhat), jnp.broadcast_to(loss, (1, LANES)))

    (dx,), (dgain, loss) = _rowwise(fn, [x, target], [gain], [(D, F32)], [D, LANES], name=name)
    return dx, dgain, loss


SCAN_BLOCK = 256


def _triangle(n, lower):
    r = lax.broadcasted_iota(jnp.int32, (n, n), 0)
    c = lax.broadcasted_iota(jnp.int32, (n, n), 1)
    return (r >= c if lower else r <= c).astype(F32)


def _forget_cumsum(logits, bias, name):
    S = logits.shape[0]
    blk = min(SCAN_BLOCK, S)
    nb = S // blk

    def body(z_ref, b_ref, f_ref):
        z = z_ref[...] + b_ref[...]
        f_ref[...] = jnp.minimum(z, 0.0) - jnp.log(1.0 + jnp.exp(-jnp.abs(z)))
        tri = _triangle(blk, lower=True)

        def step(i, carry):
            off = pl.multiple_of(i * blk, blk)
            cs = jnp.dot(tri, f_ref[pl.ds(off, blk), :], preferred_element_type=F32, precision=HIGHEST) + carry
            f_ref[pl.ds(off, blk), :] = cs
            return cs[blk - 1:blk, :]

        lax.fori_loop(0, nb, step, jnp.zeros((1, LANES), F32))

    return pl.pallas_call(body, name=name, out_shape=jax.ShapeDtypeStruct((S, LANES), F32))(logits, bias)


def _forget_bwd(dfk, logits, bias, name):
    S = logits.shape[0]
    blk = min(SCAN_BLOCK, S)
    nb = S // blk

    def body(d_ref, z_ref, b_ref, o_ref, db_ref):
        tri = _triangle(blk, lower=False)

        def step(t, carry):
            off = pl.multiple_of((nb - 1 - t) * blk, blk)
            cs = jnp.dot(tri, d_ref[pl.ds(off, blk), :], preferred_element_type=F32, precision=HIGHEST) + carry
            o_ref[pl.ds(off, blk), :] = cs
            return cs[0:1, :]

        lax.fori_loop(0, nb, step, jnp.zeros((1, LANES), F32))
        z = z_ref[...] + b_ref[...]
        dz = -o_ref[...] / (1.0 + jnp.exp(z))
        o_ref[...] = dz
        db_ref[...] = _colsum(dz)

    return pl.pallas_call(
        body, name=name,
        out_shape=(jax.ShapeDtypeStruct((S, LANES), F32), jax.ShapeDtypeStruct((1, LANES), F32)),
    )(dfk, logits, bias)


ATTN_BLOCK = 512
_NT = (((1,), (1,)), ((), ()))


def _attn_specs(S, H, tb):
    q_blk = lambda part: pl.BlockSpec((tb, HEAD_DIM), lambda h, i: (i, part * H + h))
    q_all = lambda part: pl.BlockSpec((S, HEAD_DIM), lambda h, i: (0, part * H + h))
    col_blk = pl.BlockSpec((None, tb, 1), lambda h, i: (h, i, 0))
    row_all = pl.BlockSpec((None, 1, S), lambda h, i: (h, 0, 0))
    return q_blk, q_all, col_blk, row_all


def _attn_params(S, tb):
    return _params(("parallel", "parallel"), 4 * S * HEAD_DIM * 2, 10 * tb * tb * 4)


def _attn_fwd(qkv, f_col, f_row, name):
    S, H = qkv.shape[0], qkv.shape[1] // (3 * HEAD_DIM)
    tb = min(ATTN_BLOCK, S)
    scale = HEAD_DIM ** -0.5
    q_blk, q_all, col_blk, row_all = _attn_specs(S, H, tb)

    def body(q_ref, k_ref, v_ref, fc_ref, fr_ref, o_ref, lse_ref):
        i = pl.program_id(1)
        q, fc = q_ref[...], fc_ref[...]

        def step(j, carry, diagonal):
            m, l, acc = carry
            off = pl.multiple_of(j * tb, tb)
            k, v = k_ref[pl.ds(off, tb), :], v_ref[pl.ds(off, tb), :]
            s = lax.dot_general(q, k, _NT, preferred_element_type=F32) * scale + (fc - fr_ref[:, pl.ds(off, tb)])
            if diagonal:
                row = lax.broadcasted_iota(jnp.int32, (tb, tb), 0)
                col = lax.broadcasted_iota(jnp.int32, (tb, tb), 1)
                s = jnp.where(col <= row, s, NEG)
            m_new = jnp.maximum(m, jnp.max(s, axis=-1, keepdims=True))
            p = jnp.exp(s - m_new)
            alpha = jnp.exp(m - m_new)
            l = alpha * l + jnp.sum(p, axis=-1, keepdims=True)
            acc = alpha * acc + jnp.dot(p.astype(BF16), v, preferred_element_type=F32)
            return m_new, l, acc

        init = (jnp.full((tb, 1), NEG, F32), jnp.zeros((tb, 1), F32), jnp.zeros((tb, HEAD_DIM), F32))
        carry = lax.fori_loop(0, i, lambda j, c: step(j, c, False), init)
        m, l, acc = step(i, carry, True)
        o_ref[...] = (acc / l).astype(o_ref.dtype)
        lse_ref[...] = m + jnp.log(l)

    return pl.pallas_call(
        body, name=name, grid=(H, S // tb),
        in_specs=[q_blk(0), q_all(1), q_all(2), col_blk, row_all],
        out_specs=[pl.BlockSpec((tb, HEAD_DIM), lambda h, i: (i, h)), col_blk],
        out_shape=[jax.ShapeDtypeStruct((S, H * HEAD_DIM), BF16), jax.ShapeDtypeStruct((H, S, 1), F32)],
        compiler_params=_attn_params(S, tb),
    )(qkv, qkv, qkv, f_col, f_row)


def _attn_bwd_q(qkv, do, f_col, f_row, lse_col, name):
    S, H = qkv.shape[0], qkv.shape[1] // (3 * HEAD_DIM)
    tb = min(ATTN_BLOCK, S)
    nq = S // tb
    scale = HEAD_DIM ** -0.5
    q_blk, q_all, col_blk, row_all = _attn_specs(S, H, tb)
    head_blk = pl.BlockSpec((tb, HEAD_DIM), lambda h, i: (i, h))

    def body(q_ref, k_ref, v_ref, do_ref, fc_ref, fr_ref, lse_ref, dq_ref, delta_ref, p_buf, dp_buf):
        i = pl.program_id(1)
        q, do, fc, lse = q_ref[...], do_ref[...], fc_ref[...], lse_ref[...]

        def scores(j, delta, diagonal):
            off = pl.multiple_of(j * tb, tb)
            k, v = k_ref[pl.ds(off, tb), :], v_ref[pl.ds(off, tb), :]
            s = lax.dot_general(q, k, _NT, preferred_element_type=F32) * scale + (fc - fr_ref[:, pl.ds(off, tb)])
            if diagonal:
                row = lax.broadcasted_iota(jnp.int32, (tb, tb), 0)
                col = lax.broadcasted_iota(jnp.int32, (tb, tb), 1)
                s = jnp.where(col <= row, s, NEG)
            p = jnp.exp(s - lse)
            dp = lax.dot_general(do, v, _NT, preferred_element_type=F32)
            p_buf[j] = p
            dp_buf[j] = dp
            return delta + jnp.sum(p * dp, axis=-1, keepdims=True)

        delta = lax.fori_loop(0, i, lambda j, c: scores(j, c, False), jnp.zeros((tb, 1), F32))
        delta = scores(i, delta, True)
        delta_ref[...] = delta

        def grad(j, dq):
            off = pl.multiple_of(j * tb, tb)
            ds = p_buf[j] * (dp_buf[j] - delta)
            return dq + jnp.dot(ds.astype(BF16), k_ref[pl.ds(off, tb), :], preferred_element_type=F32)

        dq = lax.fori_loop(0, i + 1, grad, jnp.zeros((tb, HEAD_DIM), F32))
        dq_ref[...] = (dq * scale).astype(dq_ref.dtype)

    return pl.pallas_call(
        body, name=name, grid=(H, nq),
        in_specs=[q_blk(0), q_all(1), q_all(2), head_blk, col_blk, row_all, col_blk],
        out_specs=[head_blk, col_blk],
        out_shape=[jax.ShapeDtypeStruct((S, H * HEAD_DIM), BF16), jax.ShapeDtypeStruct((H, S, 1), F32)],
        scratch_shapes=[pltpu.VMEM((nq, tb, tb), F32), pltpu.VMEM((nq, tb, tb), F32)],
        compiler_params=_params(("parallel", "parallel"), 4 * S * HEAD_DIM * 2, 2 * nq * tb * tb * 4 + 10 * tb * tb * 4),
    )(qkv, qkv, qkv, do, f_col, f_row, lse_col)


def _attn_bwd_kv(qkv, do, f_col, f_row, lse_row, delta_row, name):
    S, H = qkv.shape[0], qkv.shape[1] // (3 * HEAD_DIM)
    tb = min(ATTN_BLOCK, S)
    nq = S // tb
    scale = HEAD_DIM ** -0.5
    q_blk, q_all, col_blk, row_all = _attn_specs(S, H, tb)
    head_blk = pl.BlockSpec((tb, HEAD_DIM), lambda h, i: (i, h))
    head_all = pl.BlockSpec((S, HEAD_DIM), lambda h, i: (0, h))

    def body(k_ref, v_ref, q_ref, do_ref, fc_ref, fr_ref, lse_ref, delta_ref, dk_ref, dv_ref, dfk_ref):
        j = pl.program_id(1)
        k, v, fck = k_ref[...], v_ref[...], fc_ref[...]

        def step(i, carry, diagonal):
            dk, dv, dfk = carry
            off = pl.multiple_of(i * tb, tb)
            q, do = q_ref[pl.ds(off, tb), :], do_ref[pl.ds(off, tb), :]
            st = lax.dot_general(k, q, _NT, preferred_element_type=F32) * scale + (fr_ref[:, pl.ds(off, tb)] - fck)
            if diagonal:
                row = lax.broadcasted_iota(jnp.int32, (tb, tb), 0)
                col = lax.broadcasted_iota(jnp.int32, (tb, tb), 1)
                st = jnp.where(col >= row, st, NEG)
            pt = jnp.exp(st - lse_ref[:, pl.ds(off, tb)])
            dv = dv + jnp.dot(pt.astype(BF16), do, preferred_element_type=F32)
            dpt = lax.dot_general(v, do, _NT, preferred_element_type=F32)
            dst = pt * (dpt - delta_ref[:, pl.ds(off, tb)])
            dk = dk + jnp.dot(dst.astype(BF16), q, preferred_element_type=F32)
            return dk, dv, dfk + jnp.sum(dst, axis=-1, keepdims=True)

        zeros = jnp.zeros((tb, HEAD_DIM), F32)
        carry = step(j, (zeros, zeros, jnp.zeros((tb, 1), F32)), True)
        dk, dv, dfk = lax.fori_loop(j + 1, nq, lambda i, c: step(i, c, False), carry)
        dk_ref[...] = (dk * scale).astype(dk_ref.dtype)
        dv_ref[...] = dv.astype(dv_ref.dtype)
        dfk_ref[...] = dfk

    return pl.pallas_call(
        body, name=name, grid=(H, nq),
        in_specs=[q_blk(1), q_blk(2), q_all(0), head_all, col_blk, row_all, row_all, row_all],
        out_specs=[head_blk, head_blk, col_blk],
        out_shape=[jax.ShapeDtypeStruct((S, H * HEAD_DIM), BF16), jax.ShapeDtypeStruct((S, H * HEAD_DIM), BF16),
                   jax.ShapeDtypeStruct((H, S, 1), F32)],
        compiler_params=_attn_params(S, tb),
    )(qkv, qkv, qkv, do, f_col, f_row, lse_row, delta_row)


CONV_TILE = 128


def _shift_down(v, n):
    row = lax.broadcasted_iota(jnp.int32, v.shape, 0)
    return jnp.where(row >= n, pltpu.roll(v, n, 0), 0.0)


def _shift_up(v, n):
    S = v.shape[0]
    row = lax.broadcasted_iota(jnp.int32, v.shape, 0)
    return jnp.where(row < S - n, pltpu.roll(v, S - n, 0), 0.0)


def _conv_specs(S, D, tc):
    nb = D // tc
    part = lambda p: pl.BlockSpec((S, tc), lambda j: (0, p * nb + j))
    return part, pl.BlockSpec((S, tc), lambda j: (0, j)), pl.BlockSpec((8, tc), lambda j: (0, j))


def _conv_fwd(proj, conv_w8, name):
    S, D = proj.shape[0], proj.shape[1] // 3
    tc = min(CONV_TILE, D)
    part, chan, taps = _conv_specs(S, D, tc)

    def body(b_ref, c_ref, u_ref, w_ref, z_ref):
        cu = c_ref[...].astype(F32) * u_ref[...].astype(F32)
        w = w_ref[...]
        y = w[0:1, :] * _shift_down(cu, 2) + w[1:2, :] * _shift_down(cu, 1) + w[2:3, :] * cu
        z_ref[...] = (b_ref[...].astype(F32) * y).astype(z_ref.dtype)

    return pl.pallas_call(
        body, name=name, grid=(D // tc,), in_specs=[part(0), part(1), part(2), taps], out_specs=chan,
        out_shape=jax.ShapeDtypeStruct((S, D), BF16),
        compiler_params=_params(("parallel",), 3 * _nbytes((S, tc), proj.dtype) + S * tc * 2, 6 * S * tc * 4),
    )(proj, proj, proj, conv_w8)


def _conv_bwd(proj, dz, conv_w8, name):
    S, D = proj.shape[0], proj.shape[1] // 3
    tc = min(CONV_TILE, D)
    part, chan, taps = _conv_specs(S, D, tc)

    def body(b_ref, c_ref, u_ref, dz_ref, w_ref, db_ref, dc_ref, du_ref, dw_ref):
        cv, uv = c_ref[...].astype(F32), u_ref[...].astype(F32)
        dzv, w = dz_ref[...].astype(F32), w_ref[...]
        cu = cv * uv
        cu1, cu2 = _shift_down(cu, 1), _shift_down(cu, 2)
        y = w[0:1, :] * cu2 + w[1:2, :] * cu1 + w[2:3, :] * cu
        db_ref[...] = (dzv * y).astype(db_ref.dtype)
        dy = dzv * b_ref[...].astype(F32)
        dcu = w[2:3, :] * dy + w[1:2, :] * _shift_up(dy, 1) + w[0:1, :] * _shift_up(dy, 2)
        dc_ref[...] = (dcu * uv).astype(dc_ref.dtype)
        du_ref[...] = (dcu * cv).astype(du_ref.dtype)
        dw_ref[...] = jnp.concatenate(
            [_colsum(dy * cu2), _colsum(dy * cu1), _colsum(dy * cu), jnp.zeros((8 - CONV_WIDTH, tc), F32)], axis=0)

    return pl.pallas_call(
        body, name=name, grid=(D // tc,), in_specs=[part(0), part(1), part(2), chan, taps],
        out_specs=[chan, chan, chan, taps],
        out_shape=[jax.ShapeDtypeStruct((S, D), BF16)] * 3 + [jax.ShapeDtypeStruct((8, D), F32)],
        compiler_params=_params(("parallel",), 3 * _nbytes((S, tc), proj.dtype) + _nbytes((S, tc), dz.dtype)
                                + 3 * S * tc * 2, 10 * S * tc * 4),
    )(proj, proj, proj, dz, conv_w8)


def _adamw(w, m, v, parts, name, layer=0, prev=None):
    L, R, C = w.shape
    P = parts.shape[0]
    assert parts.shape[1:] == (R, C), (name, parts.shape, w.shape)
    row_bytes = C * (12 + 16 + P * parts.dtype.itemsize)
    tr = R if R * row_bytes <= (4 << 20) else max(8, ((4 << 20) // row_bytes) // 8 * 8)
    while R % tr:
        tr -= 8
    c1, c2 = 1.0 - ADAM_B1 ** ADAM_STEP, 1.0 - ADAM_B2 ** ADAM_STEP

    def body(w_ref, m_ref, v_ref, p_ref, *rest):
        g_ref, d_ref, nm_ref, nv_ref = rest[-4:]
        g = p_ref[0].astype(F32)
        for p in range(1, P):
            g = g + p_ref[p].astype(F32)
        nm = ADAM_B1 * m_ref[...] + (1.0 - ADAM_B1) * g
        nv = ADAM_B2 * v_ref[...] + (1.0 - ADAM_B2) * (g * g)
        g_ref[...] = g
        nm_ref[...] = nm
        nv_ref[...] = nv
        d_ref[...] = -ADAM_LR * ((nm / c1) / (jnp.sqrt(nv / c2) + ADAM_EPS) + ADAM_WD * w_ref[...])

    blk = pl.BlockSpec((None, tr, C), lambda i: (layer, i, 0))
    prev = [] if prev is None else list(prev)
    return pl.pallas_call(
        body, name=name, grid=(R // tr,),
        in_specs=[blk, blk, blk, pl.BlockSpec((P, tr, C), lambda i: (0, i, 0))] + [ANY] * len(prev),
        out_specs=[blk] * 4, out_shape=[jax.ShapeDtypeStruct((L, R, C), F32)] * 4,
        input_output_aliases={4 + k: k for k in range(len(prev))},
        compiler_params=_params(("parallel",), tr * row_bytes),
    )(w, m, v, parts, *prev)


def _silu(v):
    return v / (1.0 + jnp.exp(-v))


def _pad_rows(a, rows):
    return jnp.pad(a, ((0, rows - a.shape[0]), (0, 0)))


def _pad_cols(a, cols):
    return jnp.pad(a, ((0, 0), (0, cols - a.shape[1])))


def kernel(x, c, ada_w, ada_b, norm_mix, norm_mlp, fox_w_in, fox_b_f, fox_w_out, conv_w_in, conv_w, conv_w_out, mlp_w_up, mlp_w_down, final_norm, loss_target, m_ada_w, m_ada_b, m_norm_mix, m_norm_mlp, m_fox_w_in, m_fox_b_f, m_fox_w_out, m_conv_w_in, m_conv_w, m_conv_w_out, m_mlp_w_up, m_mlp_w_down, m_final_norm, v_ada_w, v_ada_b, v_norm_mix, v_norm_mlp, v_fox_w_in, v_fox_b_f, v_fox_w_out, v_conv_w_in, v_conv_w, v_conv_w_out, v_mlp_w_up, v_mlp_w_down, v_final_norm):
    S, D = x.shape[1], x.shape[2]
    H = D // HEAD_DIM
    FF = mlp_w_up.shape[2] * NDEV
    depth = ada_w.shape[0]
    n_mod = 6
    assert depth == 2 and fox_w_in.shape[0] == 1 and conv_w_in.shape[0] == 1 and H <= LANES
    me = _my_index()
    x0, target = x[0], loss_target[0]
    row = lambda vec: vec.reshape(1, -1)

    def tied(vec, token):
        return vec + token[0, 0]

    bf = lambda w: w.astype(BF16)
    gather_groups = {
        "fox": [bf(fox_w_in[0]), bf(fox_w_out[0])],
        "mlp0": [bf(mlp_w_up[0]), bf(mlp_w_down[0])],
        "conv": [bf(conv_w_in[0]), conv_w[0], bf(conv_w_out[0])],
        "mlp1": [bf(mlp_w_up[1]), bf(mlp_w_down[1])],
    }

    def start_gather(group, after):
        return _exchange_start(gather_groups[group], f"gather_{group}_start", False, after)

    c_all = _all_gather([c], "gather_cond")[0].reshape(NDEV, D)
    ncol = ada_w.shape[2]
    ada_b_mine = lax.dynamic_slice_in_dim(ada_b, me * ncol, ncol, axis=1)
    mod_cols = jnp.stack([
        _matmul(c_all, ada_w[i], mode="nn", name=f"ada_fwd_{i}", out_dtypes=[F32], tm=NDEV, tn=ncol // 2, tk=D,
                a_pre=_silu, precision=HIGHEST, epilogue=lambda acc, b: (acc + b,), extras=[(ada_b_mine[i:i + 1], "row")])
        for i in range(depth)])
    mod_all = _all_gather([mod_cols], "gather_mod")[0]
    mod = lax.dynamic_index_in_dim(mod_all, me, axis=2, keepdims=False)
    fox_handle, token = start_gather("fox", mod_all)
    mod = tied(mod, token).transpose(1, 0, 2).reshape(depth, n_mod, 1, D)
    sh_mix, sc_mix, g_mix, sh_mlp, sc_mlp, g_mlp = (mod[:, k] for k in range(n_mod))
    b_f = _pad_cols(fox_b_f, LANES)

    def residual(acc, x_in, gate):
        return (x_in + gate * acc, acc)

    def mlp_fwd(i, x_in, handle, next_groups):
        h, inv = _rms_mod_fwd(x_in, row(norm_mlp[i]), sh_mlp[i], sc_mlp[i], f"mlp_norm_{i}")
        w_up, w_down = _exchange_wait(handle, f"gather_mlp{i}_wait", h)
        next_handles, token = [], w_up
        for group in next_groups:
            next_handle, token = start_gather(group, token)
            next_handles.append(next_handle)
        w_down = w_down.reshape(FF, D)
        r, a = _matmul(h, w_up, mode="nn", name=f"mlp_up_{i}", out_dtypes=[BF16, BF16], tm=1024, tn=FF // NDEV, tk=D,
                       b_shards=True, after=[token],
                       epilogue=lambda acc: (jnp.maximum(acc, 0.0), jnp.square(jnp.maximum(acc, 0.0))))
        x_out, y = _matmul(a, w_down, mode="nn", name=f"mlp_down_{i}", out_dtypes=[F32, F32], tm=512, tn=1024, tk=1024,
                           epilogue=residual, extras=[(x_in, "tile"), (g_mlp[i], "row")])
        return x_out, (x_in, h, inv, r, a, y, w_up, w_down), next_handles

    def mlp_bwd(i, dx, saved, gate):
        x_in, h, inv, r, a, y, w_up, w_down = saved
        dy, dgate = _gate_bwd(dx, y, gate, f"mlp_gate_bwd_{i}")
        du = _matmul(dy, w_down, mode="nt", name=f"mlp_down_bwd_{i}", out_dtypes=[BF16], tm=1024, tn=1024, tk=D,
                     epilogue=lambda acc, rv: (acc * (2.0 * rv.astype(F32)),), extras=[(r, "tile")])
        d_down = _matmul(a, dy, mode="tn", name=f"mlp_down_wgrad_{i}", out_dtypes=[BF16], tm=1024, tn=1024, tk=1024)
        dh = _matmul(du, w_up, mode="nt", name=f"mlp_up_bwd_{i}", out_dtypes=[F32], tm=1024, tn=1024, tk=FF // NDEV,
                     b_shards=True)
        d_up = _matmul(h, du, mode="tn", name=f"mlp_up_wgrad_{i}", out_dtypes=[BF16], tm=1024, tn=FF // NDEV, tk=1024,
                       out_shards=True)
        dx, dsh, dsc, dgain = _rms_mod_bwd(dh, x_in, inv, dx, row(norm_mlp[i]), sc_mlp[i], f"mlp_norm_bwd_{i}")
        handle, token = _exchange_start([d_up, d_down.reshape(NDEV, FF // NDEV, D)], f"scatter_mlp{i}_start", True, dx)
        return dx, (dsh, dsc, dgate, dgain), handle, token

    h0, inv0 = _rms_mod_fwd(x0, row(norm_mix[0]), sh_mix[0], sc_mix[0], "fox_norm")
    fox_in, w_fox_out = _exchange_wait(fox_handle, "gather_fox_wait", h0)
    mlp0_handle, token = start_gather("mlp0", fox_in)
    fox_in = fox_in.transpose(1, 0, 2).reshape(D, 3 * D + H)
    w_qkv, w_f = fox_in[:, :3 * D], _pad_cols(fox_in[:, 3 * D:], LANES)
    w_fox_out = w_fox_out.reshape(D, D)
    qkv = _matmul(h0, w_qkv, mode="nn", name="fox_qkv", out_dtypes=[BF16], tm=1024, tn=1024, tk=D, after=[token])
    f_logit = _matmul(h0, w_f, mode="nn", name="fox_forget_logits", out_dtypes=[F32], tm=1024, tn=LANES, tk=D)
    f_cum = _forget_cumsum(f_logit, b_f, "fox_forget_cumsum")
    f_heads = f_cum[:, :H].T
    f_col, f_row = f_heads.reshape(H, S, 1), f_heads.reshape(H, 1, S)
    o, lse = _attn_fwd(qkv, f_col, f_row, "fox_attention")
    x1, mix0 = _matmul(o, w_fox_out, mode="nn", name="fox_out", out_dtypes=[F32, F32], tm=512, tn=1024, tk=D,
                       epilogue=residual, extras=[(x0, "tile"), (g_mix[0], "row")])
    x2, mlp0, (conv_handle, mlp1_handle) = mlp_fwd(0, x1, mlp0_handle, ["conv", "mlp1"])

    h1, inv1 = _rms_mod_fwd(x2, row(norm_mix[1]), sh_mix[1], sc_mix[1], "conv_norm")
    w_conv_in, w_taps, w_conv_out = _exchange_wait(conv_handle, "gather_conv_wait", h1)
    w_taps = _pad_rows(w_taps.transpose(1, 0, 2).reshape(CONV_WIDTH, D), 8)
    w_conv_out = w_conv_out.reshape(D, D)
    proj = _matmul(h1, w_conv_in, mode="nn", name="conv_in", out_dtypes=[F32], tm=1024, tn=3 * D // NDEV, tk=D,
                   b_shards=True)
    z = _conv_fwd(proj, w_taps, "conv_mix")
    x3, mix1 = _matmul(z, w_conv_out, mode="nn", name="conv_out", out_dtypes=[F32, F32], tm=512, tn=1024, tk=D,
                       epilogue=residual, extras=[(x2, "tile"), (g_mix[1], "row")])
    x4, mlp1, _ = mlp_fwd(1, x3, mlp1_handle, [])

    dx, d_final, loss_lanes = _final_loss_bwd(x4, target, row(final_norm), "loss_head")
    loss = lax.psum(loss_lanes[0, 0], ("x", "y", "c"))

    dx, dmod_mlp1, mlp1_scatter, token = mlp_bwd(1, dx, mlp1, g_mlp[1])
    dmix, dg_mix1 = _gate_bwd(dx, mix1, tied(g_mix[1], token), "conv_gate_bwd")
    dz = _matmul(dmix, w_conv_out, mode="nt", name="conv_out_bwd", out_dtypes=[F32], tm=1024, tn=1024, tk=D)
    d_conv_out = _matmul(z, dmix, mode="tn", name="conv_out_wgrad", out_dtypes=[BF16], tm=1024, tn=1024, tk=1024)
    db, dc, du, d_taps = _conv_bwd(proj, dz, w_taps, "conv_mix_bwd")
    dproj = jnp.concatenate([db, dc, du], axis=1)
    dh1 = _matmul(dproj, w_conv_in, mode="nt", name="conv_in_bwd", out_dtypes=[F32], tm=1024, tn=1024, tk=3 * D // NDEV,
                  b_shards=True)
    d_conv_in = _matmul(h1, dproj, mode="tn", name="conv_in_wgrad", out_dtypes=[BF16], tm=1024, tn=3 * D // NDEV, tk=1024,
                        out_shards=True)
    dx, dsh1, dsc1, dgain_mix1 = _rms_mod_bwd(dh1, x2, inv1, dx, row(norm_mix[1]), sc_mix[1], "conv_norm_bwd")
    d_taps_split = d_taps[:CONV_WIDTH].reshape(CONV_WIDTH, NDEV, -1).transpose(1, 0, 2)
    conv_scatter, token = _exchange_start([d_conv_in, d_taps_split, d_conv_out.reshape(NDEV, D // NDEV, D)],
                                          "scatter_conv_start", True, dx)

    dx, dmod_mlp0, mlp0_scatter, token = mlp_bwd(0, dx, mlp0, tied(g_mlp[0], token))
    dmix, dg_mix0 = _gate_bwd(dx, mix0, tied(g_mix[0], token), "fox_gate_bwd")
    do = _matmul(dmix, w_fox_out, mode="nt", name="fox_out_bwd", out_dtypes=[BF16], tm=1024, tn=1024, tk=D)
    d_fox_out = _matmul(o, dmix, mode="tn", name="fox_out_wgrad", out_dtypes=[BF16], tm=1024, tn=1024, tk=1024)
    dq, delta = _attn_bwd_q(qkv, do, f_col, f_row, lse, "fox_attention_bwd_q")
    dk, dv, dfk = _attn_bwd_kv(qkv, do, f_col, f_row, lse.reshape(H, 1, S), delta.reshape(H, 1, S), "fox_attention_bwd_kv")
    dqkv = jnp.concatenate([dq, dk, dv], axis=1)
    dfk_lanes = _pad_cols(dfk.reshape(H, S).T, LANES)
    df_logit, db_f = _forget_bwd(dfk_lanes, f_logit, b_f, "fox_forget_bwd")
    d_qkv = _matmul(h0, dqkv, mode="tn", name="fox_qkv_wgrad", out_dtypes=[BF16], tm=1024, tn=1024, tk=1024)
    d_f = _matmul(h0, df_logit, mode="tn", name="fox_forget_wgrad", out_dtypes=[BF16], tm=1024, tn=LANES, tk=1024)
    d_fox_in = jnp.concatenate([d_qkv, d_f[:, :H]], axis=1).reshape(D, NDEV, -1).transpose(1, 0, 2)
    fox_scatter, token = _exchange_start([d_fox_in, d_fox_out.reshape(NDEV, D // NDEV, D)], "scatter_fox_start", True,
                                         d_fox_in)
    dh0_f = _matmul(df_logit, w_f, mode="nt", name="fox_forget_logits_bwd", out_dtypes=[F32], tm=1024, tn=1024, tk=LANES,
                    after=[token])
    dh0 = _matmul(dqkv, w_qkv, mode="nt", name="fox_qkv_bwd", out_dtypes=[F32], tm=1024, tn=1024, tk=1024,
                  epilogue=lambda acc, extra: (acc + extra,), extras=[(dh0_f, "tile")])
    dx, dsh0, dsc0, dgain_mix0 = _rms_mod_bwd(dh0, x0, inv0, dx, row(norm_mix[0]), sc_mix[0], "fox_norm_bwd")
    grad_x = dx.reshape(1, S, D)

    up1, down1 = _exchange_wait(mlp1_scatter, "scatter_mlp1_wait", dx)
    up_out = _adamw(mlp_w_up, m_mlp_w_up, v_mlp_w_up, up1, "adamw_mlp_w_up_1", layer=1)
    down_out = _adamw(mlp_w_down, m_mlp_w_down, v_mlp_w_down, down1, "adamw_mlp_w_down_1", layer=1)
    cin, taps, cout = _exchange_wait(conv_scatter, "scatter_conv_wait", down_out[0])
    conv_in_out = _adamw(conv_w_in, m_conv_w_in, v_conv_w_in, cin, "adamw_conv_w_in")
    conv_w_res = _adamw(conv_w, m_conv_w, v_conv_w, taps, "adamw_conv_w")
    conv_out_out = _adamw(conv_w_out, m_conv_w_out, v_conv_w_out, cout, "adamw_conv_w_out")
    up0, down0 = _exchange_wait(mlp0_scatter, "scatter_mlp0_wait", conv_out_out[0])
    up_out = _adamw(mlp_w_up, m_mlp_w_up, v_mlp_w_up, up0, "adamw_mlp_w_up_0", layer=0, prev=up_out)
    down_out = _adamw(mlp_w_down, m_mlp_w_down, v_mlp_w_down, down0, "adamw_mlp_w_down_0", layer=0, prev=down_out)

    dmod = jnp.concatenate([
        jnp.concatenate([dsh0, dsc0, dg_mix0, dmod_mlp0[0], dmod_mlp0[1], dmod_mlp0[2]], axis=1),
        jnp.concatenate([dsh1, dsc1, dg_mix1, dmod_mlp1[0], dmod_mlp1[1], dmod_mlp1[2]], axis=1)], axis=0)
    small_sizes = [depth * n_mod * D, depth * D, depth * D, H, D]
    n_small = sum(small_sizes)
    n_rows = -(-n_small // (8 * LANES)) * 8

    def pack(parts):
        flat = jnp.concatenate([p.reshape(-1) for p in parts])
        return jnp.pad(flat, (0, n_rows * LANES - n_small)).reshape(n_rows, LANES)

    def unpack(packed, shapes):
        flat, out, at = packed.reshape(-1), [], 0
        for size, shape in zip(small_sizes, shapes):
            out.append(flat[at:at + size].reshape(shape))
            at += size
        return out

    small_partial = pack([dmod, jnp.concatenate([dgain_mix0, dgain_mix1], axis=0),
                          jnp.concatenate([dmod_mlp0[3], dmod_mlp1[3]], axis=0), db_f[0, :H], d_final])
    small_parts = _all_gather([small_partial], "gather_small_grads", after=[down_out[0]])[0]
    small_shapes = [ada_b.shape, norm_mix.shape, norm_mlp.shape, fox_b_f.shape, final_norm.shape]
    small_out = _adamw(pack([ada_b, norm_mix, norm_mlp, fox_b_f, final_norm])[None],
                       pack([m_ada_b, m_norm_mix, m_norm_mlp, m_fox_b_f, m_final_norm])[None],
                       pack([v_ada_b, v_norm_mix, v_norm_mlp, v_fox_b_f, v_final_norm])[None], small_parts, "adamw_small")
    small_out = [unpack(t, small_shapes) for t in small_out]

    dmod_all = small_parts.reshape(NDEV, -1)[:, :depth * n_mod * D].reshape(NDEV, depth, n_mod * D)
    dmod_mine = lax.dynamic_slice_in_dim(dmod_all, me * ncol, ncol, axis=2)
    ada_out = None
    for i in range(depth):
        d_ada = _matmul(c_all, dmod_mine[:, i], mode="tn", name=f"ada_wgrad_{i}", out_dtypes=[F32], tm=1024, tn=ncol // 2,
                        tk=NDEV, a_pre=_silu, precision=HIGHEST)
        ada_out = _adamw(ada_w, m_ada_w, v_ada_w, d_ada[None], f"adamw_ada_w_{i}", layer=i, prev=ada_out)

    fin, fout = _exchange_wait(fox_scatter, "scatter_fox_wait", ada_out[0])
    fox_in_out = _adamw(fox_w_in, m_fox_w_in, v_fox_w_in, fin, "adamw_fox_w_in")
    fox_out_out = _adamw(fox_w_out, m_fox_w_out, v_fox_w_out, fout, "adamw_fox_w_out")

    outputs = [loss, grad_x]
    for kind in range(4):
        sm = small_out[kind]
        outputs += [ada_out[kind], sm[0], sm[1], sm[2], fox_in_out[kind], sm[3], fox_out_out[kind], conv_in_out[kind],
                    conv_w_res[kind], conv_out_out[kind], up_out[kind], down_out[kind], sm[4]]
    return tuple(outputs)
```

```python
import functools
import math

import jax
import jax.numpy as jnp
from jax import lax
from jax.experimental import pallas as pl
from jax.experimental.pallas import tpu as pltpu

F32 = jnp.float32
BF16 = jnp.bfloat16
MESH = pl.DeviceIdType.MESH
NDEV = 8
HEAD_DIM = 128
LANES = 128
CONV_WIDTH = 3
RMS_EPS = 1e-6
ADAM_LR, ADAM_B1, ADAM_B2, ADAM_EPS, ADAM_WD, ADAM_STEP = 0.001, 0.9, 0.999, 1e-08, 0.01, 10
NEG = -1e30
V7X_VMEM_BYTES = 64 * 1024 * 1024
VMEM_HEADROOM = 12 * 1024 * 1024
HBM = pl.BlockSpec(memory_space=pltpu.HBM)
HIGHEST = lax.Precision.HIGHEST


def _nbytes(shape, dtype):
    return math.prod(shape) * jnp.dtype(dtype).itemsize


def _params(semantics, block_bytes, temp_bytes=0):
    limit = min(2 * block_bytes + temp_bytes + VMEM_HEADROOM, V7X_VMEM_BYTES - 4 * 1024 * 1024)
    return pltpu.CompilerParams(dimension_semantics=semantics, vmem_limit_bytes=int(limit))


def _my_index():
    return lax.axis_index("x") * 4 + lax.axis_index("y") * 2 + lax.axis_index("c")


def _peer(r):
    x, y, c = lax.axis_index("x"), lax.axis_index("y"), lax.axis_index("c")
    px = 1 - x if (r >> 2) & 1 else x
    py = 1 - y if (r >> 1) & 1 else y
    pc = 1 - c if r & 1 else c
    return (px, py, pc), px * 4 + py * 2 + pc


def _exchange(arrays, name, scatter, after=None):
    n = len(arrays)
    after = [] if after is None else list(after)

    def body(*refs):
        ins, outs = refs[:n], refs[n + len(after):2 * n + len(after)]
        send_sems, recv_sems, local_sems = refs[2 * n + len(after):]
        me = _my_index()
        local = []
        for a in range(n):
            src = ins[a].at[me] if scatter else ins[a]
            local.append(pltpu.make_async_copy(src, outs[a].at[me], local_sems.at[a]))
            local[-1].start()
        sends = []
        for r in range(1, NDEV):
            peer, pidx = _peer(r)
            for a in range(n):
                src = ins[a].at[pidx] if scatter else ins[a]
                cp = pltpu.make_async_remote_copy(
                    src_ref=src, dst_ref=outs[a].at[me],
                    send_sem=send_sems.at[a * (NDEV - 1) + r - 1], recv_sem=recv_sems.at[a * (NDEV - 1) + r - 1],
                    device_id=peer, device_id_type=MESH)
                cp.start()
                sends.append(cp)
        for r in range(1, NDEV):
            peer, pidx = _peer(r)
            for a in range(n):
                src = ins[a].at[pidx] if scatter else ins[a]
                pltpu.make_async_remote_copy(
                    src_ref=src, dst_ref=outs[a].at[pidx],
                    send_sem=send_sems.at[a * (NDEV - 1) + r - 1], recv_sem=recv_sems.at[a * (NDEV - 1) + r - 1],
                    device_id=peer, device_id_type=MESH).wait_recv()
        for cp in sends:
            cp.wait_send()
        for cp in local:
            cp.wait()

    out_shape = [jax.ShapeDtypeStruct(a.shape if scatter else (NDEV,) + a.shape, a.dtype) for a in arrays]
    return pl.pallas_call(
        body, name=name, out_shape=out_shape, in_specs=[HBM] * n + [ANY] * len(after), out_specs=[HBM] * n,
        scratch_shapes=[pltpu.SemaphoreType.DMA((n * (NDEV - 1),)), pltpu.SemaphoreType.DMA((n * (NDEV - 1),)),
                        pltpu.SemaphoreType.DMA((n,))],
    )(*arrays, *after)


def _all_gather(arrays, name, after=None):
    return _exchange(arrays, name, scatter=False, after=after)


SEM = pl.BlockSpec(memory_space=pltpu.SEMAPHORE)
ANY = pl.BlockSpec(memory_space=pl.ANY)
DATAFLOW = pltpu.SideEffectType.DATAFLOW_SIDE_EFFECTING
TOKEN_SHAPE = (8, LANES)


SIBLING = 1
OTHER_CHIPS = (4, 2, 6)


def _exchange_start(arrays, name, scatter, after, relay=False):
    n = len(arrays)
    n_sems = n * (NDEV - 1)
    assert not (relay and scatter)

    def body(*refs):
        ins, lands = refs[:n], refs[n:2 * n]
        send_sems, recv_sems = refs[2 * n + 1], refs[2 * n + 2]
        token = refs[2 * n + 3 + 2 * n]
        me = _my_index()
        for r in (SIBLING, *OTHER_CHIPS) if relay else range(1, NDEV):
            peer, pidx = _peer(r)
            for a in range(n):
                src = ins[a].at[pidx] if scatter else ins[a]
                pltpu.make_async_remote_copy(
                    src_ref=src, dst_ref=lands[a].at[me],
                    send_sem=send_sems.at[a * (NDEV - 1) + r - 1], recv_sem=recv_sems.at[a * (NDEV - 1) + r - 1],
                    device_id=peer, device_id_type=MESH).start()
        token[...] = jnp.zeros(TOKEN_SHAPE, F32)

    land_shapes = [a.shape if scatter else (NDEV,) + a.shape for a in arrays]
    srcs = [pltpu.with_memory_space_constraint(a, pltpu.HBM) for a in arrays]
    lands = [pltpu.with_memory_space_constraint(lax.empty(s, a.dtype), pltpu.HBM) for s, a in zip(land_shapes, arrays)]
    outs = pl.pallas_call(
        body, name=name,
        out_shape=(pltpu.SemaphoreType.DMA((n_sems,)), pltpu.SemaphoreType.DMA((n_sems,)),
                   *[pltpu.HBM(a.shape, a.dtype) for a in arrays], *[pltpu.HBM(s, a.dtype) for s, a in zip(land_shapes, arrays)],
                   jax.ShapeDtypeStruct(TOKEN_SHAPE, F32)),
        in_specs=[HBM] * (2 * n) + [ANY],
        out_specs=(SEM, SEM, *[HBM] * (2 * n), pl.BlockSpec(memory_space=pltpu.VMEM)),
        input_output_aliases={i: 2 + i for i in range(2 * n)},
        compiler_params=pltpu.CompilerParams(has_side_effects=DATAFLOW),
    )(*srcs, *lands, after)
    return (scatter, relay, [(outs[0], outs[1])], list(outs[2:2 + n]), list(outs[2 + n:2 + 2 * n])), outs[-1]


def _relay_forward_start(handle, name):
    scatter, relay, sems, srcs, lands = handle
    n = len(lands)
    n_sems = n * len(OTHER_CHIPS)

    def body(*refs):
        land_refs, send_sems, recv_sems = refs[:n], refs[n], refs[n + 1]
        sibling, _ = _peer(SIBLING)
        for j, r in enumerate(OTHER_CHIPS):
            _, pidx = _peer(r)
            for a in range(n):
                pltpu.make_async_remote_copy(
                    src_ref=land_refs[a].at[pidx], dst_ref=land_refs[a].at[pidx],
                    send_sem=send_sems.at[a * len(OTHER_CHIPS) + j], recv_sem=recv_sems.at[a * len(OTHER_CHIPS) + j],
                    device_id=sibling, device_id_type=MESH).start()

    outs = pl.pallas_call(
        body, name=name,
        out_shape=(pltpu.SemaphoreType.DMA((n_sems,)), pltpu.SemaphoreType.DMA((n_sems,)),
                   *[pltpu.HBM(t.shape, t.dtype) for t in lands]),
        in_specs=[HBM] * n, out_specs=(SEM, SEM, *[HBM] * n),
        input_output_aliases={i: 2 + i for i in range(n)},
        compiler_params=pltpu.CompilerParams(has_side_effects=DATAFLOW),
    )(*lands)
    return (scatter, relay, sems + [(outs[0], outs[1])], srcs, list(outs[2:]))


def _exchange_wait(handle, name, after, arrivals_only=False):
    scatter, relay, sems, srcs, lands = handle
    n = len(srcs)
    forwarded = len(sems) == 2
    assert not arrivals_only or (relay and not forwarded)

    def body(*refs):
        src_refs, land_refs = refs[:n], refs[n:2 * n]
        send_sems, recv_sems = refs[2 * n], refs[2 * n + 1]
        for r in (SIBLING, *OTHER_CHIPS) if relay else range(1, NDEV):
            peer, pidx = _peer(r)
            for a in range(n):
                src = src_refs[a].at[pidx] if scatter else src_refs[a]
                cp = pltpu.make_async_remote_copy(
                    src_ref=src, dst_ref=land_refs[a].at[pidx],
                    send_sem=send_sems.at[a * (NDEV - 1) + r - 1], recv_sem=recv_sems.at[a * (NDEV - 1) + r - 1],
                    device_id=peer, device_id_type=MESH)
                if arrivals_only:
                    if r in OTHER_CHIPS:
                        cp.wait_recv()
                else:
                    cp.wait_send()
                    if not (relay and r in OTHER_CHIPS):
                        cp.wait_recv()
        if forwarded:
            fwd_send, fwd_recv = refs[2 * n + 2], refs[2 * n + 3]
            sibling, _ = _peer(SIBLING)
            for j, r in enumerate(OTHER_CHIPS):
                _, pidx = _peer(r ^ SIBLING)
                for a in range(n):
                    cp = pltpu.make_async_remote_copy(
                        src_ref=src_refs[a], dst_ref=land_refs[a].at[pidx],
                        send_sem=fwd_send.at[a * len(OTHER_CHIPS) + j], recv_sem=fwd_recv.at[a * len(OTHER_CHIPS) + j],
                        device_id=sibling, device_id_type=MESH)
                    cp.wait_send()
                    cp.wait_recv()

    flat_sems = [s for pair in sems for s in pair]
    outs = pl.pallas_call(
        body, name=name,
        out_shape=tuple(pltpu.HBM(t.shape, t.dtype) for t in (*srcs, *lands)),
        in_specs=[HBM] * (2 * n) + [SEM] * len(flat_sems) + [ANY], out_specs=tuple([HBM] * (2 * n)),
        input_output_aliases={i: i for i in range(2 * n)},
        compiler_params=pltpu.CompilerParams(has_side_effects=DATAFLOW),
    )(*srcs, *lands, *flat_sems, after)
    if arrivals_only:
        return (scatter, relay, sems, list(outs[:n]), list(outs[n:]))
    me = _my_index()
    mine = [lax.dynamic_index_in_dim(s, me, 0, keepdims=False) if scatter else s for s in outs[:n]]
    return [lax.dynamic_update_index_in_dim(land, own, me, 0) for land, own in zip(outs[n:], mine)]


def _matmul(a, b, *, mode, name, out_dtypes, tm, tn, tk, epilogue=None, extras=(), a_pre=None,
            out_shards=False, n_outer=False, precision=None, after=()):
    after = list(after)
    n_after = len(after)
    K, M = a.shape if mode == "tn" else a.shape[::-1]
    N = b.shape[0] if mode == "nt" else b.shape[1]
    tm, tn, tk = min(tm, M), min(tn, N), min(tk, K)
    assert M % tm == 0 and N % tn == 0 and K % tk == 0, (name, M, N, K, tm, tn, tk)
    nm, nn, nk = M // tm, N // tn, K // tk
    n_out, n_ext = len(out_dtypes), len(extras)
    contract = {"nn": ((1,), (0,)), "nt": ((1,), (1,)), "tn": ((0,), (0,))}[mode]

    def body(*refs):
        a_ref, b_ref = refs[:2]
        ext_refs = refs[2:2 + n_ext]
        out_refs = refs[2 + n_ext + n_after:2 + n_ext + n_after + n_out]
        acc_ref = refs[2 + n_ext + n_after + n_out] if nk > 1 else None
        av, bv = a_ref[...], b_ref[...]
        if a_pre is not None:
            av = a_pre(av)
        if precision is None:
            av, bv = av.astype(BF16), bv.astype(BF16)
        part = lax.dot_general(av, bv, (contract, ((), ())), preferred_element_type=F32, precision=precision)

        def finish(acc):
            vals = (acc,) if epilogue is None else epilogue(acc, *[r[...] for r in ext_refs])
            for r, v in zip(out_refs, vals):
                r[...] = v.astype(r.dtype)

        if nk == 1:
            finish(part)
        else:
            k = pl.program_id(2)

            @pl.when(k == 0)
            def _():
                acc_ref[...] = part

            @pl.when(k > 0)
            def _():
                acc_ref[...] += part

            @pl.when(k == nk - 1)
            def _():
                finish(acc_ref[...])

    def at(index):
        return (lambda j, i, k: index(i, j, k)) if n_outer else index

    a_spec = pl.BlockSpec((tk, tm), at(lambda i, j, k: (k, i))) if mode == "tn" else pl.BlockSpec((tm, tk), at(lambda i, j, k: (i, k)))
    b_spec = pl.BlockSpec((tn, tk), at(lambda i, j, k: (j, k))) if mode == "nt" else pl.BlockSpec((tk, tn), at(lambda i, j, k: (k, j)))
    in_specs, block_bytes = [a_spec, b_spec], _nbytes((tm, tk), a.dtype) + _nbytes((tk, tn), b.dtype)
    for arr, kind in extras:
        if kind == "tile":
            assert arr.shape == (M, N), (name, arr.shape)
            in_specs.append(pl.BlockSpec((tm, tn), at(lambda i, j, k: (i, j))))
            block_bytes += _nbytes((tm, tn), arr.dtype)
        else:
            assert arr.shape == (1, N), (name, arr.shape)
            in_specs.append(pl.BlockSpec((1, tn), at(lambda i, j, k: (0, j))))
    in_specs += [ANY] * n_after
    if out_shards:
        assert n_out == 1 and tn * NDEV == N
        out_shape = [jax.ShapeDtypeStruct((NDEV, M, tn), out_dtypes[0])]
        out_specs = [pl.BlockSpec((None, tm, tn), at(lambda i, j, k: (j, i, 0)))]
    else:
        out_shape = [jax.ShapeDtypeStruct((M, N), d) for d in out_dtypes]
        out_specs = [pl.BlockSpec((tm, tn), at(lambda i, j, k: (i, j))) for _ in out_dtypes]
    block_bytes += sum(_nbytes((tm, tn), d) for d in out_dtypes)
    scratch = [pltpu.VMEM((tm, tn), F32)] if nk > 1 else []
    outs = pl.pallas_call(
        body, name=name, grid=(nn, nm, nk) if n_outer else (nm, nn, nk), in_specs=in_specs, out_specs=out_specs,
        out_shape=out_shape, scratch_shapes=scratch,
        compiler_params=_params(("parallel", "parallel", "arbitrary"), block_bytes, 2 * tm * tn * 4),
    )(a, b, *[arr for arr, _ in extras], *after)
    return outs[0] if n_out == 1 else outs


def _rowwise(fn, tiled, smalls, out_tiles, out_sums, *, name, ts=256):
    S = tiled[0].shape[0]
    ts = min(ts, S)
    assert S % ts == 0
    nt, ns, no, na = len(tiled), len(smalls), len(out_tiles), len(out_sums)

    def body(*refs):
        t_refs, s_refs = refs[:nt], refs[nt:nt + ns]
        o_refs, a_refs = refs[nt + ns:nt + ns + no], refs[nt + ns + no:]
        tile_vals, sum_vals = fn([r[...] for r in t_refs], [r[...] for r in s_refs])
        for r, v in zip(o_refs, tile_vals):
            r[...] = v.astype(r.dtype)

        @pl.when(pl.program_id(0) == 0)
        def _():
            for r in a_refs:
                r[...] = jnp.zeros_like(r)

        for r, v in zip(a_refs, sum_vals):
            r[...] += v

    in_specs = [pl.BlockSpec((ts, t.shape[1]), lambda i: (i, 0)) for t in tiled]
    in_specs += [pl.BlockSpec(s.shape, lambda i: (0, 0)) for s in smalls]
    out_specs = [pl.BlockSpec((ts, w), lambda i: (i, 0)) for w, _ in out_tiles]
    out_specs += [pl.BlockSpec((1, w), lambda i: (0, 0)) for w in out_sums]
    out_shape = [jax.ShapeDtypeStruct((S, w), d) for w, d in out_tiles]
    out_shape += [jax.ShapeDtypeStruct((1, w), F32) for w in out_sums]
    block_bytes = sum(_nbytes((ts, t.shape[1]), t.dtype) for t in tiled) + sum(_nbytes((ts, w), d) for w, d in out_tiles)
    width = max(t.shape[1] for t in tiled)
    outs = pl.pallas_call(
        body, name=name, grid=(S // ts,), in_specs=in_specs, out_specs=out_specs, out_shape=out_shape,
        compiler_params=_params(("arbitrary",), block_bytes, 6 * ts * width * 4),
    )(*tiled, *smalls)
    return outs[:no], outs[no:]


def _colsum(v):
    return jnp.sum(v, axis=0, keepdims=True)


def _rms_mod_fwd(x, gain, shift, scale, name):
    def fn(tiles, smalls):
        (xv,), (g, sh, sc) = tiles, smalls
        inv = lax.rsqrt(jnp.mean(xv * xv, axis=-1, keepdims=True) + RMS_EPS)
        h = (xv * inv) * g * (1.0 + sc) + sh
        return (h, inv), ()

    D = x.shape[1]
    (h, inv), _ = _rowwise(fn, [x], [gain, shift, scale], [(D, BF16), (1, F32)], [], name=name)
    return h, inv


def _gate_bwd(dx, y, gate, name):
    def fn(tiles, smalls):
        (dxv, yv), (g,) = tiles, smalls
        return (dxv * g,), (_colsum(dxv * yv),)

    D = dx.shape[1]
    (dy,), (dgate,) = _rowwise(fn, [dx, y], [gate], [(D, BF16)], [D], name=name)
    return dy, dgate


def _rms_mod_bwd(dh, x, inv, dx_res, gain, scale, name):
    def fn(tiles, smalls):
        (dhv, xv, iv, dres), (g, sc) = tiles, smalls
        dhv = dhv.astype(F32)
        xhat = xv * iv
        dr = dhv * (1.0 + sc)
        dxhat = dr * g
        dxn = iv * (dxhat - xhat * jnp.mean(dxhat * xhat, axis=-1, keepdims=True))
        return (dres + dxn,), (_colsum(dhv), _colsum(dhv * (xhat * g)), _colsum(dr * xhat))

    D = x.shape[1]
    (dx,), (dsh, dsc, dgain) = _rowwise(fn, [dh, x, inv, dx_res], [gain, scale], [(D, F32)], [D, D, D], name=name)
    return dx, dsh, dsc, dgain


def _final_loss_bwd(x, target, gain, name):
    D = x.shape[1]

    def fn(tiles, smalls):
        (xv, tv), (g,) = tiles, smalls
        inv = lax.rsqrt(jnp.mean(xv * xv, axis=-1, keepdims=True) + RMS_EPS)
        xhat = xv * inv
        err = xhat * g - tv
        loss = 0.5 * jnp.sum(jnp.mean(err * err, axis=-1, keepdims=True), axis=0, keepdims=True)
        dout = err * (1.0 / D)
        dxhat = dout * g
        dxv = inv * (dxhat - xhat * jnp.mean(dxhat * xhat, axis=-1, keepdims=True))
        return (dxv,), (_colsum(dout * xhat), jnp.broadcast_to(loss, (1, LANES)))

    (dx,), (dgain, loss) = _rowwise(fn, [x, target], [gain], [(D, F32)], [D, LANES], name=name)
    return dx, dgain, loss


SCAN_BLOCK = 256


def _triangle(n, lower):
    r = lax.broadcasted_iota(jnp.int32, (n, n), 0)
    c = lax.broadcasted_iota(jnp.int32, (n, n), 1)
    return (r >= c if lower else r <= c).astype(F32)


def _forget_cumsum(logits, bias, name):
    S = logits.shape[0]
    blk = min(SCAN_BLOCK, S)
    nb = S // blk

    def body(z_ref, b_ref, f_ref):
        z = z_ref[...] + b_ref[...]
        f_ref[...] = jnp.minimum(z, 0.0) - jnp.log(1.0 + jnp.exp(-jnp.abs(z)))
        tri = _triangle(blk, lower=True)

        def step(i, carry):
            off = pl.multiple_of(i * blk, blk)
            cs = jnp.dot(tri, f_ref[pl.ds(off, blk), :], preferred_element_type=F32, precision=HIGHEST) + carry
            f_ref[pl.ds(off, blk), :] = cs
            return cs[blk - 1:blk, :]

        lax.fori_loop(0, nb, step, jnp.zeros((1, LANES), F32))

    return pl.pallas_call(body, name=name, out_shape=jax.ShapeDtypeStruct((S, LANES), F32))(logits, bias)


def _forget_bwd(dfk, logits, bias, name):
    S = logits.shape[0]
    blk = min(SCAN_BLOCK, S)
    nb = S // blk

    def body(d_ref, z_ref, b_ref, o_ref, db_ref):
        tri = _triangle(blk, lower=False)

        def step(t, carry):
            off = pl.multiple_of((nb - 1 - t) * blk, blk)
            cs = jnp.dot(tri, d_ref[pl.ds(off, blk), :], preferred_element_type=F32, precision=HIGHEST) + carry
            o_ref[pl.ds(off, blk), :] = cs
            return cs[0:1, :]

        lax.fori_loop(0, nb, step, jnp.zeros((1, LANES), F32))
        z = z_ref[...] + b_ref[...]
        dz = -o_ref[...] / (1.0 + jnp.exp(z))
        o_ref[...] = dz
        db_ref[...] = _colsum(dz)

    return pl.pallas_call(
        body, name=name,
        out_shape=(jax.ShapeDtypeStruct((S, LANES), F32), jax.ShapeDtypeStruct((1, LANES), F32)),
    )(dfk, logits, bias)


ATTN_BLOCK = 512
_NT = (((1,), (1,)), ((), ()))


def _attn_specs(S, H, tb):
    q_blk = lambda part: pl.BlockSpec((tb, HEAD_DIM), lambda h, i: (i, part * H + h))
    q_all = lambda part: pl.BlockSpec((S, HEAD_DIM), lambda h, i: (0, part * H + h))
    col_blk = pl.BlockSpec((None, tb, 1), lambda h, i: (h, i, 0))
    row_all = pl.BlockSpec((None, 1, S), lambda h, i: (h, 0, 0))
    return q_blk, q_all, col_blk, row_all


def _attn_params(S, tb):
    return _params(("parallel", "parallel"), 4 * S * HEAD_DIM * 2, 10 * tb * tb * 4)


def _attn_fwd(qkv, f_col, f_row, name):
    S, H = qkv.shape[0], qkv.shape[1] // (3 * HEAD_DIM)
    tb = min(ATTN_BLOCK, S)
    scale = HEAD_DIM ** -0.5
    q_blk, q_all, col_blk, row_all = _attn_specs(S, H, tb)

    def body(q_ref, k_ref, v_ref, fc_ref, fr_ref, o_ref, lse_ref):
        i = pl.program_id(1)
        q, fc = q_ref[...], fc_ref[...]

        def step(j, carry, diagonal):
            m, l, acc = carry
            off = pl.multiple_of(j * tb, tb)
            k, v = k_ref[pl.ds(off, tb), :], v_ref[pl.ds(off, tb), :]
            s = lax.dot_general(q, k, _NT, preferred_element_type=F32) * scale + (fc - fr_ref[:, pl.ds(off, tb)])
            if diagonal:
                row = lax.broadcasted_iota(jnp.int32, (tb, tb), 0)
                col = lax.broadcasted_iota(jnp.int32, (tb, tb), 1)
                s = jnp.where(col <= row, s, NEG)
            m_new = jnp.maximum(m, jnp.max(s, axis=-1, keepdims=True))
            p = jnp.exp(s - m_new)
            alpha = jnp.exp(m - m_new)
            l = alpha * l + jnp.sum(p, axis=-1, keepdims=True)
            acc = alpha * acc + jnp.dot(p.astype(BF16), v, preferred_element_type=F32)
            return m_new, l, acc

        init = (jnp.full((tb, 1), NEG, F32), jnp.zeros((tb, 1), F32), jnp.zeros((tb, HEAD_DIM), F32))
        carry = lax.fori_loop(0, i, lambda j, c: step(j, c, False), init)
        m, l, acc = step(i, carry, True)
        o_ref[...] = (acc / l).astype(o_ref.dtype)
        lse_ref[...] = m + jnp.log(l)

    return pl.pallas_call(
        body, name=name, grid=(H, S // tb),
        in_specs=[q_blk(0), q_all(1), q_all(2), col_blk, row_all],
        out_specs=[pl.BlockSpec((tb, HEAD_DIM), lambda h, i: (i, h)), col_blk],
        out_shape=[jax.ShapeDtypeStruct((S, H * HEAD_DIM), BF16), jax.ShapeDtypeStruct((H, S, 1), F32)],
        compiler_params=_attn_params(S, tb),
    )(qkv, qkv, qkv, f_col, f_row)


def _attn_bwd_q(qkv, do, f_col, f_row, lse_col, name):
    S, H = qkv.shape[0], qkv.shape[1] // (3 * HEAD_DIM)
    tb = min(ATTN_BLOCK, S)
    nq = S // tb
    scale = HEAD_DIM ** -0.5
    q_blk, q_all, col_blk, row_all = _attn_specs(S, H, tb)
    head_blk = pl.BlockSpec((tb, HEAD_DIM), lambda h, i: (i, h))

    def body(q_ref, k_ref, v_ref, do_ref, fc_ref, fr_ref, lse_ref, dq_ref, delta_ref, p_buf, dp_buf):
        i = pl.program_id(1)
        q, do, fc, lse = q_ref[...], do_ref[...], fc_ref[...], lse_ref[...]

        def scores(j, delta, diagonal):
            off = pl.multiple_of(j * tb, tb)
            k, v = k_ref[pl.ds(off, tb), :], v_ref[pl.ds(off, tb), :]
            s = lax.dot_general(q, k, _NT, preferred_element_type=F32) * scale + (fc - fr_ref[:, pl.ds(off, tb)])
            if diagonal:
                row = lax.broadcasted_iota(jnp.int32, (tb, tb), 0)
                col = lax.broadcasted_iota(jnp.int32, (tb, tb), 1)
                s = jnp.where(col <= row, s, NEG)
            p = jnp.exp(s - lse)
            dp = lax.dot_general(do, v, _NT, preferred_element_type=F32)
            p_buf[j] = p
            dp_buf[j] = dp
            return delta + jnp.sum(p * dp, axis=-1, keepdims=True)

        delta = lax.fori_loop(0, i, lambda j, c: scores(j, c, False), jnp.zeros((tb, 1), F32))
        delta = scores(i, delta, True)
        delta_ref[...] = delta

        def grad(j, dq):
            off = pl.multiple_of(j * tb, tb)
            ds = p_buf[j] * (dp_buf[j] - delta)
            return dq + jnp.dot(ds.astype(BF16), k_ref[pl.ds(off, tb), :], preferred_element_type=F32)

        dq = lax.fori_loop(0, i + 1, grad, jnp.zeros((tb, HEAD_DIM), F32))
        dq_ref[...] = (dq * scale).astype(dq_ref.dtype)

    return pl.pallas_call(
        body, name=name, grid=(H, nq),
        in_specs=[q_blk(0), q_all(1), q_all(2), head_blk, col_blk, row_all, col_blk],
        out_specs=[head_blk, col_blk],
        out_shape=[jax.ShapeDtypeStruct((S, H * HEAD_DIM), BF16), jax.ShapeDtypeStruct((H, S, 1), F32)],
        scratch_shapes=[pltpu.VMEM((nq, tb, tb), F32), pltpu.VMEM((nq, tb, tb), F32)],
        compiler_params=_params(("parallel", "parallel"), 4 * S * HEAD_DIM * 2, 2 * nq * tb * tb * 4 + 10 * tb * tb * 4),
    )(qkv, qkv, qkv, do, f_col, f_row, lse_col)


def _attn_bwd_kv(qkv, do, f_col, f_row, lse_row, delta_row, name):
    S, H = qkv.shape[0], qkv.shape[1] // (3 * HEAD_DIM)
    tb = min(ATTN_BLOCK, S)
    nq = S // tb
    scale = HEAD_DIM ** -0.5
    q_blk, q_all, col_blk, row_all = _attn_specs(S, H, tb)
    head_blk = pl.BlockSpec((tb, HEAD_DIM), lambda h, i: (i, h))
    head_all = pl.BlockSpec((S, HEAD_DIM), lambda h, i: (0, h))

    def body(k_ref, v_ref, q_ref, do_ref, fc_ref, fr_ref, lse_ref, delta_ref, dk_ref, dv_ref, dfk_ref):
        j = pl.program_id(1)
        k, v, fck = k_ref[...], v_ref[...], fc_ref[...]

        def step(i, carry, diagonal):
            dk, dv, dfk = carry
            off = pl.multiple_of(i * tb, tb)
            q, do = q_ref[pl.ds(off, tb), :], do_ref[pl.ds(off, tb), :]
            st = lax.dot_general(k, q, _NT, preferred_element_type=F32) * scale + (fr_ref[:, pl.ds(off, tb)] - fck)
            if diagonal:
                row = lax.broadcasted_iota(jnp.int32, (tb, tb), 0)
                col = lax.broadcasted_iota(jnp.int32, (tb, tb), 1)
                st = jnp.where(col >= row, st, NEG)
            pt = jnp.exp(st - lse_ref[:, pl.ds(off, tb)])
            dv = dv + jnp.dot(pt.astype(BF16), do, preferred_element_type=F32)
            dpt = lax.dot_general(v, do, _NT, preferred_element_type=F32)
            dst = pt * (dpt - delta_ref[:, pl.ds(off, tb)])
            dk = dk + jnp.dot(dst.astype(BF16), q, preferred_element_type=F32)
            return dk, dv, dfk + jnp.sum(dst, axis=-1, keepdims=True)

        zeros = jnp.zeros((tb, HEAD_DIM), F32)
        carry = step(j, (zeros, zeros, jnp.zeros((tb, 1), F32)), True)
        dk, dv, dfk = lax.fori_loop(j + 1, nq, lambda i, c: step(i, c, False), carry)
        dk_ref[...] = (dk * scale).astype(dk_ref.dtype)
        dv_ref[...] = dv.astype(dv_ref.dtype)
        dfk_ref[...] = dfk

    return pl.pallas_call(
        body, name=name, grid=(H, nq),
        in_specs=[q_blk(1), q_blk(2), q_all(0), head_all, col_blk, row_all, row_all, row_all],
        out_specs=[head_blk, head_blk, col_blk],
        out_shape=[jax.ShapeDtypeStruct((S, H * HEAD_DIM), BF16), jax.ShapeDtypeStruct((S, H * HEAD_DIM), BF16),
                   jax.ShapeDtypeStruct((H, S, 1), F32)],
        compiler_params=_attn_params(S, tb),
    )(qkv, qkv, qkv, do, f_col, f_row, lse_row, delta_row)


CONV_TILE = 128


def _shift_down(v, n):
    row = lax.broadcasted_iota(jnp.int32, v.shape, 0)
    return jnp.where(row >= n, pltpu.roll(v, n, 0), 0.0)


def _shift_up(v, n):
    S = v.shape[0]
    row = lax.broadcasted_iota(jnp.int32, v.shape, 0)
    return jnp.where(row < S - n, pltpu.roll(v, S - n, 0), 0.0)


def _conv_specs(S, D, tc):
    nb = D // tc
    part = lambda p: pl.BlockSpec((S, tc), lambda j: (0, p * nb + j))
    return part, pl.BlockSpec((S, tc), lambda j: (0, j)), pl.BlockSpec((8, tc), lambda j: (0, j))


def _conv_fwd(proj, conv_w8, name):
    S, D = proj.shape[0], proj.shape[1] // 3
    tc = min(CONV_TILE, D)
    part, chan, taps = _conv_specs(S, D, tc)

    def body(b_ref, c_ref, u_ref, w_ref, z_ref):
        cu = c_ref[...].astype(F32) * u_ref[...].astype(F32)
        w = w_ref[...]
        y = w[0:1, :] * _shift_down(cu, 2) + w[1:2, :] * _shift_down(cu, 1) + w[2:3, :] * cu
        z_ref[...] = (b_ref[...].astype(F32) * y).astype(z_ref.dtype)

    return pl.pallas_call(
        body, name=name, grid=(D // tc,), in_specs=[part(0), part(1), part(2), taps], out_specs=chan,
        out_shape=jax.ShapeDtypeStruct((S, D), BF16),
        compiler_params=_params(("parallel",), 3 * _nbytes((S, tc), proj.dtype) + S * tc * 2, 6 * S * tc * 4),
    )(proj, proj, proj, conv_w8)


def _conv_bwd(proj, dz, conv_w8, name):
    S, D = proj.shape[0], proj.shape[1] // 3
    tc = min(CONV_TILE, D)
    part, chan, taps = _conv_specs(S, D, tc)

    def body(b_ref, c_ref, u_ref, dz_ref, w_ref, db_ref, dc_ref, du_ref, dw_ref):
        cv, uv = c_ref[...].astype(F32), u_ref[...].astype(F32)
        dzv, w = dz_ref[...].astype(F32), w_ref[...]
        cu = cv * uv
        cu1, cu2 = _shift_down(cu, 1), _shift_down(cu, 2)
        y = w[0:1, :] * cu2 + w[1:2, :] * cu1 + w[2:3, :] * cu
        db_ref[...] = (dzv * y).astype(db_ref.dtype)
        dy = dzv * b_ref[...].astype(F32)
        dcu = w[2:3, :] * dy + w[1:2, :] * _shift_up(dy, 1) + w[0:1, :] * _shift_up(dy, 2)
        dc_ref[...] = (dcu * uv).astype(dc_ref.dtype)
        du_ref[...] = (dcu * cv).astype(du_ref.dtype)
        dw_ref[...] = jnp.concatenate(
            [_colsum(dy * cu2), _colsum(dy * cu1), _colsum(dy * cu), jnp.zeros((8 - CONV_WIDTH, tc), F32)], axis=0)

    return pl.pallas_call(
        body, name=name, grid=(D // tc,), in_specs=[part(0), part(1), part(2), chan, taps],
        out_specs=[chan, chan, chan, taps],
        out_shape=[jax.ShapeDtypeStruct((S, D), BF16)] * 3 + [jax.ShapeDtypeStruct((8, D), F32)],
        compiler_params=_params(("parallel",), 3 * _nbytes((S, tc), proj.dtype) + _nbytes((S, tc), dz.dtype)
                                + 3 * S * tc * 2, 10 * S * tc * 4),
    )(proj, proj, proj, dz, conv_w8)


def _adamw(w, m, v, parts, name, layer=0, prev=None):
    L, R, C = w.shape
    P = parts.shape[0]
    assert parts.shape[1:] == (R, C), (name, parts.shape, w.shape)
    row_bytes = C * (12 + 16 + P * parts.dtype.itemsize)
    tr = R if R * row_bytes <= (4 << 20) else max(8, ((4 << 20) // row_bytes) // 8 * 8)
    while R % tr:
        tr -= 8
    c1, c2 = 1.0 - ADAM_B1 ** ADAM_STEP, 1.0 - ADAM_B2 ** ADAM_STEP

    def body(w_ref, m_ref, v_ref, p_ref, *rest):
        g_ref, d_ref, nm_ref, nv_ref = rest[-4:]
        g = p_ref[0].astype(F32)
        for p in range(1, P):
            g = g + p_ref[p].astype(F32)
        nm = ADAM_B1 * m_ref[...] + (1.0 - ADAM_B1) * g
        nv = ADAM_B2 * v_ref[...] + (1.0 - ADAM_B2) * (g * g)
        g_ref[...] = g
        nm_ref[...] = nm
        nv_ref[...] = nv
        d_ref[...] = -ADAM_LR * ((nm / c1) / (jnp.sqrt(nv / c2) + ADAM_EPS) + ADAM_WD * w_ref[...])

    blk = pl.BlockSpec((None, tr, C), lambda i: (layer, i, 0))
    prev = [] if prev is None else list(prev)
    return pl.pallas_call(
        body, name=name, grid=(R // tr,),
        in_specs=[blk, blk, blk, pl.BlockSpec((P, tr, C), lambda i: (0, i, 0))] + [ANY] * len(prev),
        out_specs=[blk] * 4, out_shape=[jax.ShapeDtypeStruct((L, R, C), F32)] * 4,
        input_output_aliases={4 + k: k for k in range(len(prev))},
        compiler_params=_params(("parallel",), tr * row_bytes),
    )(w, m, v, parts, *prev)


def _silu(v):
    return v / (1.0 + jnp.exp(-v))


def _pad_rows(a, rows):
    return jnp.pad(a, ((0, rows - a.shape[0]), (0, 0)))


def _pad_cols(a, cols):
    return jnp.pad(a, ((0, 0), (0, cols - a.shape[1])))


def kernel(x, c, ada_w, ada_b, norm_mix, norm_mlp, fox_w_in, fox_b_f, fox_w_out, conv_w_in, conv_w, conv_w_out, mlp_w_up, mlp_w_down, final_norm, loss_target, m_ada_w, m_ada_b, m_norm_mix, m_norm_mlp, m_fox_w_in, m_fox_b_f, m_fox_w_out, m_conv_w_in, m_conv_w, m_conv_w_out, m_mlp_w_up, m_mlp_w_down, m_final_norm, v_ada_w, v_ada_b, v_norm_mix, v_norm_mlp, v_fox_w_in, v_fox_b_f, v_fox_w_out, v_conv_w_in, v_conv_w, v_conv_w_out, v_mlp_w_up, v_mlp_w_down, v_final_norm):
    S, D = x.shape[1], x.shape[2]
    H = D // HEAD_DIM
    FF = mlp_w_up.shape[2] * NDEV
    depth = ada_w.shape[0]
    n_mod = 6
    assert depth == 2 and fox_w_in.shape[0] == 1 and conv_w_in.shape[0] == 1 and H <= LANES
    me = _my_index()
    x0, target = x[0], loss_target[0]
    row = lambda vec: vec.reshape(1, -1)

    def tied(vec, token):
        return vec + token[0, 0]

    bf = lambda w: w.astype(BF16)
    gather_groups = {
        "fox": [bf(fox_w_in[0]), bf(fox_w_out[0])],
        "mlp0": [bf(mlp_w_up[0]).T, bf(mlp_w_down[0])],
        "conv": [bf(conv_w_in[0]).T, conv_w[0], bf(conv_w_out[0])],
        "mlp1": [bf(mlp_w_up[1]).T, bf(mlp_w_down[1])],
    }

    def start_gather(group, after):
        return _exchange_start(gather_groups[group], f"gather_{group}_start", False, after, relay=True)

    def finish_gather(handle, group, after):
        handle = _exchange_wait(handle, f"gather_{group}_arrivals", after, arrivals_only=True)
        handle = _relay_forward_start(handle, f"gather_{group}_forward")
        return _exchange_wait(handle, f"gather_{group}_wait", after)

    c_all = _all_gather([c], "gather_cond")[0].reshape(NDEV, D)
    ncol = ada_w.shape[2]
    ada_b_mine = lax.dynamic_slice_in_dim(ada_b, me * ncol, ncol, axis=1)
    mod_cols = jnp.stack([
        _matmul(c_all, ada_w[i], mode="nn", name=f"ada_fwd_{i}", out_dtypes=[F32], tm=NDEV, tn=ncol // 2, tk=D,
                a_pre=_silu, precision=HIGHEST, epilogue=lambda acc, b: (acc + b,), extras=[(ada_b_mine[i:i + 1], "row")])
        for i in range(depth)])
    mod_all = _all_gather([mod_cols], "gather_mod")[0]
    mod = lax.dynamic_index_in_dim(mod_all, me, axis=2, keepdims=False)
    fox_handle, token = start_gather("fox", mod_all)
    mod = tied(mod, token).transpose(1, 0, 2).reshape(depth, n_mod, 1, D)
    sh_mix, sc_mix, g_mix, sh_mlp, sc_mlp, g_mlp = (mod[:, k] for k in range(n_mod))
    b_f = _pad_cols(fox_b_f, LANES)

    def residual(acc, x_in, gate):
        return (x_in + gate * acc, acc)

    def mlp_fwd(i, x_in, handle, next_groups):
        h, inv = _rms_mod_fwd(x_in, row(norm_mlp[i]), sh_mlp[i], sc_mlp[i], f"mlp_norm_{i}")
        w_up_t, w_down = finish_gather(handle, f"mlp{i}", h)
        next_handles, token = [], w_up_t
        for group in next_groups:
            next_handle, token = start_gather(group, token)
            next_handles.append(next_handle)
        w_up_t, w_down = w_up_t.reshape(FF, D), w_down.reshape(FF, D)
        r, a = _matmul(h, w_up_t, mode="nt", name=f"mlp_up_{i}", out_dtypes=[BF16, BF16], tm=1024, tn=1024, tk=D,
                       after=[token], epilogue=lambda acc: (jnp.maximum(acc, 0.0), jnp.square(jnp.maximum(acc, 0.0))))
        x_out, y = _matmul(a, w_down, mode="nn", name=f"mlp_down_{i}", out_dtypes=[F32, F32], tm=256, tn=512, tk=FF,
                           n_outer=True, epilogue=residual, extras=[(x_in, "tile"), (g_mlp[i], "row")])
        return x_out, (x_in, h, inv, r, a, y, w_up_t, w_down), next_handles

    def mlp_bwd(i, dx, saved, gate):
        x_in, h, inv, r, a, y, w_up_t, w_down = saved
        dy, dgate = _gate_bwd(dx, y, gate, f"mlp_gate_bwd_{i}")
        du = _matmul(dy, w_down, mode="nt", name=f"mlp_down_bwd_{i}", out_dtypes=[BF16], tm=1024, tn=1024, tk=D,
                     epilogue=lambda acc, rv: (acc * (2.0 * rv.astype(F32)),), extras=[(r, "tile")])
        d_down = _matmul(a, dy, mode="tn", name=f"mlp_down_wgrad_{i}", out_dtypes=[BF16], tm=512, tn=1024, tk=S)
        dh = _matmul(du, w_up_t, mode="nn", name=f"mlp_up_bwd_{i}", out_dtypes=[F32], tm=256, tn=512, tk=FF, n_outer=True)
        d_up = _matmul(h, du, mode="tn", name=f"mlp_up_wgrad_{i}", out_dtypes=[BF16], tm=512, tn=FF // NDEV, tk=S,
                       out_shards=True)
        dx, dsh, dsc, dgain = _rms_mod_bwd(dh, x_in, inv, dx, row(norm_mlp[i]), sc_mlp[i], f"mlp_norm_bwd_{i}")
        handle, token = _exchange_start([d_up, d_down.reshape(NDEV, FF // NDEV, D)], f"scatter_mlp{i}_start", True, dx)
        return dx, (dsh, dsc, dgate, dgain), handle, token

    h0, inv0 = _rms_mod_fwd(x0, row(norm_mix[0]), sh_mix[0], sc_mix[0], "fox_norm")
    fox_in, w_fox_out = finish_gather(fox_handle, "fox", h0)
    mlp0_handle, token = start_gather("mlp0", fox_in)
    fox_in = fox_in.transpose(1, 0, 2).reshape(D, 3 * D + H)
    w_qkv, w_f = fox_in[:, :3 * D], _pad_cols(fox_in[:, 3 * D:], LANES)
    w_fox_out = w_fox_out.reshape(D, D)
    qkv = _matmul(h0, w_qkv, mode="nn", name="fox_qkv", out_dtypes=[BF16], tm=1024, tn=1024, tk=D, after=[token])
    f_logit = _matmul(h0, w_f, mode="nn", name="fox_forget_logits", out_dtypes=[F32], tm=1024, tn=LANES, tk=D)
    f_cum = _forget_cumsum(f_logit, b_f, "fox_forget_cumsum")
    f_heads = f_cum[:, :H].T
    f_col, f_row = f_heads.reshape(H, S, 1), f_heads.reshape(H, 1, S)
    o, lse = _attn_fwd(qkv, f_col, f_row, "fox_attention")
    x1, mix0 = _matmul(o, w_fox_out, mode="nn", name="fox_out", out_dtypes=[F32, F32], tm=512, tn=1024, tk=D,
                       epilogue=residual, extras=[(x0, "tile"), (g_mix[0], "row")])
    x2, mlp0, (conv_handle, mlp1_handle) = mlp_fwd(0, x1, mlp0_handle, ["conv", "mlp1"])

    h1, inv1 = _rms_mod_fwd(x2, row(norm_mix[1]), sh_mix[1], sc_mix[1], "conv_norm")
    w_conv_in_t, w_taps, w_conv_out = finish_gather(conv_handle, "conv", h1)
    w_conv_in_t = w_conv_in_t.reshape(3 * D, D)
    w_taps = _pad_rows(w_taps.transpose(1, 0, 2).reshape(CONV_WIDTH, D), 8)
    w_conv_out = w_conv_out.reshape(D, D)
    proj = _matmul(h1, w_conv_in_t, mode="nt", name="conv_in", out_dtypes=[F32], tm=1024, tn=1024, tk=D)
    z = _conv_fwd(proj, w_taps, "conv_mix")
    x3, mix1 = _matmul(z, w_conv_out, mode="nn", name="conv_out", out_dtypes=[F32, F32], tm=512, tn=1024, tk=D,
                       epilogue=residual, extras=[(x2, "tile"), (g_mix[1], "row")])
    x4, mlp1, _ = mlp_fwd(1, x3, mlp1_handle, [])

    dx, d_final, loss_lanes = _final_loss_bwd(x4, target, row(final_norm), "loss_head")
    loss = lax.psum(loss_lanes[0, 0], ("x", "y", "c"))

    dx, dmod_mlp1, mlp1_scatter, token = mlp_bwd(1, dx, mlp1, g_mlp[1])
    dmix, dg_mix1 = _gate_bwd(dx, mix1, tied(g_mix[1], token), "conv_gate_bwd")
    dz = _matmul(dmix, w_conv_out, mode="nt", name="conv_out_bwd", out_dtypes=[F32], tm=1024, tn=1024, tk=D)
    d_conv_out = _matmul(z, dmix, mode="tn", name="conv_out_wgrad", out_dtypes=[BF16], tm=512, tn=1024, tk=S)
    db, dc, du, d_taps = _conv_bwd(proj, dz, w_taps, "conv_mix_bwd")
    dproj = jnp.concatenate([db, dc, du], axis=1)
    dh1 = _matmul(dproj, w_conv_in_t, mode="nn", name="conv_in_bwd", out_dtypes=[F32], tm=512, tn=512, tk=3 * D, n_outer=True)
    d_conv_in = _matmul(h1, dproj, mode="tn", name="conv_in_wgrad", out_dtypes=[BF16], tm=512, tn=3 * D // NDEV, tk=S,
                        out_shards=True)
    dx, dsh1, dsc1, dgain_mix1 = _rms_mod_bwd(dh1, x2, inv1, dx, row(norm_mix[1]), sc_mix[1], "conv_norm_bwd")
    d_taps_split = d_taps[:CONV_WIDTH].reshape(CONV_WIDTH, NDEV, -1).transpose(1, 0, 2)
    conv_scatter, token = _exchange_start([d_conv_in, d_taps_split, d_conv_out.reshape(NDEV, D // NDEV, D)],
                                          "scatter_conv_start", True, dx)

    dx, dmod_mlp0, mlp0_scatter, token = mlp_bwd(0, dx, mlp0, tied(g_mlp[0], token))
    dmix, dg_mix0 = _gate_bwd(dx, mix0, tied(g_mix[0], token), "fox_gate_bwd")
    do = _matmul(dmix, w_fox_out, mode="nt", name="fox_out_bwd", out_dtypes=[BF16], tm=1024, tn=1024, tk=D)
    d_fox_out = _matmul(o, dmix, mode="tn", name="fox_out_wgrad", out_dtypes=[BF16], tm=512, tn=1024, tk=S)
    dq, delta = _attn_bwd_q(qkv, do, f_col, f_row, lse, "fox_attention_bwd_q")
    dk, dv, dfk = _attn_bwd_kv(qkv, do, f_col, f_row, lse.reshape(H, 1, S), delta.reshape(H, 1, S), "fox_attention_bwd_kv")
    dqkv = jnp.concatenate([dq, dk, dv], axis=1)
    dfk_lanes = _pad_cols(dfk.reshape(H, S).T, LANES)
    df_logit, db_f = _forget_bwd(dfk_lanes, f_logit, b_f, "fox_forget_bwd")
    d_qkv = _matmul(h0, dqkv, mode="tn", name="fox_qkv_wgrad", out_dtypes=[BF16], tm=512, tn=1024, tk=S)
    d_f = _matmul(h0, df_logit, mode="tn", name="fox_forget_wgrad", out_dtypes=[BF16], tm=512, tn=LANES, tk=S)
    d_fox_in = jnp.concatenate([d_qkv, d_f[:, :H]], axis=1).reshape(D, NDEV, -1).transpose(1, 0, 2)
    fox_scatter, token = _exchange_start([d_fox_in, d_fox_out.reshape(NDEV, D // NDEV, D)], "scatter_fox_start", True,
                                         d_fox_in)
    dh0_f = _matmul(df_logit, w_f, mode="nt", name="fox_forget_logits_bwd", out_dtypes=[F32], tm=1024, tn=1024, tk=LANES,
                    after=[token])
    dh0 = _matmul(dqkv, w_qkv, mode="nt", name="fox_qkv_bwd", out_dtypes=[F32], tm=512, tn=512, tk=3 * D,
                  epilogue=lambda acc, extra: (acc + extra,), extras=[(dh0_f, "tile")])
    dx, dsh0, dsc0, dgain_mix0 = _rms_mod_bwd(dh0, x0, inv0, dx, row(norm_mix[0]), sc_mix[0], "fox_norm_bwd")
    grad_x = dx.reshape(1, S, D)

    up1, down1 = _exchange_wait(mlp1_scatter, "scatter_mlp1_wait", dx)
    up_out = _adamw(mlp_w_up, m_mlp_w_up, v_mlp_w_up, up1, "adamw_mlp_w_up_1", layer=1)
    down_out = _adamw(mlp_w_down, m_mlp_w_down, v_mlp_w_down, down1, "adamw_mlp_w_down_1", layer=1)
    cin, taps, cout = _exchange_wait(conv_scatter, "scatter_conv_wait", down_out[0])
    conv_in_out = _adamw(conv_w_in, m_conv_w_in, v_conv_w_in, cin, "adamw_conv_w_in")
    conv_w_res = _adamw(conv_w, m_conv_w, v_conv_w, taps, "adamw_conv_w")
    conv_out_out = _adamw(conv_w_out, m_conv_w_out, v_conv_w_out, cout, "adamw_conv_w_out")
    up0, down0 = _exchange_wait(mlp0_scatter, "scatter_mlp0_wait", conv_out_out[0])
    up_out = _adamw(mlp_w_up, m_mlp_w_up, v_mlp_w_up, up0, "adamw_mlp_w_up_0", layer=0, prev=up_out)
    down_out = _adamw(mlp_w_down, m_mlp_w_down, v_mlp_w_down, down0, "adamw_mlp_w_down_0", layer=0, prev=down_out)

    dmod = jnp.concatenate([
        jnp.concatenate([dsh0, dsc0, dg_mix0, dmod_mlp0[0], dmod_mlp0[1], dmod_mlp0[2]], axis=1),
        jnp.concatenate([dsh1, dsc1, dg_mix1, dmod_mlp1[0], dmod_mlp1[1], dmod_mlp1[2]], axis=1)], axis=0)
    small_sizes = [depth * n_mod * D, depth * D, depth * D, H, D]
    n_small = sum(small_sizes)
    n_rows = -(-n_small // (8 * LANES)) * 8

    def pack(parts):
        flat = jnp.concatenate([p.reshape(-1) for p in parts])
        return jnp.pad(flat, (0, n_rows * LANES - n_small)).reshape(n_rows, LANES)

    def unpack(packed, shapes):
        flat, out, at = packed.reshape(-1), [], 0
        for size, shape in zip(small_sizes, shapes):
            out.append(flat[at:at + size].reshape(shape))
            at += size
        return out

    small_partial = pack([dmod, jnp.concatenate([dgain_mix0, dgain_mix1], axis=0),
                          jnp.concatenate([dmod_mlp0[3], dmod_mlp1[3]], axis=0), db_f[0, :H], d_final])
    small_parts = _all_gather([small_partial], "gather_small_grads", after=[down_out[0]])[0]
    small_shapes = [ada_b.shape, norm_mix.shape, norm_mlp.shape, fox_b_f.shape, final_norm.shape]
    small_out = _adamw(pack([ada_b, norm_mix, norm_mlp, fox_b_f, final_norm])[None],
                       pack([m_ada_b, m_norm_mix, m_norm_mlp, m_fox_b_f, m_final_norm])[None],
                       pack([v_ada_b, v_norm_mix, v_norm_mlp, v_fox_b_f, v_final_norm])[None], small_parts, "adamw_small")
    small_out = [unpack(t, small_shapes) for t in small_out]

    dmod_all = small_parts.reshape(NDEV, -1)[:, :depth * n_mod * D].reshape(NDEV, depth, n_mod * D)
    dmod_mine = lax.dynamic_slice_in_dim(dmod_all, me * ncol, ncol, axis=2)
    ada_out = None
    for i in range(depth):
        d_ada = _matmul(c_all, dmod_mine[:, i], mode="tn", name=f"ada_wgrad_{i}", out_dtypes=[F32], tm=1024, tn=ncol // 2,
                        tk=NDEV, a_pre=_silu, precision=HIGHEST)
        ada_out = _adamw(ada_w, m_ada_w, v_ada_w, d_ada[None], f"adamw_ada_w_{i}", layer=i, prev=ada_out)

    fin, fout = _exchange_wait(fox_scatter, "scatter_fox_wait", ada_out[0])
    fox_in_out = _adamw(fox_w_in, m_fox_w_in, v_fox_w_in, fin, "adamw_fox_w_in")
    fox_out_out = _adamw(fox_w_out, m_fox_w_out, v_fox_w_out, fout, "adamw_fox_w_out")

    outputs = [loss, grad_x]
    for kind in range(4):
        sm = small_out[kind]
        outputs += [ada_out[kind], sm[0], sm[1], sm[2], fox_in_out[kind], sm[3], fox_out_out[kind], conv_in_out[kind],
                    conv_w_res[kind], conv_out_out[kind], up_out[kind], down_out[kind], sm[4]]
    return tuple(outputs)
```

```python
import functools
import math

import jax
import jax.numpy as jnp
from jax import lax
from jax.experimental import pallas as pl
from jax.experimental.pallas import tpu as pltpu

F32 = jnp.float32
BF16 = jnp.bfloat16
MESH = pl.DeviceIdType.MESH
NDEV = 8
HEAD_DIM = 128
LANES = 128
CONV_WIDTH = 3
RMS_EPS = 1e-6
ADAM_LR, ADAM_B1, ADAM_B2, ADAM_EPS, ADAM_WD, ADAM_STEP = 0.001, 0.9, 0.999, 1e-08, 0.01, 10
NEG = -1e30
V7X_VMEM_BYTES = 64 * 1024 * 1024
VMEM_HEADROOM = 12 * 1024 * 1024
HBM = pl.BlockSpec(memory_space=pltpu.HBM)
HIGHEST = lax.Precision.HIGHEST


def _nbytes(shape, dtype):
    return math.prod(shape) * jnp.dtype(dtype).itemsize


def _params(semantics, block_bytes, temp_bytes=0):
    limit = min(2 * block_bytes + temp_bytes + VMEM_HEADROOM, V7X_VMEM_BYTES - 4 * 1024 * 1024)
    return pltpu.CompilerParams(dimension_semantics=semantics, vmem_limit_bytes=int(limit))


def _my_index():
    return lax.axis_index("x") * 4 + lax.axis_index("y") * 2 + lax.axis_index("c")


def _peer(r):
    x, y, c = lax.axis_index("x"), lax.axis_index("y"), lax.axis_index("c")
    px = 1 - x if (r >> 2) & 1 else x
    py = 1 - y if (r >> 1) & 1 else y
    pc = 1 - c if r & 1 else c
    return (px, py, pc), px * 4 + py * 2 + pc


def _exchange(arrays, name, scatter, after=None):
    n = len(arrays)
    after = [] if after is None else list(after)

    def body(*refs):
        ins, outs = refs[:n], refs[n + len(after):2 * n + len(after)]
        send_sems, recv_sems, local_sems = refs[2 * n + len(after):]
        me = _my_index()
        local = []
        for a in range(n):
            src = ins[a].at[me] if scatter else ins[a]
            local.append(pltpu.make_async_copy(src, outs[a].at[me], local_sems.at[a]))
            local[-1].start()
        sends = []
        for r in range(1, NDEV):
            peer, pidx = _peer(r)
            for a in range(n):
                src = ins[a].at[pidx] if scatter else ins[a]
                cp = pltpu.make_async_remote_copy(
                    src_ref=src, dst_ref=outs[a].at[me],
                    send_sem=send_sems.at[a * (NDEV - 1) + r - 1], recv_sem=recv_sems.at[a * (NDEV - 1) + r - 1],
                    device_id=peer, device_id_type=MESH)
                cp.start()
                sends.append(cp)
        for r in range(1, NDEV):
            peer, pidx = _peer(r)
            for a in range(n):
                src = ins[a].at[pidx] if scatter else ins[a]
                pltpu.make_async_remote_copy(
                    src_ref=src, dst_ref=outs[a].at[pidx],
                    send_sem=send_sems.at[a * (NDEV - 1) + r - 1], recv_sem=recv_sems.at[a * (NDEV - 1) + r - 1],
                    device_id=peer, device_id_type=MESH).wait_recv()
        for cp in sends:
            cp.wait_send()
        for cp in local:
            cp.wait()

    out_shape = [jax.ShapeDtypeStruct(a.shape if scatter else (NDEV,) + a.shape, a.dtype) for a in arrays]
    return pl.pallas_call(
        body, name=name, out_shape=out_shape, in_specs=[HBM] * n + [ANY] * len(after), out_specs=[HBM] * n,
        scratch_shapes=[pltpu.SemaphoreType.DMA((n * (NDEV - 1),)), pltpu.SemaphoreType.DMA((n * (NDEV - 1),)),
                        pltpu.SemaphoreType.DMA((n,))],
    )(*arrays, *after)


def _all_gather(arrays, name, after=None):
    return _exchange(arrays, name, scatter=False, after=after)


SEM = pl.BlockSpec(memory_space=pltpu.SEMAPHORE)
ANY = pl.BlockSpec(memory_space=pl.ANY)
DATAFLOW = pltpu.SideEffectType.DATAFLOW_SIDE_EFFECTING
TOKEN_SHAPE = (8, LANES)


SIBLING = 1
OTHER_CHIPS = (4, 2, 6)


def _exchange_start(arrays, name, scatter, after, relay=False):
    n = len(arrays)
    n_sems = n * (NDEV - 1)
    assert not (relay and scatter)

    def body(*refs):
        ins, lands = refs[:n], refs[n:2 * n]
        send_sems, recv_sems = refs[2 * n + 1], refs[2 * n + 2]
        token = refs[2 * n + 3 + 2 * n]
        me = _my_index()
        for r in (SIBLING, *OTHER_CHIPS) if relay else range(1, NDEV):
            peer, pidx = _peer(r)
            for a in range(n):
                src = ins[a].at[pidx] if scatter else ins[a]
                pltpu.make_async_remote_copy(
                    src_ref=src, dst_ref=lands[a].at[me],
                    send_sem=send_sems.at[a * (NDEV - 1) + r - 1], recv_sem=recv_sems.at[a * (NDEV - 1) + r - 1],
                    device_id=peer, device_id_type=MESH).start()
        token[...] = jnp.zeros(TOKEN_SHAPE, F32)

    land_shapes = [a.shape if scatter else (NDEV,) + a.shape for a in arrays]
    srcs = [pltpu.with_memory_space_constraint(a, pltpu.HBM) for a in arrays]
    lands = [pltpu.with_memory_space_constraint(lax.empty(s, a.dtype), pltpu.HBM) for s, a in zip(land_shapes, arrays)]
    outs = pl.pallas_call(
        body, name=name,
        out_shape=(pltpu.SemaphoreType.DMA((n_sems,)), pltpu.SemaphoreType.DMA((n_sems,)),
                   *[pltpu.HBM(a.shape, a.dtype) for a in arrays], *[pltpu.HBM(s, a.dtype) for s, a in zip(land_shapes, arrays)],
                   jax.ShapeDtypeStruct(TOKEN_SHAPE, F32)),
        in_specs=[HBM] * (2 * n) + [ANY],
        out_specs=(SEM, SEM, *[HBM] * (2 * n), pl.BlockSpec(memory_space=pltpu.VMEM)),
        input_output_aliases={i: 2 + i for i in range(2 * n)},
        compiler_params=pltpu.CompilerParams(has_side_effects=DATAFLOW),
    )(*srcs, *lands, after)
    return (scatter, relay, [(outs[0], outs[1])], list(outs[2:2 + n]), list(outs[2 + n:2 + 2 * n])), outs[-1]


def _relay_forward_start(handle, name):
    scatter, relay, sems, srcs, lands = handle
    n = len(lands)
    n_sems = n * len(OTHER_CHIPS)

    def body(*refs):
        land_refs, send_sems, recv_sems = refs[:n], refs[n], refs[n + 1]
        sibling, _ = _peer(SIBLING)
        for j, r in enumerate(OTHER_CHIPS):
            _, pidx = _peer(r)
            for a in range(n):
                pltpu.make_async_remote_copy(
                    src_ref=land_refs[a].at[pidx], dst_ref=land_refs[a].at[pidx],
                    send_sem=send_sems.at[a * len(OTHER_CHIPS) + j], recv_sem=recv_sems.at[a * len(OTHER_CHIPS) + j],
                    device_id=sibling, device_id_type=MESH).start()

    outs = pl.pallas_call(
        body, name=name,
        out_shape=(pltpu.SemaphoreType.DMA((n_sems,)), pltpu.SemaphoreType.DMA((n_sems,)),
                   *[pltpu.HBM(t.shape, t.dtype) for t in lands]),
        in_specs=[HBM] * n, out_specs=(SEM, SEM, *[HBM] * n),
        input_output_aliases={i: 2 + i for i in range(n)},
        compiler_params=pltpu.CompilerParams(has_side_effects=DATAFLOW),
    )(*lands)
    return (scatter, relay, sems + [(outs[0], outs[1])], srcs, list(outs[2:]))


def _exchange_wait(handle, name, after, arrivals_only=False):
    scatter, relay, sems, srcs, lands = handle
    n = len(srcs)
    forwarded = len(sems) == 2
    assert not arrivals_only or (relay and not forwarded)

    def body(*refs):
        src_refs, land_refs = refs[:n], refs[n:2 * n]
        send_sems, recv_sems = refs[2 * n], refs[2 * n + 1]
        for r in (SIBLING, *OTHER_CHIPS) if relay else range(1, NDEV):
            peer, pidx = _peer(r)
            for a in range(n):
                src = src_refs[a].at[pidx] if scatter else src_refs[a]
                cp = pltpu.make_async_remote_copy(
                    src_ref=src, dst_ref=land_refs[a].at[pidx],
                    send_sem=send_sems.at[a * (NDEV - 1) + r - 1], recv_sem=recv_sems.at[a * (NDEV - 1) + r - 1],
                    device_id=peer, device_id_type=MESH)
                if arrivals_only:
                    if r in OTHER_CHIPS:
                        cp.wait_recv()
                else:
                    cp.wait_send()
                    if not (relay and r in OTHER_CHIPS):
                        cp.wait_recv()
        if forwarded:
            fwd_send, fwd_recv = refs[2 * n + 2], refs[2 * n + 3]
            sibling, _ = _peer(SIBLING)
            for j, r in enumerate(OTHER_CHIPS):
                _, pidx = _peer(r ^ SIBLING)
                for a in range(n):
                    cp = pltpu.make_async_remote_copy(
                        src_ref=src_refs[a], dst_ref=land_refs[a].at[pidx],
                        send_sem=fwd_send.at[a * len(OTHER_CHIPS) + j], recv_sem=fwd_recv.at[a * len(OTHER_CHIPS) + j],
                        device_id=sibling, device_id_type=MESH)
                    cp.wait_send()
                    cp.wait_recv()

    flat_sems = [s for pair in sems for s in pair]
    outs = pl.pallas_call(
        body, name=name,
        out_shape=tuple(pltpu.HBM(t.shape, t.dtype) for t in (*srcs, *lands)),
        in_specs=[HBM] * (2 * n) + [SEM] * len(flat_sems) + [ANY], out_specs=tuple([HBM] * (2 * n)),
        input_output_aliases={i: i for i in range(2 * n)},
        compiler_params=pltpu.CompilerParams(has_side_effects=DATAFLOW),
    )(*srcs, *lands, *flat_sems, after)
    if arrivals_only:
        return (scatter, relay, sems, list(outs[:n]), list(outs[n:]))
    me = _my_index()
    mine = [lax.dynamic_index_in_dim(s, me, 0, keepdims=False) if scatter else s for s in outs[:n]]
    return [lax.dynamic_update_index_in_dim(land, own, me, 0) for land, own in zip(outs[n:], mine)]


def _matmul(a, b, *, mode, name, out_dtypes, tm, tn, tk, epilogue=None, extras=(), a_pre=None,
            out_shards=False, n_outer=False, precision=None, after=()):
    after = list(after)
    n_after = len(after)
    K, M = a.shape if mode == "tn" else a.shape[::-1]
    N = b.shape[0] if mode == "nt" else b.shape[1]
    tm, tn, tk = min(tm, M), min(tn, N), min(tk, K)
    assert M % tm == 0 and N % tn == 0 and K % tk == 0, (name, M, N, K, tm, tn, tk)
    nm, nn, nk = M // tm, N // tn, K // tk
    n_out, n_ext = len(out_dtypes), len(extras)
    contract = {"nn": ((1,), (0,)), "nt": ((1,), (1,)), "tn": ((0,), (0,))}[mode]

    def body(*refs):
        a_ref, b_ref = refs[:2]
        ext_refs = refs[2:2 + n_ext]
        out_refs = refs[2 + n_ext + n_after:2 + n_ext + n_after + n_out]
        acc_ref = refs[2 + n_ext + n_after + n_out] if nk > 1 else None
        av, bv = a_ref[...], b_ref[...]
        if a_pre is not None:
            av = a_pre(av)
        if precision is None:
            av, bv = av.astype(BF16), bv.astype(BF16)
        part = lax.dot_general(av, bv, (contract, ((), ())), preferred_element_type=F32, precision=precision)

        def finish(acc):
            vals = (acc,) if epilogue is None else epilogue(acc, *[r[...] for r in ext_refs])
            for r, v in zip(out_refs, vals):
                r[...] = v.astype(r.dtype)

        if nk == 1:
            finish(part)
        else:
            k = pl.program_id(2)

            @pl.when(k == 0)
            def _():
                acc_ref[...] = part

            @pl.when(k > 0)
            def _():
                acc_ref[...] += part

            @pl.when(k == nk - 1)
            def _():
                finish(acc_ref[...])

    def at(index):
        return (lambda j, i, k: index(i, j, k)) if n_outer else index

    a_spec = pl.BlockSpec((tk, tm), at(lambda i, j, k: (k, i))) if mode == "tn" else pl.BlockSpec((tm, tk), at(lambda i, j, k: (i, k)))
    b_spec = pl.BlockSpec((tn, tk), at(lambda i, j, k: (j, k))) if mode == "nt" else pl.BlockSpec((tk, tn), at(lambda i, j, k: (k, j)))
    in_specs, block_bytes = [a_spec, b_spec], _nbytes((tm, tk), a.dtype) + _nbytes((tk, tn), b.dtype)
    for arr, kind in extras:
        if kind == "tile":
            assert arr.shape == (M, N), (name, arr.shape)
            in_specs.append(pl.BlockSpec((tm, tn), at(lambda i, j, k: (i, j))))
            block_bytes += _nbytes((tm, tn), arr.dtype)
        else:
            assert arr.shape == (1, N), (name, arr.shape)
            in_specs.append(pl.BlockSpec((1, tn), at(lambda i, j, k: (0, j))))
    in_specs += [ANY] * n_after
    if out_shards:
        assert n_out == 1 and tn * NDEV == N
        out_shape = [jax.ShapeDtypeStruct((NDEV, M, tn), out_dtypes[0])]
        out_specs = [pl.BlockSpec((None, tm, tn), at(lambda i, j, k: (j, i, 0)))]
    else:
        out_shape = [jax.ShapeDtypeStruct((M, N), d) for d in out_dtypes]
        out_specs = [pl.BlockSpec((tm, tn), at(lambda i, j, k: (i, j))) for _ in out_dtypes]
    block_bytes += sum(_nbytes((tm, tn), d) for d in out_dtypes)
    scratch = [pltpu.VMEM((tm, tn), F32)] if nk > 1 else []
    outs = pl.pallas_call(
        body, name=name, grid=(nn, nm, nk) if n_outer else (nm, nn, nk), in_specs=in_specs, out_specs=out_specs,
        out_shape=out_shape, scratch_shapes=scratch,
        compiler_params=_params(("parallel", "parallel", "arbitrary"), block_bytes, 2 * tm * tn * 4),
    )(a, b, *[arr for arr, _ in extras], *after)
    return outs[0] if n_out == 1 else outs


def _rowwise(fn, tiled, smalls, out_tiles, out_sums, *, name, ts=256):
    S = tiled[0].shape[0]
    ts = min(ts, S)
    assert S % ts == 0
    nt, ns, no, na = len(tiled), len(smalls), len(out_tiles), len(out_sums)

    def body(*refs):
        t_refs, s_refs = refs[:nt], refs[nt:nt + ns]
        o_refs, a_refs = refs[nt + ns:nt + ns + no], refs[nt + ns + no:]
        tile_vals, sum_vals = fn([r[...] for r in t_refs], [r[...] for r in s_refs])
        for r, v in zip(o_refs, tile_vals):
            r[...] = v.astype(r.dtype)

        @pl.when(pl.program_id(0) == 0)
        def _():
            for r in a_refs:
                r[...] = jnp.zeros_like(r)

        for r, v in zip(a_refs, sum_vals):
            r[...] += v

    in_specs = [pl.BlockSpec((ts, t.shape[1]), lambda i: (i, 0)) for t in tiled]
    in_specs += [pl.BlockSpec(s.shape, lambda i: (0, 0)) for s in smalls]
    out_specs = [pl.BlockSpec((ts, w), lambda i: (i, 0)) for w, _ in out_tiles]
    out_specs += [pl.BlockSpec((1, w), lambda i: (0, 0)) for w in out_sums]
    out_shape = [jax.ShapeDtypeStruct((S, w), d) for w, d in out_tiles]
    out_shape += [jax.ShapeDtypeStruct((1, w), F32) for w in out_sums]
    block_bytes = sum(_nbytes((ts, t.shape[1]), t.dtype) for t in tiled) + sum(_nbytes((ts, w), d) for w, d in out_tiles)
    width = max(t.shape[1] for t in tiled)
    outs = pl.pallas_call(
        body, name=name, grid=(S // ts,), in_specs=in_specs, out_specs=out_specs, out_shape=out_shape,
        compiler_params=_params(("arbitrary",), block_bytes, 6 * ts * width * 4),
    )(*tiled, *smalls)
    return outs[:no], outs[no:]


def _colsum(v):
    return jnp.sum(v, axis=0, keepdims=True)


def _rms_mod_fwd(x, gain, shift, scale, name):
    def fn(tiles, smalls):
        (xv,), (g, sh, sc) = tiles, smalls
        inv = lax.rsqrt(jnp.mean(xv * xv, axis=-1, keepdims=True) + RMS_EPS)
        h = (xv * inv) * g * (1.0 + sc) + sh
        return (h, inv), ()

    D = x.shape[1]
    (h, inv), _ = _rowwise(fn, [x], [gain, shift, scale], [(D, BF16), (1, F32)], [], name=name)
    return h, inv


def _gate_bwd(dx, y, gate, name):
    def fn(tiles, smalls):
        (dxv, yv), (g,) = tiles, smalls
        return (dxv * g,), (_colsum(dxv * yv),)

    D = dx.shape[1]
    (dy,), (dgate,) = _rowwise(fn, [dx, y], [gate], [(D, BF16)], [D], name=name)
    return dy, dgate


def _rms_mod_bwd(dh, x, inv, dx_res, gain, scale, name):
    def fn(tiles, smalls):
        (dhv, xv, iv, dres), (g, sc) = tiles, smalls
        dhv = dhv.astype(F32)
        xhat = xv * iv
        dr = dhv * (1.0 + sc)
        dxhat = dr * g
        dxn = iv * (dxhat - xhat * jnp.mean(dxhat * xhat, axis=-1, keepdims=True))
        return (dres + dxn,), (_colsum(dhv), _colsum(dhv * (xhat * g)), _colsum(dr * xhat))

    D = x.shape[1]
    (dx,), (dsh, dsc, dgain) = _rowwise(fn, [dh, x, inv, dx_res], [gain, scale], [(D, F32)], [D, D, D], name=name)
    return dx, dsh, dsc, dgain


def _final_loss_bwd(x, target, gain, name):
    D = x.shape[1]

    def fn(tiles, smalls):
        (xv, tv), (g,) = tiles, smalls
        inv = lax.rsqrt(jnp.mean(xv * xv, axis=-1, keepdims=True) + RMS_EPS)
        xhat = xv * inv
        err = xhat * g - tv
        loss = 0.5 * jnp.sum(jnp.mean(err * err, axis=-1, keepdims=True), axis=0, keepdims=True)
        dout = err * (1.0 / D)
        dxhat = dout * g
        dxv = inv * (dxhat - xhat * jnp.mean(dxhat * xhat, axis=-1, keepdims=True))
        return (dxv,), (_colsum(dout * xhat), jnp.broadcast_to(loss, (1, LANES)))

    (dx,), (dgain, loss) = _rowwise(fn, [x, target], [gain], [(D, F32)], [D, LANES], name=name)
    return dx, dgain, loss


SCAN_BLOCK = 256


def _triangle(n, lower):
    r = lax.broadcasted_iota(jnp.int32, (n, n), 0)
    c = lax.broadcasted_iota(jnp.int32, (n, n), 1)
    return (r >= c if lower else r <= c).astype(F32)


def _forget_cumsum(logits, bias, name):
    S = logits.shape[0]
    blk = min(SCAN_BLOCK, S)
    nb = S // blk

    def body(z_ref, b_ref, f_ref):
        z = z_ref[...] + b_ref[...]
        f_ref[...] = jnp.minimum(z, 0.0) - jnp.log(1.0 + jnp.exp(-jnp.abs(z)))
        tri = _triangle(blk, lower=True)

        def step(i, carry):
            off = pl.multiple_of(i * blk, blk)
            cs = jnp.dot(tri, f_ref[pl.ds(off, blk), :], preferred_element_type=F32, precision=HIGHEST) + carry
            f_ref[pl.ds(off, blk), :] = cs
            return cs[blk - 1:blk, :]

        lax.fori_loop(0, nb, step, jnp.zeros((1, LANES), F32))

    return pl.pallas_call(body, name=name, out_shape=jax.ShapeDtypeStruct((S, LANES), F32))(logits, bias)


def _forget_bwd(dfk, logits, bias, name):
    S = logits.shape[0]
    blk = min(SCAN_BLOCK, S)
    nb = S // blk

    def body(d_ref, z_ref, b_ref, o_ref, db_ref):
        tri = _triangle(blk, lower=False)

        def step(t, carry):
            off = pl.multiple_of((nb - 1 - t) * blk, blk)
            cs = jnp.dot(tri, d_ref[pl.ds(off, blk), :], preferred_element_type=F32, precision=HIGHEST) + carry
            o_ref[pl.ds(off, blk), :] = cs
            return cs[0:1, :]

        lax.fori_loop(0, nb, step, jnp.zeros((1, LANES), F32))
        z = z_ref[...] + b_ref[...]
        dz = -o_ref[...] / (1.0 + jnp.exp(z))
        o_ref[...] = dz
        db_ref[...] = _colsum(dz)

    return pl.pallas_call(
        body, name=name,
        out_shape=(jax.ShapeDtypeStruct((S, LANES), F32), jax.ShapeDtypeStruct((1, LANES), F32)),
    )(dfk, logits, bias)


ATTN_BLOCK = 512
_NT = (((1,), (1,)), ((), ()))


def _attn_specs(S, H, tb):
    q_blk = lambda part: pl.BlockSpec((tb, HEAD_DIM), lambda h, i: (i, part * H + h))
    q_all = lambda part: pl.BlockSpec((S, HEAD_DIM), lambda h, i: (0, part * H + h))
    col_blk = pl.BlockSpec((None, tb, 1), lambda h, i: (h, i, 0))
    row_all = pl.BlockSpec((None, 1, S), lambda h, i: (h, 0, 0))
    return q_blk, q_all, col_blk, row_all


def _attn_params(S, tb):
    return _params(("parallel", "parallel"), 4 * S * HEAD_DIM * 2, 10 * tb * tb * 4)


def _attn_fwd(qkv, f_col, f_row, name):
    S, H = qkv.shape[0], qkv.shape[1] // (3 * HEAD_DIM)
    tb = min(ATTN_BLOCK, S)
    scale = HEAD_DIM ** -0.5
    q_blk, q_all, col_blk, row_all = _attn_specs(S, H, tb)

    def body(q_ref, k_ref, v_ref, fc_ref, fr_ref, o_ref, lse_ref):
        i = pl.program_id(1)
        q, fc = q_ref[...], fc_ref[...]

        def step(j, carry, diagonal):
            m, l, acc = carry
            off = pl.multiple_of(j * tb, tb)
            k, v = k_ref[pl.ds(off, tb), :], v_ref[pl.ds(off, tb), :]
            s = lax.dot_general(q, k, _NT, preferred_element_type=F32) * scale + (fc - fr_ref[:, pl.ds(off, tb)])
            if diagonal:
                row = lax.broadcasted_iota(jnp.int32, (tb, tb), 0)
                col = lax.broadcasted_iota(jnp.int32, (tb, tb), 1)
                s = jnp.where(col <= row, s, NEG)
            m_new = jnp.maximum(m, jnp.max(s, axis=-1, keepdims=True))
            p = jnp.exp(s - m_new)
            alpha = jnp.exp(m - m_new)
            l = alpha * l + jnp.sum(p, axis=-1, keepdims=True)
            acc = alpha * acc + jnp.dot(p.astype(BF16), v, preferred_element_type=F32)
            return m_new, l, acc

        init = (jnp.full((tb, 1), NEG, F32), jnp.zeros((tb, 1), F32), jnp.zeros((tb, HEAD_DIM), F32))
        carry = lax.fori_loop(0, i, lambda j, c: step(j, c, False), init)
        m, l, acc = step(i, carry, True)
        o_ref[...] = (acc / l).astype(o_ref.dtype)
        lse_ref[...] = m + jnp.log(l)

    return pl.pallas_call(
        body, name=name, grid=(H, S // tb),
        in_specs=[q_blk(0), q_all(1), q_all(2), col_blk, row_all],
        out_specs=[pl.BlockSpec((tb, HEAD_DIM), lambda h, i: (i, h)), col_blk],
        out_shape=[jax.ShapeDtypeStruct((S, H * HEAD_DIM), BF16), jax.ShapeDtypeStruct((H, S, 1), F32)],
        compiler_params=_attn_params(S, tb),
    )(qkv, qkv, qkv, f_col, f_row)


_TN = (((0,), (0,)), ((), ()))


def _attn_bwd(qkv, do, f_col, f_row, lse_col, name):
    S, H = qkv.shape[0], qkv.shape[1] // (3 * HEAD_DIM)
    tb = min(ATTN_BLOCK, S)
    nq = S // tb
    scale = HEAD_DIM ** -0.5
    q_blk, q_all, col_blk, row_all = _attn_specs(S, H, tb)
    head_blk = pl.BlockSpec((tb, HEAD_DIM), lambda h, i: (i, h))
    head_all = pl.BlockSpec((S, HEAD_DIM), lambda h, i: (0, h))

    def body(q_ref, k_ref, v_ref, do_ref, fc_ref, fr_ref, lse_ref, dq_ref, dk_ref, dv_ref, dfk_ref,
             p_buf, dp_buf, dk_acc, dv_acc, dfk_acc):
        i = pl.program_id(1)
        q, do, fc, lse = q_ref[...], do_ref[...], fc_ref[...], lse_ref[...]

        @pl.when(i == 0)
        def _():
            dk_acc[...] = jnp.zeros_like(dk_acc)
            dv_acc[...] = jnp.zeros_like(dv_acc)
            dfk_acc[...] = jnp.zeros_like(dfk_acc)

        def scores(j, delta, diagonal):
            off = pl.multiple_of(j * tb, tb)
            k, v = k_ref[pl.ds(off, tb), :], v_ref[pl.ds(off, tb), :]
            s = lax.dot_general(q, k, _NT, preferred_element_type=F32) * scale + (fc - fr_ref[:, pl.ds(off, tb)])
            if diagonal:
                row = lax.broadcasted_iota(jnp.int32, (tb, tb), 0)
                col = lax.broadcasted_iota(jnp.int32, (tb, tb), 1)
                s = jnp.where(col <= row, s, NEG)
            p = jnp.exp(s - lse)
            dp = lax.dot_general(do, v, _NT, preferred_element_type=F32)
            p_buf[j] = p
            dp_buf[j] = dp
            return delta + jnp.sum(p * dp, axis=-1, keepdims=True)

        delta = lax.fori_loop(0, i, lambda j, c: scores(j, c, False), jnp.zeros((tb, 1), F32))
        delta = scores(i, delta, True)

        def grad(j, dq):
            off = pl.multiple_of(j * tb, tb)
            p = p_buf[j]
            ds = p * (dp_buf[j] - delta)
            ds_lo = ds.astype(BF16)
            dk_acc[pl.ds(off, tb), :] += lax.dot_general(ds_lo, q, _TN, preferred_element_type=F32)
            dv_acc[pl.ds(off, tb), :] += lax.dot_general(p.astype(BF16), do, _TN, preferred_element_type=F32)
            dfk_acc[:, pl.ds(off, tb)] += jnp.sum(ds, axis=0, keepdims=True)
            return dq + jnp.dot(ds_lo, k_ref[pl.ds(off, tb), :], preferred_element_type=F32)

        dq = lax.fori_loop(0, i + 1, grad, jnp.zeros((tb, HEAD_DIM), F32))
        dq_ref[...] = (dq * scale).astype(dq_ref.dtype)

        @pl.when(i == nq - 1)
        def _():
            dk_ref[...] = (dk_acc[...] * scale).astype(dk_ref.dtype)
            dv_ref[...] = dv_acc[...].astype(dv_ref.dtype)
            dfk_ref[...] = dfk_acc[...]

    wide = jax.ShapeDtypeStruct((S, H * HEAD_DIM), BF16)
    return pl.pallas_call(
        body, name=name, grid=(H, nq),
        in_specs=[q_blk(0), q_all(1), q_all(2), head_blk, col_blk, row_all, col_blk],
        out_specs=[head_blk, head_all, head_all, row_all],
        out_shape=[wide, wide, wide, jax.ShapeDtypeStruct((H, 1, S), F32)],
        scratch_shapes=[pltpu.VMEM((nq, tb, tb), F32), pltpu.VMEM((nq, tb, tb), F32),
                        pltpu.VMEM((S, HEAD_DIM), F32), pltpu.VMEM((S, HEAD_DIM), F32), pltpu.VMEM((1, S), F32)],
        compiler_params=_params(("parallel", "arbitrary"), 6 * S * HEAD_DIM * 2,
                                2 * nq * tb * tb * 4 + 2 * S * HEAD_DIM * 4 + 10 * tb * tb * 4),
    )(qkv, qkv, qkv, do, f_col, f_row, lse_col)


CONV_TILE = 128


def _shift_down(v, n):
    row = lax.broadcasted_iota(jnp.int32, v.shape, 0)
    return jnp.where(row >= n, pltpu.roll(v, n, 0), 0.0)


def _shift_up(v, n):
    S = v.shape[0]
    row = lax.broadcasted_iota(jnp.int32, v.shape, 0)
    return jnp.where(row < S - n, pltpu.roll(v, S - n, 0), 0.0)


def _conv_specs(S, D, tc):
    nb = D // tc
    part = lambda p: pl.BlockSpec((S, tc), lambda j: (0, p * nb + j))
    return part, pl.BlockSpec((S, tc), lambda j: (0, j)), pl.BlockSpec((8, tc), lambda j: (0, j))


def _conv_fwd(proj, conv_w8, name):
    S, D = proj.shape[0], proj.shape[1] // 3
    tc = min(CONV_TILE, D)
    part, chan, taps = _conv_specs(S, D, tc)

    def body(b_ref, c_ref, u_ref, w_ref, z_ref):
        cu = c_ref[...].astype(F32) * u_ref[...].astype(F32)
        w = w_ref[...]
        y = w[0:1, :] * _shift_down(cu, 2) + w[1:2, :] * _shift_down(cu, 1) + w[2:3, :] * cu
        z_ref[...] = (b_ref[...].astype(F32) * y).astype(z_ref.dtype)

    return pl.pallas_call(
        body, name=name, grid=(D // tc,), in_specs=[part(0), part(1), part(2), taps], out_specs=chan,
        out_shape=jax.ShapeDtypeStruct((S, D), BF16),
        compiler_params=_params(("parallel",), 3 * _nbytes((S, tc), proj.dtype) + S * tc * 2, 6 * S * tc * 4),
    )(proj, proj, proj, conv_w8)


def _conv_bwd(proj, dz, conv_w8, name):
    S, D = proj.shape[0], proj.shape[1] // 3
    tc = min(CONV_TILE, D)
    part, chan, taps = _conv_specs(S, D, tc)

    def body(b_ref, c_ref, u_ref, dz_ref, w_ref, db_ref, dc_ref, du_ref, dw_ref):
        cv, uv = c_ref[...].astype(F32), u_ref[...].astype(F32)
        dzv, w = dz_ref[...].astype(F32), w_ref[...]
        cu = cv * uv
        cu1, cu2 = _shift_down(cu, 1), _shift_down(cu, 2)
        y = w[0:1, :] * cu2 + w[1:2, :] * cu1 + w[2:3, :] * cu
        db_ref[...] = (dzv * y).astype(db_ref.dtype)
        dy = dzv * b_ref[...].astype(F32)
        dcu = w[2:3, :] * dy + w[1:2, :] * _shift_up(dy, 1) + w[0:1, :] * _shift_up(dy, 2)
        dc_ref[...] = (dcu * uv).astype(dc_ref.dtype)
        du_ref[...] = (dcu * cv).astype(du_ref.dtype)
        dw_ref[...] = jnp.concatenate(
            [_colsum(dy * cu2), _colsum(dy * cu1), _colsum(dy * cu), jnp.zeros((8 - CONV_WIDTH, tc), F32)], axis=0)

    return pl.pallas_call(
        body, name=name, grid=(D // tc,), in_specs=[part(0), part(1), part(2), chan, taps],
        out_specs=[chan, chan, chan, taps],
        out_shape=[jax.ShapeDtypeStruct((S, D), BF16)] * 3 + [jax.ShapeDtypeStruct((8, D), F32)],
        compiler_params=_params(("parallel",), 3 * _nbytes((S, tc), proj.dtype) + _nbytes((S, tc), dz.dtype)
                                + 3 * S * tc * 2, 10 * S * tc * 4),
    )(proj, proj, proj, dz, conv_w8)


def _adamw(w, m, v, parts, name, layer=0, prev=None):
    L, R, C = w.shape
    P = parts.shape[0]
    assert parts.shape[1:] == (R, C), (name, parts.shape, w.shape)
    row_bytes = C * (12 + 16 + P * parts.dtype.itemsize)
    tr = R if R * row_bytes <= (4 << 20) else max(8, ((4 << 20) // row_bytes) // 8 * 8)
    while R % tr:
        tr -= 8
    c1, c2 = 1.0 - ADAM_B1 ** ADAM_STEP, 1.0 - ADAM_B2 ** ADAM_STEP

    def body(w_ref, m_ref, v_ref, p_ref, *rest):
        g_ref, d_ref, nm_ref, nv_ref = rest[-4:]
        g = p_ref[0].astype(F32)
        for p in range(1, P):
            g = g + p_ref[p].astype(F32)
        nm = ADAM_B1 * m_ref[...] + (1.0 - ADAM_B1) * g
        nv = ADAM_B2 * v_ref[...] + (1.0 - ADAM_B2) * (g * g)
        g_ref[...] = g
        nm_ref[...] = nm
        nv_ref[...] = nv
        d_ref[...] = -ADAM_LR * ((nm / c1) / (jnp.sqrt(nv / c2) + ADAM_EPS) + ADAM_WD * w_ref[...])

    blk = pl.BlockSpec((None, tr, C), lambda i: (layer, i, 0))
    prev = [] if prev is None else list(prev)
    return pl.pallas_call(
        body, name=name, grid=(R // tr,),
        in_specs=[blk, blk, blk, pl.BlockSpec((P, tr, C), lambda i: (0, i, 0))] + [ANY] * len(prev),
        out_specs=[blk] * 4, out_shape=[jax.ShapeDtypeStruct((L, R, C), F32)] * 4,
        input_output_aliases={4 + k: k for k in range(len(prev))},
        compiler_params=_params(("parallel",), tr * row_bytes),
    )(w, m, v, parts, *prev)


def _silu(v):
    return v / (1.0 + jnp.exp(-v))


def _pad_rows(a, rows):
    return jnp.pad(a, ((0, rows - a.shape[0]), (0, 0)))


def _pad_cols(a, cols):
    return jnp.pad(a, ((0, 0), (0, cols - a.shape[1])))


def kernel(x, c, ada_w, ada_b, norm_mix, norm_mlp, fox_w_in, fox_b_f, fox_w_out, conv_w_in, conv_w, conv_w_out, mlp_w_up, mlp_w_down, final_norm, loss_target, m_ada_w, m_ada_b, m_norm_mix, m_norm_mlp, m_fox_w_in, m_fox_b_f, m_fox_w_out, m_conv_w_in, m_conv_w, m_conv_w_out, m_mlp_w_up, m_mlp_w_down, m_final_norm, v_ada_w, v_ada_b, v_norm_mix, v_norm_mlp, v_fox_w_in, v_fox_b_f, v_fox_w_out, v_conv_w_in, v_conv_w, v_conv_w_out, v_mlp_w_up, v_mlp_w_down, v_final_norm):
    S, D = x.shape[1], x.shape[2]
    H = D // HEAD_DIM
    FF = mlp_w_up.shape[2] * NDEV
    depth = ada_w.shape[0]
    n_mod = 6
    assert depth == 2 and fox_w_in.shape[0] == 1 and conv_w_in.shape[0] == 1 and H <= LANES
    me = _my_index()
    x0, target = x[0], loss_target[0]
    row = lambda vec: vec.reshape(1, -1)

    def tied(vec, token):
        return vec + token[0, 0]

    bf = lambda w: w.astype(BF16)
    gather_groups = {
        "fox": [bf(fox_w_in[0]), bf(fox_w_out[0])],
        "mlp0": [bf(mlp_w_up[0]).T, bf(mlp_w_down[0])],
        "conv": [bf(conv_w_in[0]).T, conv_w[0], bf(conv_w_out[0])],
        "mlp1": [bf(mlp_w_up[1]).T, bf(mlp_w_down[1])],
    }

    def start_gather(group, after):
        return _exchange_start(gather_groups[group], f"gather_{group}_start", False, after, relay=True)

    def relay_gather(handle, group, after):
        handle = _exchange_wait(handle, f"gather_{group}_arrivals", after, arrivals_only=True)
        return _relay_forward_start(handle, f"gather_{group}_forward")

    def finish_gather(handle, group, after):
        return _exchange_wait(handle, f"gather_{group}_wait", after)

    landed = lambda handle: handle[4][0]

    c_all = _all_gather([c], "gather_cond")[0].reshape(NDEV, D)
    ncol = ada_w.shape[2]
    ada_b_mine = lax.dynamic_slice_in_dim(ada_b, me * ncol, ncol, axis=1)
    mod_cols = jnp.stack([
        _matmul(c_all, ada_w[i], mode="nn", name=f"ada_fwd_{i}", out_dtypes=[F32], tm=NDEV, tn=ncol // 2, tk=D,
                a_pre=_silu, precision=HIGHEST, epilogue=lambda acc, b: (acc + b,), extras=[(ada_b_mine[i:i + 1], "row")])
        for i in range(depth)])
    mod_all = _all_gather([mod_cols], "gather_mod")[0]
    mod = lax.dynamic_index_in_dim(mod_all, me, axis=2, keepdims=False)
    fox_handle, token = start_gather("fox", mod_all)
    mod = tied(mod, token).transpose(1, 0, 2).reshape(depth, n_mod, 1, D)
    sh_mix, sc_mix, g_mix, sh_mlp, sc_mlp, g_mlp = (mod[:, k] for k in range(n_mod))
    b_f = _pad_cols(fox_b_f, LANES)

    def residual(acc, x_in, gate):
        return (x_in + gate * acc, acc)

    def mlp_fwd(i, x_in, handle, relay_next=None):
        h, inv = _rms_mod_fwd(x_in, row(norm_mlp[i]), sh_mlp[i], sc_mlp[i], f"mlp_norm_{i}")
        w_up_t, w_down = finish_gather(handle, f"mlp{i}", h)
        w_up_t, w_down = w_up_t.reshape(FF, D), w_down.reshape(FF, D)
        r, a = _matmul(h, w_up_t, mode="nt", name=f"mlp_up_{i}", out_dtypes=[BF16, BF16], tm=1024, tn=1024, tk=D,
                       epilogue=lambda acc: (jnp.maximum(acc, 0.0), jnp.square(jnp.maximum(acc, 0.0))))
        next_handle = relay_gather(relay_next[1], relay_next[0], a) if relay_next else None
        x_out, y = _matmul(a, w_down, mode="nn", name=f"mlp_down_{i}", out_dtypes=[F32, F32], tm=256, tn=512, tk=FF,
                           n_outer=True, epilogue=residual, extras=[(x_in, "tile"), (g_mlp[i], "row")],
                           after=[landed(next_handle)] if relay_next else [])
        return x_out, (x_in, h, inv, r, a, y, w_up_t, w_down), next_handle

    def mlp_bwd(i, dx, saved, gate):
        x_in, h, inv, r, a, y, w_up_t, w_down = saved
        dy, dgate = _gate_bwd(dx, y, gate, f"mlp_gate_bwd_{i}")
        du = _matmul(dy, w_down, mode="nt", name=f"mlp_down_bwd_{i}", out_dtypes=[BF16], tm=1024, tn=1024, tk=D,
                     epilogue=lambda acc, rv: (acc * (2.0 * rv.astype(F32)),), extras=[(r, "tile")])
        d_down = _matmul(a, dy, mode="tn", name=f"mlp_down_wgrad_{i}", out_dtypes=[BF16], tm=512, tn=1024, tk=S)
        dh = _matmul(du, w_up_t, mode="nn", name=f"mlp_up_bwd_{i}", out_dtypes=[F32], tm=256, tn=512, tk=FF, n_outer=True)
        d_up = _matmul(h, du, mode="tn", name=f"mlp_up_wgrad_{i}", out_dtypes=[BF16], tm=512, tn=FF // NDEV, tk=S,
                       out_shards=True)
        dx, dsh, dsc, dgain = _rms_mod_bwd(dh, x_in, inv, dx, row(norm_mlp[i]), sc_mlp[i], f"mlp_norm_bwd_{i}")
        handle, token = _exchange_start([d_up, d_down.reshape(NDEV, FF // NDEV, D)], f"scatter_mlp{i}_start", True, dx)
        return dx, (dsh, dsc, dgate, dgain), handle, token

    h0, inv0 = _rms_mod_fwd(x0, row(norm_mix[0]), sh_mix[0], sc_mix[0], "fox_norm")
    fox_in, w_fox_out = finish_gather(relay_gather(fox_handle, "fox", h0), "fox", h0)
    mlp0_handle, token = start_gather("mlp0", fox_in)
    fox_in = fox_in.transpose(1, 0, 2).reshape(D, 3 * D + H)
    w_qkv, w_f = fox_in[:, :3 * D], _pad_cols(fox_in[:, 3 * D:], LANES)
    w_fox_out = w_fox_out.reshape(D, D)
    qkv = _matmul(h0, w_qkv, mode="nn", name="fox_qkv", out_dtypes=[BF16], tm=1024, tn=1024, tk=D, after=[token])
    f_logit = _matmul(h0, w_f, mode="nn", name="fox_forget_logits", out_dtypes=[F32], tm=1024, tn=LANES, tk=D)
    f_cum = _forget_cumsum(f_logit, b_f, "fox_forget_cumsum")
    f_heads = f_cum[:, :H].T
    f_col, f_row = f_heads.reshape(H, S, 1), f_heads.reshape(H, 1, S)
    o, lse = _attn_fwd(qkv, f_col, f_row, "fox_attention")
    mlp0_handle = relay_gather(mlp0_handle, "mlp0", o)
    conv_handle, token = start_gather("conv", landed(mlp0_handle))
    mlp1_handle, token = start_gather("mlp1", token)
    x1, mix0 = _matmul(o, w_fox_out, mode="nn", name="fox_out", out_dtypes=[F32, F32], tm=512, tn=1024, tk=D,
                       epilogue=residual, extras=[(x0, "tile"), (g_mix[0], "row")], after=[token])
    x2, mlp0, conv_handle = mlp_fwd(0, x1, mlp0_handle, ("conv", conv_handle))

    h1, inv1 = _rms_mod_fwd(x2, row(norm_mix[1]), sh_mix[1], sc_mix[1], "conv_norm")
    w_conv_in_t, w_taps, w_conv_out = finish_gather(conv_handle, "conv", h1)
    w_conv_in_t = w_conv_in_t.reshape(3 * D, D)
    w_taps = _pad_rows(w_taps.transpose(1, 0, 2).reshape(CONV_WIDTH, D), 8)
    w_conv_out = w_conv_out.reshape(D, D)
    proj = _matmul(h1, w_conv_in_t, mode="nt", name="conv_in", out_dtypes=[F32], tm=1024, tn=1024, tk=D)
    mlp1_handle = relay_gather(mlp1_handle, "mlp1", proj)
    z = _conv_fwd(proj, w_taps, "conv_mix")
    x3, mix1 = _matmul(z, w_conv_out, mode="nn", name="conv_out", out_dtypes=[F32, F32], tm=512, tn=1024, tk=D,
                       epilogue=residual, extras=[(x2, "tile"), (g_mix[1], "row")], after=[landed(mlp1_handle)])
    x4, mlp1, _ = mlp_fwd(1, x3, mlp1_handle)

    dx, d_final, loss_lanes = _final_loss_bwd(x4, target, row(final_norm), "loss_head")

    dx, dmod_mlp1, mlp1_scatter, token = mlp_bwd(1, dx, mlp1, g_mlp[1])
    dmix, dg_mix1 = _gate_bwd(dx, mix1, tied(g_mix[1], token), "conv_gate_bwd")
    dz = _matmul(dmix, w_conv_out, mode="nt", name="conv_out_bwd", out_dtypes=[F32], tm=1024, tn=1024, tk=D)
    d_conv_out = _matmul(z, dmix, mode="tn", name="conv_out_wgrad", out_dtypes=[BF16], tm=512, tn=1024, tk=S)
    db, dc, du, d_taps = _conv_bwd(proj, dz, w_taps, "conv_mix_bwd")
    dproj = jnp.concatenate([db, dc, du], axis=1)
    dh1 = _matmul(dproj, w_conv_in_t, mode="nn", name="conv_in_bwd", out_dtypes=[F32], tm=512, tn=512, tk=3 * D, n_outer=True)
    d_conv_in = _matmul(h1, dproj, mode="tn", name="conv_in_wgrad", out_dtypes=[BF16], tm=512, tn=3 * D // NDEV, tk=S,
                        out_shards=True)
    dx, dsh1, dsc1, dgain_mix1 = _rms_mod_bwd(dh1, x2, inv1, dx, row(norm_mix[1]), sc_mix[1], "conv_norm_bwd")
    d_taps_split = d_taps[:CONV_WIDTH].reshape(CONV_WIDTH, NDEV, -1).transpose(1, 0, 2)
    conv_scatter, token = _exchange_start([d_conv_in, d_taps_split, d_conv_out.reshape(NDEV, D // NDEV, D)],
                                          "scatter_conv_start", True, dx)

    dx, dmod_mlp0, mlp0_scatter, token = mlp_bwd(0, dx, mlp0, tied(g_mlp[0], token))
    dmix, dg_mix0 = _gate_bwd(dx, mix0, tied(g_mix[0], token), "fox_gate_bwd")
    do = _matmul(dmix, w_fox_out, mode="nt", name="fox_out_bwd", out_dtypes=[BF16], tm=1024, tn=1024, tk=D)
    d_fox_out = _matmul(o, dmix, mode="tn", name="fox_out_wgrad", out_dtypes=[BF16], tm=512, tn=1024, tk=S)
    dq, dk, dv, dfk = _attn_bwd(qkv, do, f_col, f_row, lse, "fox_attention_bwd")
    dqkv = jnp.concatenate([dq, dk, dv], axis=1)
    dfk_lanes = _pad_cols(dfk.reshape(H, S).T, LANES)
    df_logit, db_f = _forget_bwd(dfk_lanes, f_logit, b_f, "fox_forget_bwd")
    d_qkv = _matmul(h0, dqkv, mode="tn", name="fox_qkv_wgrad", out_dtypes=[BF16], tm=512, tn=1024, tk=S)
    d_f = _matmul(h0, df_logit, mode="tn", name="fox_forget_wgrad", out_dtypes=[BF16], tm=512, tn=LANES, tk=S)
    d_fox_in = jnp.concatenate([d_qkv, d_f[:, :H]], axis=1).reshape(D, NDEV, -1).transpose(1, 0, 2)
    fox_scatter, token = _exchange_start([d_fox_in, d_fox_out.reshape(NDEV, D // NDEV, D)], "scatter_fox_start", True,
                                         d_fox_in)
    dh0_f = _matmul(df_logit, w_f, mode="nt", name="fox_forget_logits_bwd", out_dtypes=[F32], tm=1024, tn=1024, tk=LANES,
                    after=[token])
    dh0 = _matmul(dqkv, w_qkv, mode="nt", name="fox_qkv_bwd", out_dtypes=[F32], tm=512, tn=512, tk=3 * D,
                  epilogue=lambda acc, extra: (acc + extra,), extras=[(dh0_f, "tile")])
    dx, dsh0, dsc0, dgain_mix0 = _rms_mod_bwd(dh0, x0, inv0, dx, row(norm_mix[0]), sc_mix[0], "fox_norm_bwd")
    grad_x = dx.reshape(1, S, D)

    up1, down1 = _exchange_wait(mlp1_scatter, "scatter_mlp1_wait", dx)
    up_out = _adamw(mlp_w_up, m_mlp_w_up, v_mlp_w_up, up1, "adamw_mlp_w_up_1", layer=1)
    down_out = _adamw(mlp_w_down, m_mlp_w_down, v_mlp_w_down, down1, "adamw_mlp_w_down_1", layer=1)
    cin, taps, cout = _exchange_wait(conv_scatter, "scatter_conv_wait", down_out[0])
    conv_in_out = _adamw(conv_w_in, m_conv_w_in, v_conv_w_in, cin, "adamw_conv_w_in")
    conv_w_res = _adamw(conv_w, m_conv_w, v_conv_w, taps, "adamw_conv_w")
    conv_out_out = _adamw(conv_w_out, m_conv_w_out, v_conv_w_out, cout, "adamw_conv_w_out")
    up0, down0 = _exchange_wait(mlp0_scatter, "scatter_mlp0_wait", conv_out_out[0])
    up_out = _adamw(mlp_w_up, m_mlp_w_up, v_mlp_w_up, up0, "adamw_mlp_w_up_0", layer=0, prev=up_out)
    down_out = _adamw(mlp_w_down, m_mlp_w_down, v_mlp_w_down, down0, "adamw_mlp_w_down_0", layer=0, prev=down_out)

    dmod = jnp.concatenate([
        jnp.concatenate([dsh0, dsc0, dg_mix0, dmod_mlp0[0], dmod_mlp0[1], dmod_mlp0[2]], axis=1),
        jnp.concatenate([dsh1, dsc1, dg_mix1, dmod_mlp1[0], dmod_mlp1[1], dmod_mlp1[2]], axis=1)], axis=0)
    small_sizes = [depth * n_mod * D, depth * D, depth * D, H, D, 1]
    n_small = sum(small_sizes)
    n_rows = -(-n_small // (8 * LANES)) * 8

    def pack(parts):
        flat = jnp.concatenate([p.reshape(-1) for p in parts])
        return jnp.pad(flat, (0, n_rows * LANES - n_small)).reshape(n_rows, LANES)

    def unpack(packed, shapes):
        flat, out, at = packed.reshape(-1), [], 0
        for size, shape in zip(small_sizes, shapes):
            out.append(flat[at:at + size].reshape(shape))
            at += size
        return out

    small_partial = pack([dmod, jnp.concatenate([dgain_mix0, dgain_mix1], axis=0),
                          jnp.concatenate([dmod_mlp0[3], dmod_mlp1[3]], axis=0), db_f[0, :H], d_final, loss_lanes[0, :1]])
    small_parts = _all_gather([small_partial], "gather_small_grads", after=[down_out[0]])[0]
    small_shapes = [ada_b.shape, norm_mix.shape, norm_mlp.shape, fox_b_f.shape, final_norm.shape]
    loss = jnp.sum(small_parts.reshape(NDEV, -1)[:, n_small - 1])
    unused = jnp.zeros((1,), F32)
    small_out = _adamw(pack([ada_b, norm_mix, norm_mlp, fox_b_f, final_norm, unused])[None],
                       pack([m_ada_b, m_norm_mix, m_norm_mlp, m_fox_b_f, m_final_norm, unused])[None],
                       pack([v_ada_b, v_norm_mix, v_norm_mlp, v_fox_b_f, v_final_norm, unused])[None], small_parts,
                       "adamw_small")
    small_out = [unpack(t, small_shapes) for t in small_out]

    dmod_all = small_parts.reshape(NDEV, -1)[:, :depth * n_mod * D].reshape(NDEV, depth, n_mod * D)
    dmod_mine = lax.dynamic_slice_in_dim(dmod_all, me * ncol, ncol, axis=2)
    ada_out = None
    for i in range(depth):
        d_ada = _matmul(c_all, dmod_mine[:, i], mode="tn", name=f"ada_wgrad_{i}", out_dtypes=[F32], tm=1024, tn=ncol // 2,
                        tk=NDEV, a_pre=_silu, precision=HIGHEST)
        ada_out = _adamw(ada_w, m_ada_w, v_ada_w, d_ada[None], f"adamw_ada_w_{i}", layer=i, prev=ada_out)

    fin, fout = _exchange_wait(fox_scatter, "scatter_fox_wait", ada_out[0])
    fox_in_out = _adamw(fox_w_in, m_fox_w_in, v_fox_w_in, fin, "adamw_fox_w_in")
    fox_out_out = _adamw(fox_w_out, m_fox_w_out, v_fox_w_out, fout, "adamw_fox_w_out")

    outputs = [loss, grad_x]
    for kind in range(4):
        sm = small_out[kind]
        outputs += [ada_out[kind], sm[0], sm[1], sm[2], fox_in_out[kind], sm[3], fox_out_out[kind], conv_in_out[kind],
                    conv_w_res[kind], conv_out_out[kind], up_out[kind], down_out[kind], sm[4]]
    return tuple(outputs)
```

```python
import math

import jax
import jax.numpy as jnp
from jax import lax
from jax.experimental import pallas as pl
from jax.experimental.pallas import tpu as pltpu

F32 = jnp.float32
BF16 = jnp.bfloat16
MESH = pl.DeviceIdType.MESH
NDEV = 8
HEAD_DIM = 128
LANES = 128
CONV_WIDTH = 3
RMS_EPS = 1e-6
ADAM_LR, ADAM_B1, ADAM_B2, ADAM_EPS, ADAM_WD, ADAM_STEP = 0.001, 0.9, 0.999, 1e-08, 0.01, 10
NEG = -1e30
V7X_VMEM_BYTES = 64 * 1024 * 1024
VMEM_HEADROOM = 12 * 1024 * 1024
HBM = pl.BlockSpec(memory_space=pltpu.HBM)
HIGHEST = lax.Precision.HIGHEST


def _nbytes(shape, dtype):
    return math.prod(shape) * jnp.dtype(dtype).itemsize


def _params(semantics, block_bytes, temp_bytes=0):
    limit = min(2 * block_bytes + temp_bytes + VMEM_HEADROOM, V7X_VMEM_BYTES - 4 * 1024 * 1024)
    return pltpu.CompilerParams(dimension_semantics=semantics, vmem_limit_bytes=int(limit))


def _my_index():
    return lax.axis_index("x") * 4 + lax.axis_index("y") * 2 + lax.axis_index("c")


def _peer(r):
    x, y, c = lax.axis_index("x"), lax.axis_index("y"), lax.axis_index("c")
    px = 1 - x if (r >> 2) & 1 else x
    py = 1 - y if (r >> 1) & 1 else y
    pc = 1 - c if r & 1 else c
    return (px, py, pc), px * 4 + py * 2 + pc


def _exchange(arrays, name, scatter, after=None):
    n = len(arrays)
    after = [] if after is None else list(after)

    def body(*refs):
        ins, outs = refs[:n], refs[n + len(after):2 * n + len(after)]
        send_sems, recv_sems, local_sems = refs[2 * n + len(after):]
        me = _my_index()
        local = []
        for a in range(n):
            src = ins[a].at[me] if scatter else ins[a]
            local.append(pltpu.make_async_copy(src, outs[a].at[me], local_sems.at[a]))
            local[-1].start()
        sends = []
        for r in range(1, NDEV):
            peer, pidx = _peer(r)
            for a in range(n):
                src = ins[a].at[pidx] if scatter else ins[a]
                cp = pltpu.make_async_remote_copy(
                    src_ref=src, dst_ref=outs[a].at[me],
                    send_sem=send_sems.at[a * (NDEV - 1) + r - 1], recv_sem=recv_sems.at[a * (NDEV - 1) + r - 1],
                    device_id=peer, device_id_type=MESH)
                cp.start()
                sends.append(cp)
        for r in range(1, NDEV):
            peer, pidx = _peer(r)
            for a in range(n):
                src = ins[a].at[pidx] if scatter else ins[a]
                pltpu.make_async_remote_copy(
                    src_ref=src, dst_ref=outs[a].at[pidx],
                    send_sem=send_sems.at[a * (NDEV - 1) + r - 1], recv_sem=recv_sems.at[a * (NDEV - 1) + r - 1],
                    device_id=peer, device_id_type=MESH).wait_recv()
        for cp in sends:
            cp.wait_send()
        for cp in local:
            cp.wait()

    out_shape = [jax.ShapeDtypeStruct(a.shape if scatter else (NDEV,) + a.shape, a.dtype) for a in arrays]
    return pl.pallas_call(
        body, name=name, out_shape=out_shape, in_specs=[HBM] * n + [ANY] * len(after), out_specs=[HBM] * n,
        scratch_shapes=[pltpu.SemaphoreType.DMA((n * (NDEV - 1),)), pltpu.SemaphoreType.DMA((n * (NDEV - 1),)),
                        pltpu.SemaphoreType.DMA((n,))],
    )(*arrays, *after)


def _all_gather(arrays, name, after=None):
    return _exchange(arrays, name, scatter=False, after=after)


SEM = pl.BlockSpec(memory_space=pltpu.SEMAPHORE)
ANY = pl.BlockSpec(memory_space=pl.ANY)
DATAFLOW = pltpu.SideEffectType.DATAFLOW_SIDE_EFFECTING
TOKEN_SHAPE = (8, LANES)


SIBLING = 1
OTHER_CHIPS = (4, 2, 6)


def _exchange_start(arrays, name, scatter, after, relay=False):
    n = len(arrays)
    n_sems = n * (NDEV - 1)
    assert not (relay and scatter)

    def body(*refs):
        ins = refs[:n]
        send_sems, recv_sems = refs[n + 1], refs[n + 2]
        lands, token = refs[2 * n + 3:3 * n + 3], refs[3 * n + 3]
        me = _my_index()
        for r in (SIBLING, *OTHER_CHIPS) if relay else range(1, NDEV):
            peer, pidx = _peer(r)
            for a in range(n):
                src = ins[a].at[pidx] if scatter else ins[a]
                pltpu.make_async_remote_copy(
                    src_ref=src, dst_ref=lands[a].at[me],
                    send_sem=send_sems.at[a * (NDEV - 1) + r - 1], recv_sem=recv_sems.at[a * (NDEV - 1) + r - 1],
                    device_id=peer, device_id_type=MESH).start()
        token[...] = jnp.zeros(TOKEN_SHAPE, F32)

    land_shapes = [a.shape if scatter else (NDEV,) + a.shape for a in arrays]
    srcs = [pltpu.with_memory_space_constraint(a, pltpu.HBM) for a in arrays]
    outs = pl.pallas_call(
        body, name=name,
        out_shape=(pltpu.SemaphoreType.DMA((n_sems,)), pltpu.SemaphoreType.DMA((n_sems,)),
                   *[pltpu.HBM(a.shape, a.dtype) for a in arrays], *[pltpu.HBM(s, a.dtype) for s, a in zip(land_shapes, arrays)],
                   jax.ShapeDtypeStruct(TOKEN_SHAPE, F32)),
        in_specs=[HBM] * n + [ANY],
        out_specs=(SEM, SEM, *[HBM] * (2 * n), pl.BlockSpec(memory_space=pltpu.VMEM)),
        input_output_aliases={i: 2 + i for i in range(n)},
        compiler_params=pltpu.CompilerParams(has_side_effects=DATAFLOW),
    )(*srcs, after)
    return (scatter, relay, [(outs[0], outs[1])], list(outs[2:2 + n]), list(outs[2 + n:2 + 2 * n])), outs[-1]


def _relay_forward_start(handle, name):
    scatter, relay, sems, srcs, lands = handle
    n = len(lands)
    n_sems = n * len(OTHER_CHIPS)

    def body(*refs):
        land_refs, send_sems, recv_sems = refs[:n], refs[n], refs[n + 1]
        sibling, _ = _peer(SIBLING)
        for j, r in enumerate(OTHER_CHIPS):
            _, pidx = _peer(r)
            for a in range(n):
                pltpu.make_async_remote_copy(
                    src_ref=land_refs[a].at[pidx], dst_ref=land_refs[a].at[pidx],
                    send_sem=send_sems.at[a * len(OTHER_CHIPS) + j], recv_sem=recv_sems.at[a * len(OTHER_CHIPS) + j],
                    device_id=sibling, device_id_type=MESH).start()

    outs = pl.pallas_call(
        body, name=name,
        out_shape=(pltpu.SemaphoreType.DMA((n_sems,)), pltpu.SemaphoreType.DMA((n_sems,)),
                   *[pltpu.HBM(t.shape, t.dtype) for t in lands]),
        in_specs=[HBM] * n, out_specs=(SEM, SEM, *[HBM] * n),
        input_output_aliases={i: 2 + i for i in range(n)},
        compiler_params=pltpu.CompilerParams(has_side_effects=DATAFLOW),
    )(*lands)
    return (scatter, relay, sems + [(outs[0], outs[1])], srcs, list(outs[2:]))


def _exchange_wait(handle, name, after, arrivals_only=False):
    scatter, relay, sems, srcs, lands = handle
    n = len(srcs)
    forwarded = len(sems) == 2
    assert not arrivals_only or (relay and not forwarded)

    def body(*refs):
        src_refs, land_refs = refs[:n], refs[n:2 * n]
        send_sems, recv_sems = refs[2 * n], refs[2 * n + 1]
        for r in (SIBLING, *OTHER_CHIPS) if relay else range(1, NDEV):
            peer, pidx = _peer(r)
            for a in range(n):
                src = src_refs[a].at[pidx] if scatter else src_refs[a]
                cp = pltpu.make_async_remote_copy(
                    src_ref=src, dst_ref=land_refs[a].at[pidx],
                    send_sem=send_sems.at[a * (NDEV - 1) + r - 1], recv_sem=recv_sems.at[a * (NDEV - 1) + r - 1],
                    device_id=peer, device_id_type=MESH)
                if arrivals_only:
                    if r in OTHER_CHIPS:
                        cp.wait_recv()
                else:
                    cp.wait_send()
                    if not (relay and r in OTHER_CHIPS):
                        cp.wait_recv()
        if forwarded:
            fwd_send, fwd_recv = refs[2 * n + 2], refs[2 * n + 3]
            sibling, _ = _peer(SIBLING)
            for j, r in enumerate(OTHER_CHIPS):
                _, pidx = _peer(r ^ SIBLING)
                for a in range(n):
                    cp = pltpu.make_async_remote_copy(
                        src_ref=src_refs[a], dst_ref=land_refs[a].at[pidx],
                        send_sem=fwd_send.at[a * len(OTHER_CHIPS) + j], recv_sem=fwd_recv.at[a * len(OTHER_CHIPS) + j],
                        device_id=sibling, device_id_type=MESH)
                    cp.wait_send()
                    cp.wait_recv()

    flat_sems = [s for pair in sems for s in pair]
    outs = pl.pallas_call(
        body, name=name,
        out_shape=tuple(pltpu.HBM(t.shape, t.dtype) for t in (*srcs, *lands)),
        in_specs=[HBM] * (2 * n) + [SEM] * len(flat_sems) + [ANY], out_specs=tuple([HBM] * (2 * n)),
        input_output_aliases={i: i for i in range(2 * n)},
        compiler_params=pltpu.CompilerParams(has_side_effects=DATAFLOW),
    )(*srcs, *lands, *flat_sems, after)
    if arrivals_only:
        return (scatter, relay, sems, list(outs[:n]), list(outs[n:]))
    me = _my_index()
    mine = [lax.dynamic_index_in_dim(s, me, 0, keepdims=False) if scatter else s for s in outs[:n]]
    return [lax.dynamic_update_index_in_dim(land, own, me, 0) for land, own in zip(outs[n:], mine)]


def _matmul(a, b, *, mode, name, out_dtypes, tm, tn, tk, epilogue=None, extras=(), a_pre=None,
            out_shards=False, n_outer=False, precision=None, after=(), n=None):
    after = list(after)
    n_after = len(after)
    K, M = a.shape if mode == "tn" else a.shape[::-1]
    N = n if n is not None else (b.shape[0] if mode == "nt" else b.shape[1])
    tm, tn, tk = min(tm, M), min(tn, N), min(tk, K)
    assert M % tm == 0 and N % tn == 0 and K % tk == 0, (name, M, N, K, tm, tn, tk)
    nm, nn, nk = M // tm, N // tn, K // tk
    n_out, n_ext = len(out_dtypes), len(extras)
    contract = {"nn": ((1,), (0,)), "nt": ((1,), (1,)), "tn": ((0,), (0,))}[mode]

    def body(*refs):
        a_ref, b_ref = refs[:2]
        ext_refs = refs[2:2 + n_ext]
        out_refs = refs[2 + n_ext + n_after:2 + n_ext + n_after + n_out]
        acc_ref = refs[2 + n_ext + n_after + n_out] if nk > 1 else None
        av, bv = a_ref[...], b_ref[...]
        if a_pre is not None:
            av = a_pre(av)
        if precision is None:
            av, bv = av.astype(BF16), bv.astype(BF16)
        part = lax.dot_general(av, bv, (contract, ((), ())), preferred_element_type=F32, precision=precision)

        def finish(acc):
            vals = (acc,) if epilogue is None else epilogue(acc, *[r[...] for r in ext_refs])
            for r, v in zip(out_refs, vals):
                r[...] = v.astype(r.dtype)

        if nk == 1:
            finish(part)
        else:
            k = pl.program_id(2)

            @pl.when(k == 0)
            def _():
                acc_ref[...] = part

            @pl.when(k > 0)
            def _():
                acc_ref[...] += part

            @pl.when(k == nk - 1)
            def _():
                finish(acc_ref[...])

    def at(index):
        return (lambda j, i, k: index(i, j, k)) if n_outer else index

    a_spec = pl.BlockSpec((tk, tm), at(lambda i, j, k: (k, i))) if mode == "tn" else pl.BlockSpec((tm, tk), at(lambda i, j, k: (i, k)))
    b_spec = pl.BlockSpec((tn, tk), at(lambda i, j, k: (j, k))) if mode == "nt" else pl.BlockSpec((tk, tn), at(lambda i, j, k: (k, j)))
    in_specs, block_bytes = [a_spec, b_spec], _nbytes((tm, tk), a.dtype) + _nbytes((tk, tn), b.dtype)
    for arr, kind in extras:
        if kind == "tile":
            assert arr.shape == (M, N), (name, arr.shape)
            in_specs.append(pl.BlockSpec((tm, tn), at(lambda i, j, k: (i, j))))
            block_bytes += _nbytes((tm, tn), arr.dtype)
        else:
            assert arr.shape == (1, N), (name, arr.shape)
            in_specs.append(pl.BlockSpec((1, tn), at(lambda i, j, k: (0, j))))
    in_specs += [ANY] * n_after
    if out_shards:
        assert n_out == 1 and tn * NDEV == N
        out_shape = [jax.ShapeDtypeStruct((NDEV, M, tn), out_dtypes[0])]
        out_specs = [pl.BlockSpec((None, tm, tn), at(lambda i, j, k: (j, i, 0)))]
    else:
        out_shape = [jax.ShapeDtypeStruct((M, N), d) for d in out_dtypes]
        out_specs = [pl.BlockSpec((tm, tn), at(lambda i, j, k: (i, j))) for _ in out_dtypes]
    block_bytes += sum(_nbytes((tm, tn), d) for d in out_dtypes)
    scratch = [pltpu.VMEM((tm, tn), F32)] if nk > 1 else []
    outs = pl.pallas_call(
        body, name=name, grid=(nn, nm, nk) if n_outer else (nm, nn, nk), in_specs=in_specs, out_specs=out_specs,
        out_shape=out_shape, scratch_shapes=scratch,
        compiler_params=_params(("parallel", "parallel", "arbitrary"), block_bytes, 2 * tm * tn * 4),
    )(a, b, *[arr for arr, _ in extras], *after)
    return outs[0] if n_out == 1 else outs


def _rowwise(fn, tiled, smalls, out_tiles, out_sums, *, name, ts=256):
    S = tiled[0].shape[0]
    ts = min(ts, S)
    assert S % ts == 0
    nt, ns, no, na = len(tiled), len(smalls), len(out_tiles), len(out_sums)

    def body(*refs):
        t_refs, s_refs = refs[:nt], refs[nt:nt + ns]
        o_refs, a_refs = refs[nt + ns:nt + ns + no], refs[nt + ns + no:]
        tile_vals, sum_vals = fn([r[...] for r in t_refs], [r[...] for r in s_refs])
        for r, v in zip(o_refs, tile_vals):
            r[...] = v.astype(r.dtype)

        @pl.when(pl.program_id(0) == 0)
        def _():
            for r in a_refs:
                r[...] = jnp.zeros_like(r)

        for r, v in zip(a_refs, sum_vals):
            r[...] += v

    in_specs = [pl.BlockSpec((ts, t.shape[1]), lambda i: (i, 0)) for t in tiled]
    in_specs += [pl.BlockSpec(s.shape, lambda i: (0, 0)) for s in smalls]
    out_specs = [pl.BlockSpec((ts, w), lambda i: (i, 0)) for w, _ in out_tiles]
    out_specs += [pl.BlockSpec((1, w), lambda i: (0, 0)) for w in out_sums]
    out_shape = [jax.ShapeDtypeStruct((S, w), d) for w, d in out_tiles]
    out_shape += [jax.ShapeDtypeStruct((1, w), F32) for w in out_sums]
    block_bytes = sum(_nbytes((ts, t.shape[1]), t.dtype) for t in tiled) + sum(_nbytes((ts, w), d) for w, d in out_tiles)
    width = max(t.shape[1] for t in tiled)
    outs = pl.pallas_call(
        body, name=name, grid=(S // ts,), in_specs=in_specs, out_specs=out_specs, out_shape=out_shape,
        compiler_params=_params(("arbitrary",), block_bytes, 6 * ts * width * 4),
    )(*tiled, *smalls)
    return outs[:no], outs[no:]


def _colsum(v):
    return jnp.sum(v, axis=0, keepdims=True)


def _rms_mod_fwd(x, gain, shift, scale, name):
    def fn(tiles, smalls):
        (xv,), (g, sh, sc) = tiles, smalls
        inv = lax.rsqrt(jnp.mean(xv * xv, axis=-1, keepdims=True) + RMS_EPS)
        h = (xv * inv) * g * (1.0 + sc) + sh
        return (h, inv), ()

    D = x.shape[1]
    (h, inv), _ = _rowwise(fn, [x], [gain, shift, scale], [(D, BF16), (1, F32)], [], name=name)
    return h, inv


def _gate_bwd(dx, y, gate, name):
    def fn(tiles, smalls):
        (dxv, yv), (g,) = tiles, smalls
        return (dxv * g,), (_colsum(dxv * yv),)

    D = dx.shape[1]
    (dy,), (dgate,) = _rowwise(fn, [dx, y], [gate], [(D, BF16)], [D], name=name)
    return dy, dgate


def _rms_mod_bwd(dh, x, inv, dx_res, gain, scale, name):
    def fn(tiles, smalls):
        (dhv, xv, iv, dres), (g, sc) = tiles, smalls
        dhv = dhv.astype(F32)
        xhat = xv * iv
        dr = dhv * (1.0 + sc)
        dxhat = dr * g
        dxn = iv * (dxhat - xhat * jnp.mean(dxhat * xhat, axis=-1, keepdims=True))
        return (dres + dxn,), (_colsum(dhv), _colsum(dhv * (xhat * g)), _colsum(dr * xhat))

    D = x.shape[1]
    (dx,), (dsh, dsc, dgain) = _rowwise(fn, [dh, x, inv, dx_res], [gain, scale], [(D, F32)], [D, D, D], name=name)
    return dx, dsh, dsc, dgain


def _final_loss_bwd(x, target, gain, name):
    D = x.shape[1]

    def fn(tiles, smalls):
        (xv, tv), (g,) = tiles, smalls
        inv = lax.rsqrt(jnp.mean(xv * xv, axis=-1, keepdims=True) + RMS_EPS)
        xhat = xv * inv
        err = xhat * g - tv
        loss = 0.5 * jnp.sum(jnp.mean(err * err, axis=-1, keepdims=True), axis=0, keepdims=True)
        dout = err * (1.0 / D)
        dxhat = dout * g
        dxv = inv * (dxhat - xhat * jnp.mean(dxhat * xhat, axis=-1, keepdims=True))
        return (dxv,), (_colsum(dout * xhat), jnp.broadcast_to(loss, (1, LANES)))

    (dx,), (dgain, loss) = _rowwise(fn, [x, target], [gain], [(D, F32)], [D, LANES], name=name)
    return dx, dgain, loss


SCAN_BLOCK = 256


def _triangle(n, lower):
    r = lax.broadcasted_iota(jnp.int32, (n, n), 0)
    c = lax.broadcasted_iota(jnp.int32, (n, n), 1)
    return (r >= c if lower else r <= c).astype(F32)


def _forget_cumsum(logits, bias, name):
    S = logits.shape[0]
    blk = min(SCAN_BLOCK, S)
    nb = S // blk

    def body(z_ref, b_ref, f_ref):
        z = z_ref[...] + b_ref[...]
        f_ref[...] = jnp.minimum(z, 0.0) - jnp.log(1.0 + jnp.exp(-jnp.abs(z)))
        tri = _triangle(blk, lower=True)

        def step(i, carry):
            off = pl.multiple_of(i * blk, blk)
            cs = jnp.dot(tri, f_ref[pl.ds(off, blk), :], preferred_element_type=F32, precision=HIGHEST) + carry
            f_ref[pl.ds(off, blk), :] = cs
            return cs[blk - 1:blk, :]

        lax.fori_loop(0, nb, step, jnp.zeros((1, LANES), F32))

    return pl.pallas_call(body, name=name, out_shape=jax.ShapeDtypeStruct((S, LANES), F32))(logits, bias)


def _forget_bwd(dfk, logits, bias, name):
    S = logits.shape[0]
    blk = min(SCAN_BLOCK, S)
    nb = S // blk

    def body(d_ref, z_ref, b_ref, o_ref, db_ref):
        tri = _triangle(blk, lower=False)

        def step(t, carry):
            off = pl.multiple_of((nb - 1 - t) * blk, blk)
            cs = jnp.dot(tri, d_ref[pl.ds(off, blk), :], preferred_element_type=F32, precision=HIGHEST) + carry
            o_ref[pl.ds(off, blk), :] = cs
            return cs[0:1, :]

        lax.fori_loop(0, nb, step, jnp.zeros((1, LANES), F32))
        z = z_ref[...] + b_ref[...]
        dz = -o_ref[...] / (1.0 + jnp.exp(z))
        o_ref[...] = dz
        db_ref[...] = _colsum(dz)

    return pl.pallas_call(
        body, name=name,
        out_shape=(jax.ShapeDtypeStruct((S, LANES), F32), jax.ShapeDtypeStruct((1, LANES), F32)),
    )(dfk, logits, bias)


ATTN_BLOCK = 512
_NT = (((1,), (1,)), ((), ()))


def _attn_specs(S, H, tb):
    q_blk = lambda part: pl.BlockSpec((tb, HEAD_DIM), lambda h, i: (i, part * H + h))
    q_all = lambda part: pl.BlockSpec((S, HEAD_DIM), lambda h, i: (0, part * H + h))
    col_blk = pl.BlockSpec((None, tb, 1), lambda h, i: (h, i, 0))
    row_all = pl.BlockSpec((None, 1, S), lambda h, i: (h, 0, 0))
    return q_blk, q_all, col_blk, row_all


FWD_HEADS = 2


def _attn_fwd(qkv, f_col, f_row, name):
    S, H = qkv.shape[0], qkv.shape[1] // (3 * HEAD_DIM)
    tb = min(ATTN_BLOCK, S)
    scale = HEAD_DIM ** -0.5
    hp = FWD_HEADS if H % FWD_HEADS == 0 else 1
    groups, wide = H // hp, hp * HEAD_DIM
    lanes = lambda u: pl.ds(u * HEAD_DIM, HEAD_DIM)

    def body(q_ref, k_ref, v_ref, fc_ref, fr_ref, o_ref, lse_ref):
        i = pl.program_id(1)

        def step(j, carry, diagonal):
            off = pl.multiple_of(j * tb, tb)
            out = []
            for u in range(hp):
                m, l, acc = carry[u]
                k, v = k_ref[pl.ds(off, tb), lanes(u)], v_ref[pl.ds(off, tb), lanes(u)]
                s = lax.dot_general(q_ref[:, lanes(u)], k, _NT, preferred_element_type=F32) * scale
                s = s + (fc_ref[u] - fr_ref[u, :, pl.ds(off, tb)])
                if diagonal:
                    row = lax.broadcasted_iota(jnp.int32, (tb, tb), 0)
                    col = lax.broadcasted_iota(jnp.int32, (tb, tb), 1)
                    s = jnp.where(col <= row, s, NEG)
                m_new = jnp.maximum(m, jnp.max(s, axis=-1, keepdims=True))
                p = jnp.exp(s - m_new)
                alpha = jnp.exp(m - m_new)
                l = alpha * l + jnp.sum(p, axis=-1, keepdims=True)
                acc = alpha * acc + jnp.dot(p.astype(BF16), v, preferred_element_type=F32)
                out.append((m_new, l, acc))
            return tuple(out)

        init = (jnp.full((tb, 1), NEG, F32), jnp.zeros((tb, 1), F32), jnp.zeros((tb, HEAD_DIM), F32))
        carry = lax.fori_loop(0, i, lambda j, c: step(j, c, False), (init,) * hp)
        for u, (m, l, acc) in enumerate(step(i, carry, True)):
            o_ref[:, lanes(u)] = (acc / l).astype(o_ref.dtype)
            lse_ref[u] = m + jnp.log(l)

    part = lambda p, rows: pl.BlockSpec((rows, wide), lambda g, i: (i if rows == tb else 0, p * groups + g))
    col_blk = pl.BlockSpec((hp, tb, 1), lambda g, i: (g, i, 0))
    return pl.pallas_call(
        body, name=name, grid=(groups, S // tb),
        in_specs=[part(0, tb), part(1, S), part(2, S), col_blk, pl.BlockSpec((hp, 1, S), lambda g, i: (g, 0, 0))],
        out_specs=[pl.BlockSpec((tb, wide), lambda g, i: (i, g)), col_blk],
        out_shape=[jax.ShapeDtypeStruct((S, H * HEAD_DIM), BF16), jax.ShapeDtypeStruct((H, S, 1), F32)],
        compiler_params=_params(("parallel", "parallel"), 4 * S * wide * 2, 10 * hp * tb * tb * 4),
    )(qkv, qkv, qkv, f_col, f_row)


_TN = (((0,), (0,)), ((), ()))


def _attn_bwd(qkv, do, f_col, f_row, lse_col, name):
    S, H = qkv.shape[0], qkv.shape[1] // (3 * HEAD_DIM)
    tb = min(ATTN_BLOCK, S)
    nq = S // tb
    scale = HEAD_DIM ** -0.5
    q_blk, q_all, col_blk, row_all = _attn_specs(S, H, tb)
    head_blk = pl.BlockSpec((tb, HEAD_DIM), lambda h, i: (i, h))
    head_all = pl.BlockSpec((S, HEAD_DIM), lambda h, i: (0, h))

    def body(q_ref, k_ref, v_ref, do_ref, fc_ref, fr_ref, lse_ref, dq_ref, dk_ref, dv_ref, dfk_ref,
             p_buf, dp_buf, dk_acc, dv_acc, dfk_acc):
        i = pl.program_id(1)
        q, do, fc, lse = q_ref[...], do_ref[...], fc_ref[...], lse_ref[...]

        @pl.when(i == 0)
        def _():
            dk_acc[...] = jnp.zeros_like(dk_acc)
            dv_acc[...] = jnp.zeros_like(dv_acc)
            dfk_acc[...] = jnp.zeros_like(dfk_acc)

        def scores(j, delta, diagonal):
            off = pl.multiple_of(j * tb, tb)
            k, v = k_ref[pl.ds(off, tb), :], v_ref[pl.ds(off, tb), :]
            s = lax.dot_general(q, k, _NT, preferred_element_type=F32) * scale + (fc - fr_ref[:, pl.ds(off, tb)])
            if diagonal:
                row = lax.broadcasted_iota(jnp.int32, (tb, tb), 0)
                col = lax.broadcasted_iota(jnp.int32, (tb, tb), 1)
                s = jnp.where(col <= row, s, NEG)
            p = jnp.exp(s - lse)
            dp = lax.dot_general(do, v, _NT, preferred_element_type=F32)
            p_buf[j] = p
            dp_buf[j] = dp
            return delta + jnp.sum(p * dp, axis=-1, keepdims=True)

        delta = lax.fori_loop(0, i, lambda j, c: scores(j, c, False), jnp.zeros((tb, 1), F32))
        delta = scores(i, delta, True)

        def grad(j, dq):
            off = pl.multiple_of(j * tb, tb)
            p = p_buf[j]
            ds = p * (dp_buf[j] - delta)
            ds_lo = ds.astype(BF16)
            dk_acc[pl.ds(off, tb), :] += lax.dot_general(ds_lo, q, _TN, preferred_element_type=F32)
            dv_acc[pl.ds(off, tb), :] += lax.dot_general(p.astype(BF16), do, _TN, preferred_element_type=F32)
            dfk_acc[:, pl.ds(off, tb)] += jnp.sum(ds, axis=0, keepdims=True)
            return dq + jnp.dot(ds_lo, k_ref[pl.ds(off, tb), :], preferred_element_type=F32)

        dq = lax.fori_loop(0, i + 1, grad, jnp.zeros((tb, HEAD_DIM), F32))
        dq_ref[...] = (dq * scale).astype(dq_ref.dtype)

        @pl.when(i == nq - 1)
        def _():
            dk_ref[...] = (dk_acc[...] * scale).astype(dk_ref.dtype)
            dv_ref[...] = dv_acc[...].astype(dv_ref.dtype)
            dfk_ref[...] = dfk_acc[...]

    wide = jax.ShapeDtypeStruct((S, H * HEAD_DIM), BF16)
    return pl.pallas_call(
        body, name=name, grid=(H, nq),
        in_specs=[q_blk(0), q_all(1), q_all(2), head_blk, col_blk, row_all, col_blk],
        out_specs=[head_blk, head_all, head_all, row_all],
        out_shape=[wide, wide, wide, jax.ShapeDtypeStruct((H, 1, S), F32)],
        scratch_shapes=[pltpu.VMEM((nq, tb, tb), F32), pltpu.VMEM((nq, tb, tb), F32),
                        pltpu.VMEM((S, HEAD_DIM), F32), pltpu.VMEM((S, HEAD_DIM), F32), pltpu.VMEM((1, S), F32)],
        compiler_params=_params(("parallel", "arbitrary"), 6 * S * HEAD_DIM * 2,
                                2 * nq * tb * tb * 4 + 2 * S * HEAD_DIM * 4 + 10 * tb * tb * 4),
    )(qkv, qkv, qkv, do, f_col, f_row, lse_col)


CONV_TILE = 128


def _shift_down(v, n):
    row = lax.broadcasted_iota(jnp.int32, v.shape, 0)
    return jnp.where(row >= n, pltpu.roll(v, n, 0), 0.0)


def _shift_up(v, n):
    S = v.shape[0]
    row = lax.broadcasted_iota(jnp.int32, v.shape, 0)
    return jnp.where(row < S - n, pltpu.roll(v, S - n, 0), 0.0)


def _conv_specs(S, D, tc):
    nb = D // tc
    part = lambda p: pl.BlockSpec((S, tc), lambda j: (0, p * nb + j))
    return part, pl.BlockSpec((S, tc), lambda j: (0, j)), pl.BlockSpec((8, tc), lambda j: (0, j))


def _conv_fwd(proj, conv_w8, name):
    S, D = proj.shape[0], proj.shape[1] // 3
    tc = min(CONV_TILE, D)
    part, chan, taps = _conv_specs(S, D, tc)

    def body(b_ref, c_ref, u_ref, w_ref, z_ref):
        cu = c_ref[...].astype(F32) * u_ref[...].astype(F32)
        w = w_ref[...]
        y = w[0:1, :] * _shift_down(cu, 2) + w[1:2, :] * _shift_down(cu, 1) + w[2:3, :] * cu
        z_ref[...] = (b_ref[...].astype(F32) * y).astype(z_ref.dtype)

    return pl.pallas_call(
        body, name=name, grid=(D // tc,), in_specs=[part(0), part(1), part(2), taps], out_specs=chan,
        out_shape=jax.ShapeDtypeStruct((S, D), BF16),
        compiler_params=_params(("parallel",), 3 * _nbytes((S, tc), proj.dtype) + S * tc * 2, 6 * S * tc * 4),
    )(proj, proj, proj, conv_w8)


def _conv_bwd(proj, dz, conv_w8, name):
    S, D = proj.shape[0], proj.shape[1] // 3
    tc = min(CONV_TILE, D)
    part, chan, taps = _conv_specs(S, D, tc)

    def body(b_ref, c_ref, u_ref, dz_ref, w_ref, db_ref, dc_ref, du_ref, dw_ref):
        cv, uv = c_ref[...].astype(F32), u_ref[...].astype(F32)
        dzv, w = dz_ref[...].astype(F32), w_ref[...]
        cu = cv * uv
        cu1, cu2 = _shift_down(cu, 1), _shift_down(cu, 2)
        y = w[0:1, :] * cu2 + w[1:2, :] * cu1 + w[2:3, :] * cu
        db_ref[...] = (dzv * y).astype(db_ref.dtype)
        dy = dzv * b_ref[...].astype(F32)
        dcu = w[2:3, :] * dy + w[1:2, :] * _shift_up(dy, 1) + w[0:1, :] * _shift_up(dy, 2)
        dc_ref[...] = (dcu * uv).astype(dc_ref.dtype)
        du_ref[...] = (dcu * cv).astype(du_ref.dtype)
        dw_ref[...] = jnp.concatenate(
            [_colsum(dy * cu2), _colsum(dy * cu1), _colsum(dy * cu), jnp.zeros((8 - CONV_WIDTH, tc), F32)], axis=0)

    return pl.pallas_call(
        body, name=name, grid=(D // tc,), in_specs=[part(0), part(1), part(2), chan, taps],
        out_specs=[chan, chan, chan, taps],
        out_shape=[jax.ShapeDtypeStruct((S, D), BF16)] * 3 + [jax.ShapeDtypeStruct((8, D), F32)],
        compiler_params=_params(("parallel",), 3 * _nbytes((S, tc), proj.dtype) + _nbytes((S, tc), dz.dtype)
                                + 3 * S * tc * 2, 10 * S * tc * 4),
    )(proj, proj, proj, dz, conv_w8)


def _adamw(w, m, v, parts, name, layer=0, prev=None):
    L, R, C = w.shape
    P = parts.shape[0]
    assert parts.shape[1:] == (R, C), (name, parts.shape, w.shape)
    elem_bytes = 12 + 16 + P * parts.dtype.itemsize
    budget = 4 << 20
    tr, tc = R, C
    if R * C * elem_bytes > budget:
        if R % 8 == 0:
            tr = max(8, (budget // (C * elem_bytes)) // 8 * 8)
            while R % tr:
                tr -= 8
        else:
            tc = LANES
            while C % (2 * tc) == 0 and R * 2 * tc * elem_bytes <= budget:
                tc *= 2
            assert C % tc == 0, (name, R, C)
    c1, c2 = 1.0 - ADAM_B1 ** ADAM_STEP, 1.0 - ADAM_B2 ** ADAM_STEP

    def body(w_ref, m_ref, v_ref, p_ref, *rest):
        g_ref, d_ref, nm_ref, nv_ref = rest[-4:]
        g = p_ref[0].astype(F32)
        for p in range(1, P):
            g = g + p_ref[p].astype(F32)
        nm = ADAM_B1 * m_ref[...] + (1.0 - ADAM_B1) * g
        nv = ADAM_B2 * v_ref[...] + (1.0 - ADAM_B2) * (g * g)
        g_ref[...] = g
        nm_ref[...] = nm
        nv_ref[...] = nv
        d_ref[...] = -ADAM_LR * ((nm / c1) / (jnp.sqrt(nv / c2) + ADAM_EPS) + ADAM_WD * w_ref[...])

    blk = pl.BlockSpec((None, tr, tc), lambda i, j: (layer, i, j))
    prev = [] if prev is None else list(prev)
    return pl.pallas_call(
        body, name=name, grid=(R // tr, C // tc),
        in_specs=[blk, blk, blk, pl.BlockSpec((P, tr, tc), lambda i, j: (0, i, j))] + [ANY] * len(prev),
        out_specs=[blk] * 4, out_shape=[jax.ShapeDtypeStruct((L, R, C), F32)] * 4,
        input_output_aliases={4 + k: k for k in range(len(prev))},
        compiler_params=_params(("parallel", "parallel"), tr * tc * elem_bytes),
    )(w, m, v, parts, *prev)


def _silu(v):
    return v / (1.0 + jnp.exp(-v))


def _pad_rows(a, rows):
    return jnp.pad(a, ((0, rows - a.shape[0]), (0, 0)))


def _pad_cols(a, cols):
    return jnp.pad(a, ((0, 0), (0, cols - a.shape[1])))


def kernel(x, c, ada_w, ada_b, norm_mix, norm_mlp, fox_w_in, fox_b_f, fox_w_out, conv_w_in, conv_w, conv_w_out, mlp_w_up, mlp_w_down, final_norm, loss_target, m_ada_w, m_ada_b, m_norm_mix, m_norm_mlp, m_fox_w_in, m_fox_b_f, m_fox_w_out, m_conv_w_in, m_conv_w, m_conv_w_out, m_mlp_w_up, m_mlp_w_down, m_final_norm, v_ada_w, v_ada_b, v_norm_mix, v_norm_mlp, v_fox_w_in, v_fox_b_f, v_fox_w_out, v_conv_w_in, v_conv_w, v_conv_w_out, v_mlp_w_up, v_mlp_w_down, v_final_norm):
    S, D = x.shape[1], x.shape[2]
    H = D // HEAD_DIM
    FF = mlp_w_up.shape[2] * NDEV
    depth = ada_w.shape[0]
    n_mod = 6
    assert depth == 2 and fox_w_in.shape[0] == 1 and conv_w_in.shape[0] == 1 and H <= LANES
    me = _my_index()
    x0, target = x[0], loss_target[0]
    row = lambda vec: vec.reshape(1, -1)

    def tied(vec, token):
        return vec + token[0, 0]

    bf = lambda w: w.astype(BF16)
    gather_groups = {
        "fox": [bf(fox_w_in[0]).T, bf(fox_w_out[0])],
        "mlp0": [bf(mlp_w_up[0]).T, bf(mlp_w_down[0])],
        "conv": [bf(conv_w_in[0]).T, conv_w[0], bf(conv_w_out[0])],
        "mlp1": [bf(mlp_w_up[1]).T, bf(mlp_w_down[1])],
    }

    def start_gather(group, after):
        return _exchange_start(gather_groups[group], f"gather_{group}_start", False, after, relay=True)

    def relay_gather(handle, group, after):
        handle = _exchange_wait(handle, f"gather_{group}_arrivals", after, arrivals_only=True)
        return _relay_forward_start(handle, f"gather_{group}_forward")

    def finish_gather(handle, group, after):
        return _exchange_wait(handle, f"gather_{group}_wait", after)

    landed = lambda handle: handle[4][0]

    c_all = _all_gather([c], "gather_cond")[0].reshape(NDEV, D)
    ncol = ada_w.shape[2]
    ada_b_mine = lax.dynamic_slice_in_dim(ada_b, me * ncol, ncol, axis=1)
    mod_cols = jnp.stack([
        _matmul(c_all, ada_w[i], mode="nn", name=f"ada_fwd_{i}", out_dtypes=[F32], tm=NDEV, tn=ncol // 2, tk=D,
                a_pre=_silu, precision=HIGHEST, epilogue=lambda acc, b: (acc + b,), extras=[(ada_b_mine[i:i + 1], "row")])
        for i in range(depth)])
    mod_all = _all_gather([mod_cols], "gather_mod")[0]
    mod = lax.dynamic_index_in_dim(mod_all, me, axis=2, keepdims=False)
    fox_handle, token = start_gather("fox", mod_all)
    mod = tied(mod, token).transpose(1, 0, 2).reshape(depth, n_mod, 1, D)
    sh_mix, sc_mix, g_mix, sh_mlp, sc_mlp, g_mlp = (mod[:, k] for k in range(n_mod))
    b_f = _pad_cols(fox_b_f, LANES)

    def residual(acc, x_in, gate):
        return (x_in + gate * acc, acc)

    def mlp_fwd(i, x_in, handle, relay_next=None):
        h, inv = _rms_mod_fwd(x_in, row(norm_mlp[i]), sh_mlp[i], sc_mlp[i], f"mlp_norm_{i}")
        w_up_t, w_down = finish_gather(handle, f"mlp{i}", h)
        w_up_t, w_down = w_up_t.reshape(FF, D), w_down.reshape(FF, D)
        r, a = _matmul(h, w_up_t, mode="nt", name=f"mlp_up_{i}", out_dtypes=[BF16, BF16], tm=1024, tn=1024, tk=D,
                       epilogue=lambda acc: (jnp.maximum(acc, 0.0), jnp.square(jnp.maximum(acc, 0.0))))
        next_handle = relay_gather(relay_next[1], relay_next[0], a) if relay_next else None
        x_out, y = _matmul(a, w_down, mode="nn", name=f"mlp_down_{i}", out_dtypes=[F32, F32], tm=256, tn=512, tk=FF,
                           n_outer=True, epilogue=residual, extras=[(x_in, "tile"), (g_mlp[i], "row")],
                           after=[landed(next_handle)] if relay_next else [])
        return x_out, (x_in, h, inv, r, a, y, w_up_t, w_down), next_handle

    def mlp_bwd(i, dx, saved, gate):
        x_in, h, inv, r, a, y, w_up_t, w_down = saved
        dy, dgate = _gate_bwd(dx, y, gate, f"mlp_gate_bwd_{i}")
        du = _matmul(dy, w_down, mode="nt", name=f"mlp_down_bwd_{i}", out_dtypes=[BF16], tm=1024, tn=1024, tk=D,
                     epilogue=lambda acc, rv: (acc * (2.0 * rv.astype(F32)),), extras=[(r, "tile")])
        d_down = _matmul(a, dy, mode="tn", name=f"mlp_down_wgrad_{i}", out_dtypes=[BF16], tm=512, tn=1024, tk=S)
        dh = _matmul(du, w_up_t, mode="nn", name=f"mlp_up_bwd_{i}", out_dtypes=[F32], tm=256, tn=512, tk=FF, n_outer=True)
        d_up = _matmul(h, du, mode="tn", name=f"mlp_up_wgrad_{i}", out_dtypes=[BF16], tm=512, tn=FF // NDEV, tk=S,
                       out_shards=True)
        dx, dsh, dsc, dgain = _rms_mod_bwd(dh, x_in, inv, dx, row(norm_mlp[i]), sc_mlp[i], f"mlp_norm_bwd_{i}")
        handle, token = _exchange_start([d_up, d_down.reshape(NDEV, FF // NDEV, D)], f"scatter_mlp{i}_start", True, dx)
        return dx, (dsh, dsc, dgate, dgain), handle, token

    h0, inv0 = _rms_mod_fwd(x0, row(norm_mix[0]), sh_mix[0], sc_mix[0], "fox_norm")
    w_in_t, w_fox_out = finish_gather(relay_gather(fox_handle, "fox", h0), "fox", h0)
    mlp0_handle, token = start_gather("mlp0", w_in_t)
    w_in_t = w_in_t.reshape(3 * D + H, D)
    w_f_t = _pad_rows(w_in_t[3 * D:], LANES)
    w_fox_out = w_fox_out.reshape(D, D)
    qkv = _matmul(h0, w_in_t, mode="nt", name="fox_qkv", out_dtypes=[BF16], tm=1024, tn=1024, tk=D, n=3 * D, after=[token])
    f_logit = _matmul(h0, w_f_t, mode="nt", name="fox_forget_logits", out_dtypes=[F32], tm=1024, tn=LANES, tk=D)
    f_cum = _forget_cumsum(f_logit, b_f, "fox_forget_cumsum")
    f_heads = f_cum[:, :H].T
    f_col, f_row = f_heads.reshape(H, S, 1), f_heads.reshape(H, 1, S)
    o, lse = _attn_fwd(qkv, f_col, f_row, "fox_attention")
    mlp0_handle = relay_gather(mlp0_handle, "mlp0", o)
    conv_handle, token = start_gather("conv", landed(mlp0_handle))
    mlp1_handle, token = start_gather("mlp1", token)
    x1, mix0 = _matmul(o, w_fox_out, mode="nn", name="fox_out", out_dtypes=[F32, F32], tm=512, tn=1024, tk=D,
                       epilogue=residual, extras=[(x0, "tile"), (g_mix[0], "row")], after=[token])
    x2, mlp0, conv_handle = mlp_fwd(0, x1, mlp0_handle, ("conv", conv_handle))

    h1, inv1 = _rms_mod_fwd(x2, row(norm_mix[1]), sh_mix[1], sc_mix[1], "conv_norm")
    w_conv_in_t, w_taps, w_conv_out = finish_gather(conv_handle, "conv", h1)
    w_conv_in_t = w_conv_in_t.reshape(3 * D, D)
    w_taps = _pad_rows(w_taps.transpose(1, 0, 2).reshape(CONV_WIDTH, D), 8)
    w_conv_out = w_conv_out.reshape(D, D)
    proj = _matmul(h1, w_conv_in_t, mode="nt", name="conv_in", out_dtypes=[F32], tm=1024, tn=1024, tk=D)
    mlp1_handle = relay_gather(mlp1_handle, "mlp1", proj)
    z = _conv_fwd(proj, w_taps, "conv_mix")
    x3, mix1 = _matmul(z, w_conv_out, mode="nn", name="conv_out", out_dtypes=[F32, F32], tm=512, tn=1024, tk=D,
                       epilogue=residual, extras=[(x2, "tile"), (g_mix[1], "row")], after=[landed(mlp1_handle)])
    x4, mlp1, _ = mlp_fwd(1, x3, mlp1_handle)

    dx, d_final, loss_lanes = _final_loss_bwd(x4, target, row(final_norm), "loss_head")

    dx, dmod_mlp1, mlp1_scatter, token = mlp_bwd(1, dx, mlp1, g_mlp[1])
    dmix, dg_mix1 = _gate_bwd(dx, mix1, tied(g_mix[1], token), "conv_gate_bwd")
    dz = _matmul(dmix, w_conv_out, mode="nt", name="conv_out_bwd", out_dtypes=[F32], tm=1024, tn=1024, tk=D)
    d_conv_out = _matmul(z, dmix, mode="tn", name="conv_out_wgrad", out_dtypes=[BF16], tm=512, tn=1024, tk=S)
    db, dc, du, d_taps = _conv_bwd(proj, dz, w_taps, "conv_mix_bwd")
    dproj = jnp.concatenate([db, dc, du], axis=1)
    dh1 = _matmul(dproj, w_conv_in_t, mode="nn", name="conv_in_bwd", out_dtypes=[F32], tm=512, tn=512, tk=3 * D, n_outer=True)
    d_conv_in = _matmul(h1, dproj, mode="tn", name="conv_in_wgrad", out_dtypes=[BF16], tm=512, tn=3 * D // NDEV, tk=S,
                        out_shards=True)
    dx, dsh1, dsc1, dgain_mix1 = _rms_mod_bwd(dh1, x2, inv1, dx, row(norm_mix[1]), sc_mix[1], "conv_norm_bwd")
    d_taps_split = d_taps[:CONV_WIDTH].reshape(CONV_WIDTH, NDEV, -1).transpose(1, 0, 2)
    conv_scatter, token = _exchange_start([d_conv_in, d_taps_split, d_conv_out.reshape(NDEV, D // NDEV, D)],
                                          "scatter_conv_start", True, dx)

    dx, dmod_mlp0, mlp0_scatter, token = mlp_bwd(0, dx, mlp0, tied(g_mlp[0], token))
    dmix, dg_mix0 = _gate_bwd(dx, mix0, tied(g_mix[0], token), "fox_gate_bwd")
    do = _matmul(dmix, w_fox_out, mode="nt", name="fox_out_bwd", out_dtypes=[BF16], tm=1024, tn=1024, tk=D)
    d_fox_out = _matmul(o, dmix, mode="tn", name="fox_out_wgrad", out_dtypes=[BF16], tm=512, tn=1024, tk=S)
    dq, dk, dv, dfk = _attn_bwd(qkv, do, f_col, f_row, lse, "fox_attention_bwd")
    dqkv = jnp.concatenate([dq, dk, dv], axis=1)
    dfk_lanes = _pad_cols(dfk.reshape(H, S).T, LANES)
    df_logit, db_f = _forget_bwd(dfk_lanes, f_logit, b_f, "fox_forget_bwd")
    d_qkv_t = _matmul(dqkv, h0, mode="tn", name="fox_qkv_wgrad", out_dtypes=[BF16], tm=512, tn=1024, tk=S)
    d_f_t = _matmul(df_logit, h0, mode="tn", name="fox_forget_wgrad", out_dtypes=[BF16], tm=LANES, tn=1024, tk=S)
    d_fox_in = jnp.concatenate([d_qkv_t, d_f_t[:H]], axis=0).reshape(NDEV, -1, D)
    fox_scatter, token = _exchange_start([d_fox_in, d_fox_out.reshape(NDEV, D // NDEV, D)], "scatter_fox_start", True,
                                         d_fox_in)
    dh0_f = _matmul(df_logit, w_f_t, mode="nn", name="fox_forget_logits_bwd", out_dtypes=[F32], tm=1024, tn=1024, tk=LANES,
                    after=[token])
    dh0 = _matmul(dqkv, w_in_t, mode="nn", name="fox_qkv_bwd", out_dtypes=[F32], tm=512, tn=512, tk=3 * D, n_outer=True,
                  epilogue=lambda acc, extra: (acc + extra,), extras=[(dh0_f, "tile")])
    dx, dsh0, dsc0, dgain_mix0 = _rms_mod_bwd(dh0, x0, inv0, dx, row(norm_mix[0]), sc_mix[0], "fox_norm_bwd")
    grad_x = dx.reshape(1, S, D)

    up1, down1 = _exchange_wait(mlp1_scatter, "scatter_mlp1_wait", dx)
    up_out = _adamw(mlp_w_up, m_mlp_w_up, v_mlp_w_up, up1, "adamw_mlp_w_up_1", layer=1)
    down_out = _adamw(mlp_w_down, m_mlp_w_down, v_mlp_w_down, down1, "adamw_mlp_w_down_1", layer=1)
    cin, taps, cout = _exchange_wait(conv_scatter, "scatter_conv_wait", down_out[0])
    conv_in_out = _adamw(conv_w_in, m_conv_w_in, v_conv_w_in, cin, "adamw_conv_w_in")
    conv_w_res = _adamw(conv_w, m_conv_w, v_conv_w, taps, "adamw_conv_w")
    conv_out_out = _adamw(conv_w_out, m_conv_w_out, v_conv_w_out, cout, "adamw_conv_w_out")
    up0, down0 = _exchange_wait(mlp0_scatter, "scatter_mlp0_wait", conv_out_out[0])
    up_out = _adamw(mlp_w_up, m_mlp_w_up, v_mlp_w_up, up0, "adamw_mlp_w_up_0", layer=0, prev=up_out)
    down_out = _adamw(mlp_w_down, m_mlp_w_down, v_mlp_w_down, down0, "adamw_mlp_w_down_0", layer=0, prev=down_out)

    dmod = jnp.concatenate([
        jnp.concatenate([dsh0, dsc0, dg_mix0, dmod_mlp0[0], dmod_mlp0[1], dmod_mlp0[2]], axis=1),
        jnp.concatenate([dsh1, dsc1, dg_mix1, dmod_mlp1[0], dmod_mlp1[1], dmod_mlp1[2]], axis=1)], axis=0)
    small_sizes = [depth * n_mod * D, depth * D, depth * D, H, D, 1]
    n_small = sum(small_sizes)
    n_rows = -(-n_small // (8 * LANES)) * 8

    def pack(parts):
        flat = jnp.concatenate([p.reshape(-1) for p in parts])
        return jnp.pad(flat, (0, n_rows * LANES - n_small)).reshape(n_rows, LANES)

    def unpack(packed, shapes):
        flat, out, at = packed.reshape(-1), [], 0
        for size, shape in zip(small_sizes, shapes):
            out.append(flat[at:at + size].reshape(shape))
            at += size
        return out

    small_partial = pack([dmod, jnp.concatenate([dgain_mix0, dgain_mix1], axis=0),
                          jnp.concatenate([dmod_mlp0[3], dmod_mlp1[3]], axis=0), db_f[0, :H], d_final, loss_lanes[0, :1]])
    small_parts = _all_gather([small_partial], "gather_small_grads", after=[down_out[0]])[0]
    small_shapes = [ada_b.shape, norm_mix.shape, norm_mlp.shape, fox_b_f.shape, final_norm.shape]
    loss = jnp.sum(small_parts.reshape(NDEV, -1)[:, n_small - 1])
    unused = jnp.zeros((1,), F32)
    small_out = _adamw(pack([ada_b, norm_mix, norm_mlp, fox_b_f, final_norm, unused])[None],
                       pack([m_ada_b, m_norm_mix, m_norm_mlp, m_fox_b_f, m_final_norm, unused])[None],
                       pack([v_ada_b, v_norm_mix, v_norm_mlp, v_fox_b_f, v_final_norm, unused])[None], small_parts,
                       "adamw_small")
    small_out = [unpack(t, small_shapes) for t in small_out]

    dmod_all = small_parts.reshape(NDEV, -1)[:, :depth * n_mod * D].reshape(NDEV, depth, n_mod * D)
    dmod_mine = lax.dynamic_slice_in_dim(dmod_all, me * ncol, ncol, axis=2)
    ada_out = None
    for i in range(depth):
        d_ada = _matmul(c_all, dmod_mine[:, i], mode="tn", name=f"ada_wgrad_{i}", out_dtypes=[F32], tm=1024, tn=ncol // 2,
                        tk=NDEV, a_pre=_silu, precision=HIGHEST)
        ada_out = _adamw(ada_w, m_ada_w, v_ada_w, d_ada[None], f"adamw_ada_w_{i}", layer=i, prev=ada_out)

    fin, fout = _exchange_wait(fox_scatter, "scatter_fox_wait", ada_out[0])
    swap = lambda t: jnp.swapaxes(t, 1, 2)
    fox_in_out = [swap(t) for t in _adamw(swap(fox_w_in), swap(m_fox_w_in), swap(v_fox_w_in), fin, "adamw_fox_w_in")]
    fox_out_out = _adamw(fox_w_out, m_fox_w_out, v_fox_w_out, fout, "adamw_fox_w_out")

    outputs = [loss, grad_x]
    for kind in range(4):
        sm = small_out[kind]
        outputs += [ada_out[kind], sm[0], sm[1], sm[2], fox_in_out[kind], sm[3], fox_out_out[kind], conv_in_out[kind],
                    conv_w_res[kind], conv_out_out[kind], up_out[kind], down_out[kind], sm[4]]
    return tuple(outputs)
```

```python
import math

import jax
import jax.numpy as jnp
from jax import lax
from jax.experimental import pallas as pl
from jax.experimental.pallas import tpu as pltpu

F32 = jnp.float32
BF16 = jnp.bfloat16
MESH = pl.DeviceIdType.MESH
NDEV = 8
HEAD_DIM = 128
LANES = 128
CONV_WIDTH = 3
RMS_EPS = 1e-6
ADAM_LR, ADAM_B1, ADAM_B2, ADAM_EPS, ADAM_WD, ADAM_STEP = 0.001, 0.9, 0.999, 1e-08, 0.01, 10
NEG = -1e30
V7X_VMEM_BYTES = 64 * 1024 * 1024
VMEM_HEADROOM = 12 * 1024 * 1024
HBM = pl.BlockSpec(memory_space=pltpu.HBM)
HIGHEST = lax.Precision.HIGHEST


def _nbytes(shape, dtype):
    return math.prod(shape) * jnp.dtype(dtype).itemsize


def _params(semantics, block_bytes, temp_bytes=0):
    limit = min(2 * block_bytes + temp_bytes + VMEM_HEADROOM, V7X_VMEM_BYTES - 4 * 1024 * 1024)
    return pltpu.CompilerParams(dimension_semantics=semantics, vmem_limit_bytes=int(limit))


def _my_index():
    return lax.axis_index("x") * 4 + lax.axis_index("y") * 2 + lax.axis_index("c")


def _peer(r):
    x, y, c = lax.axis_index("x"), lax.axis_index("y"), lax.axis_index("c")
    px = 1 - x if (r >> 2) & 1 else x
    py = 1 - y if (r >> 1) & 1 else y
    pc = 1 - c if r & 1 else c
    return (px, py, pc), px * 4 + py * 2 + pc


def _exchange(arrays, name, scatter, after=None):
    n = len(arrays)
    after = [] if after is None else list(after)

    def body(*refs):
        ins, outs = refs[:n], refs[n + len(after):2 * n + len(after)]
        send_sems, recv_sems, local_sems = refs[2 * n + len(after):]
        me = _my_index()
        local = []
        for a in range(n):
            src = ins[a].at[me] if scatter else ins[a]
            local.append(pltpu.make_async_copy(src, outs[a].at[me], local_sems.at[a]))
            local[-1].start()
        sends = []
        for r in range(1, NDEV):
            peer, pidx = _peer(r)
            for a in range(n):
                src = ins[a].at[pidx] if scatter else ins[a]
                cp = pltpu.make_async_remote_copy(
                    src_ref=src, dst_ref=outs[a].at[me],
                    send_sem=send_sems.at[a * (NDEV - 1) + r - 1], recv_sem=recv_sems.at[a * (NDEV - 1) + r - 1],
                    device_id=peer, device_id_type=MESH)
                cp.start()
                sends.append(cp)
        for r in range(1, NDEV):
            peer, pidx = _peer(r)
            for a in range(n):
                src = ins[a].at[pidx] if scatter else ins[a]
                pltpu.make_async_remote_copy(
                    src_ref=src, dst_ref=outs[a].at[pidx],
                    send_sem=send_sems.at[a * (NDEV - 1) + r - 1], recv_sem=recv_sems.at[a * (NDEV - 1) + r - 1],
                    device_id=peer, device_id_type=MESH).wait_recv()
        for cp in sends:
            cp.wait_send()
        for cp in local:
            cp.wait()

    out_shape = [jax.ShapeDtypeStruct(a.shape if scatter else (NDEV,) + a.shape, a.dtype) for a in arrays]
    return pl.pallas_call(
        body, name=name, out_shape=out_shape, in_specs=[HBM] * n + [ANY] * len(after), out_specs=[HBM] * n,
        scratch_shapes=[pltpu.SemaphoreType.DMA((n * (NDEV - 1),)), pltpu.SemaphoreType.DMA((n * (NDEV - 1),)),
                        pltpu.SemaphoreType.DMA((n,))],
    )(*arrays, *after)


def _all_gather(arrays, name, after=None):
    return _exchange(arrays, name, scatter=False, after=after)


SEM = pl.BlockSpec(memory_space=pltpu.SEMAPHORE)
ANY = pl.BlockSpec(memory_space=pl.ANY)
DATAFLOW = pltpu.SideEffectType.DATAFLOW_SIDE_EFFECTING
TOKEN_SHAPE = (8, LANES)


SIBLING = 1
OTHER_CHIPS = (4, 2, 6)


def _exchange_start(arrays, name, scatter, after, relay=False):
    n = len(arrays)
    n_sems = n * (NDEV - 1)
    assert not (relay and scatter)

    def body(*refs):
        ins = refs[:n]
        send_sems, recv_sems = refs[n + 1], refs[n + 2]
        lands, token = refs[2 * n + 3:3 * n + 3], refs[3 * n + 3]
        me = _my_index()
        for r in (SIBLING, *OTHER_CHIPS) if relay else range(1, NDEV):
            peer, pidx = _peer(r)
            for a in range(n):
                src = ins[a].at[pidx] if scatter else ins[a]
                pltpu.make_async_remote_copy(
                    src_ref=src, dst_ref=lands[a].at[me],
                    send_sem=send_sems.at[a * (NDEV - 1) + r - 1], recv_sem=recv_sems.at[a * (NDEV - 1) + r - 1],
                    device_id=peer, device_id_type=MESH).start()
        token[...] = jnp.zeros(TOKEN_SHAPE, F32)

    land_shapes = [a.shape if scatter else (NDEV,) + a.shape for a in arrays]
    srcs = [pltpu.with_memory_space_constraint(a, pltpu.HBM) for a in arrays]
    outs = pl.pallas_call(
        body, name=name,
        out_shape=(pltpu.SemaphoreType.DMA((n_sems,)), pltpu.SemaphoreType.DMA((n_sems,)),
                   *[pltpu.HBM(a.shape, a.dtype) for a in arrays], *[pltpu.HBM(s, a.dtype) for s, a in zip(land_shapes, arrays)],
                   jax.ShapeDtypeStruct(TOKEN_SHAPE, F32)),
        in_specs=[HBM] * n + [ANY],
        out_specs=(SEM, SEM, *[HBM] * (2 * n), pl.BlockSpec(memory_space=pltpu.VMEM)),
        input_output_aliases={i: 2 + i for i in range(n)},
        compiler_params=pltpu.CompilerParams(has_side_effects=DATAFLOW),
    )(*srcs, after)
    return (scatter, relay, [(outs[0], outs[1])], list(outs[2:2 + n]), list(outs[2 + n:2 + 2 * n])), outs[-1]


def _relay_forward_start(handle, name):
    scatter, relay, sems, srcs, lands = handle
    n = len(lands)
    n_sems = n * len(OTHER_CHIPS)

    def body(*refs):
        land_refs, send_sems, recv_sems = refs[:n], refs[n], refs[n + 1]
        sibling, _ = _peer(SIBLING)
        for j, r in enumerate(OTHER_CHIPS):
            _, pidx = _peer(r)
            for a in range(n):
                pltpu.make_async_remote_copy(
                    src_ref=land_refs[a].at[pidx], dst_ref=land_refs[a].at[pidx],
                    send_sem=send_sems.at[a * len(OTHER_CHIPS) + j], recv_sem=recv_sems.at[a * len(OTHER_CHIPS) + j],
                    device_id=sibling, device_id_type=MESH).start()

    outs = pl.pallas_call(
        body, name=name,
        out_shape=(pltpu.SemaphoreType.DMA((n_sems,)), pltpu.SemaphoreType.DMA((n_sems,)),
                   *[pltpu.HBM(t.shape, t.dtype) for t in lands]),
        in_specs=[HBM] * n, out_specs=(SEM, SEM, *[HBM] * n),
        input_output_aliases={i: 2 + i for i in range(n)},
        compiler_params=pltpu.CompilerParams(has_side_effects=DATAFLOW),
    )(*lands)
    return (scatter, relay, sems + [(outs[0], outs[1])], srcs, list(outs[2:]))


def _exchange_wait(handle, name, after, arrivals_only=False):
    scatter, relay, sems, srcs, lands = handle
    n = len(srcs)
    forwarded = len(sems) == 2
    assert not arrivals_only or (relay and not forwarded)

    def body(*refs):
        src_refs, land_refs = refs[:n], refs[n:2 * n]
        send_sems, recv_sems = refs[2 * n], refs[2 * n + 1]
        for r in (SIBLING, *OTHER_CHIPS) if relay else range(1, NDEV):
            peer, pidx = _peer(r)
            for a in range(n):
                src = src_refs[a].at[pidx] if scatter else src_refs[a]
                cp = pltpu.make_async_remote_copy(
                    src_ref=src, dst_ref=land_refs[a].at[pidx],
                    send_sem=send_sems.at[a * (NDEV - 1) + r - 1], recv_sem=recv_sems.at[a * (NDEV - 1) + r - 1],
                    device_id=peer, device_id_type=MESH)
                if arrivals_only:
                    if r in OTHER_CHIPS:
                        cp.wait_recv()
                else:
                    cp.wait_send()
                    if not (relay and r in OTHER_CHIPS):
                        cp.wait_recv()
        if forwarded:
            fwd_send, fwd_recv = refs[2 * n + 2], refs[2 * n + 3]
            sibling, _ = _peer(SIBLING)
            for j, r in enumerate(OTHER_CHIPS):
                _, pidx = _peer(r ^ SIBLING)
                for a in range(n):
                    cp = pltpu.make_async_remote_copy(
                        src_ref=src_refs[a], dst_ref=land_refs[a].at[pidx],
                        send_sem=fwd_send.at[a * len(OTHER_CHIPS) + j], recv_sem=fwd_recv.at[a * len(OTHER_CHIPS) + j],
                        device_id=sibling, device_id_type=MESH)
                    cp.wait_send()
                    cp.wait_recv()

    flat_sems = [s for pair in sems for s in pair]
    outs = pl.pallas_call(
        body, name=name,
        out_shape=tuple(pltpu.HBM(t.shape, t.dtype) for t in (*srcs, *lands)),
        in_specs=[HBM] * (2 * n) + [SEM] * len(flat_sems) + [ANY], out_specs=tuple([HBM] * (2 * n)),
        input_output_aliases={i: i for i in range(2 * n)},
        compiler_params=pltpu.CompilerParams(has_side_effects=DATAFLOW),
    )(*srcs, *lands, *flat_sems, after)
    if arrivals_only:
        return (scatter, relay, sems, list(outs[:n]), list(outs[n:]))
    me = _my_index()
    mine = [lax.dynamic_index_in_dim(s, me, 0, keepdims=False) if scatter else s for s in outs[:n]]
    return [lax.dynamic_update_index_in_dim(land, own, me, 0) for land, own in zip(outs[n:], mine)]


def _matmul(a, b, *, mode, name, out_dtypes, tm, tn, tk, epilogue=None, extras=(), a_pre=None,
            out_shards=False, n_outer=False, precision=None, after=(), n=None):
    after = list(after)
    n_after = len(after)
    K, M = a.shape if mode == "tn" else a.shape[::-1]
    N = n if n is not None else (b.shape[0] if mode == "nt" else b.shape[1])
    tm, tn, tk = min(tm, M), min(tn, N), min(tk, K)
    assert M % tm == 0 and N % tn == 0 and K % tk == 0, (name, M, N, K, tm, tn, tk)
    nm, nn, nk = M // tm, N // tn, K // tk
    n_out, n_ext = len(out_dtypes), len(extras)
    contract = {"nn": ((1,), (0,)), "nt": ((1,), (1,)), "tn": ((0,), (0,))}[mode]

    def body(*refs):
        a_ref, b_ref = refs[:2]
        ext_refs = refs[2:2 + n_ext]
        out_refs = refs[2 + n_ext + n_after:2 + n_ext + n_after + n_out]
        acc_ref = refs[2 + n_ext + n_after + n_out] if nk > 1 else None
        av, bv = a_ref[...], b_ref[...]
        if a_pre is not None:
            av = a_pre(av)
        if precision is None:
            av, bv = av.astype(BF16), bv.astype(BF16)
        part = lax.dot_general(av, bv, (contract, ((), ())), preferred_element_type=F32, precision=precision)

        def finish(acc):
            vals = (acc,) if epilogue is None else epilogue(acc, *[r[...] for r in ext_refs])
            for r, v in zip(out_refs, vals):
                r[...] = v.astype(r.dtype)

        if nk == 1:
            finish(part)
        else:
            k = pl.program_id(2)

            @pl.when(k == 0)
            def _():
                acc_ref[...] = part

            @pl.when(k > 0)
            def _():
                acc_ref[...] += part

            @pl.when(k == nk - 1)
            def _():
                finish(acc_ref[...])

    def at(index):
        return (lambda j, i, k: index(i, j, k)) if n_outer else index

    a_spec = pl.BlockSpec((tk, tm), at(lambda i, j, k: (k, i))) if mode == "tn" else pl.BlockSpec((tm, tk), at(lambda i, j, k: (i, k)))
    b_spec = pl.BlockSpec((tn, tk), at(lambda i, j, k: (j, k))) if mode == "nt" else pl.BlockSpec((tk, tn), at(lambda i, j, k: (k, j)))
    in_specs, block_bytes = [a_spec, b_spec], _nbytes((tm, tk), a.dtype) + _nbytes((tk, tn), b.dtype)
    for arr, kind in extras:
        if kind == "tile":
            assert arr.shape == (M, N), (name, arr.shape)
            in_specs.append(pl.BlockSpec((tm, tn), at(lambda i, j, k: (i, j))))
            block_bytes += _nbytes((tm, tn), arr.dtype)
        else:
            assert arr.shape == (1, N), (name, arr.shape)
            in_specs.append(pl.BlockSpec((1, tn), at(lambda i, j, k: (0, j))))
    in_specs += [ANY] * n_after
    if out_shards:
        assert n_out == 1 and tn * NDEV == N
        out_shape = [jax.ShapeDtypeStruct((NDEV, M, tn), out_dtypes[0])]
        out_specs = [pl.BlockSpec((None, tm, tn), at(lambda i, j, k: (j, i, 0)))]
    else:
        out_shape = [jax.ShapeDtypeStruct((M, N), d) for d in out_dtypes]
        out_specs = [pl.BlockSpec((tm, tn), at(lambda i, j, k: (i, j))) for _ in out_dtypes]
    block_bytes += sum(_nbytes((tm, tn), d) for d in out_dtypes)
    scratch = [pltpu.VMEM((tm, tn), F32)] if nk > 1 else []
    outs = pl.pallas_call(
        body, name=name, grid=(nn, nm, nk) if n_outer else (nm, nn, nk), in_specs=in_specs, out_specs=out_specs,
        out_shape=out_shape, scratch_shapes=scratch,
        compiler_params=_params(("parallel", "parallel", "arbitrary"), block_bytes, 2 * tm * tn * 4),
    )(a, b, *[arr for arr, _ in extras], *after)
    return outs[0] if n_out == 1 else outs


def _rowwise(fn, tiled, smalls, out_tiles, out_sums, *, name, ts=256):
    S = tiled[0].shape[0]
    ts = min(ts, S)
    assert S % ts == 0
    nt, ns, no, na = len(tiled), len(smalls), len(out_tiles), len(out_sums)

    def body(*refs):
        t_refs, s_refs = refs[:nt], refs[nt:nt + ns]
        o_refs, a_refs = refs[nt + ns:nt + ns + no], refs[nt + ns + no:]
        tile_vals, sum_vals = fn([r[...] for r in t_refs], [r[...] for r in s_refs])
        for r, v in zip(o_refs, tile_vals):
            r[...] = v.astype(r.dtype)

        @pl.when(pl.program_id(0) == 0)
        def _():
            for r in a_refs:
                r[...] = jnp.zeros_like(r)

        for r, v in zip(a_refs, sum_vals):
            r[...] += v

    in_specs = [pl.BlockSpec((ts, t.shape[1]), lambda i: (i, 0)) for t in tiled]
    in_specs += [pl.BlockSpec(s.shape, lambda i: (0, 0)) for s in smalls]
    out_specs = [pl.BlockSpec((ts, w), lambda i: (i, 0)) for w, _ in out_tiles]
    out_specs += [pl.BlockSpec((1, w), lambda i: (0, 0)) for w in out_sums]
    out_shape = [jax.ShapeDtypeStruct((S, w), d) for w, d in out_tiles]
    out_shape += [jax.ShapeDtypeStruct((1, w), F32) for w in out_sums]
    block_bytes = sum(_nbytes((ts, t.shape[1]), t.dtype) for t in tiled) + sum(_nbytes((ts, w), d) for w, d in out_tiles)
    width = max(t.shape[1] for t in tiled)
    outs = pl.pallas_call(
        body, name=name, grid=(S // ts,), in_specs=in_specs, out_specs=out_specs, out_shape=out_shape,
        compiler_params=_params(("arbitrary",), block_bytes, 6 * ts * width * 4),
    )(*tiled, *smalls)
    return outs[:no], outs[no:]


def _colsum(v):
    return jnp.sum(v, axis=0, keepdims=True)


def _rms_mod_fwd(x, gain, shift, scale, name):
    def fn(tiles, smalls):
        (xv,), (g, sh, sc) = tiles, smalls
        inv = lax.rsqrt(jnp.mean(xv * xv, axis=-1, keepdims=True) + RMS_EPS)
        h = (xv * inv) * g * (1.0 + sc) + sh
        return (h, inv), ()

    D = x.shape[1]
    (h, inv), _ = _rowwise(fn, [x], [gain, shift, scale], [(D, BF16), (1, F32)], [], name=name)
    return h, inv


def _gated(dxv, following):
    yv, gate = following
    return dxv * gate, _colsum(dxv * yv)


def _rms_mod_bwd(dh, x, inv, dx_res, gain, scale, name, following=None):
    def fn(tiles, smalls):
        dhv, xv, iv, dres = tiles[:4]
        g, sc = smalls[:2]
        dhv = dhv.astype(F32)
        xhat = xv * iv
        dr = dhv * (1.0 + sc)
        dxhat = dr * g
        dxv = dres + iv * (dxhat - xhat * jnp.mean(dxhat * xhat, axis=-1, keepdims=True))
        sums = (_colsum(dhv), _colsum(dhv * (xhat * g)), _colsum(dr * xhat))
        if following is None:
            return (dxv,), sums
        dy, dgate = _gated(dxv, (tiles[4], smalls[2]))
        return (dxv, dy), (*sums, dgate)

    D = x.shape[1]
    extra = [] if following is None else [following]
    tiles, sums = _rowwise(fn, [dh, x, inv, dx_res] + [f[0] for f in extra], [gain, scale] + [f[1] for f in extra],
                           [(D, F32)] + [(D, BF16)] * len(extra), [D] * (3 + len(extra)), name=name)
    return (tiles[0], *sums[:3]) if following is None else (tiles[0], *sums[:3], tiles[1], sums[3])


def _final_loss_bwd(x, target, gain, following, name):
    D = x.shape[1]

    def fn(tiles, smalls):
        xv, tv, g = tiles[0], tiles[1], smalls[0]
        inv = lax.rsqrt(jnp.mean(xv * xv, axis=-1, keepdims=True) + RMS_EPS)
        xhat = xv * inv
        err = xhat * g - tv
        loss = 0.5 * jnp.sum(jnp.mean(err * err, axis=-1, keepdims=True), axis=0, keepdims=True)
        dout = err * (1.0 / D)
        dxhat = dout * g
        dxv = inv * (dxhat - xhat * jnp.mean(dxhat * xhat, axis=-1, keepdims=True))
        dy, dgate = _gated(dxv, (tiles[2], smalls[1]))
        return (dxv, dy), (_colsum(dout * xhat), jnp.broadcast_to(loss, (1, LANES)), dgate)

    (dx, dy), (dgain, loss, dgate) = _rowwise(fn, [x, target, following[0]], [gain, following[1]],
                                             [(D, F32), (D, BF16)], [D, LANES, D], name=name)
    return dx, dgain, loss, dy, dgate


SCAN_BLOCK = 256


def _triangle(n, lower):
    r = lax.broadcasted_iota(jnp.int32, (n, n), 0)
    c = lax.broadcasted_iota(jnp.int32, (n, n), 1)
    return (r >= c if lower else r <= c).astype(F32)


def _forget_cumsum(logits, bias, name):
    S = logits.shape[0]
    blk = min(SCAN_BLOCK, S)
    nb = S // blk

    def body(z_ref, b_ref, f_ref):
        z = z_ref[...] + b_ref[...]
        f_ref[...] = jnp.minimum(z, 0.0) - jnp.log(1.0 + jnp.exp(-jnp.abs(z)))
        tri = _triangle(blk, lower=True)

        def step(i, carry):
            off = pl.multiple_of(i * blk, blk)
            cs = jnp.dot(tri, f_ref[pl.ds(off, blk), :], preferred_element_type=F32, precision=HIGHEST) + carry
            f_ref[pl.ds(off, blk), :] = cs
            return cs[blk - 1:blk, :]

        lax.fori_loop(0, nb, step, jnp.zeros((1, LANES), F32))

    return pl.pallas_call(body, name=name, out_shape=jax.ShapeDtypeStruct((S, LANES), F32))(logits, bias)


def _forget_bwd(dfk, logits, bias, name):
    S = logits.shape[0]
    blk = min(SCAN_BLOCK, S)
    nb = S // blk

    def body(d_ref, z_ref, b_ref, o_ref, db_ref):
        tri = _triangle(blk, lower=False)

        def step(t, carry):
            off = pl.multiple_of((nb - 1 - t) * blk, blk)
            cs = jnp.dot(tri, d_ref[pl.ds(off, blk), :], preferred_element_type=F32, precision=HIGHEST) + carry
            o_ref[pl.ds(off, blk), :] = cs
            return cs[0:1, :]

        lax.fori_loop(0, nb, step, jnp.zeros((1, LANES), F32))
        z = z_ref[...] + b_ref[...]
        dz = -o_ref[...] / (1.0 + jnp.exp(z))
        o_ref[...] = dz
        db_ref[...] = _colsum(dz)

    return pl.pallas_call(
        body, name=name,
        out_shape=(jax.ShapeDtypeStruct((S, LANES), F32), jax.ShapeDtypeStruct((1, LANES), F32)),
    )(dfk, logits, bias)


ATTN_BLOCK = 512
_NT = (((1,), (1,)), ((), ()))


def _attn_specs(S, H, tb):
    q_blk = lambda part: pl.BlockSpec((tb, HEAD_DIM), lambda h, i: (i, part * H + h))
    q_all = lambda part: pl.BlockSpec((S, HEAD_DIM), lambda h, i: (0, part * H + h))
    col_blk = pl.BlockSpec((None, tb, 1), lambda h, i: (h, i, 0))
    row_all = pl.BlockSpec((None, 1, S), lambda h, i: (h, 0, 0))
    return q_blk, q_all, col_blk, row_all


FWD_HEADS = 2


def _attn_fwd(qkv, f_col, f_row, name):
    S, H = qkv.shape[0], qkv.shape[1] // (3 * HEAD_DIM)
    tb = min(ATTN_BLOCK, S)
    scale = HEAD_DIM ** -0.5
    hp = FWD_HEADS if H % FWD_HEADS == 0 else 1
    groups, wide = H // hp, hp * HEAD_DIM
    lanes = lambda u: pl.ds(u * HEAD_DIM, HEAD_DIM)

    def body(q_ref, k_ref, v_ref, fc_ref, fr_ref, o_ref, lse_ref):
        i = pl.program_id(1)

        def step(j, carry, diagonal):
            off = pl.multiple_of(j * tb, tb)
            out = []
            for u in range(hp):
                m, l, acc = carry[u]
                k, v = k_ref[pl.ds(off, tb), lanes(u)], v_ref[pl.ds(off, tb), lanes(u)]
                s = lax.dot_general(q_ref[:, lanes(u)], k, _NT, preferred_element_type=F32) * scale
                s = s + (fc_ref[u] - fr_ref[u, :, pl.ds(off, tb)])
                if diagonal:
                    row = lax.broadcasted_iota(jnp.int32, (tb, tb), 0)
                    col = lax.broadcasted_iota(jnp.int32, (tb, tb), 1)
                    s = jnp.where(col <= row, s, NEG)
                m_new = jnp.maximum(m, jnp.max(s, axis=-1, keepdims=True))
                p = jnp.exp(s - m_new)
                alpha = jnp.exp(m - m_new)
                l = alpha * l + jnp.sum(p, axis=-1, keepdims=True)
                acc = alpha * acc + jnp.dot(p.astype(BF16), v, preferred_element_type=F32)
                out.append((m_new, l, acc))
            return tuple(out)

        init = (jnp.full((tb, 1), NEG, F32), jnp.zeros((tb, 1), F32), jnp.zeros((tb, HEAD_DIM), F32))
        carry = lax.fori_loop(0, i, lambda j, c: step(j, c, False), (init,) * hp)
        for u, (m, l, acc) in enumerate(step(i, carry, True)):
            o_ref[:, lanes(u)] = (acc / l).astype(o_ref.dtype)
            lse_ref[u] = m + jnp.log(l)

    part = lambda p, rows: pl.BlockSpec((rows, wide), lambda g, i: (i if rows == tb else 0, p * groups + g))
    col_blk = pl.BlockSpec((hp, tb, 1), lambda g, i: (g, i, 0))
    return pl.pallas_call(
        body, name=name, grid=(groups, S // tb),
        in_specs=[part(0, tb), part(1, S), part(2, S), col_blk, pl.BlockSpec((hp, 1, S), lambda g, i: (g, 0, 0))],
        out_specs=[pl.BlockSpec((tb, wide), lambda g, i: (i, g)), col_blk],
        out_shape=[jax.ShapeDtypeStruct((S, H * HEAD_DIM), BF16), jax.ShapeDtypeStruct((H, S, 1), F32)],
        compiler_params=_params(("parallel", "parallel"), 4 * S * wide * 2, 10 * hp * tb * tb * 4),
    )(qkv, qkv, qkv, f_col, f_row)


_TN = (((0,), (0,)), ((), ()))


def _attn_bwd(qkv, do, f_col, f_row, lse_col, name):
    S, H = qkv.shape[0], qkv.shape[1] // (3 * HEAD_DIM)
    tb = min(ATTN_BLOCK, S)
    nq = S // tb
    scale = HEAD_DIM ** -0.5
    q_blk, q_all, col_blk, row_all = _attn_specs(S, H, tb)
    head_blk = pl.BlockSpec((tb, HEAD_DIM), lambda h, i: (i, h))
    head_all = pl.BlockSpec((S, HEAD_DIM), lambda h, i: (0, h))

    def body(q_ref, k_ref, v_ref, do_ref, fc_ref, fr_ref, lse_ref, dq_ref, dk_ref, dv_ref, dfk_ref,
             p_buf, dp_buf, dk_acc, dv_acc, dfk_acc):
        i = pl.program_id(1)
        q, do, fc, lse = q_ref[...], do_ref[...], fc_ref[...], lse_ref[...]

        @pl.when(i == 0)
        def _():
            dk_acc[...] = jnp.zeros_like(dk_acc)
            dv_acc[...] = jnp.zeros_like(dv_acc)
            dfk_acc[...] = jnp.zeros_like(dfk_acc)

        def scores(j, delta, diagonal):
            off = pl.multiple_of(j * tb, tb)
            k, v = k_ref[pl.ds(off, tb), :], v_ref[pl.ds(off, tb), :]
            s = lax.dot_general(q, k, _NT, preferred_element_type=F32) * scale + (fc - fr_ref[:, pl.ds(off, tb)])
            if diagonal:
                row = lax.broadcasted_iota(jnp.int32, (tb, tb), 0)
                col = lax.broadcasted_iota(jnp.int32, (tb, tb), 1)
                s = jnp.where(col <= row, s, NEG)
            p = jnp.exp(s - lse)
            dp = lax.dot_general(do, v, _NT, preferred_element_type=F32)
            p_buf[j] = p
            dp_buf[j] = dp
            return delta + jnp.sum(p * dp, axis=-1, keepdims=True)

        delta = lax.fori_loop(0, i, lambda j, c: scores(j, c, False), jnp.zeros((tb, 1), F32))
        delta = scores(i, delta, True)

        def grad(j, dq):
            off = pl.multiple_of(j * tb, tb)
            p = p_buf[j]
            ds = p * (dp_buf[j] - delta)
            ds_lo = ds.astype(BF16)
            dk_acc[pl.ds(off, tb), :] += lax.dot_general(ds_lo, q, _TN, preferred_element_type=F32)
            dv_acc[pl.ds(off, tb), :] += lax.dot_general(p.astype(BF16), do, _TN, preferred_element_type=F32)
            dfk_acc[:, pl.ds(off, tb)] += jnp.sum(ds, axis=0, keepdims=True)
            return dq + jnp.dot(ds_lo, k_ref[pl.ds(off, tb), :], preferred_element_type=F32)

        dq = lax.fori_loop(0, i + 1, grad, jnp.zeros((tb, HEAD_DIM), F32))
        dq_ref[...] = (dq * scale).astype(dq_ref.dtype)

        @pl.when(i == nq - 1)
        def _():
            dk_ref[...] = (dk_acc[...] * scale).astype(dk_ref.dtype)
            dv_ref[...] = dv_acc[...].astype(dv_ref.dtype)
            dfk_ref[...] = dfk_acc[...]

    wide = jax.ShapeDtypeStruct((S, H * HEAD_DIM), BF16)
    return pl.pallas_call(
        body, name=name, grid=(H, nq),
        in_specs=[q_blk(0), q_all(1), q_all(2), head_blk, col_blk, row_all, col_blk],
        out_specs=[head_blk, head_all, head_all, row_all],
        out_shape=[wide, wide, wide, jax.ShapeDtypeStruct((H, 1, S), F32)],
        scratch_shapes=[pltpu.VMEM((nq, tb, tb), F32), pltpu.VMEM((nq, tb, tb), F32),
                        pltpu.VMEM((S, HEAD_DIM), F32), pltpu.VMEM((S, HEAD_DIM), F32), pltpu.VMEM((1, S), F32)],
        compiler_params=_params(("parallel", "arbitrary"), 6 * S * HEAD_DIM * 2,
                                2 * nq * tb * tb * 4 + 2 * S * HEAD_DIM * 4 + 10 * tb * tb * 4),
    )(qkv, qkv, qkv, do, f_col, f_row, lse_col)


CONV_TILE = 128


def _shift_down(v, n):
    row = lax.broadcasted_iota(jnp.int32, v.shape, 0)
    return jnp.where(row >= n, pltpu.roll(v, n, 0), 0.0)


def _shift_up(v, n):
    S = v.shape[0]
    row = lax.broadcasted_iota(jnp.int32, v.shape, 0)
    return jnp.where(row < S - n, pltpu.roll(v, S - n, 0), 0.0)


def _conv_specs(S, D, tc):
    nb = D // tc
    part = lambda p: pl.BlockSpec((S, tc), lambda j: (0, p * nb + j))
    return part, pl.BlockSpec((S, tc), lambda j: (0, j)), pl.BlockSpec((8, tc), lambda j: (0, j))


def _conv_fwd(proj, conv_w8, name):
    S, D = proj.shape[0], proj.shape[1] // 3
    tc = min(CONV_TILE, D)
    part, chan, taps = _conv_specs(S, D, tc)

    def body(b_ref, c_ref, u_ref, w_ref, z_ref):
        cu = c_ref[...].astype(F32) * u_ref[...].astype(F32)
        w = w_ref[...]
        y = w[0:1, :] * _shift_down(cu, 2) + w[1:2, :] * _shift_down(cu, 1) + w[2:3, :] * cu
        z_ref[...] = (b_ref[...].astype(F32) * y).astype(z_ref.dtype)

    return pl.pallas_call(
        body, name=name, grid=(D // tc,), in_specs=[part(0), part(1), part(2), taps], out_specs=chan,
        out_shape=jax.ShapeDtypeStruct((S, D), BF16),
        compiler_params=_params(("parallel",), 3 * _nbytes((S, tc), proj.dtype) + S * tc * 2, 6 * S * tc * 4),
    )(proj, proj, proj, conv_w8)


def _conv_bwd(proj, dz, conv_w8, name):
    S, D = proj.shape[0], proj.shape[1] // 3
    tc = min(CONV_TILE, D)
    part, chan, taps = _conv_specs(S, D, tc)

    def body(b_ref, c_ref, u_ref, dz_ref, w_ref, db_ref, dc_ref, du_ref, dw_ref):
        cv, uv = c_ref[...].astype(F32), u_ref[...].astype(F32)
        dzv, w = dz_ref[...].astype(F32), w_ref[...]
        cu = cv * uv
        cu1, cu2 = _shift_down(cu, 1), _shift_down(cu, 2)
        y = w[0:1, :] * cu2 + w[1:2, :] * cu1 + w[2:3, :] * cu
        db_ref[...] = (dzv * y).astype(db_ref.dtype)
        dy = dzv * b_ref[...].astype(F32)
        dcu = w[2:3, :] * dy + w[1:2, :] * _shift_up(dy, 1) + w[0:1, :] * _shift_up(dy, 2)
        dc_ref[...] = (dcu * uv).astype(dc_ref.dtype)
        du_ref[...] = (dcu * cv).astype(du_ref.dtype)
        dw_ref[...] = jnp.concatenate(
            [_colsum(dy * cu2), _colsum(dy * cu1), _colsum(dy * cu), jnp.zeros((8 - CONV_WIDTH, tc), F32)], axis=0)

    return pl.pallas_call(
        body, name=name, grid=(D // tc,), in_specs=[part(0), part(1), part(2), chan, taps],
        out_specs=[chan, chan, chan, taps],
        out_shape=[jax.ShapeDtypeStruct((S, D), BF16)] * 3 + [jax.ShapeDtypeStruct((8, D), F32)],
        compiler_params=_params(("parallel",), 3 * _nbytes((S, tc), proj.dtype) + _nbytes((S, tc), dz.dtype)
                                + 3 * S * tc * 2, 10 * S * tc * 4),
    )(proj, proj, proj, dz, conv_w8)


def _adamw(w, m, v, parts, name, layer=0, prev=None):
    L, R, C = w.shape
    P = parts.shape[0]
    assert parts.shape[1:] == (R, C), (name, parts.shape, w.shape)
    elem_bytes = 12 + 16 + P * parts.dtype.itemsize
    budget = 4 << 20
    tr, tc = R, C
    if R * C * elem_bytes > budget:
        if R % 8 == 0:
            tr = max(8, (budget // (C * elem_bytes)) // 8 * 8)
            while R % tr:
                tr -= 8
        else:
            tc = LANES
            while C % (2 * tc) == 0 and R * 2 * tc * elem_bytes <= budget:
                tc *= 2
            assert C % tc == 0, (name, R, C)
    c1, c2 = 1.0 - ADAM_B1 ** ADAM_STEP, 1.0 - ADAM_B2 ** ADAM_STEP

    def body(w_ref, m_ref, v_ref, p_ref, *rest):
        g_ref, d_ref, nm_ref, nv_ref = rest[-4:]
        g = p_ref[0].astype(F32)
        for p in range(1, P):
            g = g + p_ref[p].astype(F32)
        nm = ADAM_B1 * m_ref[...] + (1.0 - ADAM_B1) * g
        nv = ADAM_B2 * v_ref[...] + (1.0 - ADAM_B2) * (g * g)
        g_ref[...] = g
        nm_ref[...] = nm
        nv_ref[...] = nv
        d_ref[...] = -ADAM_LR * ((nm / c1) / (jnp.sqrt(nv / c2) + ADAM_EPS) + ADAM_WD * w_ref[...])

    blk = pl.BlockSpec((None, tr, tc), lambda i, j: (layer, i, j))
    prev = [] if prev is None else list(prev)
    return pl.pallas_call(
        body, name=name, grid=(R // tr, C // tc),
        in_specs=[blk, blk, blk, pl.BlockSpec((P, tr, tc), lambda i, j: (0, i, j))] + [ANY] * len(prev),
        out_specs=[blk] * 4, out_shape=[jax.ShapeDtypeStruct((L, R, C), F32)] * 4,
        input_output_aliases={4 + k: k for k in range(len(prev))},
        compiler_params=_params(("parallel", "parallel"), tr * tc * elem_bytes),
    )(w, m, v, parts, *prev)


def _silu(v):
    return v / (1.0 + jnp.exp(-v))


def _pad_rows(a, rows):
    return jnp.pad(a, ((0, rows - a.shape[0]), (0, 0)))


def _pad_cols(a, cols):
    return jnp.pad(a, ((0, 0), (0, cols - a.shape[1])))


def kernel(x, c, ada_w, ada_b, norm_mix, norm_mlp, fox_w_in, fox_b_f, fox_w_out, conv_w_in, conv_w, conv_w_out, mlp_w_up, mlp_w_down, final_norm, loss_target, m_ada_w, m_ada_b, m_norm_mix, m_norm_mlp, m_fox_w_in, m_fox_b_f, m_fox_w_out, m_conv_w_in, m_conv_w, m_conv_w_out, m_mlp_w_up, m_mlp_w_down, m_final_norm, v_ada_w, v_ada_b, v_norm_mix, v_norm_mlp, v_fox_w_in, v_fox_b_f, v_fox_w_out, v_conv_w_in, v_conv_w, v_conv_w_out, v_mlp_w_up, v_mlp_w_down, v_final_norm):
    S, D = x.shape[1], x.shape[2]
    H = D // HEAD_DIM
    FF = mlp_w_up.shape[2] * NDEV
    depth = ada_w.shape[0]
    n_mod = 6
    assert depth == 2 and fox_w_in.shape[0] == 1 and conv_w_in.shape[0] == 1 and H <= LANES
    me = _my_index()
    x0, target = x[0], loss_target[0]
    row = lambda vec: vec.reshape(1, -1)

    def tied(vec, token):
        return vec + token[0, 0]

    bf = lambda w: w.astype(BF16)
    gather_groups = {
        "fox": [bf(fox_w_in[0]).T, bf(fox_w_out[0])],
        "mlp0": [bf(mlp_w_up[0]).T, bf(mlp_w_down[0])],
        "conv": [bf(conv_w_in[0]).T, conv_w[0], bf(conv_w_out[0])],
        "mlp1": [bf(mlp_w_up[1]).T, bf(mlp_w_down[1])],
    }

    def start_gather(group, after):
        return _exchange_start(gather_groups[group], f"gather_{group}_start", False, after, relay=True)

    def relay_gather(handle, group, after):
        handle = _exchange_wait(handle, f"gather_{group}_arrivals", after, arrivals_only=True)
        return _relay_forward_start(handle, f"gather_{group}_forward")

    def finish_gather(handle, group, after):
        return _exchange_wait(handle, f"gather_{group}_wait", after)

    landed = lambda handle: handle[4][0]

    c_all = _all_gather([c], "gather_cond")[0].reshape(NDEV, D)
    ncol = ada_w.shape[2]
    ada_b_mine = lax.dynamic_slice_in_dim(ada_b, me * ncol, ncol, axis=1)
    mod_cols = jnp.stack([
        _matmul(c_all, ada_w[i], mode="nn", name=f"ada_fwd_{i}", out_dtypes=[F32], tm=NDEV, tn=ncol // 2, tk=D,
                a_pre=_silu, precision=HIGHEST, epilogue=lambda acc, b: (acc + b,), extras=[(ada_b_mine[i:i + 1], "row")])
        for i in range(depth)])
    mod_all = _all_gather([mod_cols], "gather_mod")[0]
    mod = lax.dynamic_index_in_dim(mod_all, me, axis=2, keepdims=False)
    fox_handle, token = start_gather("fox", mod_all)
    mod = tied(mod, token).transpose(1, 0, 2).reshape(depth, n_mod, 1, D)
    sh_mix, sc_mix, g_mix, sh_mlp, sc_mlp, g_mlp = (mod[:, k] for k in range(n_mod))
    b_f = _pad_cols(fox_b_f, LANES)

    def residual(acc, x_in, gate):
        return (x_in + gate * acc, acc)

    def mlp_fwd(i, x_in, handle, relay_next=None):
        h, inv = _rms_mod_fwd(x_in, row(norm_mlp[i]), sh_mlp[i], sc_mlp[i], f"mlp_norm_{i}")
        w_up_t, w_down = finish_gather(handle, f"mlp{i}", h)
        w_up_t, w_down = w_up_t.reshape(FF, D), w_down.reshape(FF, D)
        r, a = _matmul(h, w_up_t, mode="nt", name=f"mlp_up_{i}", out_dtypes=[BF16, BF16], tm=1024, tn=1024, tk=D,
                       epilogue=lambda acc: (jnp.maximum(acc, 0.0), jnp.square(jnp.maximum(acc, 0.0))))
        next_handle = relay_gather(relay_next[1], relay_next[0], a) if relay_next else None
        x_out, y = _matmul(a, w_down, mode="nn", name=f"mlp_down_{i}", out_dtypes=[F32, BF16], tm=512, tn=512, tk=FF,
                           n_outer=True, epilogue=residual, extras=[(x_in, "tile"), (g_mlp[i], "row")],
                           after=[landed(next_handle)] if relay_next else [])
        return x_out, (x_in, h, inv, r, a, y, w_up_t, w_down), next_handle

    def mlp_bwd(i, dx, dy, dgate, saved, following, after):
        x_in, h, inv, r, a, y, w_up_t, w_down = saved
        du = _matmul(dy, w_down, mode="nt", name=f"mlp_down_bwd_{i}", out_dtypes=[BF16], tm=1024, tn=1024, tk=D,
                     epilogue=lambda acc, rv: (acc * (2.0 * rv.astype(F32)),), extras=[(r, "tile")], after=after)
        d_down = _matmul(a, dy, mode="tn", name=f"mlp_down_wgrad_{i}", out_dtypes=[BF16], tm=512, tn=1024, tk=S)
        dh = _matmul(du, w_up_t, mode="nn", name=f"mlp_up_bwd_{i}", out_dtypes=[F32], tm=512, tn=512, tk=FF, n_outer=True)
        d_up = _matmul(h, du, mode="tn", name=f"mlp_up_wgrad_{i}", out_dtypes=[BF16], tm=512, tn=FF // NDEV, tk=S,
                       out_shards=True)
        dx, dsh, dsc, dgain, dy_next, dgate_next = _rms_mod_bwd(dh, x_in, inv, dx, row(norm_mlp[i]), sc_mlp[i],
                                                                f"mlp_norm_bwd_{i}", following)
        handle, token = _exchange_start([d_up, d_down.reshape(NDEV, FF // NDEV, D)], f"scatter_mlp{i}_start", True, dx)
        return dx, (dsh, dsc, dgate, dgain), handle, token, dy_next, dgate_next

    h0, inv0 = _rms_mod_fwd(x0, row(norm_mix[0]), sh_mix[0], sc_mix[0], "fox_norm")
    w_in_t, w_fox_out = finish_gather(relay_gather(fox_handle, "fox", h0), "fox", h0)
    mlp0_handle, token = start_gather("mlp0", w_in_t)
    w_in_t = w_in_t.reshape(3 * D + H, D)
    w_f_t = _pad_rows(w_in_t[3 * D:], LANES)
    w_fox_out = w_fox_out.reshape(D, D)
    qkv = _matmul(h0, w_in_t, mode="nt", name="fox_qkv", out_dtypes=[BF16], tm=1024, tn=1024, tk=D, n=3 * D, after=[token])
    f_logit = _matmul(h0, w_f_t, mode="nt", name="fox_forget_logits", out_dtypes=[F32], tm=1024, tn=LANES, tk=D)
    f_cum = _forget_cumsum(f_logit, b_f, "fox_forget_cumsum")
    f_heads = f_cum[:, :H].T
    f_col, f_row = f_heads.reshape(H, S, 1), f_heads.reshape(H, 1, S)
    o, lse = _attn_fwd(qkv, f_col, f_row, "fox_attention")
    mlp0_handle = relay_gather(mlp0_handle, "mlp0", o)
    conv_handle, token = start_gather("conv", landed(mlp0_handle))
    mlp1_handle, token = start_gather("mlp1", token)
    x1, mix0 = _matmul(o, w_fox_out, mode="nn", name="fox_out", out_dtypes=[F32, BF16], tm=512, tn=1024, tk=D,
                       epilogue=residual, extras=[(x0, "tile"), (g_mix[0], "row")], after=[token])
    x2, mlp0, conv_handle = mlp_fwd(0, x1, mlp0_handle, ("conv", conv_handle))

    h1, inv1 = _rms_mod_fwd(x2, row(norm_mix[1]), sh_mix[1], sc_mix[1], "conv_norm")
    w_conv_in_t, w_taps, w_conv_out = finish_gather(conv_handle, "conv", h1)
    w_conv_in_t = w_conv_in_t.reshape(3 * D, D)
    w_taps = _pad_rows(w_taps.transpose(1, 0, 2).reshape(CONV_WIDTH, D), 8)
    w_conv_out = w_conv_out.reshape(D, D)
    proj = _matmul(h1, w_conv_in_t, mode="nt", name="conv_in", out_dtypes=[BF16], tm=1024, tn=1024, tk=D)
    mlp1_handle = relay_gather(mlp1_handle, "mlp1", proj)
    z = _conv_fwd(proj, w_taps, "conv_mix")
    x3, mix1 = _matmul(z, w_conv_out, mode="nn", name="conv_out", out_dtypes=[F32, BF16], tm=512, tn=1024, tk=D,
                       epilogue=residual, extras=[(x2, "tile"), (g_mix[1], "row")], after=[landed(mlp1_handle)])
    x4, mlp1, _ = mlp_fwd(1, x3, mlp1_handle)

    dx, d_final, loss_lanes, dy, dgate = _final_loss_bwd(x4, target, row(final_norm), (mlp1[5], g_mlp[1]), "loss_head")

    dx, dmod_mlp1, mlp1_scatter, token, dmix, dg_mix1 = mlp_bwd(1, dx, dy, dgate, mlp1, (mix1, g_mix[1]), [])
    dz = _matmul(dmix, w_conv_out, mode="nt", name="conv_out_bwd", out_dtypes=[BF16], tm=1024, tn=1024, tk=D,
                 after=[token])
    d_conv_out = _matmul(z, dmix, mode="tn", name="conv_out_wgrad", out_dtypes=[BF16], tm=512, tn=1024, tk=S)
    db, dc, du, d_taps = _conv_bwd(proj, dz, w_taps, "conv_mix_bwd")
    dproj = jnp.concatenate([db, dc, du], axis=1)
    dh1 = _matmul(dproj, w_conv_in_t, mode="nn", name="conv_in_bwd", out_dtypes=[F32], tm=512, tn=512, tk=3 * D, n_outer=True)
    d_conv_in = _matmul(h1, dproj, mode="tn", name="conv_in_wgrad", out_dtypes=[BF16], tm=512, tn=3 * D // NDEV, tk=S,
                        out_shards=True)
    dx, dsh1, dsc1, dgain_mix1, dy, dgate = _rms_mod_bwd(dh1, x2, inv1, dx, row(norm_mix[1]), sc_mix[1], "conv_norm_bwd",
                                                         (mlp0[5], g_mlp[0]))
    d_taps_split = d_taps[:CONV_WIDTH].reshape(CONV_WIDTH, NDEV, -1).transpose(1, 0, 2)
    conv_scatter, token = _exchange_start([d_conv_in, d_taps_split, d_conv_out.reshape(NDEV, D // NDEV, D)],
                                          "scatter_conv_start", True, dx)

    dx, dmod_mlp0, mlp0_scatter, token, dmix, dg_mix0 = mlp_bwd(0, dx, dy, dgate, mlp0, (mix0, g_mix[0]), [token])
    do = _matmul(dmix, w_fox_out, mode="nt", name="fox_out_bwd", out_dtypes=[BF16], tm=1024, tn=1024, tk=D,
                 after=[token])
    d_fox_out = _matmul(o, dmix, mode="tn", name="fox_out_wgrad", out_dtypes=[BF16], tm=512, tn=1024, tk=S)
    dq, dk, dv, dfk = _attn_bwd(qkv, do, f_col, f_row, lse, "fox_attention_bwd")
    dqkv = jnp.concatenate([dq, dk, dv], axis=1)
    dfk_lanes = _pad_cols(dfk.reshape(H, S).T, LANES)
    df_logit, db_f = _forget_bwd(dfk_lanes, f_logit, b_f, "fox_forget_bwd")
    d_qkv_t = _matmul(dqkv, h0, mode="tn", name="fox_qkv_wgrad", out_dtypes=[BF16], tm=512, tn=1024, tk=S)
    d_f_t = _matmul(df_logit, h0, mode="tn", name="fox_forget_wgrad", out_dtypes=[BF16], tm=LANES, tn=1024, tk=S)
    d_fox_in = jnp.concatenate([d_qkv_t, d_f_t[:H]], axis=0).reshape(NDEV, -1, D)
    fox_scatter, token = _exchange_start([d_fox_in, d_fox_out.reshape(NDEV, D // NDEV, D)], "scatter_fox_start", True,
                                         d_fox_in)
    dh0_f = _matmul(df_logit, w_f_t, mode="nn", name="fox_forget_logits_bwd", out_dtypes=[F32], tm=1024, tn=1024, tk=LANES,
                    after=[token])
    dh0 = _matmul(dqkv, w_in_t, mode="nn", name="fox_qkv_bwd", out_dtypes=[F32], tm=512, tn=512, tk=3 * D, n_outer=True,
                  epilogue=lambda acc, extra: (acc + extra,), extras=[(dh0_f, "tile")])
    dx, dsh0, dsc0, dgain_mix0 = _rms_mod_bwd(dh0, x0, inv0, dx, row(norm_mix[0]), sc_mix[0], "fox_norm_bwd")
    grad_x = dx.reshape(1, S, D)

    up1, down1 = _exchange_wait(mlp1_scatter, "scatter_mlp1_wait", dx)
    up_out = _adamw(mlp_w_up, m_mlp_w_up, v_mlp_w_up, up1, "adamw_mlp_w_up_1", layer=1)
    down_out = _adamw(mlp_w_down, m_mlp_w_down, v_mlp_w_down, down1, "adamw_mlp_w_down_1", layer=1)
    cin, taps, cout = _exchange_wait(conv_scatter, "scatter_conv_wait", down_out[0])
    conv_in_out = _adamw(conv_w_in, m_conv_w_in, v_conv_w_in, cin, "adamw_conv_w_in")
    conv_w_res = _adamw(conv_w, m_conv_w, v_conv_w, taps, "adamw_conv_w")
    conv_out_out = _adamw(conv_w_out, m_conv_w_out, v_conv_w_out, cout, "adamw_conv_w_out")
    up0, down0 = _exchange_wait(mlp0_scatter, "scatter_mlp0_wait", conv_out_out[0])
    up_out = _adamw(mlp_w_up, m_mlp_w_up, v_mlp_w_up, up0, "adamw_mlp_w_up_0", layer=0, prev=up_out)
    down_out = _adamw(mlp_w_down, m_mlp_w_down, v_mlp_w_down, down0, "adamw_mlp_w_down_0", layer=0, prev=down_out)

    dmod = jnp.concatenate([
        jnp.concatenate([dsh0, dsc0, dg_mix0, dmod_mlp0[0], dmod_mlp0[1], dmod_mlp0[2]], axis=1),
        jnp.concatenate([dsh1, dsc1, dg_mix1, dmod_mlp1[0], dmod_mlp1[1], dmod_mlp1[2]], axis=1)], axis=0)
    small_sizes = [depth * n_mod * D, depth * D, depth * D, H, D, 1]
    n_small = sum(small_sizes)
    n_rows = -(-n_small // (8 * LANES)) * 8

    def pack(parts):
        flat = jnp.concatenate([p.reshape(-1) for p in parts])
        return jnp.pad(flat, (0, n_rows * LANES - n_small)).reshape(n_rows, LANES)

    def unpack(packed, shapes):
        flat, out, at = packed.reshape(-1), [], 0
        for size, shape in zip(small_sizes, shapes):
            out.append(flat[at:at + size].reshape(shape))
            at += size
        return out

    small_partial = pack([dmod, jnp.concatenate([dgain_mix0, dgain_mix1], axis=0),
                          jnp.concatenate([dmod_mlp0[3], dmod_mlp1[3]], axis=0), db_f[0, :H], d_final, loss_lanes[0, :1]])
    small_parts = _all_gather([small_partial], "gather_small_grads", after=[down_out[0]])[0]
    small_shapes = [ada_b.shape, norm_mix.shape, norm_mlp.shape, fox_b_f.shape, final_norm.shape]
    loss = jnp.sum(small_parts.reshape(NDEV, -1)[:, n_small - 1])
    unused = jnp.zeros((1,), F32)
    small_out = _adamw(pack([ada_b, norm_mix, norm_mlp, fox_b_f, final_norm, unused])[None],
                       pack([m_ada_b, m_norm_mix, m_norm_mlp, m_fox_b_f, m_final_norm, unused])[None],
                       pack([v_ada_b, v_norm_mix, v_norm_mlp, v_fox_b_f, v_final_norm, unused])[None], small_parts,
                       "adamw_small")
    small_out = [unpack(t, small_shapes) for t in small_out]

    dmod_all = small_parts.reshape(NDEV, -1)[:, :depth * n_mod * D].reshape(NDEV, depth, n_mod * D)
    dmod_mine = lax.dynamic_slice_in_dim(dmod_all, me * ncol, ncol, axis=2)
    ada_out = None
    for i in range(depth):
        d_ada = _matmul(c_all, dmod_mine[:, i], mode="tn", name=f"ada_wgrad_{i}", out_dtypes=[F32], tm=1024, tn=ncol // 2,
                        tk=NDEV, a_pre=_silu, precision=HIGHEST)
        ada_out = _adamw(ada_w, m_ada_w, v_ada_w, d_ada[None], f"adamw_ada_w_{i}", layer=i, prev=ada_out)

    fin, fout = _exchange_wait(fox_scatter, "scatter_fox_wait", ada_out[0])
    swap = lambda t: jnp.swapaxes(t, 1, 2)
    fox_in_out = [swap(t) for t in _adamw(swap(fox_w_in), swap(m_fox_w_in), swap(v_fox_w_in), fin, "adamw_fox_w_in")]
    fox_out_out = _adamw(fox_w_out, m_fox_w_out, v_fox_w_out, fout, "adamw_fox_w_out")

    outputs = [loss, grad_x]
    for kind in range(4):
        sm = small_out[kind]
        outputs += [ada_out[kind], sm[0], sm[1], sm[2], fox_in_out[kind], sm[3], fox_out_out[kind], conv_in_out[kind],
                    conv_w_res[kind], conv_out_out[kind], up_out[kind], down_out[kind], sm[4]]
    return tuple(outputs)
```

```python
import math

import jax
import jax.numpy as jnp
from jax import lax
from jax.experimental import pallas as pl
from jax.experimental.pallas import tpu as pltpu

F32 = jnp.float32
BF16 = jnp.bfloat16
MESH = pl.DeviceIdType.MESH
NDEV = 8
HEAD_DIM = 128
LANES = 128
CONV_WIDTH = 3
RMS_EPS = 1e-6
ADAM_LR, ADAM_B1, ADAM_B2, ADAM_EPS, ADAM_WD, ADAM_STEP = 0.001, 0.9, 0.999, 1e-08, 0.01, 10
NEG = -1e30
V7X_VMEM_BYTES = 64 * 1024 * 1024
VMEM_HEADROOM = 12 * 1024 * 1024
HBM = pl.BlockSpec(memory_space=pltpu.HBM)
HIGHEST = lax.Precision.HIGHEST


def _nbytes(shape, dtype):
    return math.prod(shape) * jnp.dtype(dtype).itemsize


def _params(semantics, block_bytes, temp_bytes=0):
    limit = min(2 * block_bytes + temp_bytes + VMEM_HEADROOM, V7X_VMEM_BYTES - 4 * 1024 * 1024)
    return pltpu.CompilerParams(dimension_semantics=semantics, vmem_limit_bytes=int(limit))


def _my_index():
    return lax.axis_index("x") * 4 + lax.axis_index("y") * 2 + lax.axis_index("c")


def _peer(r):
    x, y, c = lax.axis_index("x"), lax.axis_index("y"), lax.axis_index("c")
    px = 1 - x if (r >> 2) & 1 else x
    py = 1 - y if (r >> 1) & 1 else y
    pc = 1 - c if r & 1 else c
    return (px, py, pc), px * 4 + py * 2 + pc


def _exchange(arrays, name, scatter, after=None):
    n = len(arrays)
    after = [] if after is None else list(after)

    def body(*refs):
        ins, outs = refs[:n], refs[n + len(after):2 * n + len(after)]
        send_sems, recv_sems, local_sems = refs[2 * n + len(after):]
        me = _my_index()
        local = []
        for a in range(n):
            src = ins[a].at[me] if scatter else ins[a]
            local.append(pltpu.make_async_copy(src, outs[a].at[me], local_sems.at[a]))
            local[-1].start()
        sends = []
        for r in range(1, NDEV):
            peer, pidx = _peer(r)
            for a in range(n):
                src = ins[a].at[pidx] if scatter else ins[a]
                cp = pltpu.make_async_remote_copy(
                    src_ref=src, dst_ref=outs[a].at[me],
                    send_sem=send_sems.at[a * (NDEV - 1) + r - 1], recv_sem=recv_sems.at[a * (NDEV - 1) + r - 1],
                    device_id=peer, device_id_type=MESH)
                cp.start()
                sends.append(cp)
        for r in range(1, NDEV):
            peer, pidx = _peer(r)
            for a in range(n):
                src = ins[a].at[pidx] if scatter else ins[a]
                pltpu.make_async_remote_copy(
                    src_ref=src, dst_ref=outs[a].at[pidx],
                    send_sem=send_sems.at[a * (NDEV - 1) + r - 1], recv_sem=recv_sems.at[a * (NDEV - 1) + r - 1],
                    device_id=peer, device_id_type=MESH).wait_recv()
        for cp in sends:
            cp.wait_send()
        for cp in local:
            cp.wait()

    out_shape = [jax.ShapeDtypeStruct(a.shape if scatter else (NDEV,) + a.shape, a.dtype) for a in arrays]
    return pl.pallas_call(
        body, name=name, out_shape=out_shape, in_specs=[HBM] * n + [ANY] * len(after), out_specs=[HBM] * n,
        scratch_shapes=[pltpu.SemaphoreType.DMA((n * (NDEV - 1),)), pltpu.SemaphoreType.DMA((n * (NDEV - 1),)),
                        pltpu.SemaphoreType.DMA((n,))],
    )(*arrays, *after)


def _all_gather(arrays, name, after=None):
    return _exchange(arrays, name, scatter=False, after=after)


SEM = pl.BlockSpec(memory_space=pltpu.SEMAPHORE)
ANY = pl.BlockSpec(memory_space=pl.ANY)
DATAFLOW = pltpu.SideEffectType.DATAFLOW_SIDE_EFFECTING
TOKEN_SHAPE = (8, LANES)


SIBLING = 1
OTHER_CHIPS = (4, 2, 6)


def _exchange_start(arrays, name, scatter, after, relay=False):
    n = len(arrays)
    n_sems = n * (NDEV - 1)
    assert not (relay and scatter)

    def body(*refs):
        ins = refs[:n]
        send_sems, recv_sems = refs[n + 1], refs[n + 2]
        lands, token = refs[2 * n + 3:3 * n + 3], refs[3 * n + 3]
        me = _my_index()
        for r in (SIBLING, *OTHER_CHIPS) if relay else range(1, NDEV):
            peer, pidx = _peer(r)
            for a in range(n):
                src = ins[a].at[pidx] if scatter else ins[a]
                pltpu.make_async_remote_copy(
                    src_ref=src, dst_ref=lands[a].at[me],
                    send_sem=send_sems.at[a * (NDEV - 1) + r - 1], recv_sem=recv_sems.at[a * (NDEV - 1) + r - 1],
                    device_id=peer, device_id_type=MESH).start()
        token[...] = jnp.zeros(TOKEN_SHAPE, F32)

    land_shapes = [a.shape if scatter else (NDEV,) + a.shape for a in arrays]
    srcs = [pltpu.with_memory_space_constraint(a, pltpu.HBM) for a in arrays]
    outs = pl.pallas_call(
        body, name=name,
        out_shape=(pltpu.SemaphoreType.DMA((n_sems,)), pltpu.SemaphoreType.DMA((n_sems,)),
                   *[pltpu.HBM(a.shape, a.dtype) for a in arrays], *[pltpu.HBM(s, a.dtype) for s, a in zip(land_shapes, arrays)],
                   jax.ShapeDtypeStruct(TOKEN_SHAPE, F32)),
        in_specs=[HBM] * n + [ANY],
        out_specs=(SEM, SEM, *[HBM] * (2 * n), pl.BlockSpec(memory_space=pltpu.VMEM)),
        input_output_aliases={i: 2 + i for i in range(n)},
        compiler_params=pltpu.CompilerParams(has_side_effects=DATAFLOW),
    )(*srcs, after)
    return (scatter, relay, [(outs[0], outs[1])], list(outs[2:2 + n]), list(outs[2 + n:2 + 2 * n])), outs[-1]


def _relay_forward_start(handle, name):
    scatter, relay, sems, srcs, lands = handle
    n = len(lands)
    n_sems = n * len(OTHER_CHIPS)

    def body(*refs):
        land_refs, send_sems, recv_sems = refs[:n], refs[n], refs[n + 1]
        sibling, _ = _peer(SIBLING)
        for j, r in enumerate(OTHER_CHIPS):
            _, pidx = _peer(r)
            for a in range(n):
                pltpu.make_async_remote_copy(
                    src_ref=land_refs[a].at[pidx], dst_ref=land_refs[a].at[pidx],
                    send_sem=send_sems.at[a * len(OTHER_CHIPS) + j], recv_sem=recv_sems.at[a * len(OTHER_CHIPS) + j],
                    device_id=sibling, device_id_type=MESH).start()

    outs = pl.pallas_call(
        body, name=name,
        out_shape=(pltpu.SemaphoreType.DMA((n_sems,)), pltpu.SemaphoreType.DMA((n_sems,)),
                   *[pltpu.HBM(t.shape, t.dtype) for t in lands]),
        in_specs=[HBM] * n, out_specs=(SEM, SEM, *[HBM] * n),
        input_output_aliases={i: 2 + i for i in range(n)},
        compiler_params=pltpu.CompilerParams(has_side_effects=DATAFLOW),
    )(*lands)
    return (scatter, relay, sems + [(outs[0], outs[1])], srcs, list(outs[2:]))


def _exchange_wait(handle, name, after, arrivals_only=False):
    scatter, relay, sems, srcs, lands = handle
    n = len(srcs)
    forwarded = len(sems) == 2
    assert not arrivals_only or (relay and not forwarded)

    def body(*refs):
        src_refs, land_refs = refs[:n], refs[n:2 * n]
        send_sems, recv_sems = refs[2 * n], refs[2 * n + 1]
        for r in (SIBLING, *OTHER_CHIPS) if relay else range(1, NDEV):
            peer, pidx = _peer(r)
            for a in range(n):
                src = src_refs[a].at[pidx] if scatter else src_refs[a]
                cp = pltpu.make_async_remote_copy(
                    src_ref=src, dst_ref=land_refs[a].at[pidx],
                    send_sem=send_sems.at[a * (NDEV - 1) + r - 1], recv_sem=recv_sems.at[a * (NDEV - 1) + r - 1],
                    device_id=peer, device_id_type=MESH)
                if arrivals_only:
                    if r in OTHER_CHIPS:
                        cp.wait_recv()
                else:
                    cp.wait_send()
                    if not (relay and r in OTHER_CHIPS):
                        cp.wait_recv()
        if forwarded:
            fwd_send, fwd_recv = refs[2 * n + 2], refs[2 * n + 3]
            sibling, _ = _peer(SIBLING)
            for j, r in enumerate(OTHER_CHIPS):
                _, pidx = _peer(r ^ SIBLING)
                for a in range(n):
                    cp = pltpu.make_async_remote_copy(
                        src_ref=src_refs[a], dst_ref=land_refs[a].at[pidx],
                        send_sem=fwd_send.at[a * len(OTHER_CHIPS) + j], recv_sem=fwd_recv.at[a * len(OTHER_CHIPS) + j],
                        device_id=sibling, device_id_type=MESH)
                    cp.wait_send()
                    cp.wait_recv()

    flat_sems = [s for pair in sems for s in pair]
    outs = pl.pallas_call(
        body, name=name,
        out_shape=tuple(pltpu.HBM(t.shape, t.dtype) for t in (*srcs, *lands)),
        in_specs=[HBM] * (2 * n) + [SEM] * len(flat_sems) + [ANY], out_specs=tuple([HBM] * (2 * n)),
        input_output_aliases={i: i for i in range(2 * n)},
        compiler_params=pltpu.CompilerParams(has_side_effects=DATAFLOW),
    )(*srcs, *lands, *flat_sems, after)
    if arrivals_only:
        return (scatter, relay, sems, list(outs[:n]), list(outs[n:]))
    me = _my_index()
    mine = [lax.dynamic_index_in_dim(s, me, 0, keepdims=False) if scatter else s for s in outs[:n]]
    return [lax.dynamic_update_index_in_dim(land, own, me, 0) for land, own in zip(outs[n:], mine)]


def _matmul(a, b, *, mode, name, out_dtypes, tm, tn, tk, epilogue=None, extras=(), a_pre=None,
            out_shards=False, n_outer=False, precision=None, after=(), n=None, b_first_block=0):
    after = list(after)
    n_after = len(after)
    K, M = a.shape if mode == "tn" else a.shape[::-1]
    N = n if n is not None else (b.shape[0] if mode == "nt" else b.shape[1])
    tm, tn, tk = min(tm, M), min(tn, N), min(tk, K)
    assert M % tm == 0 and N % tn == 0 and K % tk == 0, (name, M, N, K, tm, tn, tk)
    nm, nn, nk = M // tm, N // tn, K // tk
    n_out, n_ext = len(out_dtypes), len(extras)
    contract = {"nn": ((1,), (0,)), "nt": ((1,), (1,)), "tn": ((0,), (0,))}[mode]

    def body(*refs):
        a_ref, b_ref = refs[:2]
        ext_refs = refs[2:2 + n_ext]
        out_refs = refs[2 + n_ext + n_after:2 + n_ext + n_after + n_out]
        acc_ref = refs[2 + n_ext + n_after + n_out] if nk > 1 else None
        av, bv = a_ref[...], b_ref[...]
        if a_pre is not None:
            av = a_pre(av)
        if precision is None:
            av, bv = av.astype(BF16), bv.astype(BF16)
        part = lax.dot_general(av, bv, (contract, ((), ())), preferred_element_type=F32, precision=precision)

        def finish(acc):
            vals = (acc,) if epilogue is None else epilogue(acc, *[r[...] for r in ext_refs])
            for r, v in zip(out_refs, vals):
                r[...] = v.astype(r.dtype)

        if nk == 1:
            finish(part)
        else:
            k = pl.program_id(2)

            @pl.when(k == 0)
            def _():
                acc_ref[...] = part

            @pl.when(k > 0)
            def _():
                acc_ref[...] += part

            @pl.when(k == nk - 1)
            def _():
                finish(acc_ref[...])

    def at(index):
        return (lambda j, i, k: index(i, j, k)) if n_outer else index

    a_spec = pl.BlockSpec((tk, tm), at(lambda i, j, k: (k, i))) if mode == "tn" else pl.BlockSpec((tm, tk), at(lambda i, j, k: (i, k)))
    b_spec = (pl.BlockSpec((tn, tk), at(lambda i, j, k: (j, k))) if mode == "nt"
              else pl.BlockSpec((tk, tn), at(lambda i, j, k: (k + b_first_block, j))))
    in_specs, block_bytes = [a_spec, b_spec], _nbytes((tm, tk), a.dtype) + _nbytes((tk, tn), b.dtype)
    for arr, kind in extras:
        if kind == "tile":
            assert arr.shape == (M, N), (name, arr.shape)
            in_specs.append(pl.BlockSpec((tm, tn), at(lambda i, j, k: (i, j))))
            block_bytes += _nbytes((tm, tn), arr.dtype)
        else:
            assert arr.shape == (1, N), (name, arr.shape)
            in_specs.append(pl.BlockSpec((1, tn), at(lambda i, j, k: (0, j))))
    in_specs += [ANY] * n_after
    if out_shards:
        assert n_out == 1 and tn * NDEV == N
        out_shape = [jax.ShapeDtypeStruct((NDEV, M, tn), out_dtypes[0])]
        out_specs = [pl.BlockSpec((None, tm, tn), at(lambda i, j, k: (j, i, 0)))]
    else:
        out_shape = [jax.ShapeDtypeStruct((M, N), d) for d in out_dtypes]
        out_specs = [pl.BlockSpec((tm, tn), at(lambda i, j, k: (i, j))) for _ in out_dtypes]
    block_bytes += sum(_nbytes((tm, tn), d) for d in out_dtypes)
    scratch = [pltpu.VMEM((tm, tn), F32)] if nk > 1 else []
    outs = pl.pallas_call(
        body, name=name, grid=(nn, nm, nk) if n_outer else (nm, nn, nk), in_specs=in_specs, out_specs=out_specs,
        out_shape=out_shape, scratch_shapes=scratch,
        compiler_params=_params(("parallel", "parallel", "arbitrary"), block_bytes, 2 * tm * tn * 4),
    )(a, b, *[arr for arr, _ in extras], *after)
    return outs[0] if n_out == 1 else outs


def _rowwise(fn, tiled, smalls, out_tiles, out_sums, *, name, ts=256):
    S = tiled[0].shape[0]
    ts = min(ts, S)
    assert S % ts == 0
    nt, ns, no, na = len(tiled), len(smalls), len(out_tiles), len(out_sums)

    def body(*refs):
        t_refs, s_refs = refs[:nt], refs[nt:nt + ns]
        o_refs, a_refs = refs[nt + ns:nt + ns + no], refs[nt + ns + no:]
        tile_vals, sum_vals = fn([r[...] for r in t_refs], [r[...] for r in s_refs])
        for r, v in zip(o_refs, tile_vals):
            r[...] = v.astype(r.dtype)

        @pl.when(pl.program_id(0) == 0)
        def _():
            for r in a_refs:
                r[...] = jnp.zeros_like(r)

        for r, v in zip(a_refs, sum_vals):
            r[...] += v

    in_specs = [pl.BlockSpec((ts, t.shape[1]), lambda i: (i, 0)) for t in tiled]
    in_specs += [pl.BlockSpec(s.shape, lambda i: (0, 0)) for s in smalls]
    out_specs = [pl.BlockSpec((ts, w), lambda i: (i, 0)) for w, _ in out_tiles]
    out_specs += [pl.BlockSpec((1, w), lambda i: (0, 0)) for w in out_sums]
    out_shape = [jax.ShapeDtypeStruct((S, w), d) for w, d in out_tiles]
    out_shape += [jax.ShapeDtypeStruct((1, w), F32) for w in out_sums]
    block_bytes = sum(_nbytes((ts, t.shape[1]), t.dtype) for t in tiled) + sum(_nbytes((ts, w), d) for w, d in out_tiles)
    width = max(t.shape[1] for t in tiled)
    outs = pl.pallas_call(
        body, name=name, grid=(S // ts,), in_specs=in_specs, out_specs=out_specs, out_shape=out_shape,
        compiler_params=_params(("arbitrary",), block_bytes, 6 * ts * width * 4),
    )(*tiled, *smalls)
    return outs[:no], outs[no:]


def _colsum(v):
    return jnp.sum(v, axis=0, keepdims=True)


def _rms_mod_fwd(x, gain, shift, scale, name):
    def fn(tiles, smalls):
        (xv,), (g, sh, sc) = tiles, smalls
        inv = lax.rsqrt(jnp.mean(xv * xv, axis=-1, keepdims=True) + RMS_EPS)
        h = (xv * inv) * g * (1.0 + sc) + sh
        return (h, inv), ()

    D = x.shape[1]
    (h, inv), _ = _rowwise(fn, [x], [gain, shift, scale], [(D, BF16), (1, F32)], [], name=name)
    return h, inv


def _gated(dxv, following):
    yv, gate = following
    return dxv * gate, _colsum(dxv * yv)


def _rms_mod_bwd(dh, x, inv, dx_res, gain, scale, name, following=None):
    def fn(tiles, smalls):
        dhv, xv, iv, dres = tiles[:4]
        g, sc = smalls[:2]
        dhv = dhv.astype(F32)
        xhat = xv * iv
        dr = dhv * (1.0 + sc)
        dxhat = dr * g
        dxv = dres + iv * (dxhat - xhat * jnp.mean(dxhat * xhat, axis=-1, keepdims=True))
        sums = (_colsum(dhv), _colsum(dhv * (xhat * g)), _colsum(dr * xhat))
        if following is None:
            return (dxv,), sums
        dy, dgate = _gated(dxv, (tiles[4], smalls[2]))
        return (dxv, dy), (*sums, dgate)

    D = x.shape[1]
    extra = [] if following is None else [following]
    tiles, sums = _rowwise(fn, [dh, x, inv, dx_res] + [f[0] for f in extra], [gain, scale] + [f[1] for f in extra],
                           [(D, F32)] + [(D, BF16)] * len(extra), [D] * (3 + len(extra)), name=name)
    return (tiles[0], *sums[:3]) if following is None else (tiles[0], *sums[:3], tiles[1], sums[3])


def _final_loss_bwd(x, target, gain, following, name):
    D = x.shape[1]

    def fn(tiles, smalls):
        xv, tv, g = tiles[0], tiles[1], smalls[0]
        inv = lax.rsqrt(jnp.mean(xv * xv, axis=-1, keepdims=True) + RMS_EPS)
        xhat = xv * inv
        err = xhat * g - tv
        loss = 0.5 * jnp.sum(jnp.mean(err * err, axis=-1, keepdims=True), axis=0, keepdims=True)
        dout = err * (1.0 / D)
        dxhat = dout * g
        dxv = inv * (dxhat - xhat * jnp.mean(dxhat * xhat, axis=-1, keepdims=True))
        dy, dgate = _gated(dxv, (tiles[2], smalls[1]))
        return (dxv, dy), (_colsum(dout * xhat), jnp.broadcast_to(loss, (1, LANES)), dgate)

    (dx, dy), (dgain, loss, dgate) = _rowwise(fn, [x, target, following[0]], [gain, following[1]],
                                             [(D, F32), (D, BF16)], [D, LANES, D], name=name)
    return dx, dgain, loss, dy, dgate


SCAN_BLOCK = 256


def _triangle(n, lower):
    r = lax.broadcasted_iota(jnp.int32, (n, n), 0)
    c = lax.broadcasted_iota(jnp.int32, (n, n), 1)
    return (r >= c if lower else r <= c).astype(F32)


def _forget_cumsum(logits, bias, name):
    S = logits.shape[0]
    blk = min(SCAN_BLOCK, S)
    nb = S // blk

    def body(z_ref, b_ref, f_ref):
        z = z_ref[...] + b_ref[...]
        f_ref[...] = jnp.minimum(z, 0.0) - jnp.log(1.0 + jnp.exp(-jnp.abs(z)))
        tri = _triangle(blk, lower=True)

        def step(i, carry):
            off = pl.multiple_of(i * blk, blk)
            cs = jnp.dot(tri, f_ref[pl.ds(off, blk), :], preferred_element_type=F32, precision=HIGHEST) + carry
            f_ref[pl.ds(off, blk), :] = cs
            return cs[blk - 1:blk, :]

        lax.fori_loop(0, nb, step, jnp.zeros((1, LANES), F32))

    return pl.pallas_call(body, name=name, out_shape=jax.ShapeDtypeStruct((S, LANES), F32))(logits, bias)


def _forget_bwd(dfk, logits, bias, name):
    S = logits.shape[0]
    blk = min(SCAN_BLOCK, S)
    nb = S // blk

    def body(d_ref, z_ref, b_ref, o_ref, db_ref):
        tri = _triangle(blk, lower=False)

        def step(t, carry):
            off = pl.multiple_of((nb - 1 - t) * blk, blk)
            cs = jnp.dot(tri, d_ref[pl.ds(off, blk), :], preferred_element_type=F32, precision=HIGHEST) + carry
            o_ref[pl.ds(off, blk), :] = cs
            return cs[0:1, :]

        lax.fori_loop(0, nb, step, jnp.zeros((1, LANES), F32))
        z = z_ref[...] + b_ref[...]
        dz = -o_ref[...] / (1.0 + jnp.exp(z))
        o_ref[...] = dz
        db_ref[...] = _colsum(dz)

    return pl.pallas_call(
        body, name=name,
        out_shape=(jax.ShapeDtypeStruct((S, LANES), F32), jax.ShapeDtypeStruct((1, LANES), F32)),
    )(dfk, logits, bias)


ATTN_BLOCK = 512
_NT = (((1,), (1,)), ((), ()))


def _attn_specs(S, H, tb):
    q_blk = lambda part: pl.BlockSpec((tb, HEAD_DIM), lambda h, i: (i, part * H + h))
    q_all = lambda part: pl.BlockSpec((S, HEAD_DIM), lambda h, i: (0, part * H + h))
    col_blk = pl.BlockSpec((None, tb, 1), lambda h, i: (h, i, 0))
    row_all = pl.BlockSpec((None, 1, S), lambda h, i: (h, 0, 0))
    return q_blk, q_all, col_blk, row_all


FWD_HEADS = 2


def _attn_fwd(qkv, f_col, f_row, name):
    S, H = qkv.shape[0], qkv.shape[1] // (3 * HEAD_DIM)
    tb = min(ATTN_BLOCK, S)
    scale = HEAD_DIM ** -0.5
    hp = FWD_HEADS if H % FWD_HEADS == 0 else 1
    groups, wide = H // hp, hp * HEAD_DIM
    lanes = lambda u: pl.ds(u * HEAD_DIM, HEAD_DIM)

    def body(q_ref, k_ref, v_ref, fc_ref, fr_ref, o_ref, lse_ref):
        i = pl.program_id(1)

        def step(j, carry, diagonal):
            off = pl.multiple_of(j * tb, tb)
            out = []
            for u in range(hp):
                m, l, acc = carry[u]
                k, v = k_ref[pl.ds(off, tb), lanes(u)], v_ref[pl.ds(off, tb), lanes(u)]
                s = lax.dot_general(q_ref[:, lanes(u)], k, _NT, preferred_element_type=F32) * scale
                s = s + (fc_ref[u] - fr_ref[u, :, pl.ds(off, tb)])
                if diagonal:
                    row = lax.broadcasted_iota(jnp.int32, (tb, tb), 0)
                    col = lax.broadcasted_iota(jnp.int32, (tb, tb), 1)
                    s = jnp.where(col <= row, s, NEG)
                m_new = jnp.maximum(m, jnp.max(s, axis=-1, keepdims=True))
                p = jnp.exp(s - m_new)
                alpha = jnp.exp(m - m_new)
                l = alpha * l + jnp.sum(p, axis=-1, keepdims=True)
                acc = alpha * acc + jnp.dot(p.astype(BF16), v, preferred_element_type=F32)
                out.append((m_new, l, acc))
            return tuple(out)

        init = (jnp.full((tb, 1), NEG, F32), jnp.zeros((tb, 1), F32), jnp.zeros((tb, HEAD_DIM), F32))
        carry = lax.fori_loop(0, i, lambda j, c: step(j, c, False), (init,) * hp)
        for u, (m, l, acc) in enumerate(step(i, carry, True)):
            o_ref[:, lanes(u)] = (acc / l).astype(o_ref.dtype)
            lse_ref[u] = m + jnp.log(l)

    part = lambda p, rows: pl.BlockSpec((rows, wide), lambda g, i: (i if rows == tb else 0, p * groups + g))
    col_blk = pl.BlockSpec((hp, tb, 1), lambda g, i: (g, i, 0))
    return pl.pallas_call(
        body, name=name, grid=(groups, S // tb),
        in_specs=[part(0, tb), part(1, S), part(2, S), col_blk, pl.BlockSpec((hp, 1, S), lambda g, i: (g, 0, 0))],
        out_specs=[pl.BlockSpec((tb, wide), lambda g, i: (i, g)), col_blk],
        out_shape=[jax.ShapeDtypeStruct((S, H * HEAD_DIM), BF16), jax.ShapeDtypeStruct((H, S, 1), F32)],
        compiler_params=_params(("parallel", "parallel"), 4 * S * wide * 2, 10 * hp * tb * tb * 4),
    )(qkv, qkv, qkv, f_col, f_row)


_TN = (((0,), (0,)), ((), ()))


def _attn_bwd(qkv, do, f_col, f_row, lse_col, name):
    S, H = qkv.shape[0], qkv.shape[1] // (3 * HEAD_DIM)
    tb = min(ATTN_BLOCK, S)
    nq = S // tb
    scale = HEAD_DIM ** -0.5
    q_blk, q_all, col_blk, row_all = _attn_specs(S, H, tb)
    head_blk = pl.BlockSpec((tb, HEAD_DIM), lambda h, i: (i, h))
    head_all = pl.BlockSpec((S, HEAD_DIM), lambda h, i: (0, h))

    def body(q_ref, k_ref, v_ref, do_ref, fc_ref, fr_ref, lse_ref, dq_ref, dk_ref, dv_ref, dfk_ref,
             p_buf, dp_buf, dk_acc, dv_acc, dfk_acc):
        i = pl.program_id(1)
        q, do, fc, lse = q_ref[...], do_ref[...], fc_ref[...], lse_ref[...]

        @pl.when(i == 0)
        def _():
            dk_acc[...] = jnp.zeros_like(dk_acc)
            dv_acc[...] = jnp.zeros_like(dv_acc)
            dfk_acc[...] = jnp.zeros_like(dfk_acc)

        def scores(j, delta, diagonal):
            off = pl.multiple_of(j * tb, tb)
            k, v = k_ref[pl.ds(off, tb), :], v_ref[pl.ds(off, tb), :]
            s = lax.dot_general(q, k, _NT, preferred_element_type=F32) * scale + (fc - fr_ref[:, pl.ds(off, tb)])
            if diagonal:
                row = lax.broadcasted_iota(jnp.int32, (tb, tb), 0)
                col = lax.broadcasted_iota(jnp.int32, (tb, tb), 1)
                s = jnp.where(col <= row, s, NEG)
            p = jnp.exp(s - lse)
            dp = lax.dot_general(do, v, _NT, preferred_element_type=F32)
            p_buf[j] = p
            dp_buf[j] = dp
            return delta + jnp.sum(p * dp, axis=-1, keepdims=True)

        delta = lax.fori_loop(0, i, lambda j, c: scores(j, c, False), jnp.zeros((tb, 1), F32))
        delta = scores(i, delta, True)

        def grad(j, dq):
            off = pl.multiple_of(j * tb, tb)
            p = p_buf[j]
            ds = p * (dp_buf[j] - delta)
            ds_lo = ds.astype(BF16)
            dk_acc[pl.ds(off, tb), :] += lax.dot_general(ds_lo, q, _TN, preferred_element_type=F32)
            dv_acc[pl.ds(off, tb), :] += lax.dot_general(p.astype(BF16), do, _TN, preferred_element_type=F32)
            dfk_acc[:, pl.ds(off, tb)] += jnp.sum(ds, axis=0, keepdims=True)
            return dq + jnp.dot(ds_lo, k_ref[pl.ds(off, tb), :], preferred_element_type=F32)

        dq = lax.fori_loop(0, i + 1, grad, jnp.zeros((tb, HEAD_DIM), F32))
        dq_ref[...] = (dq * scale).astype(dq_ref.dtype)

        @pl.when(i == nq - 1)
        def _():
            dk_ref[...] = (dk_acc[...] * scale).astype(dk_ref.dtype)
            dv_ref[...] = dv_acc[...].astype(dv_ref.dtype)
            dfk_ref[...] = dfk_acc[...]

    wide = jax.ShapeDtypeStruct((S, H * HEAD_DIM), BF16)
    return pl.pallas_call(
        body, name=name, grid=(H, nq),
        in_specs=[q_blk(0), q_all(1), q_all(2), head_blk, col_blk, row_all, col_blk],
        out_specs=[head_blk, head_all, head_all, row_all],
        out_shape=[wide, wide, wide, jax.ShapeDtypeStruct((H, 1, S), F32)],
        scratch_shapes=[pltpu.VMEM((nq, tb, tb), F32), pltpu.VMEM((nq, tb, tb), F32),
                        pltpu.VMEM((S, HEAD_DIM), F32), pltpu.VMEM((S, HEAD_DIM), F32), pltpu.VMEM((1, S), F32)],
        compiler_params=_params(("parallel", "arbitrary"), 6 * S * HEAD_DIM * 2,
                                2 * nq * tb * tb * 4 + 2 * S * HEAD_DIM * 4 + 10 * tb * tb * 4),
    )(qkv, qkv, qkv, do, f_col, f_row, lse_col)


CONV_TILE = 128


def _shift_down(v, n):
    row = lax.broadcasted_iota(jnp.int32, v.shape, 0)
    return jnp.where(row >= n, pltpu.roll(v, n, 0), 0.0)


def _shift_up(v, n):
    S = v.shape[0]
    row = lax.broadcasted_iota(jnp.int32, v.shape, 0)
    return jnp.where(row < S - n, pltpu.roll(v, S - n, 0), 0.0)


def _conv_specs(S, D, tc):
    nb = D // tc
    part = lambda p: pl.BlockSpec((S, tc), lambda j: (0, p * nb + j))
    return part, pl.BlockSpec((S, tc), lambda j: (0, j)), pl.BlockSpec((8, tc), lambda j: (0, j))


def _conv_fwd(proj, conv_w8, name):
    S, D = proj.shape[0], proj.shape[1] // 3
    tc = min(CONV_TILE, D)
    part, chan, taps = _conv_specs(S, D, tc)

    def body(b_ref, c_ref, u_ref, w_ref, z_ref):
        cu = c_ref[...].astype(F32) * u_ref[...].astype(F32)
        w = w_ref[...]
        y = w[0:1, :] * _shift_down(cu, 2) + w[1:2, :] * _shift_down(cu, 1) + w[2:3, :] * cu
        z_ref[...] = (b_ref[...].astype(F32) * y).astype(z_ref.dtype)

    return pl.pallas_call(
        body, name=name, grid=(D // tc,), in_specs=[part(0), part(1), part(2), taps], out_specs=chan,
        out_shape=jax.ShapeDtypeStruct((S, D), BF16),
        compiler_params=_params(("parallel",), 3 * _nbytes((S, tc), proj.dtype) + S * tc * 2, 6 * S * tc * 4),
    )(proj, proj, proj, conv_w8)


def _conv_bwd(proj, dz, conv_w8, name):
    S, D = proj.shape[0], proj.shape[1] // 3
    tc = min(CONV_TILE, D)
    part, chan, taps = _conv_specs(S, D, tc)

    def body(b_ref, c_ref, u_ref, dz_ref, w_ref, db_ref, dc_ref, du_ref, dw_ref):
        cv, uv = c_ref[...].astype(F32), u_ref[...].astype(F32)
        dzv, w = dz_ref[...].astype(F32), w_ref[...]
        cu = cv * uv
        cu1, cu2 = _shift_down(cu, 1), _shift_down(cu, 2)
        y = w[0:1, :] * cu2 + w[1:2, :] * cu1 + w[2:3, :] * cu
        db_ref[...] = (dzv * y).astype(db_ref.dtype)
        dy = dzv * b_ref[...].astype(F32)
        dcu = w[2:3, :] * dy + w[1:2, :] * _shift_up(dy, 1) + w[0:1, :] * _shift_up(dy, 2)
        dc_ref[...] = (dcu * uv).astype(dc_ref.dtype)
        du_ref[...] = (dcu * cv).astype(du_ref.dtype)
        dw_ref[...] = jnp.concatenate(
            [_colsum(dy * cu2), _colsum(dy * cu1), _colsum(dy * cu), jnp.zeros((8 - CONV_WIDTH, tc), F32)], axis=0)

    return pl.pallas_call(
        body, name=name, grid=(D // tc,), in_specs=[part(0), part(1), part(2), chan, taps],
        out_specs=[chan, chan, chan, taps],
        out_shape=[jax.ShapeDtypeStruct((S, D), BF16)] * 3 + [jax.ShapeDtypeStruct((8, D), F32)],
        compiler_params=_params(("parallel",), 3 * _nbytes((S, tc), proj.dtype) + _nbytes((S, tc), dz.dtype)
                                + 3 * S * tc * 2, 10 * S * tc * 4),
    )(proj, proj, proj, dz, conv_w8)


def _adamw(w, m, v, parts, name, layer=0, prev=None):
    L, R, C = w.shape
    P = parts.shape[0]
    assert parts.shape[1:] == (R, C), (name, parts.shape, w.shape)
    elem_bytes = 12 + 16 + P * parts.dtype.itemsize
    budget = 8 << 20
    tr, tc = R, C
    if R * C * elem_bytes > budget:
        if R % 8 == 0:
            tr = max(8, (budget // (C * elem_bytes)) // 8 * 8)
            while R % tr:
                tr -= 8
        else:
            tc = LANES
            while C % (2 * tc) == 0 and R * 2 * tc * elem_bytes <= budget:
                tc *= 2
            assert C % tc == 0, (name, R, C)
    c1, c2 = 1.0 - ADAM_B1 ** ADAM_STEP, 1.0 - ADAM_B2 ** ADAM_STEP

    def body(w_ref, m_ref, v_ref, p_ref, *rest):
        g_ref, d_ref, nm_ref, nv_ref = rest[-4:]
        g = p_ref[0].astype(F32)
        for p in range(1, P):
            g = g + p_ref[p].astype(F32)
        nm = ADAM_B1 * m_ref[...] + (1.0 - ADAM_B1) * g
        nv = ADAM_B2 * v_ref[...] + (1.0 - ADAM_B2) * (g * g)
        g_ref[...] = g
        nm_ref[...] = nm
        nv_ref[...] = nv
        d_ref[...] = -ADAM_LR * ((nm / c1) / (jnp.sqrt(nv / c2) + ADAM_EPS) + ADAM_WD * w_ref[...])

    blk = pl.BlockSpec((None, tr, tc), lambda i, j: (layer, i, j))
    prev = [] if prev is None else list(prev)
    return pl.pallas_call(
        body, name=name, grid=(R // tr, C // tc),
        in_specs=[blk, blk, blk, pl.BlockSpec((P, tr, tc), lambda i, j: (0, i, j))] + [ANY] * len(prev),
        out_specs=[blk] * 4, out_shape=[jax.ShapeDtypeStruct((L, R, C), F32)] * 4,
        input_output_aliases={4 + k: k for k in range(len(prev))},
        compiler_params=_params(("parallel", "parallel"), tr * tc * elem_bytes),
    )(w, m, v, parts, *prev)


def _silu(v):
    return v / (1.0 + jnp.exp(-v))


def _pad_rows(a, rows):
    return jnp.pad(a, ((0, rows - a.shape[0]), (0, 0)))


def _pad_cols(a, cols):
    return jnp.pad(a, ((0, 0), (0, cols - a.shape[1])))


def kernel(x, c, ada_w, ada_b, norm_mix, norm_mlp, fox_w_in, fox_b_f, fox_w_out, conv_w_in, conv_w, conv_w_out, mlp_w_up, mlp_w_down, final_norm, loss_target, m_ada_w, m_ada_b, m_norm_mix, m_norm_mlp, m_fox_w_in, m_fox_b_f, m_fox_w_out, m_conv_w_in, m_conv_w, m_conv_w_out, m_mlp_w_up, m_mlp_w_down, m_final_norm, v_ada_w, v_ada_b, v_norm_mix, v_norm_mlp, v_fox_w_in, v_fox_b_f, v_fox_w_out, v_conv_w_in, v_conv_w, v_conv_w_out, v_mlp_w_up, v_mlp_w_down, v_final_norm):
    S, D = x.shape[1], x.shape[2]
    H = D // HEAD_DIM
    FF = mlp_w_up.shape[2] * NDEV
    depth = ada_w.shape[0]
    n_mod = 6
    assert depth == 2 and fox_w_in.shape[0] == 1 and conv_w_in.shape[0] == 1 and H <= LANES
    me = _my_index()
    x0, target = x[0], loss_target[0]
    row = lambda vec: vec.reshape(1, -1)

    def tied(vec, token):
        return vec + token[0, 0]

    bf = lambda w: w.astype(BF16)
    gather_groups = {
        "fox": [bf(fox_w_in[0]).T, bf(fox_w_out[0])],
        "mlp0": [bf(mlp_w_up[0]).T, bf(mlp_w_down[0])],
        "conv": [bf(conv_w_in[0]).T, conv_w[0], bf(conv_w_out[0])],
        "mlp1": [bf(mlp_w_up[1]).T, bf(mlp_w_down[1])],
    }

    def start_gather(group, after):
        return _exchange_start(gather_groups[group], f"gather_{group}_start", False, after, relay=True)

    def relay_gather(handle, group, after):
        handle = _exchange_wait(handle, f"gather_{group}_arrivals", after, arrivals_only=True)
        return _relay_forward_start(handle, f"gather_{group}_forward")

    def finish_gather(handle, group, after):
        return _exchange_wait(handle, f"gather_{group}_wait", after)

    landed = lambda handle: handle[4][0]

    c_all = _all_gather([c], "gather_cond")[0].reshape(NDEV, D)
    ncol = ada_w.shape[2]
    ada_b_mine = lax.dynamic_slice_in_dim(ada_b, me * ncol, ncol, axis=1)
    mod_cols = jnp.stack([
        _matmul(c_all, ada_w.reshape(depth * D, ncol), mode="nn", name=f"ada_fwd_{i}", out_dtypes=[F32], tm=NDEV,
                tn=ncol // 2, tk=D, b_first_block=i,
                a_pre=_silu, precision=HIGHEST, epilogue=lambda acc, b: (acc + b,), extras=[(ada_b_mine[i:i + 1], "row")])
        for i in range(depth)])
    mod_all = _all_gather([mod_cols], "gather_mod")[0]
    mod = lax.dynamic_index_in_dim(mod_all, me, axis=2, keepdims=False)
    fox_handle, token = start_gather("fox", mod_all)
    mod = tied(mod, token).transpose(1, 0, 2).reshape(depth, n_mod, 1, D)
    sh_mix, sc_mix, g_mix, sh_mlp, sc_mlp, g_mlp = (mod[:, k] for k in range(n_mod))
    b_f = _pad_cols(fox_b_f, LANES)

    def residual(acc, x_in, gate):
        return (x_in + gate * acc, acc)

    def mlp_fwd(i, x_in, handle, relay_next=None):
        h, inv = _rms_mod_fwd(x_in, row(norm_mlp[i]), sh_mlp[i], sc_mlp[i], f"mlp_norm_{i}")
        w_up_t, w_down = finish_gather(handle, f"mlp{i}", h)
        w_up_t, w_down = w_up_t.reshape(FF, D), w_down.reshape(FF, D)
        r, a = _matmul(h, w_up_t, mode="nt", name=f"mlp_up_{i}", out_dtypes=[BF16, BF16], tm=1024, tn=1024, tk=D,
                       epilogue=lambda acc: (jnp.maximum(acc, 0.0), jnp.square(jnp.maximum(acc, 0.0))))
        next_handle = relay_gather(relay_next[1], relay_next[0], a) if relay_next else None
        x_out, y = _matmul(a, w_down, mode="nn", name=f"mlp_down_{i}", out_dtypes=[F32, BF16], tm=512, tn=512, tk=FF,
                           n_outer=True, epilogue=residual, extras=[(x_in, "tile"), (g_mlp[i], "row")],
                           after=[landed(next_handle)] if relay_next else [])
        return x_out, (x_in, h, inv, r, a, y, w_up_t, w_down), next_handle

    def mlp_bwd(i, dx, dy, dgate, saved, following, after):
        x_in, h, inv, r, a, y, w_up_t, w_down = saved
        du = _matmul(dy, w_down, mode="nt", name=f"mlp_down_bwd_{i}", out_dtypes=[BF16], tm=1024, tn=1024, tk=D,
                     epilogue=lambda acc, rv: (acc * (2.0 * rv.astype(F32)),), extras=[(r, "tile")], after=after)
        d_down = _matmul(a, dy, mode="tn", name=f"mlp_down_wgrad_{i}", out_dtypes=[BF16], tm=512, tn=1024, tk=S)
        dh = _matmul(du, w_up_t, mode="nn", name=f"mlp_up_bwd_{i}", out_dtypes=[F32], tm=512, tn=512, tk=FF, n_outer=True)
        d_up = _matmul(h, du, mode="tn", name=f"mlp_up_wgrad_{i}", out_dtypes=[BF16], tm=512, tn=FF // NDEV, tk=S,
                       out_shards=True)
        dx, dsh, dsc, dgain, dy_next, dgate_next = _rms_mod_bwd(dh, x_in, inv, dx, row(norm_mlp[i]), sc_mlp[i],
                                                                f"mlp_norm_bwd_{i}", following)
        handle, token = _exchange_start([d_up, d_down.reshape(NDEV, FF // NDEV, D)], f"scatter_mlp{i}_start", True, dx)
        return dx, (dsh, dsc, dgate, dgain), handle, token, dy_next, dgate_next

    h0, inv0 = _rms_mod_fwd(x0, row(norm_mix[0]), sh_mix[0], sc_mix[0], "fox_norm")
    w_in_t, w_fox_out = finish_gather(relay_gather(fox_handle, "fox", h0), "fox", h0)
    mlp0_handle, token = start_gather("mlp0", w_in_t)
    w_in_t = w_in_t.reshape(3 * D + H, D)
    w_f_t = _pad_rows(w_in_t[3 * D:], LANES)
    w_fox_out = w_fox_out.reshape(D, D)
    qkv = _matmul(h0, w_in_t, mode="nt", name="fox_qkv", out_dtypes=[BF16], tm=1024, tn=1024, tk=D, n=3 * D, after=[token])
    f_logit = _matmul(h0, w_f_t, mode="nt", name="fox_forget_logits", out_dtypes=[F32], tm=1024, tn=LANES, tk=D)
    f_cum = _forget_cumsum(f_logit, b_f, "fox_forget_cumsum")
    f_heads = f_cum[:, :H].T
    f_col, f_row = f_heads.reshape(H, S, 1), f_heads.reshape(H, 1, S)
    o, lse = _attn_fwd(qkv, f_col, f_row, "fox_attention")
    mlp0_handle = relay_gather(mlp0_handle, "mlp0", o)
    conv_handle, token = start_gather("conv", landed(mlp0_handle))
    mlp1_handle, token = start_gather("mlp1", token)
    x1, mix0 = _matmul(o, w_fox_out, mode="nn", name="fox_out", out_dtypes=[F32, BF16], tm=512, tn=1024, tk=D,
                       epilogue=residual, extras=[(x0, "tile"), (g_mix[0], "row")], after=[token])
    x2, mlp0, conv_handle = mlp_fwd(0, x1, mlp0_handle, ("conv", conv_handle))

    h1, inv1 = _rms_mod_fwd(x2, row(norm_mix[1]), sh_mix[1], sc_mix[1], "conv_norm")
    w_conv_in_t, w_taps, w_conv_out = finish_gather(conv_handle, "conv", h1)
    w_conv_in_t = w_conv_in_t.reshape(3 * D, D)
    w_taps = _pad_rows(w_taps.transpose(1, 0, 2).reshape(CONV_WIDTH, D), 8)
    w_conv_out = w_conv_out.reshape(D, D)
    proj = _matmul(h1, w_conv_in_t, mode="nt", name="conv_in", out_dtypes=[BF16], tm=1024, tn=1024, tk=D)
    mlp1_handle = relay_gather(mlp1_handle, "mlp1", proj)
    z = _conv_fwd(proj, w_taps, "conv_mix")
    x3, mix1 = _matmul(z, w_conv_out, mode="nn", name="conv_out", out_dtypes=[F32, BF16], tm=512, tn=1024, tk=D,
                       epilogue=residual, extras=[(x2, "tile"), (g_mix[1], "row")], after=[landed(mlp1_handle)])
    x4, mlp1, _ = mlp_fwd(1, x3, mlp1_handle)

    dx, d_final, loss_lanes, dy, dgate = _final_loss_bwd(x4, target, row(final_norm), (mlp1[5], g_mlp[1]), "loss_head")

    dx, dmod_mlp1, mlp1_scatter, token, dmix, dg_mix1 = mlp_bwd(1, dx, dy, dgate, mlp1, (mix1, g_mix[1]), [])
    dz = _matmul(dmix, w_conv_out, mode="nt", name="conv_out_bwd", out_dtypes=[BF16], tm=1024, tn=1024, tk=D,
                 after=[token])
    d_conv_out = _matmul(z, dmix, mode="tn", name="conv_out_wgrad", out_dtypes=[BF16], tm=512, tn=1024, tk=S)
    db, dc, du, d_taps = _conv_bwd(proj, dz, w_taps, "conv_mix_bwd")
    dproj = jnp.concatenate([db, dc, du], axis=1)
    dh1 = _matmul(dproj, w_conv_in_t, mode="nn", name="conv_in_bwd", out_dtypes=[F32], tm=512, tn=512, tk=3 * D, n_outer=True)
    d_conv_in = _matmul(h1, dproj, mode="tn", name="conv_in_wgrad", out_dtypes=[BF16], tm=512, tn=3 * D // NDEV, tk=S,
                        out_shards=True)
    dx, dsh1, dsc1, dgain_mix1, dy, dgate = _rms_mod_bwd(dh1, x2, inv1, dx, row(norm_mix[1]), sc_mix[1], "conv_norm_bwd",
                                                         (mlp0[5], g_mlp[0]))
    d_taps_split = d_taps[:CONV_WIDTH].reshape(CONV_WIDTH, NDEV, -1).transpose(1, 0, 2)
    conv_scatter, token = _exchange_start([d_conv_in, d_taps_split, d_conv_out.reshape(NDEV, D // NDEV, D)],
                                          "scatter_conv_start", True, dx)

    dx, dmod_mlp0, mlp0_scatter, token, dmix, dg_mix0 = mlp_bwd(0, dx, dy, dgate, mlp0, (mix0, g_mix[0]), [token])
    do = _matmul(dmix, w_fox_out, mode="nt", name="fox_out_bwd", out_dtypes=[BF16], tm=1024, tn=1024, tk=D,
                 after=[token])
    d_fox_out = _matmul(o, dmix, mode="tn", name="fox_out_wgrad", out_dtypes=[BF16], tm=512, tn=1024, tk=S)
    dq, dk, dv, dfk = _attn_bwd(qkv, do, f_col, f_row, lse, "fox_attention_bwd")
    dqkv = jnp.concatenate([dq, dk, dv], axis=1)
    dfk_lanes = _pad_cols(dfk.reshape(H, S).T, LANES)
    df_logit, db_f = _forget_bwd(dfk_lanes, f_logit, b_f, "fox_forget_bwd")
    d_qkv_t = _matmul(dqkv, h0, mode="tn", name="fox_qkv_wgrad", out_dtypes=[BF16], tm=512, tn=1024, tk=S)
    d_f_t = _matmul(df_logit, h0, mode="tn", name="fox_forget_wgrad", out_dtypes=[BF16], tm=LANES, tn=1024, tk=S)
    d_fox_in = jnp.concatenate([d_qkv_t, d_f_t[:H]], axis=0).reshape(NDEV, -1, D)
    fox_scatter, token = _exchange_start([d_fox_in, d_fox_out.reshape(NDEV, D // NDEV, D)], "scatter_fox_start", True,
                                         d_fox_in)
    dh0_f = _matmul(df_logit, w_f_t, mode="nn", name="fox_forget_logits_bwd", out_dtypes=[F32], tm=1024, tn=1024, tk=LANES,
                    after=[token])
    dh0 = _matmul(dqkv, w_in_t, mode="nn", name="fox_qkv_bwd", out_dtypes=[F32], tm=512, tn=512, tk=3 * D, n_outer=True,
                  epilogue=lambda acc, extra: (acc + extra,), extras=[(dh0_f, "tile")])
    dx, dsh0, dsc0, dgain_mix0 = _rms_mod_bwd(dh0, x0, inv0, dx, row(norm_mix[0]), sc_mix[0], "fox_norm_bwd")
    grad_x = dx.reshape(1, S, D)

    dmod = jnp.concatenate([
        jnp.concatenate([dsh0, dsc0, dg_mix0, dmod_mlp0[0], dmod_mlp0[1], dmod_mlp0[2]], axis=1),
        jnp.concatenate([dsh1, dsc1, dg_mix1, dmod_mlp1[0], dmod_mlp1[1], dmod_mlp1[2]], axis=1)], axis=0)
    small_sizes = [depth * n_mod * D, depth * D, depth * D, H, D, 1]
    n_small = sum(small_sizes)
    n_rows = -(-n_small // (8 * LANES)) * 8

    def pack(parts):
        flat = jnp.concatenate([p.reshape(-1) for p in parts])
        return jnp.pad(flat, (0, n_rows * LANES - n_small)).reshape(n_rows, LANES)

    def unpack(packed, shapes):
        flat, out, at = packed.reshape(-1), [], 0
        for size, shape in zip(small_sizes, shapes):
            out.append(flat[at:at + size].reshape(shape))
            at += size
        return out

    small_partial = pack([dmod, jnp.concatenate([dgain_mix0, dgain_mix1], axis=0),
                          jnp.concatenate([dmod_mlp0[3], dmod_mlp1[3]], axis=0), db_f[0, :H], d_final, loss_lanes[0, :1]])
    small_handle, token = _exchange_start([small_partial], "gather_small_start", False, dx)

    up1, down1 = _exchange_wait(mlp1_scatter, "scatter_mlp1_wait", token)
    up_out = _adamw(mlp_w_up, m_mlp_w_up, v_mlp_w_up, up1, "adamw_mlp_w_up_1", layer=1)
    down_out = _adamw(mlp_w_down, m_mlp_w_down, v_mlp_w_down, down1, "adamw_mlp_w_down_1", layer=1)
    cin, taps, cout = _exchange_wait(conv_scatter, "scatter_conv_wait", down_out[0])
    conv_in_out = _adamw(conv_w_in, m_conv_w_in, v_conv_w_in, cin, "adamw_conv_w_in")
    conv_w_res = _adamw(conv_w, m_conv_w, v_conv_w, taps, "adamw_conv_w")
    conv_out_out = _adamw(conv_w_out, m_conv_w_out, v_conv_w_out, cout, "adamw_conv_w_out")
    up0, down0 = _exchange_wait(mlp0_scatter, "scatter_mlp0_wait", conv_out_out[0])
    up_out = _adamw(mlp_w_up, m_mlp_w_up, v_mlp_w_up, up0, "adamw_mlp_w_up_0", layer=0, prev=up_out)
    down_out = _adamw(mlp_w_down, m_mlp_w_down, v_mlp_w_down, down0, "adamw_mlp_w_down_0", layer=0, prev=down_out)

    small_parts = _exchange_wait(small_handle, "gather_small_wait", down_out[0])[0]
    small_shapes = [ada_b.shape, norm_mix.shape, norm_mlp.shape, fox_b_f.shape, final_norm.shape]
    loss = jnp.sum(small_parts.reshape(NDEV, -1)[:, n_small - 1])
    unused = jnp.zeros((1,), F32)
    small_out = _adamw(pack([ada_b, norm_mix, norm_mlp, fox_b_f, final_norm, unused])[None],
                       pack([m_ada_b, m_norm_mix, m_norm_mlp, m_fox_b_f, m_final_norm, unused])[None],
                       pack([v_ada_b, v_norm_mix, v_norm_mlp, v_fox_b_f, v_final_norm, unused])[None], small_parts,
                       "adamw_small")
    small_out = [unpack(t, small_shapes) for t in small_out]

    dmod_all = small_parts.reshape(NDEV, -1)[:, :depth * n_mod * D].reshape(NDEV, depth, n_mod * D)
    dmod_mine = lax.dynamic_slice_in_dim(dmod_all, me * ncol, ncol, axis=2)
    ada_out = None
    for i in range(depth):
        d_ada = _matmul(c_all, dmod_mine[:, i], mode="tn", name=f"ada_wgrad_{i}", out_dtypes=[F32], tm=1024, tn=ncol // 2,
                        tk=NDEV, a_pre=_silu, precision=HIGHEST)
        ada_out = _adamw(ada_w, m_ada_w, v_ada_w, d_ada[None], f"adamw_ada_w_{i}", layer=i, prev=ada_out)

    fin, fout = _exchange_wait(fox_scatter, "scatter_fox_wait", ada_out[0])
    swap = lambda t: jnp.swapaxes(t, 1, 2)
    fox_in_out = [swap(t) for t in _adamw(swap(fox_w_in), swap(m_fox_w_in), swap(v_fox_w_in), fin, "adamw_fox_w_in")]
    fox_out_out = _adamw(fox_w_out, m_fox_w_out, v_fox_w_out, fout, "adamw_fox_w_out")

    outputs = [loss, grad_x]
    for kind in range(4):
        sm = small_out[kind]
        outputs += [ada_out[kind], sm[0], sm[1], sm[2], fox_in_out[kind], sm[3], fox_out_out[kind], conv_in_out[kind],
                    conv_w_res[kind], conv_out_out[kind], up_out[kind], down_out[kind], sm[4]]
    return tuple(outputs)
```

```python
import math

import jax
import jax.numpy as jnp
from jax import lax
from jax.experimental import pallas as pl
from jax.experimental.pallas import tpu as pltpu

F32 = jnp.float32
BF16 = jnp.bfloat16
MESH = pl.DeviceIdType.MESH
NDEV = 8
HEAD_DIM = 128
LANES = 128
CONV_WIDTH = 3
RMS_EPS = 1e-6
ADAM_LR, ADAM_B1, ADAM_B2, ADAM_EPS, ADAM_WD, ADAM_STEP = 0.001, 0.9, 0.999, 1e-08, 0.01, 10
NEG = -1e30
V7X_VMEM_BYTES = 64 * 1024 * 1024
VMEM_HEADROOM = 12 * 1024 * 1024
HBM = pl.BlockSpec(memory_space=pltpu.HBM)
HIGHEST = lax.Precision.HIGHEST


def _nbytes(shape, dtype):
    return math.prod(shape) * jnp.dtype(dtype).itemsize


def _params(semantics, block_bytes, temp_bytes=0):
    limit = min(2 * block_bytes + temp_bytes + VMEM_HEADROOM, V7X_VMEM_BYTES - 4 * 1024 * 1024)
    return pltpu.CompilerParams(dimension_semantics=semantics, vmem_limit_bytes=int(limit))


def _my_index():
    return lax.axis_index("x") * 4 + lax.axis_index("y") * 2 + lax.axis_index("c")


def _peer(r):
    x, y, c = lax.axis_index("x"), lax.axis_index("y"), lax.axis_index("c")
    px = 1 - x if (r >> 2) & 1 else x
    py = 1 - y if (r >> 1) & 1 else y
    pc = 1 - c if r & 1 else c
    return (px, py, pc), px * 4 + py * 2 + pc


def _exchange(arrays, name, scatter, after=None):
    n = len(arrays)
    after = [] if after is None else list(after)

    def body(*refs):
        ins, outs = refs[:n], refs[n + len(after):2 * n + len(after)]
        send_sems, recv_sems, local_sems = refs[2 * n + len(after):]
        me = _my_index()
        local = []
        for a in range(n):
            src = ins[a].at[me] if scatter else ins[a]
            local.append(pltpu.make_async_copy(src, outs[a].at[me], local_sems.at[a]))
            local[-1].start()
        sends = []
        for r in range(1, NDEV):
            peer, pidx = _peer(r)
            for a in range(n):
                src = ins[a].at[pidx] if scatter else ins[a]
                cp = pltpu.make_async_remote_copy(
                    src_ref=src, dst_ref=outs[a].at[me],
                    send_sem=send_sems.at[a * (NDEV - 1) + r - 1], recv_sem=recv_sems.at[a * (NDEV - 1) + r - 1],
                    device_id=peer, device_id_type=MESH)
                cp.start()
                sends.append(cp)
        for r in range(1, NDEV):
            peer, pidx = _peer(r)
            for a in range(n):
                src = ins[a].at[pidx] if scatter else ins[a]
                pltpu.make_async_remote_copy(
                    src_ref=src, dst_ref=outs[a].at[pidx],
                    send_sem=send_sems.at[a * (NDEV - 1) + r - 1], recv_sem=recv_sems.at[a * (NDEV - 1) + r - 1],
                    device_id=peer, device_id_type=MESH).wait_recv()
        for cp in sends:
            cp.wait_send()
        for cp in local:
            cp.wait()

    out_shape = [jax.ShapeDtypeStruct(a.shape if scatter else (NDEV,) + a.shape, a.dtype) for a in arrays]
    return pl.pallas_call(
        body, name=name, out_shape=out_shape, in_specs=[HBM] * n + [ANY] * len(after), out_specs=[HBM] * n,
        scratch_shapes=[pltpu.SemaphoreType.DMA((n * (NDEV - 1),)), pltpu.SemaphoreType.DMA((n * (NDEV - 1),)),
                        pltpu.SemaphoreType.DMA((n,))],
    )(*arrays, *after)


def _all_gather(arrays, name, after=None):
    return _exchange(arrays, name, scatter=False, after=after)


SEM = pl.BlockSpec(memory_space=pltpu.SEMAPHORE)
ANY = pl.BlockSpec(memory_space=pl.ANY)
DATAFLOW = pltpu.SideEffectType.DATAFLOW_SIDE_EFFECTING
TOKEN_SHAPE = (8, LANES)


SIBLING = 1
OTHER_CHIPS = (4, 2, 6)


def _exchange_start(arrays, name, scatter, after, relay=False):
    n = len(arrays)
    n_sems = n * (NDEV - 1)
    assert not (relay and scatter)

    def body(*refs):
        ins = refs[:n]
        send_sems, recv_sems = refs[n + 1], refs[n + 2]
        lands, token = refs[2 * n + 3:3 * n + 3], refs[3 * n + 3]
        me = _my_index()
        for r in (SIBLING, *OTHER_CHIPS) if relay else range(1, NDEV):
            peer, pidx = _peer(r)
            for a in range(n):
                src = ins[a].at[pidx] if scatter else ins[a]
                pltpu.make_async_remote_copy(
                    src_ref=src, dst_ref=lands[a].at[me],
                    send_sem=send_sems.at[a * (NDEV - 1) + r - 1], recv_sem=recv_sems.at[a * (NDEV - 1) + r - 1],
                    device_id=peer, device_id_type=MESH).start()
        token[...] = jnp.zeros(TOKEN_SHAPE, F32)

    land_shapes = [a.shape if scatter else (NDEV,) + a.shape for a in arrays]
    srcs = [pltpu.with_memory_space_constraint(a, pltpu.HBM) for a in arrays]
    outs = pl.pallas_call(
        body, name=name,
        out_shape=(pltpu.SemaphoreType.DMA((n_sems,)), pltpu.SemaphoreType.DMA((n_sems,)),
                   *[pltpu.HBM(a.shape, a.dtype) for a in arrays], *[pltpu.HBM(s, a.dtype) for s, a in zip(land_shapes, arrays)],
                   jax.ShapeDtypeStruct(TOKEN_SHAPE, F32)),
        in_specs=[HBM] * n + [ANY],
        out_specs=(SEM, SEM, *[HBM] * (2 * n), pl.BlockSpec(memory_space=pltpu.VMEM)),
        input_output_aliases={i: 2 + i for i in range(n)},
        compiler_params=pltpu.CompilerParams(has_side_effects=DATAFLOW),
    )(*srcs, after)
    return (scatter, relay, [(outs[0], outs[1])], list(outs[2:2 + n]), list(outs[2 + n:2 + 2 * n])), outs[-1]


def _relay_forward_start(handle, name):
    scatter, relay, sems, srcs, lands = handle
    n = len(lands)
    n_sems = n * len(OTHER_CHIPS)

    def body(*refs):
        land_refs, send_sems, recv_sems = refs[:n], refs[n], refs[n + 1]
        sibling, _ = _peer(SIBLING)
        for j, r in enumerate(OTHER_CHIPS):
            _, pidx = _peer(r)
            for a in range(n):
                pltpu.make_async_remote_copy(
                    src_ref=land_refs[a].at[pidx], dst_ref=land_refs[a].at[pidx],
                    send_sem=send_sems.at[a * len(OTHER_CHIPS) + j], recv_sem=recv_sems.at[a * len(OTHER_CHIPS) + j],
                    device_id=sibling, device_id_type=MESH).start()

    outs = pl.pallas_call(
        body, name=name,
        out_shape=(pltpu.SemaphoreType.DMA((n_sems,)), pltpu.SemaphoreType.DMA((n_sems,)),
                   *[pltpu.HBM(t.shape, t.dtype) for t in lands]),
        in_specs=[HBM] * n, out_specs=(SEM, SEM, *[HBM] * n),
        input_output_aliases={i: 2 + i for i in range(n)},
        compiler_params=pltpu.CompilerParams(has_side_effects=DATAFLOW),
    )(*lands)
    return (scatter, relay, sems + [(outs[0], outs[1])], srcs, list(outs[2:]))


def _exchange_wait(handle, name, after, arrivals_only=False):
    scatter, relay, sems, srcs, lands = handle
    n = len(srcs)
    forwarded = len(sems) == 2
    assert not arrivals_only or (relay and not forwarded)

    def body(*refs):
        src_refs, land_refs = refs[:n], refs[n:2 * n]
        send_sems, recv_sems = refs[2 * n], refs[2 * n + 1]
        for r in (SIBLING, *OTHER_CHIPS) if relay else range(1, NDEV):
            peer, pidx = _peer(r)
            for a in range(n):
                src = src_refs[a].at[pidx] if scatter else src_refs[a]
                cp = pltpu.make_async_remote_copy(
                    src_ref=src, dst_ref=land_refs[a].at[pidx],
                    send_sem=send_sems.at[a * (NDEV - 1) + r - 1], recv_sem=recv_sems.at[a * (NDEV - 1) + r - 1],
                    device_id=peer, device_id_type=MESH)
                if arrivals_only:
                    if r in OTHER_CHIPS:
                        cp.wait_recv()
                else:
                    cp.wait_send()
                    if not (relay and r in OTHER_CHIPS):
                        cp.wait_recv()
        if forwarded:
            fwd_send, fwd_recv = refs[2 * n + 2], refs[2 * n + 3]
            sibling, _ = _peer(SIBLING)
            for j, r in enumerate(OTHER_CHIPS):
                _, pidx = _peer(r ^ SIBLING)
                for a in range(n):
                    cp = pltpu.make_async_remote_copy(
                        src_ref=src_refs[a], dst_ref=land_refs[a].at[pidx],
                        send_sem=fwd_send.at[a * len(OTHER_CHIPS) + j], recv_sem=fwd_recv.at[a * len(OTHER_CHIPS) + j],
                        device_id=sibling, device_id_type=MESH)
                    cp.wait_send()
                    cp.wait_recv()

    flat_sems = [s for pair in sems for s in pair]
    outs = pl.pallas_call(
        body, name=name,
        out_shape=tuple(pltpu.HBM(t.shape, t.dtype) for t in (*srcs, *lands)),
        in_specs=[HBM] * (2 * n) + [SEM] * len(flat_sems) + [ANY], out_specs=tuple([HBM] * (2 * n)),
        input_output_aliases={i: i for i in range(2 * n)},
        compiler_params=pltpu.CompilerParams(has_side_effects=DATAFLOW),
    )(*srcs, *lands, *flat_sems, after)
    if arrivals_only:
        return (scatter, relay, sems, list(outs[:n]), list(outs[n:]))
    me = _my_index()
    mine = [lax.dynamic_index_in_dim(s, me, 0, keepdims=False) if scatter else s for s in outs[:n]]
    return [lax.dynamic_update_index_in_dim(land, own, me, 0) for land, own in zip(outs[n:], mine)]


def _matmul(a, b, *, mode, name, out_dtypes, tm, tn, tk, epilogue=None, extras=(), a_pre=None,
            out_shards=False, n_outer=False, precision=None, after=(), n=None, b_first_block=0):
    after = list(after)
    n_after = len(after)
    K, M = a.shape if mode == "tn" else a.shape[::-1]
    N = n if n is not None else (b.shape[0] if mode == "nt" else b.shape[1])
    tm, tn, tk = min(tm, M), min(tn, N), min(tk, K)
    assert M % tm == 0 and N % tn == 0 and K % tk == 0, (name, M, N, K, tm, tn, tk)
    nm, nn, nk = M // tm, N // tn, K // tk
    n_out, n_ext = len(out_dtypes), len(extras)
    contract = {"nn": ((1,), (0,)), "nt": ((1,), (1,)), "tn": ((0,), (0,))}[mode]

    def body(*refs):
        a_ref, b_ref = refs[:2]
        ext_refs = refs[2:2 + n_ext]
        out_refs = refs[2 + n_ext + n_after:2 + n_ext + n_after + n_out]
        acc_ref = refs[2 + n_ext + n_after + n_out] if nk > 1 else None
        av, bv = a_ref[...], b_ref[...]
        if a_pre is not None:
            av = a_pre(av)
        if precision is None:
            av, bv = av.astype(BF16), bv.astype(BF16)
        part = lax.dot_general(av, bv, (contract, ((), ())), preferred_element_type=F32, precision=precision)

        def finish(acc):
            vals = (acc,) if epilogue is None else epilogue(acc, *[r[...] for r in ext_refs])
            for r, v in zip(out_refs, vals):
                r[...] = v.astype(r.dtype)

        if nk == 1:
            finish(part)
        else:
            k = pl.program_id(2)

            @pl.when(k == 0)
            def _():
                acc_ref[...] = part

            @pl.when(k > 0)
            def _():
                acc_ref[...] += part

            @pl.when(k == nk - 1)
            def _():
                finish(acc_ref[...])

    def at(index):
        return (lambda j, i, k: index(i, j, k)) if n_outer else index

    a_spec = pl.BlockSpec((tk, tm), at(lambda i, j, k: (k, i))) if mode == "tn" else pl.BlockSpec((tm, tk), at(lambda i, j, k: (i, k)))
    b_spec = (pl.BlockSpec((tn, tk), at(lambda i, j, k: (j, k))) if mode == "nt"
              else pl.BlockSpec((tk, tn), at(lambda i, j, k: (k + b_first_block, j))))
    in_specs, block_bytes = [a_spec, b_spec], _nbytes((tm, tk), a.dtype) + _nbytes((tk, tn), b.dtype)
    for arr, kind in extras:
        if kind == "tile":
            assert arr.shape == (M, N), (name, arr.shape)
            in_specs.append(pl.BlockSpec((tm, tn), at(lambda i, j, k: (i, j))))
            block_bytes += _nbytes((tm, tn), arr.dtype)
        else:
            assert arr.shape == (1, N), (name, arr.shape)
            in_specs.append(pl.BlockSpec((1, tn), at(lambda i, j, k: (0, j))))
    in_specs += [ANY] * n_after
    if out_shards:
        assert n_out == 1 and tn * NDEV == N
        out_shape = [jax.ShapeDtypeStruct((NDEV, M, tn), out_dtypes[0])]
        out_specs = [pl.BlockSpec((None, tm, tn), at(lambda i, j, k: (j, i, 0)))]
    else:
        out_shape = [jax.ShapeDtypeStruct((M, N), d) for d in out_dtypes]
        out_specs = [pl.BlockSpec((tm, tn), at(lambda i, j, k: (i, j))) for _ in out_dtypes]
    block_bytes += sum(_nbytes((tm, tn), d) for d in out_dtypes)
    scratch = [pltpu.VMEM((tm, tn), F32)] if nk > 1 else []
    outs = pl.pallas_call(
        body, name=name, grid=(nn, nm, nk) if n_outer else (nm, nn, nk), in_specs=in_specs, out_specs=out_specs,
        out_shape=out_shape, scratch_shapes=scratch,
        compiler_params=_params(("parallel", "parallel", "arbitrary"), block_bytes, 2 * tm * tn * 4),
    )(a, b, *[arr for arr, _ in extras], *after)
    return outs[0] if n_out == 1 else outs


def _rowwise(fn, tiled, smalls, out_tiles, out_sums, *, name, ts=256):
    S = tiled[0].shape[0]
    ts = min(ts, S)
    assert S % ts == 0
    nt, ns, no, na = len(tiled), len(smalls), len(out_tiles), len(out_sums)

    def body(*refs):
        t_refs, s_refs = refs[:nt], refs[nt:nt + ns]
        o_refs, a_refs = refs[nt + ns:nt + ns + no], refs[nt + ns + no:]
        tile_vals, sum_vals = fn([r[...] for r in t_refs], [r[...] for r in s_refs])
        for r, v in zip(o_refs, tile_vals):
            r[...] = v.astype(r.dtype)

        @pl.when(pl.program_id(0) == 0)
        def _():
            for r in a_refs:
                r[...] = jnp.zeros_like(r)

        for r, v in zip(a_refs, sum_vals):
            r[...] += v

    in_specs = [pl.BlockSpec((ts, t.shape[1]), lambda i: (i, 0)) for t in tiled]
    in_specs += [pl.BlockSpec(s.shape, lambda i: (0, 0)) for s in smalls]
    out_specs = [pl.BlockSpec((ts, w), lambda i: (i, 0)) for w, _ in out_tiles]
    out_specs += [pl.BlockSpec((1, w), lambda i: (0, 0)) for w in out_sums]
    out_shape = [jax.ShapeDtypeStruct((S, w), d) for w, d in out_tiles]
    out_shape += [jax.ShapeDtypeStruct((1, w), F32) for w in out_sums]
    block_bytes = sum(_nbytes((ts, t.shape[1]), t.dtype) for t in tiled) + sum(_nbytes((ts, w), d) for w, d in out_tiles)
    width = max(t.shape[1] for t in tiled)
    outs = pl.pallas_call(
        body, name=name, grid=(S // ts,), in_specs=in_specs, out_specs=out_specs, out_shape=out_shape,
        compiler_params=_params(("arbitrary",), block_bytes, 6 * ts * width * 4),
    )(*tiled, *smalls)
    return outs[:no], outs[no:]


def _colsum(v):
    return jnp.sum(v, axis=0, keepdims=True)


def _rms_mod_fwd(x, gain, shift, scale, name):
    def fn(tiles, smalls):
        (xv,), (g, sh, sc) = tiles, smalls
        inv = lax.rsqrt(jnp.mean(xv * xv, axis=-1, keepdims=True) + RMS_EPS)
        h = (xv * inv) * g * (1.0 + sc) + sh
        return (h, inv), ()

    D = x.shape[1]
    (h, inv), _ = _rowwise(fn, [x], [gain, shift, scale], [(D, BF16), (1, F32)], [], name=name)
    return h, inv


def _gated(dxv, following):
    yv, gate = following
    return dxv * gate, _colsum(dxv * yv)


def _rms_mod_bwd(dh, x, inv, dx_res, gain, scale, name, following=None):
    def fn(tiles, smalls):
        dhv, xv, iv, dres = tiles[:4]
        g, sc = smalls[:2]
        dhv = dhv.astype(F32)
        xhat = xv * iv
        dr = dhv * (1.0 + sc)
        dxhat = dr * g
        dxv = dres + iv * (dxhat - xhat * jnp.mean(dxhat * xhat, axis=-1, keepdims=True))
        sums = (_colsum(dhv), _colsum(dhv * (xhat * g)), _colsum(dr * xhat))
        if following is None:
            return (dxv,), sums
        dy, dgate = _gated(dxv, (tiles[4], smalls[2]))
        return (dxv, dy), (*sums, dgate)

    D = x.shape[1]
    extra = [] if following is None else [following]
    tiles, sums = _rowwise(fn, [dh, x, inv, dx_res] + [f[0] for f in extra], [gain, scale] + [f[1] for f in extra],
                           [(D, F32)] + [(D, BF16)] * len(extra), [D] * (3 + len(extra)), name=name)
    return (tiles[0], *sums[:3]) if following is None else (tiles[0], *sums[:3], tiles[1], sums[3])


def _final_loss_bwd(x, target, gain, following, name):
    D = x.shape[1]

    def fn(tiles, smalls):
        xv, tv, g = tiles[0], tiles[1], smalls[0]
        inv = lax.rsqrt(jnp.mean(xv * xv, axis=-1, keepdims=True) + RMS_EPS)
        xhat = xv * inv
        err = xhat * g - tv
        loss = 0.5 * jnp.sum(jnp.mean(err * err, axis=-1, keepdims=True), axis=0, keepdims=True)
        dout = err * (1.0 / D)
        dxhat = dout * g
        dxv = inv * (dxhat - xhat * jnp.mean(dxhat * xhat, axis=-1, keepdims=True))
        dy, dgate = _gated(dxv, (tiles[2], smalls[1]))
        return (dxv, dy), (_colsum(dout * xhat), jnp.broadcast_to(loss, (1, LANES)), dgate)

    (dx, dy), (dgain, loss, dgate) = _rowwise(fn, [x, target, following[0]], [gain, following[1]],
                                             [(D, F32), (D, BF16)], [D, LANES, D], name=name)
    return dx, dgain, loss, dy, dgate


SCAN_BLOCK = 256


def _triangle(n, lower):
    r = lax.broadcasted_iota(jnp.int32, (n, n), 0)
    c = lax.broadcasted_iota(jnp.int32, (n, n), 1)
    return (r >= c if lower else r <= c).astype(F32)


def _forget_cumsum(logits, bias, name):
    S = logits.shape[0]
    blk = min(SCAN_BLOCK, S)
    nb = S // blk

    def body(z_ref, b_ref, f_ref):
        z = z_ref[...] + b_ref[...]
        f_ref[...] = jnp.minimum(z, 0.0) - jnp.log(1.0 + jnp.exp(-jnp.abs(z)))
        tri = _triangle(blk, lower=True)

        def step(i, carry):
            off = pl.multiple_of(i * blk, blk)
            cs = jnp.dot(tri, f_ref[pl.ds(off, blk), :], preferred_element_type=F32, precision=HIGHEST) + carry
            f_ref[pl.ds(off, blk), :] = cs
            return cs[blk - 1:blk, :]

        lax.fori_loop(0, nb, step, jnp.zeros((1, LANES), F32))

    return pl.pallas_call(body, name=name, out_shape=jax.ShapeDtypeStruct((S, LANES), F32))(logits, bias)


def _forget_bwd(dfk, logits, bias, name):
    S = logits.shape[0]
    blk = min(SCAN_BLOCK, S)
    nb = S // blk

    def body(d_ref, z_ref, b_ref, o_ref, db_ref):
        tri = _triangle(blk, lower=False)

        def step(t, carry):
            off = pl.multiple_of((nb - 1 - t) * blk, blk)
            cs = jnp.dot(tri, d_ref[pl.ds(off, blk), :], preferred_element_type=F32, precision=HIGHEST) + carry
            o_ref[pl.ds(off, blk), :] = cs
            return cs[0:1, :]

        lax.fori_loop(0, nb, step, jnp.zeros((1, LANES), F32))
        z = z_ref[...] + b_ref[...]
        dz = -o_ref[...] / (1.0 + jnp.exp(z))
        o_ref[...] = dz
        db_ref[...] = _colsum(dz)

    return pl.pallas_call(
        body, name=name,
        out_shape=(jax.ShapeDtypeStruct((S, LANES), F32), jax.ShapeDtypeStruct((1, LANES), F32)),
    )(dfk, logits, bias)


KEY_SCALE = HEAD_DIM ** -0.5
ATTN_BLOCK = 512
_NT = (((1,), (1,)), ((), ()))


def _attn_specs(S, H, tb):
    q_blk = lambda part: pl.BlockSpec((tb, HEAD_DIM), lambda h, i: (i, part * H + h))
    q_all = lambda part: pl.BlockSpec((S, HEAD_DIM), lambda h, i: (0, part * H + h))
    col_blk = pl.BlockSpec((None, tb, 1), lambda h, i: (h, i, 0))
    row_all = pl.BlockSpec((None, 1, S), lambda h, i: (h, 0, 0))
    return q_blk, q_all, col_blk, row_all


FWD_HEADS = 2


def _attn_fwd(qkv, f_col, f_row, name):
    S, H = qkv.shape[0], qkv.shape[1] // (3 * HEAD_DIM)
    tb = min(ATTN_BLOCK, S)
    hp = FWD_HEADS if H % FWD_HEADS == 0 else 1
    groups, wide = H // hp, hp * HEAD_DIM
    lanes = lambda u: pl.ds(u * HEAD_DIM, HEAD_DIM)

    def body(q_ref, k_ref, v_ref, fc_ref, fr_ref, o_ref, lse_ref):
        i = pl.program_id(1)

        def step(j, carry, diagonal):
            off = pl.multiple_of(j * tb, tb)
            out = []
            for u in range(hp):
                m, l, acc = carry[u]
                k, v = k_ref[pl.ds(off, tb), lanes(u)], v_ref[pl.ds(off, tb), lanes(u)]
                s = lax.dot_general(q_ref[:, lanes(u)], k, _NT, preferred_element_type=F32)
                s = s + (fc_ref[u] - fr_ref[u, :, pl.ds(off, tb)])
                if diagonal:
                    row = lax.broadcasted_iota(jnp.int32, (tb, tb), 0)
                    col = lax.broadcasted_iota(jnp.int32, (tb, tb), 1)
                    s = jnp.where(col <= row, s, NEG)
                m_new = jnp.maximum(m, jnp.max(s, axis=-1, keepdims=True))
                p = jnp.exp(s - m_new)
                alpha = jnp.exp(m - m_new)
                l = alpha * l + jnp.sum(p, axis=-1, keepdims=True)
                acc = alpha * acc + jnp.dot(p.astype(BF16), v, preferred_element_type=F32)
                out.append((m_new, l, acc))
            return tuple(out)

        init = (jnp.full((tb, 1), NEG, F32), jnp.zeros((tb, 1), F32), jnp.zeros((tb, HEAD_DIM), F32))
        carry = lax.fori_loop(0, i, lambda j, c: step(j, c, False), (init,) * hp)
        for u, (m, l, acc) in enumerate(step(i, carry, True)):
            o_ref[:, lanes(u)] = (acc / l).astype(o_ref.dtype)
            lse_ref[u] = m + jnp.log(l)

    part = lambda p, rows: pl.BlockSpec((rows, wide), lambda g, i: (i if rows == tb else 0, p * groups + g))
    col_blk = pl.BlockSpec((hp, tb, 1), lambda g, i: (g, i, 0))
    return pl.pallas_call(
        body, name=name, grid=(groups, S // tb),
        in_specs=[part(0, tb), part(1, S), part(2, S), col_blk, pl.BlockSpec((hp, 1, S), lambda g, i: (g, 0, 0))],
        out_specs=[pl.BlockSpec((tb, wide), lambda g, i: (i, g)), col_blk],
        out_shape=[jax.ShapeDtypeStruct((S, H * HEAD_DIM), BF16), jax.ShapeDtypeStruct((H, S, 1), F32)],
        compiler_params=_params(("parallel", "parallel"), 4 * S * wide * 2, 10 * hp * tb * tb * 4),
    )(qkv, qkv, qkv, f_col, f_row)


_TN = (((0,), (0,)), ((), ()))


def _attn_bwd(qkv, do, f_col, f_row, lse_col, name):
    S, H = qkv.shape[0], qkv.shape[1] // (3 * HEAD_DIM)
    tb = min(ATTN_BLOCK, S)
    nq = S // tb
    q_blk, q_all, col_blk, row_all = _attn_specs(S, H, tb)
    head_blk = pl.BlockSpec((tb, HEAD_DIM), lambda h, i: (i, h))
    head_all = pl.BlockSpec((S, HEAD_DIM), lambda h, i: (0, h))

    def body(q_ref, k_ref, v_ref, do_ref, fc_ref, fr_ref, lse_ref, dq_ref, dk_ref, dv_ref, dfk_ref,
             p_buf, dp_buf, dk_acc, dv_acc, dfk_acc):
        i = pl.program_id(1)
        q, do, fc, lse = q_ref[...], do_ref[...], fc_ref[...], lse_ref[...]

        @pl.when(i == 0)
        def _():
            dk_acc[...] = jnp.zeros_like(dk_acc)
            dv_acc[...] = jnp.zeros_like(dv_acc)
            dfk_acc[...] = jnp.zeros_like(dfk_acc)

        def scores(j, delta, diagonal):
            off = pl.multiple_of(j * tb, tb)
            k, v = k_ref[pl.ds(off, tb), :], v_ref[pl.ds(off, tb), :]
            s = lax.dot_general(q, k, _NT, preferred_element_type=F32) + (fc - fr_ref[:, pl.ds(off, tb)])
            if diagonal:
                row = lax.broadcasted_iota(jnp.int32, (tb, tb), 0)
                col = lax.broadcasted_iota(jnp.int32, (tb, tb), 1)
                s = jnp.where(col <= row, s, NEG)
            p = jnp.exp(s - lse)
            dp = lax.dot_general(do, v, _NT, preferred_element_type=F32)
            p_buf[j] = p
            dp_buf[j] = dp
            return delta + jnp.sum(p * dp, axis=-1, keepdims=True)

        delta = lax.fori_loop(0, i, lambda j, c: scores(j, c, False), jnp.zeros((tb, 1), F32))
        delta = scores(i, delta, True)

        def grad(j, dq):
            off = pl.multiple_of(j * tb, tb)
            p = p_buf[j]
            ds = p * (dp_buf[j] - delta)
            ds_lo = ds.astype(BF16)
            dk_acc[pl.ds(off, tb), :] += lax.dot_general(ds_lo, q, _TN, preferred_element_type=F32)
            dv_acc[pl.ds(off, tb), :] += lax.dot_general(p.astype(BF16), do, _TN, preferred_element_type=F32)
            dfk_acc[:, pl.ds(off, tb)] += jnp.sum(ds, axis=0, keepdims=True)
            return dq + jnp.dot(ds_lo, k_ref[pl.ds(off, tb), :], preferred_element_type=F32)

        dq = lax.fori_loop(0, i + 1, grad, jnp.zeros((tb, HEAD_DIM), F32))
        dq_ref[...] = dq.astype(dq_ref.dtype)

        @pl.when(i == nq - 1)
        def _():
            dk_ref[...] = (dk_acc[...] * KEY_SCALE).astype(dk_ref.dtype)
            dv_ref[...] = dv_acc[...].astype(dv_ref.dtype)
            dfk_ref[...] = dfk_acc[...]

    wide = jax.ShapeDtypeStruct((S, H * HEAD_DIM), BF16)
    return pl.pallas_call(
        body, name=name, grid=(H, nq),
        in_specs=[q_blk(0), q_all(1), q_all(2), head_blk, col_blk, row_all, col_blk],
        out_specs=[head_blk, head_all, head_all, row_all],
        out_shape=[wide, wide, wide, jax.ShapeDtypeStruct((H, 1, S), F32)],
        scratch_shapes=[pltpu.VMEM((nq, tb, tb), F32), pltpu.VMEM((nq, tb, tb), F32),
                        pltpu.VMEM((S, HEAD_DIM), F32), pltpu.VMEM((S, HEAD_DIM), F32), pltpu.VMEM((1, S), F32)],
        compiler_params=_params(("parallel", "arbitrary"), 6 * S * HEAD_DIM * 2,
                                2 * nq * tb * tb * 4 + 2 * S * HEAD_DIM * 4 + 10 * tb * tb * 4),
    )(qkv, qkv, qkv, do, f_col, f_row, lse_col)


CONV_TILE = 128


def _shift_down(v, n):
    row = lax.broadcasted_iota(jnp.int32, v.shape, 0)
    return jnp.where(row >= n, pltpu.roll(v, n, 0), 0.0)


def _shift_up(v, n):
    S = v.shape[0]
    row = lax.broadcasted_iota(jnp.int32, v.shape, 0)
    return jnp.where(row < S - n, pltpu.roll(v, S - n, 0), 0.0)


def _conv_specs(S, D, tc):
    nb = D // tc
    part = lambda p: pl.BlockSpec((S, tc), lambda j: (0, p * nb + j))
    return part, pl.BlockSpec((S, tc), lambda j: (0, j)), pl.BlockSpec((8, tc), lambda j: (0, j))


def _conv_fwd(proj, conv_w8, name):
    S, D = proj.shape[0], proj.shape[1] // 3
    tc = min(CONV_TILE, D)
    part, chan, taps = _conv_specs(S, D, tc)

    def body(b_ref, c_ref, u_ref, w_ref, z_ref):
        cu = c_ref[...].astype(F32) * u_ref[...].astype(F32)
        w = w_ref[...]
        y = w[0:1, :] * _shift_down(cu, 2) + w[1:2, :] * _shift_down(cu, 1) + w[2:3, :] * cu
        z_ref[...] = (b_ref[...].astype(F32) * y).astype(z_ref.dtype)

    return pl.pallas_call(
        body, name=name, grid=(D // tc,), in_specs=[part(0), part(1), part(2), taps], out_specs=chan,
        out_shape=jax.ShapeDtypeStruct((S, D), BF16),
        compiler_params=_params(("parallel",), 3 * _nbytes((S, tc), proj.dtype) + S * tc * 2, 6 * S * tc * 4),
    )(proj, proj, proj, conv_w8)


def _conv_bwd(proj, dz, conv_w8, name):
    S, D = proj.shape[0], proj.shape[1] // 3
    tc = min(CONV_TILE, D)
    part, chan, taps = _conv_specs(S, D, tc)

    def body(b_ref, c_ref, u_ref, dz_ref, w_ref, db_ref, dc_ref, du_ref, dw_ref):
        cv, uv = c_ref[...].astype(F32), u_ref[...].astype(F32)
        dzv, w = dz_ref[...].astype(F32), w_ref[...]
        cu = cv * uv
        cu1, cu2 = _shift_down(cu, 1), _shift_down(cu, 2)
        y = w[0:1, :] * cu2 + w[1:2, :] * cu1 + w[2:3, :] * cu
        db_ref[...] = (dzv * y).astype(db_ref.dtype)
        dy = dzv * b_ref[...].astype(F32)
        dcu = w[2:3, :] * dy + w[1:2, :] * _shift_up(dy, 1) + w[0:1, :] * _shift_up(dy, 2)
        dc_ref[...] = (dcu * uv).astype(dc_ref.dtype)
        du_ref[...] = (dcu * cv).astype(du_ref.dtype)
        dw_ref[...] = jnp.concatenate(
            [_colsum(dy * cu2), _colsum(dy * cu1), _colsum(dy * cu), jnp.zeros((8 - CONV_WIDTH, tc), F32)], axis=0)

    return pl.pallas_call(
        body, name=name, grid=(D // tc,), in_specs=[part(0), part(1), part(2), chan, taps],
        out_specs=[chan, chan, chan, taps],
        out_shape=[jax.ShapeDtypeStruct((S, D), BF16)] * 3 + [jax.ShapeDtypeStruct((8, D), F32)],
        compiler_params=_params(("parallel",), 3 * _nbytes((S, tc), proj.dtype) + _nbytes((S, tc), dz.dtype)
                                + 3 * S * tc * 2, 10 * S * tc * 4),
    )(proj, proj, proj, dz, conv_w8)


def _adamw(w, m, v, parts, name, layer=0, prev=None):
    L, R, C = w.shape
    P = parts.shape[0]
    assert parts.shape[1:] == (R, C), (name, parts.shape, w.shape)
    elem_bytes = 12 + 16 + P * parts.dtype.itemsize
    budget = 8 << 20
    tr, tc = R, C
    if R * C * elem_bytes > budget:
        if R % 8 == 0:
            tr = max(8, (budget // (C * elem_bytes)) // 8 * 8)
            while R % tr:
                tr -= 8
        else:
            tc = LANES
            while C % (2 * tc) == 0 and R * 2 * tc * elem_bytes <= budget:
                tc *= 2
            assert C % tc == 0, (name, R, C)
    c1, c2 = 1.0 - ADAM_B1 ** ADAM_STEP, 1.0 - ADAM_B2 ** ADAM_STEP

    def body(w_ref, m_ref, v_ref, p_ref, *rest):
        g_ref, d_ref, nm_ref, nv_ref = rest[-4:]
        g = p_ref[0].astype(F32)
        for p in range(1, P):
            g = g + p_ref[p].astype(F32)
        nm = ADAM_B1 * m_ref[...] + (1.0 - ADAM_B1) * g
        nv = ADAM_B2 * v_ref[...] + (1.0 - ADAM_B2) * (g * g)
        g_ref[...] = g
        nm_ref[...] = nm
        nv_ref[...] = nv
        d_ref[...] = -ADAM_LR * ((nm / c1) / (jnp.sqrt(nv / c2) + ADAM_EPS) + ADAM_WD * w_ref[...])

    blk = pl.BlockSpec((None, tr, tc), lambda i, j: (layer, i, j))
    prev = [] if prev is None else list(prev)
    return pl.pallas_call(
        body, name=name, grid=(R // tr, C // tc),
        in_specs=[blk, blk, blk, pl.BlockSpec((P, tr, tc), lambda i, j: (0, i, j))] + [ANY] * len(prev),
        out_specs=[blk] * 4, out_shape=[jax.ShapeDtypeStruct((L, R, C), F32)] * 4,
        input_output_aliases={4 + k: k for k in range(len(prev))},
        compiler_params=_params(("parallel", "parallel"), tr * tc * elem_bytes),
    )(w, m, v, parts, *prev)


def _silu(v):
    return v / (1.0 + jnp.exp(-v))


def _pad_rows(a, rows):
    return jnp.pad(a, ((0, rows - a.shape[0]), (0, 0)))


def _pad_cols(a, cols):
    return jnp.pad(a, ((0, 0), (0, cols - a.shape[1])))


def kernel(x, c, ada_w, ada_b, norm_mix, norm_mlp, fox_w_in, fox_b_f, fox_w_out, conv_w_in, conv_w, conv_w_out, mlp_w_up, mlp_w_down, final_norm, loss_target, m_ada_w, m_ada_b, m_norm_mix, m_norm_mlp, m_fox_w_in, m_fox_b_f, m_fox_w_out, m_conv_w_in, m_conv_w, m_conv_w_out, m_mlp_w_up, m_mlp_w_down, m_final_norm, v_ada_w, v_ada_b, v_norm_mix, v_norm_mlp, v_fox_w_in, v_fox_b_f, v_fox_w_out, v_conv_w_in, v_conv_w, v_conv_w_out, v_mlp_w_up, v_mlp_w_down, v_final_norm):
    S, D = x.shape[1], x.shape[2]
    H = D // HEAD_DIM
    FF = mlp_w_up.shape[2] * NDEV
    depth = ada_w.shape[0]
    n_mod = 6
    assert depth == 2 and fox_w_in.shape[0] == 1 and conv_w_in.shape[0] == 1 and H <= LANES
    me = _my_index()
    x0, target = x[0], loss_target[0]
    row = lambda vec: vec.reshape(1, -1)

    def tied(vec, token):
        return vec + token[0, 0]

    bf = lambda w: w.astype(BF16)
    gather_groups = {
        "fox": [bf(fox_w_in[0]).T, bf(fox_w_out[0])],
        "mlp0": [bf(mlp_w_up[0]).T, bf(mlp_w_down[0])],
        "conv": [bf(conv_w_in[0]).T, conv_w[0], bf(conv_w_out[0])],
        "mlp1": [bf(mlp_w_up[1]).T, bf(mlp_w_down[1])],
    }

    def start_gather(group, after):
        return _exchange_start(gather_groups[group], f"gather_{group}_start", False, after, relay=True)

    def relay_gather(handle, group, after):
        handle = _exchange_wait(handle, f"gather_{group}_arrivals", after, arrivals_only=True)
        return _relay_forward_start(handle, f"gather_{group}_forward")

    def finish_gather(handle, group, after):
        return _exchange_wait(handle, f"gather_{group}_wait", after)

    landed = lambda handle: handle[4][0]

    c_all = _all_gather([c], "gather_cond")[0].reshape(NDEV, D)
    ncol = ada_w.shape[2]
    ada_b_mine = lax.dynamic_slice_in_dim(ada_b, me * ncol, ncol, axis=1)
    mod_cols = jnp.stack([
        _matmul(c_all, ada_w.reshape(depth * D, ncol), mode="nn", name=f"ada_fwd_{i}", out_dtypes=[F32], tm=NDEV,
                tn=ncol // 2, tk=D, b_first_block=i,
                a_pre=_silu, precision=HIGHEST, epilogue=lambda acc, b: (acc + b,), extras=[(ada_b_mine[i:i + 1], "row")])
        for i in range(depth)])
    mod_all = _all_gather([mod_cols], "gather_mod")[0]
    mod = lax.dynamic_index_in_dim(mod_all, me, axis=2, keepdims=False)
    fox_handle, token = start_gather("fox", mod_all)
    mod = tied(mod, token).transpose(1, 0, 2).reshape(depth, n_mod, 1, D)
    sh_mix, sc_mix, g_mix, sh_mlp, sc_mlp, g_mlp = (mod[:, k] for k in range(n_mod))
    b_f = _pad_cols(fox_b_f, LANES)

    def residual(acc, x_in, gate):
        return (x_in + gate * acc, acc)

    def mlp_fwd(i, x_in, handle, relay_next=None):
        h, inv = _rms_mod_fwd(x_in, row(norm_mlp[i]), sh_mlp[i], sc_mlp[i], f"mlp_norm_{i}")
        w_up_t, w_down = finish_gather(handle, f"mlp{i}", h)
        w_up_t, w_down = w_up_t.reshape(FF, D), w_down.reshape(FF, D)
        r = _matmul(h, w_up_t, mode="nt", name=f"mlp_up_{i}", out_dtypes=[BF16], tm=1024, tn=1024, tk=D,
                    epilogue=lambda acc: (jnp.maximum(acc, 0.0),))
        next_handle = relay_gather(relay_next[1], relay_next[0], r) if relay_next else None
        x_out, y = _matmul(r, w_down, mode="nn", name=f"mlp_down_{i}", out_dtypes=[F32, BF16], tm=512, tn=512, tk=FF,
                           n_outer=True, a_pre=jnp.square, epilogue=residual, extras=[(x_in, "tile"), (g_mlp[i], "row")],
                           after=[landed(next_handle)] if relay_next else [])
        return x_out, (x_in, h, inv, r, y, w_up_t, w_down), next_handle

    def mlp_bwd(i, dx, dy, dgate, saved, following, after):
        x_in, h, inv, r, y, w_up_t, w_down = saved
        du = _matmul(dy, w_down, mode="nt", name=f"mlp_down_bwd_{i}", out_dtypes=[BF16], tm=1024, tn=1024, tk=D,
                     epilogue=lambda acc, rv: (acc * (2.0 * rv.astype(F32)),), extras=[(r, "tile")], after=after)
        d_down = _matmul(r, dy, mode="tn", name=f"mlp_down_wgrad_{i}", out_dtypes=[BF16], tm=512, tn=1024, tk=S,
                         a_pre=jnp.square)
        dh = _matmul(du, w_up_t, mode="nn", name=f"mlp_up_bwd_{i}", out_dtypes=[F32], tm=512, tn=512, tk=FF, n_outer=True)
        d_up = _matmul(h, du, mode="tn", name=f"mlp_up_wgrad_{i}", out_dtypes=[BF16], tm=512, tn=FF // NDEV, tk=S,
                       out_shards=True)
        dx, dsh, dsc, dgain, dy_next, dgate_next = _rms_mod_bwd(dh, x_in, inv, dx, row(norm_mlp[i]), sc_mlp[i],
                                                                f"mlp_norm_bwd_{i}", following)
        handle, token = _exchange_start([d_up, d_down.reshape(NDEV, FF // NDEV, D)], f"scatter_mlp{i}_start", True, dx)
        return dx, (dsh, dsc, dgate, dgain), handle, token, dy_next, dgate_next

    h0, inv0 = _rms_mod_fwd(x0, row(norm_mix[0]), sh_mix[0], sc_mix[0], "fox_norm")
    w_in_t, w_fox_out = finish_gather(relay_gather(fox_handle, "fox", h0), "fox", h0)
    mlp0_handle, token = start_gather("mlp0", w_in_t)
    w_in_t = w_in_t.reshape(3 * D + H, D)
    w_f_t = _pad_rows(w_in_t[3 * D:], LANES)
    w_fox_out = w_fox_out.reshape(D, D)
    column_scale = jnp.concatenate([jnp.ones((1, D), F32), jnp.full((1, D), KEY_SCALE, F32), jnp.ones((1, D), F32)], axis=1)
    qkv = _matmul(h0, w_in_t, mode="nt", name="fox_qkv", out_dtypes=[BF16], tm=1024, tn=1024, tk=D, n=3 * D, after=[token],
                  epilogue=lambda acc, mult: (acc * mult,), extras=[(column_scale, "row")])
    f_logit = _matmul(h0, w_f_t, mode="nt", name="fox_forget_logits", out_dtypes=[F32], tm=1024, tn=LANES, tk=D)
    f_cum = _forget_cumsum(f_logit, b_f, "fox_forget_cumsum")
    f_heads = f_cum[:, :H].T
    f_col, f_row = f_heads.reshape(H, S, 1), f_heads.reshape(H, 1, S)
    o, lse = _attn_fwd(qkv, f_col, f_row, "fox_attention")
    mlp0_handle = relay_gather(mlp0_handle, "mlp0", o)
    conv_handle, token = start_gather("conv", landed(mlp0_handle))
    mlp1_handle, token = start_gather("mlp1", token)
    x1, mix0 = _matmul(o, w_fox_out, mode="nn", name="fox_out", out_dtypes=[F32, BF16], tm=512, tn=1024, tk=D,
                       epilogue=residual, extras=[(x0, "tile"), (g_mix[0], "row")], after=[token])
    x2, mlp0, conv_handle = mlp_fwd(0, x1, mlp0_handle, ("conv", conv_handle))

    h1, inv1 = _rms_mod_fwd(x2, row(norm_mix[1]), sh_mix[1], sc_mix[1], "conv_norm")
    w_conv_in_t, w_taps, w_conv_out = finish_gather(conv_handle, "conv", h1)
    w_conv_in_t = w_conv_in_t.reshape(3 * D, D)
    w_taps = _pad_rows(w_taps.transpose(1, 0, 2).reshape(CONV_WIDTH, D), 8)
    w_conv_out = w_conv_out.reshape(D, D)
    proj = _matmul(h1, w_conv_in_t, mode="nt", name="conv_in", out_dtypes=[BF16], tm=1024, tn=1024, tk=D)
    mlp1_handle = relay_gather(mlp1_handle, "mlp1", proj)
    z = _conv_fwd(proj, w_taps, "conv_mix")
    x3, mix1 = _matmul(z, w_conv_out, mode="nn", name="conv_out", out_dtypes=[F32, BF16], tm=512, tn=1024, tk=D,
                       epilogue=residual, extras=[(x2, "tile"), (g_mix[1], "row")], after=[landed(mlp1_handle)])
    x4, mlp1, _ = mlp_fwd(1, x3, mlp1_handle)

    dx, d_final, loss_lanes, dy, dgate = _final_loss_bwd(x4, target, row(final_norm), (mlp1[4], g_mlp[1]), "loss_head")

    dx, dmod_mlp1, mlp1_scatter, token, dmix, dg_mix1 = mlp_bwd(1, dx, dy, dgate, mlp1, (mix1, g_mix[1]), [])
    dz = _matmul(dmix, w_conv_out, mode="nt", name="conv_out_bwd", out_dtypes=[BF16], tm=1024, tn=1024, tk=D,
                 after=[token])
    d_conv_out = _matmul(z, dmix, mode="tn", name="conv_out_wgrad", out_dtypes=[BF16], tm=512, tn=1024, tk=S)
    db, dc, du, d_taps = _conv_bwd(proj, dz, w_taps, "conv_mix_bwd")
    dproj = jnp.concatenate([db, dc, du], axis=1)
    dh1 = _matmul(dproj, w_conv_in_t, mode="nn", name="conv_in_bwd", out_dtypes=[F32], tm=512, tn=512, tk=3 * D, n_outer=True)
    d_conv_in = _matmul(h1, dproj, mode="tn", name="conv_in_wgrad", out_dtypes=[BF16], tm=512, tn=3 * D // NDEV, tk=S,
                        out_shards=True)
    dx, dsh1, dsc1, dgain_mix1, dy, dgate = _rms_mod_bwd(dh1, x2, inv1, dx, row(norm_mix[1]), sc_mix[1], "conv_norm_bwd",
                                                         (mlp0[4], g_mlp[0]))
    d_taps_split = d_taps[:CONV_WIDTH].reshape(CONV_WIDTH, NDEV, -1).transpose(1, 0, 2)
    conv_scatter, token = _exchange_start([d_conv_in, d_taps_split, d_conv_out.reshape(NDEV, D // NDEV, D)],
                                          "scatter_conv_start", True, dx)

    dx, dmod_mlp0, mlp0_scatter, token, dmix, dg_mix0 = mlp_bwd(0, dx, dy, dgate, mlp0, (mix0, g_mix[0]), [token])
    do = _matmul(dmix, w_fox_out, mode="nt", name="fox_out_bwd", out_dtypes=[BF16], tm=1024, tn=1024, tk=D,
                 after=[token])
    d_fox_out = _matmul(o, dmix, mode="tn", name="fox_out_wgrad", out_dtypes=[BF16], tm=512, tn=1024, tk=S)
    dq, dk, dv, dfk = _attn_bwd(qkv, do, f_col, f_row, lse, "fox_attention_bwd")
    dqkv = jnp.concatenate([dq, dk, dv], axis=1)
    dfk_lanes = _pad_cols(dfk.reshape(H, S).T, LANES)
    df_logit, db_f = _forget_bwd(dfk_lanes, f_logit, b_f, "fox_forget_bwd")
    d_qkv_t = _matmul(dqkv, h0, mode="tn", name="fox_qkv_wgrad", out_dtypes=[BF16], tm=512, tn=1024, tk=S)
    d_f_t = _matmul(df_logit, h0, mode="tn", name="fox_forget_wgrad", out_dtypes=[BF16], tm=LANES, tn=1024, tk=S)
    d_fox_in = jnp.concatenate([d_qkv_t, d_f_t[:H]], axis=0).reshape(NDEV, -1, D)
    fox_scatter, token = _exchange_start([d_fox_in, d_fox_out.reshape(NDEV, D // NDEV, D)], "scatter_fox_start", True,
                                         d_fox_in)
    dh0_f = _matmul(df_logit, w_f_t, mode="nn", name="fox_forget_logits_bwd", out_dtypes=[F32], tm=1024, tn=1024, tk=LANES,
                    after=[token])
    dh0 = _matmul(dqkv, w_in_t, mode="nn", name="fox_qkv_bwd", out_dtypes=[F32], tm=512, tn=512, tk=3 * D, n_outer=True,
                  epilogue=lambda acc, extra: (acc + extra,), extras=[(dh0_f, "tile")])
    dx, dsh0, dsc0, dgain_mix0 = _rms_mod_bwd(dh0, x0, inv0, dx, row(norm_mix[0]), sc_mix[0], "fox_norm_bwd")
    grad_x = dx.reshape(1, S, D)

    dmod = jnp.concatenate([
        jnp.concatenate([dsh0, dsc0, dg_mix0, dmod_mlp0[0], dmod_mlp0[1], dmod_mlp0[2]], axis=1),
        jnp.concatenate([dsh1, dsc1, dg_mix1, dmod_mlp1[0], dmod_mlp1[1], dmod_mlp1[2]], axis=1)], axis=0)
    small_sizes = [depth * n_mod * D, depth * D, depth * D, H, D, 1]
    n_small = sum(small_sizes)
    n_rows = -(-n_small // (8 * LANES)) * 8

    def pack(parts):
        flat = jnp.concatenate([p.reshape(-1) for p in parts])
        return jnp.pad(flat, (0, n_rows * LANES - n_small)).reshape(n_rows, LANES)

    def unpack(packed, shapes):
        flat, out, at = packed.reshape(-1), [], 0
        for size, shape in zip(small_sizes, shapes):
            out.append(flat[at:at + size].reshape(shape))
            at += size
        return out

    small_partial = pack([dmod, jnp.concatenate([dgain_mix0, dgain_mix1], axis=0),
                          jnp.concatenate([dmod_mlp0[3], dmod_mlp1[3]], axis=0), db_f[0, :H], d_final, loss_lanes[0, :1]])
    small_handle, token = _exchange_start([small_partial], "gather_small_start", False, dx)

    up1, down1 = _exchange_wait(mlp1_scatter, "scatter_mlp1_wait", token)
    up_out = _adamw(mlp_w_up, m_mlp_w_up, v_mlp_w_up, up1, "adamw_mlp_w_up_1", layer=1)
    down_out = _adamw(mlp_w_down, m_mlp_w_down, v_mlp_w_down, down1, "adamw_mlp_w_down_1", layer=1)
    cin, taps, cout = _exchange_wait(conv_scatter, "scatter_conv_wait", down_out[0])
    conv_in_out = _adamw(conv_w_in, m_conv_w_in, v_conv_w_in, cin, "adamw_conv_w_in")
    conv_w_res = _adamw(conv_w, m_conv_w, v_conv_w, taps, "adamw_conv_w")
    conv_out_out = _adamw(conv_w_out, m_conv_w_out, v_conv_w_out, cout, "adamw_conv_w_out")
    up0, down0 = _exchange_wait(mlp0_scatter, "scatter_mlp0_wait", conv_out_out[0])
    up_out = _adamw(mlp_w_up, m_mlp_w_up, v_mlp_w_up, up0, "adamw_mlp_w_up_0", layer=0, prev=up_out)
    down_out = _adamw(mlp_w_down, m_mlp_w_down, v_mlp_w_down, down0, "adamw_mlp_w_down_0", layer=0, prev=down_out)

    small_parts = _exchange_wait(small_handle, "gather_small_wait", down_out[0])[0]
    small_shapes = [ada_b.shape, norm_mix.shape, norm_mlp.shape, fox_b_f.shape, final_norm.shape]
    loss = jnp.sum(small_parts.reshape(NDEV, -1)[:, n_small - 1])
    unused = jnp.zeros((1,), F32)
    small_out = _adamw(pack([ada_b, norm_mix, norm_mlp, fox_b_f, final_norm, unused])[None],
                       pack([m_ada_b, m_norm_mix, m_norm_mlp, m_fox_b_f, m_final_norm, unused])[None],
                       pack([v_ada_b, v_norm_mix, v_norm_mlp, v_fox_b_f, v_final_norm, unused])[None], small_parts,
                       "adamw_small")
    small_out = [unpack(t, small_shapes) for t in small_out]

    dmod_all = small_parts.reshape(NDEV, -1)[:, :depth * n_mod * D].reshape(NDEV, depth, n_mod * D)
    dmod_mine = lax.dynamic_slice_in_dim(dmod_all, me * ncol, ncol, axis=2)
    ada_out = None
    for i in range(depth):
        d_ada = _matmul(c_all, dmod_mine[:, i], mode="tn", name=f"ada_wgrad_{i}", out_dtypes=[F32], tm=1024, tn=ncol // 2,
                        tk=NDEV, a_pre=_silu, precision=HIGHEST)
        ada_out = _adamw(ada_w, m_ada_w, v_ada_w, d_ada[None], f"adamw_ada_w_{i}", layer=i, prev=ada_out)

    fin, fout = _exchange_wait(fox_scatter, "scatter_fox_wait", ada_out[0])
    swap = lambda t: jnp.swapaxes(t, 1, 2)
    fox_in_out = [swap(t) for t in _adamw(swap(fox_w_in), swap(m_fox_w_in), swap(v_fox_w_in), fin, "adamw_fox_w_in")]
    fox_out_out = _adamw(fox_w_out, m_fox_w_out, v_fox_w_out, fout, "adamw_fox_w_out")

    outputs = [loss, grad_x]
    for kind in range(4):
        sm = small_out[kind]
        outputs += [ada_out[kind], sm[0], sm[1], sm[2], fox_in_out[kind], sm[3], fox_out_out[kind], conv_in_out[kind],
                    conv_w_res[kind], conv_out_out[kind], up_out[kind], down_out[kind], sm[4]]
    return tuple(outputs)
```

```python
import math

import jax
import jax.numpy as jnp
from jax import lax
from jax.experimental import pallas as pl
from jax.experimental.pallas import tpu as pltpu

F32 = jnp.float32
BF16 = jnp.bfloat16
MESH = pl.DeviceIdType.MESH
NDEV = 8
HEAD_DIM = 128
LANES = 128
CONV_WIDTH = 3
RMS_EPS = 1e-6
ADAM_LR, ADAM_B1, ADAM_B2, ADAM_EPS, ADAM_WD, ADAM_STEP = 0.001, 0.9, 0.999, 1e-08, 0.01, 10
NEG = -1e30
V7X_VMEM_BYTES = 64 * 1024 * 1024
VMEM_HEADROOM = 12 * 1024 * 1024
HBM = pl.BlockSpec(memory_space=pltpu.HBM)
HIGHEST = lax.Precision.HIGHEST


def _nbytes(shape, dtype):
    return math.prod(shape) * jnp.dtype(dtype).itemsize


def _params(semantics, block_bytes, temp_bytes=0):
    limit = min(2 * block_bytes + temp_bytes + VMEM_HEADROOM, V7X_VMEM_BYTES - 4 * 1024 * 1024)
    return pltpu.CompilerParams(dimension_semantics=semantics, vmem_limit_bytes=int(limit))


def _my_index():
    return lax.axis_index("x") * 4 + lax.axis_index("y") * 2 + lax.axis_index("c")


def _peer(r):
    x, y, c = lax.axis_index("x"), lax.axis_index("y"), lax.axis_index("c")
    px = 1 - x if (r >> 2) & 1 else x
    py = 1 - y if (r >> 1) & 1 else y
    pc = 1 - c if r & 1 else c
    return (px, py, pc), px * 4 + py * 2 + pc


def _exchange(arrays, name, scatter, after=None):
    n = len(arrays)
    after = [] if after is None else list(after)

    def body(*refs):
        ins, outs = refs[:n], refs[n + len(after):2 * n + len(after)]
        send_sems, recv_sems, local_sems = refs[2 * n + len(after):]
        me = _my_index()
        local = []
        for a in range(n):
            src = ins[a].at[me] if scatter else ins[a]
            local.append(pltpu.make_async_copy(src, outs[a].at[me], local_sems.at[a]))
            local[-1].start()
        sends = []
        for r in range(1, NDEV):
            peer, pidx = _peer(r)
            for a in range(n):
                src = ins[a].at[pidx] if scatter else ins[a]
                cp = pltpu.make_async_remote_copy(
                    src_ref=src, dst_ref=outs[a].at[me],
                    send_sem=send_sems.at[a * (NDEV - 1) + r - 1], recv_sem=recv_sems.at[a * (NDEV - 1) + r - 1],
                    device_id=peer, device_id_type=MESH)
                cp.start()
                sends.append(cp)
        for r in range(1, NDEV):
            peer, pidx = _peer(r)
            for a in range(n):
                src = ins[a].at[pidx] if scatter else ins[a]
                pltpu.make_async_remote_copy(
                    src_ref=src, dst_ref=outs[a].at[pidx],
                    send_sem=send_sems.at[a * (NDEV - 1) + r - 1], recv_sem=recv_sems.at[a * (NDEV - 1) + r - 1],
                    device_id=peer, device_id_type=MESH).wait_recv()
        for cp in sends:
            cp.wait_send()
        for cp in local:
            cp.wait()

    out_shape = [jax.ShapeDtypeStruct(a.shape if scatter else (NDEV,) + a.shape, a.dtype) for a in arrays]
    return pl.pallas_call(
        body, name=name, out_shape=out_shape, in_specs=[HBM] * n + [ANY] * len(after), out_specs=[HBM] * n,
        scratch_shapes=[pltpu.SemaphoreType.DMA((n * (NDEV - 1),)), pltpu.SemaphoreType.DMA((n * (NDEV - 1),)),
                        pltpu.SemaphoreType.DMA((n,))],
    )(*arrays, *after)


def _all_gather(arrays, name, after=None):
    return _exchange(arrays, name, scatter=False, after=after)


SEM = pl.BlockSpec(memory_space=pltpu.SEMAPHORE)
ANY = pl.BlockSpec(memory_space=pl.ANY)
DATAFLOW = pltpu.SideEffectType.DATAFLOW_SIDE_EFFECTING
TOKEN_SHAPE = (8, LANES)


SIBLING = 1
OTHER_CHIPS = (4, 2, 6)


def _exchange_start(arrays, name, scatter, after, relay=False):
    n = len(arrays)
    n_sems = n * (NDEV - 1)
    assert not (relay and scatter)

    def body(*refs):
        ins = refs[:n]
        send_sems, recv_sems = refs[n + 1], refs[n + 2]
        lands, token = refs[2 * n + 3:3 * n + 3], refs[3 * n + 3]
        me = _my_index()
        for r in (SIBLING, *OTHER_CHIPS) if relay else range(1, NDEV):
            peer, pidx = _peer(r)
            for a in range(n):
                src = ins[a].at[pidx] if scatter else ins[a]
                pltpu.make_async_remote_copy(
                    src_ref=src, dst_ref=lands[a].at[me],
                    send_sem=send_sems.at[a * (NDEV - 1) + r - 1], recv_sem=recv_sems.at[a * (NDEV - 1) + r - 1],
                    device_id=peer, device_id_type=MESH).start()
        token[...] = jnp.zeros(TOKEN_SHAPE, F32)

    land_shapes = [a.shape if scatter else (NDEV,) + a.shape for a in arrays]
    srcs = [pltpu.with_memory_space_constraint(a, pltpu.HBM) for a in arrays]
    outs = pl.pallas_call(
        body, name=name,
        out_shape=(pltpu.SemaphoreType.DMA((n_sems,)), pltpu.SemaphoreType.DMA((n_sems,)),
                   *[pltpu.HBM(a.shape, a.dtype) for a in arrays], *[pltpu.HBM(s, a.dtype) for s, a in zip(land_shapes, arrays)],
                   jax.ShapeDtypeStruct(TOKEN_SHAPE, F32)),
        in_specs=[HBM] * n + [ANY],
        out_specs=(SEM, SEM, *[HBM] * (2 * n), pl.BlockSpec(memory_space=pltpu.VMEM)),
        input_output_aliases={i: 2 + i for i in range(n)},
        compiler_params=pltpu.CompilerParams(has_side_effects=DATAFLOW),
    )(*srcs, after)
    return (scatter, relay, [(outs[0], outs[1])], list(outs[2:2 + n]), list(outs[2 + n:2 + 2 * n])), outs[-1]


def _relay_forward_start(handle, name):
    scatter, relay, sems, srcs, lands = handle
    n = len(lands)
    n_sems = n * len(OTHER_CHIPS)

    def body(*refs):
        land_refs, send_sems, recv_sems = refs[:n], refs[n], refs[n + 1]
        sibling, _ = _peer(SIBLING)
        for j, r in enumerate(OTHER_CHIPS):
            _, pidx = _peer(r)
            for a in range(n):
                pltpu.make_async_remote_copy(
                    src_ref=land_refs[a].at[pidx], dst_ref=land_refs[a].at[pidx],
                    send_sem=send_sems.at[a * len(OTHER_CHIPS) + j], recv_sem=recv_sems.at[a * len(OTHER_CHIPS) + j],
                    device_id=sibling, device_id_type=MESH).start()

    outs = pl.pallas_call(
        body, name=name,
        out_shape=(pltpu.SemaphoreType.DMA((n_sems,)), pltpu.SemaphoreType.DMA((n_sems,)),
                   *[pltpu.HBM(t.shape, t.dtype) for t in lands]),
        in_specs=[HBM] * n, out_specs=(SEM, SEM, *[HBM] * n),
        input_output_aliases={i: 2 + i for i in range(n)},
        compiler_params=pltpu.CompilerParams(has_side_effects=DATAFLOW),
    )(*lands)
    return (scatter, relay, sems + [(outs[0], outs[1])], srcs, list(outs[2:]))


def _exchange_wait(handle, name, after, arrivals_only=False):
    scatter, relay, sems, srcs, lands = handle
    n = len(srcs)
    forwarded = len(sems) == 2
    assert not arrivals_only or (relay and not forwarded)

    def body(*refs):
        src_refs, land_refs = refs[:n], refs[n:2 * n]
        send_sems, recv_sems = refs[2 * n], refs[2 * n + 1]
        for r in (SIBLING, *OTHER_CHIPS) if relay else range(1, NDEV):
            peer, pidx = _peer(r)
            for a in range(n):
                src = src_refs[a].at[pidx] if scatter else src_refs[a]
                cp = pltpu.make_async_remote_copy(
                    src_ref=src, dst_ref=land_refs[a].at[pidx],
                    send_sem=send_sems.at[a * (NDEV - 1) + r - 1], recv_sem=recv_sems.at[a * (NDEV - 1) + r - 1],
                    device_id=peer, device_id_type=MESH)
                if arrivals_only:
                    if r in OTHER_CHIPS:
                        cp.wait_recv()
                else:
                    cp.wait_send()
                    if not (relay and r in OTHER_CHIPS):
                        cp.wait_recv()
        if forwarded:
            fwd_send, fwd_recv = refs[2 * n + 2], refs[2 * n + 3]
            sibling, _ = _peer(SIBLING)
            for j, r in enumerate(OTHER_CHIPS):
                _, pidx = _peer(r ^ SIBLING)
                for a in range(n):
                    cp = pltpu.make_async_remote_copy(
                        src_ref=src_refs[a], dst_ref=land_refs[a].at[pidx],
                        send_sem=fwd_send.at[a * len(OTHER_CHIPS) + j], recv_sem=fwd_recv.at[a * len(OTHER_CHIPS) + j],
                        device_id=sibling, device_id_type=MESH)
                    cp.wait_send()
                    cp.wait_recv()

    flat_sems = [s for pair in sems for s in pair]
    outs = pl.pallas_call(
        body, name=name,
        out_shape=tuple(pltpu.HBM(t.shape, t.dtype) for t in (*srcs, *lands)),
        in_specs=[HBM] * (2 * n) + [SEM] * len(flat_sems) + [ANY], out_specs=tuple([HBM] * (2 * n)),
        input_output_aliases={i: i for i in range(2 * n)},
        compiler_params=pltpu.CompilerParams(has_side_effects=DATAFLOW),
    )(*srcs, *lands, *flat_sems, after)
    if arrivals_only:
        return (scatter, relay, sems, list(outs[:n]), list(outs[n:]))
    me = _my_index()
    mine = [lax.dynamic_index_in_dim(s, me, 0, keepdims=False) if scatter else s for s in outs[:n]]
    return [lax.dynamic_update_index_in_dim(land, own, me, 0) for land, own in zip(outs[n:], mine)]


def _matmul(a, b, *, mode, name, out_dtypes, tm, tn, tk, epilogue=None, extras=(), a_pre=None,
            out_shards=False, n_outer=False, precision=None, after=(), n=None, b_first_block=0):
    after = list(after)
    n_after = len(after)
    parts = list(a) if isinstance(a, (list, tuple)) else [a]
    n_parts = len(parts)
    rows, cols = parts[0].shape
    K, M = (rows, cols * n_parts) if mode == "tn" else (cols * n_parts, rows)
    N = n if n is not None else (b.shape[0] if mode == "nt" else b.shape[1])
    tm, tn, tk = min(tm, M), min(tn, N), min(tk, K)
    assert M % tm == 0 and N % tn == 0 and K % tk == 0, (name, M, N, K, tm, tn, tk)
    nm, nn, nk = M // tm, N // tn, K // tk
    assert n_parts == 1 or (nk == 1 and a_pre is None and mode in ("nn", "tn") and cols % tm == 0), name
    blocks_per_part = cols // tm if mode == "tn" else 1
    n_out, n_ext = len(out_dtypes), len(extras)
    contract = {"nn": ((1,), (0,)), "nt": ((1,), (1,)), "tn": ((0,), (0,))}[mode]

    def body(*refs):
        a_refs, b_ref = refs[:n_parts], refs[n_parts]
        ext_refs = refs[n_parts + 1:n_parts + 1 + n_ext]
        first_out = n_parts + 1 + n_ext + n_after
        out_refs = refs[first_out:first_out + n_out]
        acc_ref = refs[first_out + n_out] if nk > 1 else None

        def product(a_ref, bv):
            av = a_ref[...] if a_pre is None else a_pre(a_ref[...])
            if precision is None:
                av, bv = av.astype(BF16), bv.astype(BF16)
            return lax.dot_general(av, bv, (contract, ((), ())), preferred_element_type=F32, precision=precision)

        def finish(acc):
            vals = (acc,) if epilogue is None else epilogue(acc, *[r[...] for r in ext_refs])
            for r, v in zip(out_refs, vals):
                r[...] = v.astype(r.dtype)

        if n_parts > 1 and mode == "tn":
            i = pl.program_id(1 if n_outer else 0)
            for p in range(n_parts):
                @pl.when(i // blocks_per_part == p)
                def _(p=p):
                    finish(product(a_refs[p], b_ref[...]))
            return
        part = product(a_refs[0], b_ref[...] if n_parts == 1 else b_ref[0:cols, :])
        for p in range(1, n_parts):
            part = part + product(a_refs[p], b_ref[p * cols:(p + 1) * cols, :])

        if nk == 1:
            finish(part)
        else:
            k = pl.program_id(2)

            @pl.when(k == 0)
            def _():
                acc_ref[...] = part

            @pl.when(k > 0)
            def _():
                acc_ref[...] += part

            @pl.when(k == nk - 1)
            def _():
                finish(acc_ref[...])

    def at(index):
        return (lambda j, i, k: index(i, j, k)) if n_outer else index

    if n_parts == 1:
        a_specs = [pl.BlockSpec((tk, tm), at(lambda i, j, k: (k, i))) if mode == "tn"
                   else pl.BlockSpec((tm, tk), at(lambda i, j, k: (i, k)))]
    elif mode == "tn":
        a_specs = [pl.BlockSpec((tk, tm), at(lambda i, j, k, p=p: (k, jnp.clip(i - p * blocks_per_part, 0, blocks_per_part - 1))))
                   for p in range(n_parts)]
    else:
        a_specs = [pl.BlockSpec((tm, cols), at(lambda i, j, k: (i, 0))) for _ in parts]
    b_spec = (pl.BlockSpec((tn, tk), at(lambda i, j, k: (j, k))) if mode == "nt"
              else pl.BlockSpec((tk, tn), at(lambda i, j, k: (k + b_first_block, j))))
    in_specs = a_specs + [b_spec]
    block_bytes = _nbytes((tm, tk), parts[0].dtype) * (n_parts if mode == "tn" else 1) + _nbytes((tk, tn), b.dtype)
    for arr, kind in extras:
        if kind == "tile":
            assert arr.shape == (M, N), (name, arr.shape)
            in_specs.append(pl.BlockSpec((tm, tn), at(lambda i, j, k: (i, j))))
            block_bytes += _nbytes((tm, tn), arr.dtype)
        else:
            assert arr.shape == (1, N), (name, arr.shape)
            in_specs.append(pl.BlockSpec((1, tn), at(lambda i, j, k: (0, j))))
    in_specs += [ANY] * n_after
    if out_shards:
        assert n_out == 1 and tn * NDEV == N
        out_shape = [jax.ShapeDtypeStruct((NDEV, M, tn), out_dtypes[0])]
        out_specs = [pl.BlockSpec((None, tm, tn), at(lambda i, j, k: (j, i, 0)))]
    else:
        out_shape = [jax.ShapeDtypeStruct((M, N), d) for d in out_dtypes]
        out_specs = [pl.BlockSpec((tm, tn), at(lambda i, j, k: (i, j))) for _ in out_dtypes]
    block_bytes += sum(_nbytes((tm, tn), d) for d in out_dtypes)
    scratch = [pltpu.VMEM((tm, tn), F32)] if nk > 1 else []
    outs = pl.pallas_call(
        body, name=name, grid=(nn, nm, nk) if n_outer else (nm, nn, nk), in_specs=in_specs, out_specs=out_specs,
        out_shape=out_shape, scratch_shapes=scratch,
        compiler_params=_params(("parallel", "parallel", "arbitrary"), block_bytes, 2 * tm * tn * 4),
    )(*parts, b, *[arr for arr, _ in extras], *after)
    return outs[0] if n_out == 1 else outs


def _rowwise(fn, tiled, smalls, out_tiles, out_sums, *, name, ts=256):
    S = tiled[0].shape[0]
    ts = min(ts, S)
    assert S % ts == 0
    nt, ns, no, na = len(tiled), len(smalls), len(out_tiles), len(out_sums)

    def body(*refs):
        t_refs, s_refs = refs[:nt], refs[nt:nt + ns]
        o_refs, a_refs = refs[nt + ns:nt + ns + no], refs[nt + ns + no:]
        tile_vals, sum_vals = fn([r[...] for r in t_refs], [r[...] for r in s_refs])
        for r, v in zip(o_refs, tile_vals):
            r[...] = v.astype(r.dtype)

        @pl.when(pl.program_id(0) == 0)
        def _():
            for r in a_refs:
                r[...] = jnp.zeros_like(r)

        for r, v in zip(a_refs, sum_vals):
            r[...] += v

    in_specs = [pl.BlockSpec((ts, t.shape[1]), lambda i: (i, 0)) for t in tiled]
    in_specs += [pl.BlockSpec(s.shape, lambda i: (0, 0)) for s in smalls]
    out_specs = [pl.BlockSpec((ts, w), lambda i: (i, 0)) for w, _ in out_tiles]
    out_specs += [pl.BlockSpec((1, w), lambda i: (0, 0)) for w in out_sums]
    out_shape = [jax.ShapeDtypeStruct((S, w), d) for w, d in out_tiles]
    out_shape += [jax.ShapeDtypeStruct((1, w), F32) for w in out_sums]
    block_bytes = sum(_nbytes((ts, t.shape[1]), t.dtype) for t in tiled) + sum(_nbytes((ts, w), d) for w, d in out_tiles)
    width = max(t.shape[1] for t in tiled)
    outs = pl.pallas_call(
        body, name=name, grid=(S // ts,), in_specs=in_specs, out_specs=out_specs, out_shape=out_shape,
        compiler_params=_params(("arbitrary",), block_bytes, 6 * ts * width * 4),
    )(*tiled, *smalls)
    return outs[:no], outs[no:]


def _colsum(v):
    return jnp.sum(v, axis=0, keepdims=True)


def _rms_mod_fwd(x, gain, shift, scale, name):
    def fn(tiles, smalls):
        (xv,), (g, sh, sc) = tiles, smalls
        inv = lax.rsqrt(jnp.mean(xv * xv, axis=-1, keepdims=True) + RMS_EPS)
        h = (xv * inv) * g * (1.0 + sc) + sh
        return (h, inv), ()

    D = x.shape[1]
    (h, inv), _ = _rowwise(fn, [x], [gain, shift, scale], [(D, BF16), (1, F32)], [], name=name)
    return h, inv


def _gated(dxv, following):
    yv, gate = following
    return dxv * gate, _colsum(dxv * yv)


def _rms_mod_bwd(dh, x, inv, dx_res, gain, scale, name, following=None):
    def fn(tiles, smalls):
        dhv, xv, iv, dres = tiles[:4]
        g, sc = smalls[:2]
        dhv = dhv.astype(F32)
        xhat = xv * iv
        dr = dhv * (1.0 + sc)
        dxhat = dr * g
        dxv = dres + iv * (dxhat - xhat * jnp.mean(dxhat * xhat, axis=-1, keepdims=True))
        sums = (_colsum(dhv), _colsum(dhv * (xhat * g)), _colsum(dr * xhat))
        if following is None:
            return (dxv,), sums
        dy, dgate = _gated(dxv, (tiles[4], smalls[2]))
        return (dxv, dy), (*sums, dgate)

    D = x.shape[1]
    extra = [] if following is None else [following]
    tiles, sums = _rowwise(fn, [dh, x, inv, dx_res] + [f[0] for f in extra], [gain, scale] + [f[1] for f in extra],
                           [(D, F32)] + [(D, BF16)] * len(extra), [D] * (3 + len(extra)), name=name)
    return (tiles[0], *sums[:3]) if following is None else (tiles[0], *sums[:3], tiles[1], sums[3])


def _final_loss_bwd(x, target, gain, following, name):
    D = x.shape[1]

    def fn(tiles, smalls):
        xv, tv, g = tiles[0], tiles[1], smalls[0]
        inv = lax.rsqrt(jnp.mean(xv * xv, axis=-1, keepdims=True) + RMS_EPS)
        xhat = xv * inv
        err = xhat * g - tv
        loss = 0.5 * jnp.sum(jnp.mean(err * err, axis=-1, keepdims=True), axis=0, keepdims=True)
        dout = err * (1.0 / D)
        dxhat = dout * g
        dxv = inv * (dxhat - xhat * jnp.mean(dxhat * xhat, axis=-1, keepdims=True))
        dy, dgate = _gated(dxv, (tiles[2], smalls[1]))
        return (dxv, dy), (_colsum(dout * xhat), jnp.broadcast_to(loss, (1, LANES)), dgate)

    (dx, dy), (dgain, loss, dgate) = _rowwise(fn, [x, target, following[0]], [gain, following[1]],
                                             [(D, F32), (D, BF16)], [D, LANES, D], name=name)
    return dx, dgain, loss, dy, dgate


SCAN_BLOCK = 256


def _triangle(n, lower):
    r = lax.broadcasted_iota(jnp.int32, (n, n), 0)
    c = lax.broadcasted_iota(jnp.int32, (n, n), 1)
    return (r >= c if lower else r <= c).astype(F32)


def _forget_cumsum(logits, bias, name):
    S = logits.shape[0]
    blk = min(SCAN_BLOCK, S)
    nb = S // blk

    def body(z_ref, b_ref, f_ref):
        z = z_ref[...] + b_ref[...]
        f_ref[...] = jnp.minimum(z, 0.0) - jnp.log(1.0 + jnp.exp(-jnp.abs(z)))
        tri = _triangle(blk, lower=True)

        def step(i, carry):
            off = pl.multiple_of(i * blk, blk)
            cs = jnp.dot(tri, f_ref[pl.ds(off, blk), :], preferred_element_type=F32, precision=HIGHEST) + carry
            f_ref[pl.ds(off, blk), :] = cs
            return cs[blk - 1:blk, :]

        lax.fori_loop(0, nb, step, jnp.zeros((1, LANES), F32))

    return pl.pallas_call(body, name=name, out_shape=jax.ShapeDtypeStruct((S, LANES), F32))(logits, bias)


def _forget_bwd(dfk, logits, bias, name):
    S = logits.shape[0]
    blk = min(SCAN_BLOCK, S)
    nb = S // blk

    def body(d_ref, z_ref, b_ref, o_ref, db_ref):
        tri = _triangle(blk, lower=False)

        def step(t, carry):
            off = pl.multiple_of((nb - 1 - t) * blk, blk)
            cs = jnp.dot(tri, d_ref[pl.ds(off, blk), :], preferred_element_type=F32, precision=HIGHEST) + carry
            o_ref[pl.ds(off, blk), :] = cs
            return cs[0:1, :]

        lax.fori_loop(0, nb, step, jnp.zeros((1, LANES), F32))
        z = z_ref[...] + b_ref[...]
        dz = -o_ref[...] / (1.0 + jnp.exp(z))
        o_ref[...] = dz
        db_ref[...] = _colsum(dz)

    return pl.pallas_call(
        body, name=name,
        out_shape=(jax.ShapeDtypeStruct((S, LANES), F32), jax.ShapeDtypeStruct((1, LANES), F32)),
    )(dfk, logits, bias)


KEY_SCALE = HEAD_DIM ** -0.5
ATTN_BLOCK = 512
_NT = (((1,), (1,)), ((), ()))


def _attn_specs(S, H, tb):
    q_blk = lambda part: pl.BlockSpec((tb, HEAD_DIM), lambda h, i: (i, part * H + h))
    q_all = lambda part: pl.BlockSpec((S, HEAD_DIM), lambda h, i: (0, part * H + h))
    col_blk = pl.BlockSpec((None, tb, 1), lambda h, i: (h, i, 0))
    row_all = pl.BlockSpec((None, 1, S), lambda h, i: (h, 0, 0))
    return q_blk, q_all, col_blk, row_all


FWD_HEADS = 2


def _attn_fwd(qkv, f_col, f_row, name):
    S, H = qkv.shape[0], qkv.shape[1] // (3 * HEAD_DIM)
    tb = min(ATTN_BLOCK, S)
    hp = FWD_HEADS if H % FWD_HEADS == 0 else 1
    groups, wide = H // hp, hp * HEAD_DIM
    lanes = lambda u: pl.ds(u * HEAD_DIM, HEAD_DIM)

    def body(q_ref, k_ref, v_ref, fc_ref, fr_ref, o_ref, lse_ref):
        i = pl.program_id(1)

        def step(j, carry, diagonal):
            off = pl.multiple_of(j * tb, tb)
            out = []
            for u in range(hp):
                m, l, acc = carry[u]
                k, v = k_ref[pl.ds(off, tb), lanes(u)], v_ref[pl.ds(off, tb), lanes(u)]
                s = lax.dot_general(q_ref[:, lanes(u)], k, _NT, preferred_element_type=F32)
                s = s + (fc_ref[u] - fr_ref[u, :, pl.ds(off, tb)])
                if diagonal:
                    row = lax.broadcasted_iota(jnp.int32, (tb, tb), 0)
                    col = lax.broadcasted_iota(jnp.int32, (tb, tb), 1)
                    s = jnp.where(col <= row, s, NEG)
                m_new = jnp.maximum(m, jnp.max(s, axis=-1, keepdims=True))
                p = jnp.exp(s - m_new)
                alpha = jnp.exp(m - m_new)
                l = alpha * l + jnp.sum(p, axis=-1, keepdims=True)
                acc = alpha * acc + jnp.dot(p.astype(BF16), v, preferred_element_type=F32)
                out.append((m_new, l, acc))
            return tuple(out)

        init = (jnp.full((tb, 1), NEG, F32), jnp.zeros((tb, 1), F32), jnp.zeros((tb, HEAD_DIM), F32))
        carry = lax.fori_loop(0, i, lambda j, c: step(j, c, False), (init,) * hp)
        for u, (m, l, acc) in enumerate(step(i, carry, True)):
            o_ref[:, lanes(u)] = (acc / l).astype(o_ref.dtype)
            lse_ref[u] = m + jnp.log(l)

    part = lambda p, rows: pl.BlockSpec((rows, wide), lambda g, i: (i if rows == tb else 0, p * groups + g))
    col_blk = pl.BlockSpec((hp, tb, 1), lambda g, i: (g, i, 0))
    return pl.pallas_call(
        body, name=name, grid=(groups, S // tb),
        in_specs=[part(0, tb), part(1, S), part(2, S), col_blk, pl.BlockSpec((hp, 1, S), lambda g, i: (g, 0, 0))],
        out_specs=[pl.BlockSpec((tb, wide), lambda g, i: (i, g)), col_blk],
        out_shape=[jax.ShapeDtypeStruct((S, H * HEAD_DIM), BF16), jax.ShapeDtypeStruct((H, S, 1), F32)],
        compiler_params=_params(("parallel", "parallel"), 4 * S * wide * 2, 10 * hp * tb * tb * 4),
    )(qkv, qkv, qkv, f_col, f_row)


_TN = (((0,), (0,)), ((), ()))


def _attn_bwd(qkv, do, f_col, f_row, lse_col, name):
    S, H = qkv.shape[0], qkv.shape[1] // (3 * HEAD_DIM)
    tb = min(ATTN_BLOCK, S)
    nq = S // tb
    q_blk, q_all, col_blk, row_all = _attn_specs(S, H, tb)
    head_blk = pl.BlockSpec((tb, HEAD_DIM), lambda h, i: (i, h))
    head_all = pl.BlockSpec((S, HEAD_DIM), lambda h, i: (0, h))

    def body(q_ref, k_ref, v_ref, do_ref, fc_ref, fr_ref, lse_ref, dq_ref, dk_ref, dv_ref, dfk_ref,
             p_buf, dp_buf, dk_acc, dv_acc, dfk_acc):
        i = pl.program_id(1)
        q, do = q_ref[...], do_ref[...]
        fc_lse = fc_ref[...] - lse_ref[...]

        @pl.when(i == 0)
        def _():
            dk_acc[...] = jnp.zeros_like(dk_acc)
            dv_acc[...] = jnp.zeros_like(dv_acc)
            dfk_acc[...] = jnp.zeros_like(dfk_acc)

        def scores(j, delta, diagonal):
            off = pl.multiple_of(j * tb, tb)
            k, v = k_ref[pl.ds(off, tb), :], v_ref[pl.ds(off, tb), :]
            s = (lax.dot_general(q, k, _NT, preferred_element_type=F32) + fc_lse) - fr_ref[:, pl.ds(off, tb)]
            if diagonal:
                row = lax.broadcasted_iota(jnp.int32, (tb, tb), 0)
                col = lax.broadcasted_iota(jnp.int32, (tb, tb), 1)
                s = jnp.where(col <= row, s, NEG)
            p = jnp.exp(s)
            dp = lax.dot_general(do, v, _NT, preferred_element_type=F32)
            p_buf[j] = p
            dp_buf[j] = dp
            return delta + jnp.sum(p * dp, axis=-1, keepdims=True)

        delta = lax.fori_loop(0, i, lambda j, c: scores(j, c, False), jnp.zeros((tb, 1), F32))
        delta = scores(i, delta, True)

        def grad(j, dq):
            off = pl.multiple_of(j * tb, tb)
            p = p_buf[j]
            ds = p * (dp_buf[j] - delta)
            ds_lo = ds.astype(BF16)
            dk_acc[pl.ds(off, tb), :] += lax.dot_general(ds_lo, q, _TN, preferred_element_type=F32)
            dv_acc[pl.ds(off, tb), :] += lax.dot_general(p.astype(BF16), do, _TN, preferred_element_type=F32)
            dfk_acc[:, pl.ds(off, tb)] += jnp.sum(ds, axis=0, keepdims=True)
            return dq + jnp.dot(ds_lo, k_ref[pl.ds(off, tb), :], preferred_element_type=F32)

        dq = lax.fori_loop(0, i + 1, grad, jnp.zeros((tb, HEAD_DIM), F32))
        dq_ref[...] = dq.astype(dq_ref.dtype)

        @pl.when(i == nq - 1)
        def _():
            dk_ref[...] = (dk_acc[...] * KEY_SCALE).astype(dk_ref.dtype)
            dv_ref[...] = dv_acc[...].astype(dv_ref.dtype)
            dfk_ref[...] = dfk_acc[...]

    wide = jax.ShapeDtypeStruct((S, H * HEAD_DIM), BF16)
    return pl.pallas_call(
        body, name=name, grid=(H, nq),
        in_specs=[q_blk(0), q_all(1), q_all(2), head_blk, col_blk, row_all, col_blk],
        out_specs=[head_blk, head_all, head_all, row_all],
        out_shape=[wide, wide, wide, jax.ShapeDtypeStruct((H, 1, S), F32)],
        scratch_shapes=[pltpu.VMEM((nq, tb, tb), F32), pltpu.VMEM((nq, tb, tb), F32),
                        pltpu.VMEM((S, HEAD_DIM), F32), pltpu.VMEM((S, HEAD_DIM), F32), pltpu.VMEM((1, S), F32)],
        compiler_params=_params(("parallel", "arbitrary"), 6 * S * HEAD_DIM * 2,
                                2 * nq * tb * tb * 4 + 2 * S * HEAD_DIM * 4 + 10 * tb * tb * 4),
    )(qkv, qkv, qkv, do, f_col, f_row, lse_col)


CONV_TILE = 128


def _shift_down(v, n):
    row = lax.broadcasted_iota(jnp.int32, v.shape, 0)
    return jnp.where(row >= n, pltpu.roll(v, n, 0), 0.0)


def _shift_up(v, n):
    S = v.shape[0]
    row = lax.broadcasted_iota(jnp.int32, v.shape, 0)
    return jnp.where(row < S - n, pltpu.roll(v, S - n, 0), 0.0)


def _conv_specs(S, D, tc):
    nb = D // tc
    part = lambda p: pl.BlockSpec((S, tc), lambda j: (0, p * nb + j))
    return part, pl.BlockSpec((S, tc), lambda j: (0, j)), pl.BlockSpec((8, tc), lambda j: (0, j))


def _conv_fwd(proj, conv_w8, name):
    S, D = proj.shape[0], proj.shape[1] // 3
    tc = min(CONV_TILE, D)
    part, chan, taps = _conv_specs(S, D, tc)

    def body(b_ref, c_ref, u_ref, w_ref, z_ref):
        cu = c_ref[...].astype(F32) * u_ref[...].astype(F32)
        w = w_ref[...]
        y = w[0:1, :] * _shift_down(cu, 2) + w[1:2, :] * _shift_down(cu, 1) + w[2:3, :] * cu
        z_ref[...] = (b_ref[...].astype(F32) * y).astype(z_ref.dtype)

    return pl.pallas_call(
        body, name=name, grid=(D // tc,), in_specs=[part(0), part(1), part(2), taps], out_specs=chan,
        out_shape=jax.ShapeDtypeStruct((S, D), BF16),
        compiler_params=_params(("parallel",), 3 * _nbytes((S, tc), proj.dtype) + S * tc * 2, 6 * S * tc * 4),
    )(proj, proj, proj, conv_w8)


def _conv_bwd(proj, dz, conv_w8, name):
    S, D = proj.shape[0], proj.shape[1] // 3
    tc = min(CONV_TILE, D)
    part, chan, taps = _conv_specs(S, D, tc)

    def body(b_ref, c_ref, u_ref, dz_ref, w_ref, db_ref, dc_ref, du_ref, dw_ref):
        cv, uv = c_ref[...].astype(F32), u_ref[...].astype(F32)
        dzv, w = dz_ref[...].astype(F32), w_ref[...]
        cu = cv * uv
        cu1, cu2 = _shift_down(cu, 1), _shift_down(cu, 2)
        y = w[0:1, :] * cu2 + w[1:2, :] * cu1 + w[2:3, :] * cu
        db_ref[...] = (dzv * y).astype(db_ref.dtype)
        dy = dzv * b_ref[...].astype(F32)
        dcu = w[2:3, :] * dy + w[1:2, :] * _shift_up(dy, 1) + w[0:1, :] * _shift_up(dy, 2)
        dc_ref[...] = (dcu * uv).astype(dc_ref.dtype)
        du_ref[...] = (dcu * cv).astype(du_ref.dtype)
        dw_ref[...] = jnp.concatenate(
            [_colsum(dy * cu2), _colsum(dy * cu1), _colsum(dy * cu), jnp.zeros((8 - CONV_WIDTH, tc), F32)], axis=0)

    return pl.pallas_call(
        body, name=name, grid=(D // tc,), in_specs=[part(0), part(1), part(2), chan, taps],
        out_specs=[chan, chan, chan, taps],
        out_shape=[jax.ShapeDtypeStruct((S, D), BF16)] * 3 + [jax.ShapeDtypeStruct((8, D), F32)],
        compiler_params=_params(("parallel",), 3 * _nbytes((S, tc), proj.dtype) + _nbytes((S, tc), dz.dtype)
                                + 3 * S * tc * 2, 10 * S * tc * 4),
    )(proj, proj, proj, dz, conv_w8)


def _adamw(w, m, v, parts, name, layer=0, prev=None):
    L, R, C = w.shape
    P = parts.shape[0]
    assert parts.shape[1:] == (R, C), (name, parts.shape, w.shape)
    elem_bytes = 12 + 16 + P * parts.dtype.itemsize
    budget = 8 << 20
    tr, tc = R, C
    if R * C * elem_bytes > budget:
        if R % 8 == 0:
            tr = max(8, (budget // (C * elem_bytes)) // 8 * 8)
            while R % tr:
                tr -= 8
        else:
            tc = LANES
            while C % (2 * tc) == 0 and R * 2 * tc * elem_bytes <= budget:
                tc *= 2
            assert C % tc == 0, (name, R, C)
    c1, c2 = 1.0 - ADAM_B1 ** ADAM_STEP, 1.0 - ADAM_B2 ** ADAM_STEP

    def body(w_ref, m_ref, v_ref, p_ref, *rest):
        g_ref, d_ref, nm_ref, nv_ref = rest[-4:]
        g = p_ref[0].astype(F32)
        for p in range(1, P):
            g = g + p_ref[p].astype(F32)
        nm = ADAM_B1 * m_ref[...] + (1.0 - ADAM_B1) * g
        nv = ADAM_B2 * v_ref[...] + (1.0 - ADAM_B2) * (g * g)
        g_ref[...] = g
        nm_ref[...] = nm
        nv_ref[...] = nv
        d_ref[...] = -ADAM_LR * ((nm / c1) / (jnp.sqrt(nv / c2) + ADAM_EPS) + ADAM_WD * w_ref[...])

    blk = pl.BlockSpec((None, tr, tc), lambda i, j: (layer, i, j))
    prev = [] if prev is None else list(prev)
    return pl.pallas_call(
        body, name=name, grid=(R // tr, C // tc),
        in_specs=[blk, blk, blk, pl.BlockSpec((P, tr, tc), lambda i, j: (0, i, j))] + [ANY] * len(prev),
        out_specs=[blk] * 4, out_shape=[jax.ShapeDtypeStruct((L, R, C), F32)] * 4,
        input_output_aliases={4 + k: k for k in range(len(prev))},
        compiler_params=_params(("parallel", "parallel"), tr * tc * elem_bytes),
    )(w, m, v, parts, *prev)


def _silu(v):
    return v / (1.0 + jnp.exp(-v))


def _pad_rows(a, rows):
    return jnp.pad(a, ((0, rows - a.shape[0]), (0, 0)))


def _pad_cols(a, cols):
    return jnp.pad(a, ((0, 0), (0, cols - a.shape[1])))


def kernel(x, c, ada_w, ada_b, norm_mix, norm_mlp, fox_w_in, fox_b_f, fox_w_out, conv_w_in, conv_w, conv_w_out, mlp_w_up, mlp_w_down, final_norm, loss_target, m_ada_w, m_ada_b, m_norm_mix, m_norm_mlp, m_fox_w_in, m_fox_b_f, m_fox_w_out, m_conv_w_in, m_conv_w, m_conv_w_out, m_mlp_w_up, m_mlp_w_down, m_final_norm, v_ada_w, v_ada_b, v_norm_mix, v_norm_mlp, v_fox_w_in, v_fox_b_f, v_fox_w_out, v_conv_w_in, v_conv_w, v_conv_w_out, v_mlp_w_up, v_mlp_w_down, v_final_norm):
    S, D = x.shape[1], x.shape[2]
    H = D // HEAD_DIM
    FF = mlp_w_up.shape[2] * NDEV
    depth = ada_w.shape[0]
    n_mod = 6
    assert depth == 2 and fox_w_in.shape[0] == 1 and conv_w_in.shape[0] == 1 and H <= LANES
    me = _my_index()
    x0, target = x[0], loss_target[0]
    row = lambda vec: vec.reshape(1, -1)

    def tied(vec, token):
        return vec + token[0, 0]

    bf = lambda w: w.astype(BF16)
    gather_groups = {
        "fox": [bf(fox_w_in[0]).T, bf(fox_w_out[0])],
        "mlp0": [bf(mlp_w_up[0]).T, bf(mlp_w_down[0])],
        "conv": [bf(conv_w_in[0]).T, conv_w[0], bf(conv_w_out[0])],
        "mlp1": [bf(mlp_w_up[1]).T, bf(mlp_w_down[1])],
    }

    def start_gather(group, after):
        return _exchange_start(gather_groups[group], f"gather_{group}_start", False, after, relay=True)

    def relay_gather(handle, group, after):
        handle = _exchange_wait(handle, f"gather_{group}_arrivals", after, arrivals_only=True)
        return _relay_forward_start(handle, f"gather_{group}_forward")

    def finish_gather(handle, group, after):
        return _exchange_wait(handle, f"gather_{group}_wait", after)

    landed = lambda handle: handle[4][0]

    c_all = _all_gather([c], "gather_cond")[0].reshape(NDEV, D)
    ncol = ada_w.shape[2]
    ada_b_mine = lax.dynamic_slice_in_dim(ada_b, me * ncol, ncol, axis=1)
    mod_cols = jnp.stack([
        _matmul(c_all, ada_w.reshape(depth * D, ncol), mode="nn", name=f"ada_fwd_{i}", out_dtypes=[F32], tm=NDEV,
                tn=ncol // 2, tk=D, b_first_block=i,
                a_pre=_silu, precision=HIGHEST, epilogue=lambda acc, b: (acc + b,), extras=[(ada_b_mine[i:i + 1], "row")])
        for i in range(depth)])
    mod_all = _all_gather([mod_cols], "gather_mod")[0]
    mod = lax.dynamic_index_in_dim(mod_all, me, axis=2, keepdims=False)
    fox_handle, token = start_gather("fox", mod_all)
    mod = tied(mod, token).transpose(1, 0, 2).reshape(depth, n_mod, 1, D)
    sh_mix, sc_mix, g_mix, sh_mlp, sc_mlp, g_mlp = (mod[:, k] for k in range(n_mod))
    b_f = _pad_cols(fox_b_f, LANES)

    def residual(acc, x_in, gate):
        return (x_in + gate * acc, acc)

    def mlp_fwd(i, x_in, handle, relay_next=None):
        h, inv = _rms_mod_fwd(x_in, row(norm_mlp[i]), sh_mlp[i], sc_mlp[i], f"mlp_norm_{i}")
        w_up_t, w_down = finish_gather(handle, f"mlp{i}", h)
        w_up_t, w_down = w_up_t.reshape(FF, D), w_down.reshape(FF, D)
        r = _matmul(h, w_up_t, mode="nt", name=f"mlp_up_{i}", out_dtypes=[BF16], tm=1024, tn=1024, tk=D,
                    epilogue=lambda acc: (jnp.maximum(acc, 0.0),))
        next_handle = relay_gather(relay_next[1], relay_next[0], r) if relay_next else None
        x_out, y = _matmul(r, w_down, mode="nn", name=f"mlp_down_{i}", out_dtypes=[F32, BF16], tm=512, tn=512, tk=FF,
                           n_outer=True, a_pre=jnp.square, epilogue=residual, extras=[(x_in, "tile"), (g_mlp[i], "row")],
                           after=[landed(next_handle)] if relay_next else [])
        return x_out, (x_in, h, inv, r, y, w_up_t, w_down), next_handle

    def mlp_bwd(i, dx, dy, dgate, saved, following, after):
        x_in, h, inv, r, y, w_up_t, w_down = saved
        du = _matmul(dy, w_down, mode="nt", name=f"mlp_down_bwd_{i}", out_dtypes=[BF16], tm=1024, tn=1024, tk=D,
                     epilogue=lambda acc, rv: (acc * (2.0 * rv.astype(F32)),), extras=[(r, "tile")], after=after)
        d_down = _matmul(r, dy, mode="tn", name=f"mlp_down_wgrad_{i}", out_dtypes=[BF16], tm=512, tn=1024, tk=S,
                         a_pre=jnp.square)
        dh = _matmul(du, w_up_t, mode="nn", name=f"mlp_up_bwd_{i}", out_dtypes=[F32], tm=512, tn=512, tk=FF, n_outer=True)
        d_up = _matmul(h, du, mode="tn", name=f"mlp_up_wgrad_{i}", out_dtypes=[BF16], tm=512, tn=FF // NDEV, tk=S,
                       out_shards=True)
        dx, dsh, dsc, dgain, dy_next, dgate_next = _rms_mod_bwd(dh, x_in, inv, dx, row(norm_mlp[i]), sc_mlp[i],
                                                                f"mlp_norm_bwd_{i}", following)
        handle, token = _exchange_start([d_up, d_down.reshape(NDEV, FF // NDEV, D)], f"scatter_mlp{i}_start", True, dx)
        return dx, (dsh, dsc, dgate, dgain), handle, token, dy_next, dgate_next

    h0, inv0 = _rms_mod_fwd(x0, row(norm_mix[0]), sh_mix[0], sc_mix[0], "fox_norm")
    w_in_t, w_fox_out = finish_gather(relay_gather(fox_handle, "fox", h0), "fox", h0)
    mlp0_handle, token = start_gather("mlp0", w_in_t)
    w_in_t = w_in_t.reshape(3 * D + H, D)
    w_f_t = _pad_rows(w_in_t[3 * D:], LANES)
    w_fox_out = w_fox_out.reshape(D, D)
    column_scale = jnp.concatenate([jnp.ones((1, D), F32), jnp.full((1, D), KEY_SCALE, F32), jnp.ones((1, D), F32)], axis=1)
    qkv = _matmul(h0, w_in_t, mode="nt", name="fox_qkv", out_dtypes=[BF16], tm=1024, tn=1024, tk=D, n=3 * D, after=[token],
                  epilogue=lambda acc, mult: (acc * mult,), extras=[(column_scale, "row")])
    f_logit = _matmul(h0, w_f_t, mode="nt", name="fox_forget_logits", out_dtypes=[F32], tm=1024, tn=LANES, tk=D)
    f_cum = _forget_cumsum(f_logit, b_f, "fox_forget_cumsum")
    f_heads = f_cum[:, :H].T
    f_col, f_row = f_heads.reshape(H, S, 1), f_heads.reshape(H, 1, S)
    o, lse = _attn_fwd(qkv, f_col, f_row, "fox_attention")
    mlp0_handle = relay_gather(mlp0_handle, "mlp0", o)
    conv_handle, token = start_gather("conv", landed(mlp0_handle))
    mlp1_handle, token = start_gather("mlp1", token)
    x1, mix0 = _matmul(o, w_fox_out, mode="nn", name="fox_out", out_dtypes=[F32, BF16], tm=512, tn=1024, tk=D,
                       epilogue=residual, extras=[(x0, "tile"), (g_mix[0], "row")], after=[token])
    x2, mlp0, conv_handle = mlp_fwd(0, x1, mlp0_handle, ("conv", conv_handle))

    h1, inv1 = _rms_mod_fwd(x2, row(norm_mix[1]), sh_mix[1], sc_mix[1], "conv_norm")
    w_conv_in_t, w_taps, w_conv_out = finish_gather(conv_handle, "conv", h1)
    w_conv_in_t = w_conv_in_t.reshape(3 * D, D)
    w_taps = _pad_rows(w_taps.transpose(1, 0, 2).reshape(CONV_WIDTH, D), 8)
    w_conv_out = w_conv_out.reshape(D, D)
    proj = _matmul(h1, w_conv_in_t, mode="nt", name="conv_in", out_dtypes=[BF16], tm=1024, tn=1024, tk=D)
    mlp1_handle = relay_gather(mlp1_handle, "mlp1", proj)
    z = _conv_fwd(proj, w_taps, "conv_mix")
    x3, mix1 = _matmul(z, w_conv_out, mode="nn", name="conv_out", out_dtypes=[F32, BF16], tm=512, tn=1024, tk=D,
                       epilogue=residual, extras=[(x2, "tile"), (g_mix[1], "row")], after=[landed(mlp1_handle)])
    x4, mlp1, _ = mlp_fwd(1, x3, mlp1_handle)

    dx, d_final, loss_lanes, dy, dgate = _final_loss_bwd(x4, target, row(final_norm), (mlp1[4], g_mlp[1]), "loss_head")

    dx, dmod_mlp1, mlp1_scatter, token, dmix, dg_mix1 = mlp_bwd(1, dx, dy, dgate, mlp1, (mix1, g_mix[1]), [])
    dz = _matmul(dmix, w_conv_out, mode="nt", name="conv_out_bwd", out_dtypes=[BF16], tm=1024, tn=1024, tk=D,
                 after=[token])
    d_conv_out = _matmul(z, dmix, mode="tn", name="conv_out_wgrad", out_dtypes=[BF16], tm=512, tn=1024, tk=S)
    db, dc, du, d_taps = _conv_bwd(proj, dz, w_taps, "conv_mix_bwd")
    dproj = jnp.concatenate([db, dc, du], axis=1)
    dh1 = _matmul(dproj, w_conv_in_t, mode="nn", name="conv_in_bwd", out_dtypes=[F32], tm=512, tn=512, tk=3 * D, n_outer=True)
    d_conv_in = _matmul(h1, dproj, mode="tn", name="conv_in_wgrad", out_dtypes=[BF16], tm=512, tn=3 * D // NDEV, tk=S,
                        out_shards=True)
    dx, dsh1, dsc1, dgain_mix1, dy, dgate = _rms_mod_bwd(dh1, x2, inv1, dx, row(norm_mix[1]), sc_mix[1], "conv_norm_bwd",
                                                         (mlp0[4], g_mlp[0]))
    d_taps_split = d_taps[:CONV_WIDTH].reshape(CONV_WIDTH, NDEV, -1).transpose(1, 0, 2)
    conv_scatter, token = _exchange_start([d_conv_in, d_taps_split, d_conv_out.reshape(NDEV, D // NDEV, D)],
                                          "scatter_conv_start", True, dx)

    dx, dmod_mlp0, mlp0_scatter, token, dmix, dg_mix0 = mlp_bwd(0, dx, dy, dgate, mlp0, (mix0, g_mix[0]), [token])
    do = _matmul(dmix, w_fox_out, mode="nt", name="fox_out_bwd", out_dtypes=[BF16], tm=1024, tn=1024, tk=D,
                 after=[token])
    d_fox_out = _matmul(o, dmix, mode="tn", name="fox_out_wgrad", out_dtypes=[BF16], tm=512, tn=1024, tk=S)
    dq, dk, dv, dfk = _attn_bwd(qkv, do, f_col, f_row, lse, "fox_attention_bwd")
    dqkv = [dq, dk, dv]
    dfk_lanes = _pad_cols(dfk.reshape(H, S).T, LANES)
    df_logit, db_f = _forget_bwd(dfk_lanes, f_logit, b_f, "fox_forget_bwd")
    d_qkv_t = _matmul(dqkv, h0, mode="tn", name="fox_qkv_wgrad", out_dtypes=[BF16], tm=512, tn=512, tk=S)
    d_f_t = _matmul(df_logit, h0, mode="tn", name="fox_forget_wgrad", out_dtypes=[BF16], tm=LANES, tn=1024, tk=S)
    d_fox_in = jnp.concatenate([d_qkv_t, d_f_t[:H]], axis=0).reshape(NDEV, -1, D)
    fox_scatter, token = _exchange_start([d_fox_in, d_fox_out.reshape(NDEV, D // NDEV, D)], "scatter_fox_start", True,
                                         d_fox_in)
    dh0_f = _matmul(df_logit, w_f_t, mode="nn", name="fox_forget_logits_bwd", out_dtypes=[F32], tm=1024, tn=1024, tk=LANES,
                    after=[token])
    dh0 = _matmul(dqkv, w_in_t, mode="nn", name="fox_qkv_bwd", out_dtypes=[F32], tm=512, tn=512, tk=3 * D, n_outer=True,
                  epilogue=lambda acc, extra: (acc + extra,), extras=[(dh0_f, "tile")])
    dx, dsh0, dsc0, dgain_mix0 = _rms_mod_bwd(dh0, x0, inv0, dx, row(norm_mix[0]), sc_mix[0], "fox_norm_bwd")
    grad_x = dx.reshape(1, S, D)

    dmod = jnp.concatenate([
        jnp.concatenate([dsh0, dsc0, dg_mix0, dmod_mlp0[0], dmod_mlp0[1], dmod_mlp0[2]], axis=1),
        jnp.concatenate([dsh1, dsc1, dg_mix1, dmod_mlp1[0], dmod_mlp1[1], dmod_mlp1[2]], axis=1)], axis=0)
    small_sizes = [depth * n_mod * D, depth * D, depth * D, H, D, 1]
    n_small = sum(small_sizes)
    n_rows = -(-n_small // (8 * LANES)) * 8

    def pack(parts):
        flat = jnp.concatenate([p.reshape(-1) for p in parts])
        return jnp.pad(flat, (0, n_rows * LANES - n_small)).reshape(n_rows, LANES)

    def unpack(packed, shapes):
        flat, out, at = packed.reshape(-1), [], 0
        for size, shape in zip(small_sizes, shapes):
            out.append(flat[at:at + size].reshape(shape))
            at += size
        return out

    small_partial = pack([dmod, jnp.concatenate([dgain_mix0, dgain_mix1], axis=0),
                          jnp.concatenate([dmod_mlp0[3], dmod_mlp1[3]], axis=0), db_f[0, :H], d_final, loss_lanes[0, :1]])
    small_handle, token = _exchange_start([small_partial], "gather_small_start", False, dx)

    up1, down1 = _exchange_wait(mlp1_scatter, "scatter_mlp1_wait", token)
    up_out = _adamw(mlp_w_up, m_mlp_w_up, v_mlp_w_up, up1, "adamw_mlp_w_up_1", layer=1)
    down_out = _adamw(mlp_w_down, m_mlp_w_down, v_mlp_w_down, down1, "adamw_mlp_w_down_1", layer=1)
    cin, taps, cout = _exchange_wait(conv_scatter, "scatter_conv_wait", down_out[0])
    conv_in_out = _adamw(conv_w_in, m_conv_w_in, v_conv_w_in, cin, "adamw_conv_w_in")
    conv_w_res = _adamw(conv_w, m_conv_w, v_conv_w, taps, "adamw_conv_w")
    conv_out_out = _adamw(conv_w_out, m_conv_w_out, v_conv_w_out, cout, "adamw_conv_w_out")
    up0, down0 = _exchange_wait(mlp0_scatter, "scatter_mlp0_wait", conv_out_out[0])
    up_out = _adamw(mlp_w_up, m_mlp_w_up, v_mlp_w_up, up0, "adamw_mlp_w_up_0", layer=0, prev=up_out)
    down_out = _adamw(mlp_w_down, m_mlp_w_down, v_mlp_w_down, down0, "adamw_mlp_w_down_0", layer=0, prev=down_out)

    small_parts = _exchange_wait(small_handle, "gather_small_wait", down_out[0])[0]
    small_shapes = [ada_b.shape, norm_mix.shape, norm_mlp.shape, fox_b_f.shape, final_norm.shape]
    loss = jnp.sum(small_parts.reshape(NDEV, -1)[:, n_small - 1])
    unused = jnp.zeros((1,), F32)
    small_out = _adamw(pack([ada_b, norm_mix, norm_mlp, fox_b_f, final_norm, unused])[None],
                       pack([m_ada_b, m_norm_mix, m_norm_mlp, m_fox_b_f, m_final_norm, unused])[None],
                       pack([v_ada_b, v_norm_mix, v_norm_mlp, v_fox_b_f, v_final_norm, unused])[None], small_parts,
                       "adamw_small")
    small_out = [unpack(t, small_shapes) for t in small_out]

    dmod_all = small_parts.reshape(NDEV, -1)[:, :depth * n_mod * D].reshape(NDEV, depth, n_mod * D)
    dmod_mine = lax.dynamic_slice_in_dim(dmod_all, me * ncol, ncol, axis=2)
    ada_out = None
    for i in range(depth):
        d_ada = _matmul(c_all, dmod_mine[:, i], mode="tn", name=f"ada_wgrad_{i}", out_dtypes=[F32], tm=1024, tn=ncol // 2,
                        tk=NDEV, a_pre=_silu, precision=HIGHEST)
        ada_out = _adamw(ada_w, m_ada_w, v_ada_w, d_ada[None], f"adamw_ada_w_{i}", layer=i, prev=ada_out)

    fin, fout = _exchange_wait(fox_scatter, "scatter_fox_wait", ada_out[0])
    swap = lambda t: jnp.swapaxes(t, 1, 2)
    fox_in_out = [swap(t) for t in _adamw(swap(fox_w_in), swap(m_fox_w_in), swap(v_fox_w_in), fin, "adamw_fox_w_in")]
    fox_out_out = _adamw(fox_w_out, m_fox_w_out, v_fox_w_out, fout, "adamw_fox_w_out")

    outputs = [loss, grad_x]
    for kind in range(4):
        sm = small_out[kind]
        outputs += [ada_out[kind], sm[0], sm[1], sm[2], fox_in_out[kind], sm[3], fox_out_out[kind], conv_in_out[kind],
                    conv_w_res[kind], conv_out_out[kind], up_out[kind], down_out[kind], sm[4]]
    return tuple(outputs)
```

```python
import math

import jax
import jax.numpy as jnp
from jax import lax
from jax.experimental import pallas as pl
from jax.experimental.pallas import tpu as pltpu

F32 = jnp.float32
BF16 = jnp.bfloat16
MESH = pl.DeviceIdType.MESH
NDEV = 8
HEAD_DIM = 128
LANES = 128
CONV_WIDTH = 3
RMS_EPS = 1e-6
ADAM_LR, ADAM_B1, ADAM_B2, ADAM_EPS, ADAM_WD, ADAM_STEP = 0.001, 0.9, 0.999, 1e-08, 0.01, 10
NEG = -1e30
V7X_VMEM_BYTES = 64 * 1024 * 1024
VMEM_HEADROOM = 12 * 1024 * 1024
HBM = pl.BlockSpec(memory_space=pltpu.HBM)
HIGHEST = lax.Precision.HIGHEST


def _nbytes(shape, dtype):
    return math.prod(shape) * jnp.dtype(dtype).itemsize


def _params(semantics, block_bytes, temp_bytes=0):
    limit = min(2 * block_bytes + temp_bytes + VMEM_HEADROOM, V7X_VMEM_BYTES - 4 * 1024 * 1024)
    return pltpu.CompilerParams(dimension_semantics=semantics, vmem_limit_bytes=int(limit))


def _my_index():
    return lax.axis_index("x") * 4 + lax.axis_index("y") * 2 + lax.axis_index("c")


def _peer(r):
    x, y, c = lax.axis_index("x"), lax.axis_index("y"), lax.axis_index("c")
    px = 1 - x if (r >> 2) & 1 else x
    py = 1 - y if (r >> 1) & 1 else y
    pc = 1 - c if r & 1 else c
    return (px, py, pc), px * 4 + py * 2 + pc


def _exchange(arrays, name, scatter, after=None):
    n = len(arrays)
    after = [] if after is None else list(after)

    def body(*refs):
        ins, outs = refs[:n], refs[n + len(after):2 * n + len(after)]
        send_sems, recv_sems, local_sems = refs[2 * n + len(after):]
        me = _my_index()
        local = []
        for a in range(n):
            src = ins[a].at[me] if scatter else ins[a]
            local.append(pltpu.make_async_copy(src, outs[a].at[me], local_sems.at[a]))
            local[-1].start()
        sends = []
        for r in range(1, NDEV):
            peer, pidx = _peer(r)
            for a in range(n):
                src = ins[a].at[pidx] if scatter else ins[a]
                cp = pltpu.make_async_remote_copy(
                    src_ref=src, dst_ref=outs[a].at[me],
                    send_sem=send_sems.at[a * (NDEV - 1) + r - 1], recv_sem=recv_sems.at[a * (NDEV - 1) + r - 1],
                    device_id=peer, device_id_type=MESH)
                cp.start()
                sends.append(cp)
        for r in range(1, NDEV):
            peer, pidx = _peer(r)
            for a in range(n):
                src = ins[a].at[pidx] if scatter else ins[a]
                pltpu.make_async_remote_copy(
                    src_ref=src, dst_ref=outs[a].at[pidx],
                    send_sem=send_sems.at[a * (NDEV - 1) + r - 1], recv_sem=recv_sems.at[a * (NDEV - 1) + r - 1],
                    device_id=peer, device_id_type=MESH).wait_recv()
        for cp in sends:
            cp.wait_send()
        for cp in local:
            cp.wait()

    out_shape = [jax.ShapeDtypeStruct(a.shape if scatter else (NDEV,) + a.shape, a.dtype) for a in arrays]
    return pl.pallas_call(
        body, name=name, out_shape=out_shape, in_specs=[HBM] * n + [ANY] * len(after), out_specs=[HBM] * n,
        scratch_shapes=[pltpu.SemaphoreType.DMA((n * (NDEV - 1),)), pltpu.SemaphoreType.DMA((n * (NDEV - 1),)),
                        pltpu.SemaphoreType.DMA((n,))],
    )(*arrays, *after)


def _all_gather(arrays, name, after=None):
    return _exchange(arrays, name, scatter=False, after=after)


SEM = pl.BlockSpec(memory_space=pltpu.SEMAPHORE)
ANY = pl.BlockSpec(memory_space=pl.ANY)
DATAFLOW = pltpu.SideEffectType.DATAFLOW_SIDE_EFFECTING
TOKEN_SHAPE = (8, LANES)


SIBLING = 1
OTHER_CHIPS = (4, 2, 6)


def _exchange_start(arrays, name, scatter, after, relay=False):
    n = len(arrays)
    n_sems = n * (NDEV - 1)
    assert not (relay and scatter)

    def body(*refs):
        ins = refs[:n]
        send_sems, recv_sems = refs[n + 1], refs[n + 2]
        lands, token = refs[2 * n + 3:3 * n + 3], refs[3 * n + 3]
        me = _my_index()
        for r in (SIBLING, *OTHER_CHIPS) if relay else range(1, NDEV):
            peer, pidx = _peer(r)
            for a in range(n):
                src = ins[a].at[pidx] if scatter else ins[a]
                pltpu.make_async_remote_copy(
                    src_ref=src, dst_ref=lands[a].at[me],
                    send_sem=send_sems.at[a * (NDEV - 1) + r - 1], recv_sem=recv_sems.at[a * (NDEV - 1) + r - 1],
                    device_id=peer, device_id_type=MESH).start()
        token[...] = jnp.zeros(TOKEN_SHAPE, F32)

    land_shapes = [a.shape if scatter else (NDEV,) + a.shape for a in arrays]
    srcs = [pltpu.with_memory_space_constraint(a, pltpu.HBM) for a in arrays]
    outs = pl.pallas_call(
        body, name=name,
        out_shape=(pltpu.SemaphoreType.DMA((n_sems,)), pltpu.SemaphoreType.DMA((n_sems,)),
                   *[pltpu.HBM(a.shape, a.dtype) for a in arrays], *[pltpu.HBM(s, a.dtype) for s, a in zip(land_shapes, arrays)],
                   jax.ShapeDtypeStruct(TOKEN_SHAPE, F32)),
        in_specs=[HBM] * n + [ANY],
        out_specs=(SEM, SEM, *[HBM] * (2 * n), pl.BlockSpec(memory_space=pltpu.VMEM)),
        input_output_aliases={i: 2 + i for i in range(n)},
        compiler_params=pltpu.CompilerParams(has_side_effects=DATAFLOW),
    )(*srcs, after)
    return (scatter, relay, [(outs[0], outs[1])], list(outs[2:2 + n]), list(outs[2 + n:2 + 2 * n])), outs[-1]


def _relay_forward_start(handle, name):
    scatter, relay, sems, srcs, lands = handle
    n = len(lands)
    n_sems = n * len(OTHER_CHIPS)

    def body(*refs):
        land_refs, send_sems, recv_sems = refs[:n], refs[n], refs[n + 1]
        sibling, _ = _peer(SIBLING)
        for j, r in enumerate(OTHER_CHIPS):
            _, pidx = _peer(r)
            for a in range(n):
                pltpu.make_async_remote_copy(
                    src_ref=land_refs[a].at[pidx], dst_ref=land_refs[a].at[pidx],
                    send_sem=send_sems.at[a * len(OTHER_CHIPS) + j], recv_sem=recv_sems.at[a * len(OTHER_CHIPS) + j],
                    device_id=sibling, device_id_type=MESH).start()

    outs = pl.pallas_call(
        body, name=name,
        out_shape=(pltpu.SemaphoreType.DMA((n_sems,)), pltpu.SemaphoreType.DMA((n_sems,)),
                   *[pltpu.HBM(t.shape, t.dtype) for t in lands]),
        in_specs=[HBM] * n, out_specs=(SEM, SEM, *[HBM] * n),
        input_output_aliases={i: 2 + i for i in range(n)},
        compiler_params=pltpu.CompilerParams(has_side_effects=DATAFLOW),
    )(*lands)
    return (scatter, relay, sems + [(outs[0], outs[1])], srcs, list(outs[2:]))


def _exchange_wait(handle, name, after, arrivals_only=False):
    scatter, relay, sems, srcs, lands = handle
    n = len(srcs)
    forwarded = len(sems) == 2
    assert not arrivals_only or (relay and not forwarded)

    def body(*refs):
        src_refs, land_refs = refs[:n], refs[n:2 * n]
        send_sems, recv_sems = refs[2 * n], refs[2 * n + 1]
        for r in (SIBLING, *OTHER_CHIPS) if relay else range(1, NDEV):
            peer, pidx = _peer(r)
            for a in range(n):
                src = src_refs[a].at[pidx] if scatter else src_refs[a]
                cp = pltpu.make_async_remote_copy(
                    src_ref=src, dst_ref=land_refs[a].at[pidx],
                    send_sem=send_sems.at[a * (NDEV - 1) + r - 1], recv_sem=recv_sems.at[a * (NDEV - 1) + r - 1],
                    device_id=peer, device_id_type=MESH)
                if arrivals_only:
                    if r in OTHER_CHIPS:
                        cp.wait_recv()
                else:
                    cp.wait_send()
                    if not (relay and r in OTHER_CHIPS):
                        cp.wait_recv()
        if forwarded:
            fwd_send, fwd_recv = refs[2 * n + 2], refs[2 * n + 3]
            sibling, _ = _peer(SIBLING)
            for j, r in enumerate(OTHER_CHIPS):
                _, pidx = _peer(r ^ SIBLING)
                for a in range(n):
                    cp = pltpu.make_async_remote_copy(
                        src_ref=src_refs[a], dst_ref=land_refs[a].at[pidx],
                        send_sem=fwd_send.at[a * len(OTHER_CHIPS) + j], recv_sem=fwd_recv.at[a * len(OTHER_CHIPS) + j],
                        device_id=sibling, device_id_type=MESH)
                    cp.wait_send()
                    cp.wait_recv()

    flat_sems = [s for pair in sems for s in pair]
    outs = pl.pallas_call(
        body, name=name,
        out_shape=tuple(pltpu.HBM(t.shape, t.dtype) for t in (*srcs, *lands)),
        in_specs=[HBM] * (2 * n) + [SEM] * len(flat_sems) + [ANY], out_specs=tuple([HBM] * (2 * n)),
        input_output_aliases={i: i for i in range(2 * n)},
        compiler_params=pltpu.CompilerParams(has_side_effects=DATAFLOW),
    )(*srcs, *lands, *flat_sems, after)
    if arrivals_only:
        return (scatter, relay, sems, list(outs[:n]), list(outs[n:]))
    me = _my_index()
    mine = [lax.dynamic_index_in_dim(s, me, 0, keepdims=False) if scatter else s for s in outs[:n]]
    return [lax.dynamic_update_index_in_dim(land, own, me, 0) for land, own in zip(outs[n:], mine)]


def _matmul(a, b, *, mode, name, out_dtypes, tm, tn, tk, epilogue=None, extras=(), a_pre=None,
            out_shards=False, n_outer=False, precision=None, after=(), n=None, b_first_block=0):
    after = list(after)
    n_after = len(after)
    parts = list(a) if isinstance(a, (list, tuple)) else [a]
    n_parts = len(parts)
    rows, cols = parts[0].shape
    K, M = (rows, cols * n_parts) if mode == "tn" else (cols * n_parts, rows)
    N = n if n is not None else (b.shape[0] if mode == "nt" else b.shape[1])
    tm, tn, tk = min(tm, M), min(tn, N), min(tk, K)
    assert M % tm == 0 and N % tn == 0 and K % tk == 0, (name, M, N, K, tm, tn, tk)
    nm, nn, nk = M // tm, N // tn, K // tk
    assert n_parts == 1 or (nk == 1 and a_pre is None and mode in ("nn", "tn") and cols % tm == 0), name
    blocks_per_part = cols // tm if mode == "tn" else 1
    n_out, n_ext = len(out_dtypes), len(extras)
    contract = {"nn": ((1,), (0,)), "nt": ((1,), (1,)), "tn": ((0,), (0,))}[mode]

    def body(*refs):
        a_refs, b_ref = refs[:n_parts], refs[n_parts]
        ext_refs = refs[n_parts + 1:n_parts + 1 + n_ext]
        first_out = n_parts + 1 + n_ext + n_after
        out_refs = refs[first_out:first_out + n_out]
        acc_ref = refs[first_out + n_out] if nk > 1 else None

        def product(a_ref, bv):
            av = a_ref[...] if a_pre is None else a_pre(a_ref[...])
            if precision is None:
                av, bv = av.astype(BF16), bv.astype(BF16)
            return lax.dot_general(av, bv, (contract, ((), ())), preferred_element_type=F32, precision=precision)

        def finish(acc):
            vals = (acc,) if epilogue is None else epilogue(acc, *[r[...] for r in ext_refs])
            for r, v in zip(out_refs, vals):
                r[...] = v.astype(r.dtype)

        if n_parts > 1 and mode == "tn":
            i = pl.program_id(1 if n_outer else 0)
            for p in range(n_parts):
                @pl.when(i // blocks_per_part == p)
                def _(p=p):
                    finish(product(a_refs[p], b_ref[...]))
            return
        part = product(a_refs[0], b_ref[...] if n_parts == 1 else b_ref[0:cols, :])
        for p in range(1, n_parts):
            part = part + product(a_refs[p], b_ref[p * cols:(p + 1) * cols, :])

        if nk == 1:
            finish(part)
        else:
            k = pl.program_id(2)

            @pl.when(k == 0)
            def _():
                acc_ref[...] = part

            @pl.when(k > 0)
            def _():
                acc_ref[...] += part

            @pl.when(k == nk - 1)
            def _():
                finish(acc_ref[...])

    def at(index):
        return (lambda j, i, k: index(i, j, k)) if n_outer else index

    if n_parts == 1:
        a_specs = [pl.BlockSpec((tk, tm), at(lambda i, j, k: (k, i))) if mode == "tn"
                   else pl.BlockSpec((tm, tk), at(lambda i, j, k: (i, k)))]
    elif mode == "tn":
        a_specs = [pl.BlockSpec((tk, tm), at(lambda i, j, k, p=p: (k, jnp.clip(i - p * blocks_per_part, 0, blocks_per_part - 1))))
                   for p in range(n_parts)]
    else:
        a_specs = [pl.BlockSpec((tm, cols), at(lambda i, j, k: (i, 0))) for _ in parts]
    b_spec = (pl.BlockSpec((tn, tk), at(lambda i, j, k: (j, k))) if mode == "nt"
              else pl.BlockSpec((tk, tn), at(lambda i, j, k: (k + b_first_block, j))))
    in_specs = a_specs + [b_spec]
    block_bytes = _nbytes((tm, tk), parts[0].dtype) * (n_parts if mode == "tn" else 1) + _nbytes((tk, tn), b.dtype)
    for arr, kind in extras:
        if kind == "tile":
            assert arr.shape == (M, N), (name, arr.shape)
            in_specs.append(pl.BlockSpec((tm, tn), at(lambda i, j, k: (i, j))))
            block_bytes += _nbytes((tm, tn), arr.dtype)
        else:
            assert arr.shape == (1, N), (name, arr.shape)
            in_specs.append(pl.BlockSpec((1, tn), at(lambda i, j, k: (0, j))))
    in_specs += [ANY] * n_after
    if out_shards:
        assert n_out == 1 and tn * NDEV == N
        out_shape = [jax.ShapeDtypeStruct((NDEV, M, tn), out_dtypes[0])]
        out_specs = [pl.BlockSpec((None, tm, tn), at(lambda i, j, k: (j, i, 0)))]
    else:
        out_shape = [jax.ShapeDtypeStruct((M, N), d) for d in out_dtypes]
        out_specs = [pl.BlockSpec((tm, tn), at(lambda i, j, k: (i, j))) for _ in out_dtypes]
    block_bytes += sum(_nbytes((tm, tn), d) for d in out_dtypes)
    scratch = [pltpu.VMEM((tm, tn), F32)] if nk > 1 else []
    outs = pl.pallas_call(
        body, name=name, grid=(nn, nm, nk) if n_outer else (nm, nn, nk), in_specs=in_specs, out_specs=out_specs,
        out_shape=out_shape, scratch_shapes=scratch,
        compiler_params=_params(("parallel", "parallel", "arbitrary"), block_bytes, 2 * tm * tn * 4),
    )(*parts, b, *[arr for arr, _ in extras], *after)
    return outs[0] if n_out == 1 else outs


def _rowwise(fn, tiled, smalls, out_tiles, out_sums, *, name, ts=256):
    S = tiled[0].shape[0]
    ts = min(ts, S)
    assert S % ts == 0
    nt, ns, no, na = len(tiled), len(smalls), len(out_tiles), len(out_sums)

    def body(*refs):
        t_refs, s_refs = refs[:nt], refs[nt:nt + ns]
        o_refs, a_refs = refs[nt + ns:nt + ns + no], refs[nt + ns + no:]
        tile_vals, sum_vals = fn([r[...] for r in t_refs], [r[...] for r in s_refs])
        for r, v in zip(o_refs, tile_vals):
            r[...] = v.astype(r.dtype)

        @pl.when(pl.program_id(0) == 0)
        def _():
            for r in a_refs:
                r[...] = jnp.zeros_like(r)

        for r, v in zip(a_refs, sum_vals):
            r[...] += v

    in_specs = [pl.BlockSpec((ts, t.shape[1]), lambda i: (i, 0)) for t in tiled]
    in_specs += [pl.BlockSpec(s.shape, lambda i: (0, 0)) for s in smalls]
    out_specs = [pl.BlockSpec((ts, w), lambda i: (i, 0)) for w, _ in out_tiles]
    out_specs += [pl.BlockSpec((1, w), lambda i: (0, 0)) for w in out_sums]
    out_shape = [jax.ShapeDtypeStruct((S, w), d) for w, d in out_tiles]
    out_shape += [jax.ShapeDtypeStruct((1, w), F32) for w in out_sums]
    block_bytes = sum(_nbytes((ts, t.shape[1]), t.dtype) for t in tiled) + sum(_nbytes((ts, w), d) for w, d in out_tiles)
    width = max(t.shape[1] for t in tiled)
    outs = pl.pallas_call(
        body, name=name, grid=(S // ts,), in_specs=in_specs, out_specs=out_specs, out_shape=out_shape,
        compiler_params=_params(("arbitrary",), block_bytes, 6 * ts * width * 4),
    )(*tiled, *smalls)
    return outs[:no], outs[no:]


def _colsum(v):
    return jnp.sum(v, axis=0, keepdims=True)


def _rms_mod_fwd(x, gain, shift, scale, name):
    def fn(tiles, smalls):
        (xv,), (g, sh, sc) = tiles, smalls
        inv = lax.rsqrt(jnp.mean(xv * xv, axis=-1, keepdims=True) + RMS_EPS)
        h = (xv * inv) * g * (1.0 + sc) + sh
        return (h, inv), ()

    D = x.shape[1]
    (h, inv), _ = _rowwise(fn, [x], [gain, shift, scale], [(D, BF16), (1, F32)], [], name=name)
    return h, inv


def _gated(dxv, following):
    yv, gate = following
    return dxv * gate, _colsum(dxv * yv)


def _rms_mod_bwd(dh, x, inv, dx_res, gain, scale, name, following=None):
    def fn(tiles, smalls):
        dhv, xv, iv, dres = tiles[:4]
        g, sc = smalls[:2]
        dhv = dhv.astype(F32)
        xhat = xv * iv
        dr = dhv * (1.0 + sc)
        dxhat = dr * g
        dxv = dres + iv * (dxhat - xhat * jnp.mean(dxhat * xhat, axis=-1, keepdims=True))
        sums = (_colsum(dhv), _colsum(dhv * (xhat * g)), _colsum(dr * xhat))
        if following is None:
            return (dxv,), sums
        dy, dgate = _gated(dxv, (tiles[4], smalls[2]))
        return (dxv, dy), (*sums, dgate)

    D = x.shape[1]
    extra = [] if following is None else [following]
    tiles, sums = _rowwise(fn, [dh, x, inv, dx_res] + [f[0] for f in extra], [gain, scale] + [f[1] for f in extra],
                           [(D, F32)] + [(D, BF16)] * len(extra), [D] * (3 + len(extra)), name=name)
    return (tiles[0], *sums[:3]) if following is None else (tiles[0], *sums[:3], tiles[1], sums[3])


def _final_loss_bwd(x, target, gain, following, name):
    D = x.shape[1]

    def fn(tiles, smalls):
        xv, tv, g = tiles[0], tiles[1], smalls[0]
        inv = lax.rsqrt(jnp.mean(xv * xv, axis=-1, keepdims=True) + RMS_EPS)
        xhat = xv * inv
        err = xhat * g - tv
        loss = 0.5 * jnp.sum(jnp.mean(err * err, axis=-1, keepdims=True), axis=0, keepdims=True)
        dout = err * (1.0 / D)
        dxhat = dout * g
        dxv = inv * (dxhat - xhat * jnp.mean(dxhat * xhat, axis=-1, keepdims=True))
        dy, dgate = _gated(dxv, (tiles[2], smalls[1]))
        return (dxv, dy), (_colsum(dout * xhat), jnp.broadcast_to(loss, (1, LANES)), dgate)

    (dx, dy), (dgain, loss, dgate) = _rowwise(fn, [x, target, following[0]], [gain, following[1]],
                                             [(D, F32), (D, BF16)], [D, LANES, D], name=name)
    return dx, dgain, loss, dy, dgate


SCAN_BLOCK = 256


def _triangle(n, lower):
    r = lax.broadcasted_iota(jnp.int32, (n, n), 0)
    c = lax.broadcasted_iota(jnp.int32, (n, n), 1)
    return (r >= c if lower else r <= c).astype(F32)


def _forget_cumsum(logits, bias, name):
    S = logits.shape[0]
    blk = min(SCAN_BLOCK, S)
    nb = S // blk

    def body(z_ref, b_ref, f_ref):
        z = z_ref[...] + b_ref[...]
        f_ref[...] = jnp.minimum(z, 0.0) - jnp.log(1.0 + jnp.exp(-jnp.abs(z)))
        tri = _triangle(blk, lower=True)

        def step(i, carry):
            off = pl.multiple_of(i * blk, blk)
            cs = jnp.dot(tri, f_ref[pl.ds(off, blk), :], preferred_element_type=F32, precision=HIGHEST) + carry
            f_ref[pl.ds(off, blk), :] = cs
            return cs[blk - 1:blk, :]

        lax.fori_loop(0, nb, step, jnp.zeros((1, LANES), F32))

    return pl.pallas_call(body, name=name, out_shape=jax.ShapeDtypeStruct((S, LANES), F32))(logits, bias)


def _forget_bwd(dfk, logits, bias, name):
    S = logits.shape[0]
    blk = min(SCAN_BLOCK, S)
    nb = S // blk

    def body(d_ref, z_ref, b_ref, o_ref, db_ref):
        tri = _triangle(blk, lower=False)

        def step(t, carry):
            off = pl.multiple_of((nb - 1 - t) * blk, blk)
            cs = jnp.dot(tri, d_ref[pl.ds(off, blk), :], preferred_element_type=F32, precision=HIGHEST) + carry
            o_ref[pl.ds(off, blk), :] = cs
            return cs[0:1, :]

        lax.fori_loop(0, nb, step, jnp.zeros((1, LANES), F32))
        z = z_ref[...] + b_ref[...]
        dz = -o_ref[...] / (1.0 + jnp.exp(z))
        o_ref[...] = dz
        db_ref[...] = _colsum(dz)

    return pl.pallas_call(
        body, name=name,
        out_shape=(jax.ShapeDtypeStruct((S, LANES), F32), jax.ShapeDtypeStruct((1, LANES), F32)),
    )(dfk, logits, bias)


KEY_SCALE = HEAD_DIM ** -0.5
ATTN_BLOCK = 512
_NT = (((1,), (1,)), ((), ()))


def _loop_in_pairs(step, count, init):
    carry = lax.fori_loop(0, count // 2, lambda t, c: step(2 * t + 1, step(2 * t, c)), init)
    return lax.fori_loop(count // 2 * 2, count, step, carry)


def _attn_specs(S, H, tb):
    q_blk = lambda part: pl.BlockSpec((tb, HEAD_DIM), lambda h, i: (i, part * H + h))
    q_all = lambda part: pl.BlockSpec((S, HEAD_DIM), lambda h, i: (0, part * H + h))
    col_blk = pl.BlockSpec((None, tb, 1), lambda h, i: (h, i, 0))
    row_all = pl.BlockSpec((None, 1, S), lambda h, i: (h, 0, 0))
    return q_blk, q_all, col_blk, row_all


FWD_HEADS = 2


def _attn_fwd(qkv, f_col, f_row, name):
    S, H = qkv.shape[0], qkv.shape[1] // (3 * HEAD_DIM)
    tb = min(ATTN_BLOCK, S)
    hp = FWD_HEADS if H % FWD_HEADS == 0 else 1
    groups, wide = H // hp, hp * HEAD_DIM
    lanes = lambda u: pl.ds(u * HEAD_DIM, HEAD_DIM)

    def body(q_ref, k_ref, v_ref, fc_ref, fr_ref, o_ref, lse_ref):
        i = pl.program_id(1)

        def step(j, carry, diagonal):
            off = pl.multiple_of(j * tb, tb)
            out = []
            for u in range(hp):
                m, l, acc = carry[u]
                k, v = k_ref[pl.ds(off, tb), lanes(u)], v_ref[pl.ds(off, tb), lanes(u)]
                s = lax.dot_general(q_ref[:, lanes(u)], k, _NT, preferred_element_type=F32)
                s = s + (fc_ref[u] - fr_ref[u, :, pl.ds(off, tb)])
                if diagonal:
                    row = lax.broadcasted_iota(jnp.int32, (tb, tb), 0)
                    col = lax.broadcasted_iota(jnp.int32, (tb, tb), 1)
                    s = jnp.where(col <= row, s, NEG)
                m_new = jnp.maximum(m, jnp.max(s, axis=-1, keepdims=True))
                p = jnp.exp(s - m_new)
                alpha = jnp.exp(m - m_new)
                l = alpha * l + jnp.sum(p, axis=-1, keepdims=True)
                acc = alpha * acc + jnp.dot(p.astype(BF16), v, preferred_element_type=F32)
                out.append((m_new, l, acc))
            return tuple(out)

        init = (jnp.full((tb, 1), NEG, F32), jnp.zeros((tb, 1), F32), jnp.zeros((tb, HEAD_DIM), F32))
        carry = _loop_in_pairs(lambda j, c: step(j, c, False), i, (init,) * hp)
        for u, (m, l, acc) in enumerate(step(i, carry, True)):
            o_ref[:, lanes(u)] = (acc / l).astype(o_ref.dtype)
            lse_ref[u] = m + jnp.log(l)

    part = lambda p, rows: pl.BlockSpec((rows, wide), lambda g, i: (i if rows == tb else 0, p * groups + g))
    col_blk = pl.BlockSpec((hp, tb, 1), lambda g, i: (g, i, 0))
    return pl.pallas_call(
        body, name=name, grid=(groups, S // tb),
        in_specs=[part(0, tb), part(1, S), part(2, S), col_blk, pl.BlockSpec((hp, 1, S), lambda g, i: (g, 0, 0))],
        out_specs=[pl.BlockSpec((tb, wide), lambda g, i: (i, g)), col_blk],
        out_shape=[jax.ShapeDtypeStruct((S, H * HEAD_DIM), BF16), jax.ShapeDtypeStruct((H, S, 1), F32)],
        compiler_params=_params(("parallel", "parallel"), 4 * S * wide * 2, 10 * hp * tb * tb * 4),
    )(qkv, qkv, qkv, f_col, f_row)


_TN = (((0,), (0,)), ((), ()))


def _attn_bwd(qkv, do, f_col, f_row, lse_col, name):
    S, H = qkv.shape[0], qkv.shape[1] // (3 * HEAD_DIM)
    tb = min(ATTN_BLOCK, S)
    nq = S // tb
    q_blk, q_all, col_blk, row_all = _attn_specs(S, H, tb)
    head_blk = pl.BlockSpec((tb, HEAD_DIM), lambda h, i: (i, h))
    head_all = pl.BlockSpec((S, HEAD_DIM), lambda h, i: (0, h))

    def body(q_ref, k_ref, v_ref, do_ref, fc_ref, fr_ref, lse_ref, dq_ref, dk_ref, dv_ref, dfk_ref,
             p_buf, dp_buf, dk_acc, dv_acc, dfk_acc):
        i = pl.program_id(1)
        q, do = q_ref[...], do_ref[...]
        fc_lse = fc_ref[...] - lse_ref[...]

        @pl.when(i == 0)
        def _():
            dk_acc[...] = jnp.zeros_like(dk_acc)
            dv_acc[...] = jnp.zeros_like(dv_acc)
            dfk_acc[...] = jnp.zeros_like(dfk_acc)

        def scores(j, delta, diagonal):
            off = pl.multiple_of(j * tb, tb)
            k, v = k_ref[pl.ds(off, tb), :], v_ref[pl.ds(off, tb), :]
            s = (lax.dot_general(q, k, _NT, preferred_element_type=F32) + fc_lse) - fr_ref[:, pl.ds(off, tb)]
            if diagonal:
                row = lax.broadcasted_iota(jnp.int32, (tb, tb), 0)
                col = lax.broadcasted_iota(jnp.int32, (tb, tb), 1)
                s = jnp.where(col <= row, s, NEG)
            p = jnp.exp(s)
            dp = lax.dot_general(do, v, _NT, preferred_element_type=F32)
            p_buf[j] = p
            dp_buf[j] = dp
            return delta + jnp.sum(p * dp, axis=-1, keepdims=True)

        delta = _loop_in_pairs(lambda j, c: scores(j, c, False), i, jnp.zeros((tb, 1), F32))
        delta = scores(i, delta, True)

        def grad(j, dq):
            off = pl.multiple_of(j * tb, tb)
            p = p_buf[j]
            ds = p * (dp_buf[j] - delta)
            ds_lo = ds.astype(BF16)
            dk_acc[pl.ds(off, tb), :] += lax.dot_general(ds_lo, q, _TN, preferred_element_type=F32)
            dv_acc[pl.ds(off, tb), :] += lax.dot_general(p.astype(BF16), do, _TN, preferred_element_type=F32)
            dfk_acc[:, pl.ds(off, tb)] += jnp.sum(ds, axis=0, keepdims=True)
            return dq + jnp.dot(ds_lo, k_ref[pl.ds(off, tb), :], preferred_element_type=F32)

        dq = _loop_in_pairs(grad, i + 1, jnp.zeros((tb, HEAD_DIM), F32))
        dq_ref[...] = dq.astype(dq_ref.dtype)

        @pl.when(i == nq - 1)
        def _():
            dk_ref[...] = (dk_acc[...] * KEY_SCALE).astype(dk_ref.dtype)
            dv_ref[...] = dv_acc[...].astype(dv_ref.dtype)
            dfk_ref[...] = dfk_acc[...]

    wide = jax.ShapeDtypeStruct((S, H * HEAD_DIM), BF16)
    return pl.pallas_call(
        body, name=name, grid=(H, nq),
        in_specs=[q_blk(0), q_all(1), q_all(2), head_blk, col_blk, row_all, col_blk],
        out_specs=[head_blk, head_all, head_all, row_all],
        out_shape=[wide, wide, wide, jax.ShapeDtypeStruct((H, 1, S), F32)],
        scratch_shapes=[pltpu.VMEM((nq, tb, tb), F32), pltpu.VMEM((nq, tb, tb), F32),
                        pltpu.VMEM((S, HEAD_DIM), F32), pltpu.VMEM((S, HEAD_DIM), F32), pltpu.VMEM((1, S), F32)],
        compiler_params=_params(("parallel", "arbitrary"), 6 * S * HEAD_DIM * 2,
                                2 * nq * tb * tb * 4 + 2 * S * HEAD_DIM * 4 + 10 * tb * tb * 4),
    )(qkv, qkv, qkv, do, f_col, f_row, lse_col)


CONV_TILE = 128


def _shift_down(v, n):
    row = lax.broadcasted_iota(jnp.int32, v.shape, 0)
    return jnp.where(row >= n, pltpu.roll(v, n, 0), 0.0)


def _shift_up(v, n):
    S = v.shape[0]
    row = lax.broadcasted_iota(jnp.int32, v.shape, 0)
    return jnp.where(row < S - n, pltpu.roll(v, S - n, 0), 0.0)


def _conv_specs(S, D, tc):
    nb = D // tc
    part = lambda p: pl.BlockSpec((S, tc), lambda j: (0, p * nb + j))
    return part, pl.BlockSpec((S, tc), lambda j: (0, j)), pl.BlockSpec((8, tc), lambda j: (0, j))


def _conv_fwd(proj, conv_w8, name):
    S, D = proj.shape[0], proj.shape[1] // 3
    tc = min(CONV_TILE, D)
    part, chan, taps = _conv_specs(S, D, tc)

    def body(b_ref, c_ref, u_ref, w_ref, z_ref):
        cu = c_ref[...].astype(F32) * u_ref[...].astype(F32)
        w = w_ref[...]
        y = w[0:1, :] * _shift_down(cu, 2) + w[1:2, :] * _shift_down(cu, 1) + w[2:3, :] * cu
        z_ref[...] = (b_ref[...].astype(F32) * y).astype(z_ref.dtype)

    return pl.pallas_call(
        body, name=name, grid=(D // tc,), in_specs=[part(0), part(1), part(2), taps], out_specs=chan,
        out_shape=jax.ShapeDtypeStruct((S, D), BF16),
        compiler_params=_params(("parallel",), 3 * _nbytes((S, tc), proj.dtype) + S * tc * 2, 6 * S * tc * 4),
    )(proj, proj, proj, conv_w8)


def _conv_bwd(proj, dz, conv_w8, name):
    S, D = proj.shape[0], proj.shape[1] // 3
    tc = min(CONV_TILE, D)
    part, chan, taps = _conv_specs(S, D, tc)

    def body(b_ref, c_ref, u_ref, dz_ref, w_ref, db_ref, dc_ref, du_ref, dw_ref):
        cv, uv = c_ref[...].astype(F32), u_ref[...].astype(F32)
        dzv, w = dz_ref[...].astype(F32), w_ref[...]
        cu = cv * uv
        cu1, cu2 = _shift_down(cu, 1), _shift_down(cu, 2)
        y = w[0:1, :] * cu2 + w[1:2, :] * cu1 + w[2:3, :] * cu
        db_ref[...] = (dzv * y).astype(db_ref.dtype)
        dy = dzv * b_ref[...].astype(F32)
        dcu = w[2:3, :] * dy + w[1:2, :] * _shift_up(dy, 1) + w[0:1, :] * _shift_up(dy, 2)
        dc_ref[...] = (dcu * uv).astype(dc_ref.dtype)
        du_ref[...] = (dcu * cv).astype(du_ref.dtype)
        dw_ref[...] = jnp.concatenate(
            [_colsum(dy * cu2), _colsum(dy * cu1), _colsum(dy * cu), jnp.zeros((8 - CONV_WIDTH, tc), F32)], axis=0)

    return pl.pallas_call(
        body, name=name, grid=(D // tc,), in_specs=[part(0), part(1), part(2), chan, taps],
        out_specs=[chan, chan, chan, taps],
        out_shape=[jax.ShapeDtypeStruct((S, D), BF16)] * 3 + [jax.ShapeDtypeStruct((8, D), F32)],
        compiler_params=_params(("parallel",), 3 * _nbytes((S, tc), proj.dtype) + _nbytes((S, tc), dz.dtype)
                                + 3 * S * tc * 2, 10 * S * tc * 4),
    )(proj, proj, proj, dz, conv_w8)


def _adamw(w, m, v, parts, name, layer=0, prev=None):
    L, R, C = w.shape
    P = parts.shape[0]
    assert parts.shape[1:] == (R, C), (name, parts.shape, w.shape)
    elem_bytes = 12 + 16 + P * parts.dtype.itemsize
    budget = 8 << 20
    tr, tc = R, C
    if R * C * elem_bytes > budget:
        if R % 8 == 0:
            tr = max(8, (budget // (C * elem_bytes)) // 8 * 8)
            while R % tr:
                tr -= 8
        else:
            tc = LANES
            while C % (2 * tc) == 0 and R * 2 * tc * elem_bytes <= budget:
                tc *= 2
            assert C % tc == 0, (name, R, C)
    c1, c2 = 1.0 - ADAM_B1 ** ADAM_STEP, 1.0 - ADAM_B2 ** ADAM_STEP

    def body(w_ref, m_ref, v_ref, p_ref, *rest):
        g_ref, d_ref, nm_ref, nv_ref = rest[-4:]
        g = p_ref[0].astype(F32)
        for p in range(1, P):
            g = g + p_ref[p].astype(F32)
        nm = ADAM_B1 * m_ref[...] + (1.0 - ADAM_B1) * g
        nv = ADAM_B2 * v_ref[...] + (1.0 - ADAM_B2) * (g * g)
        g_ref[...] = g
        nm_ref[...] = nm
        nv_ref[...] = nv
        d_ref[...] = -ADAM_LR * ((nm / c1) / (jnp.sqrt(nv / c2) + ADAM_EPS) + ADAM_WD * w_ref[...])

    blk = pl.BlockSpec((None, tr, tc), lambda i, j: (layer, i, j))
    prev = [] if prev is None else list(prev)
    return pl.pallas_call(
        body, name=name, grid=(R // tr, C // tc),
        in_specs=[blk, blk, blk, pl.BlockSpec((P, tr, tc), lambda i, j: (0, i, j))] + [ANY] * len(prev),
        out_specs=[blk] * 4, out_shape=[jax.ShapeDtypeStruct((L, R, C), F32)] * 4,
        input_output_aliases={4 + k: k for k in range(len(prev))},
        compiler_params=_params(("parallel", "parallel"), tr * tc * elem_bytes),
    )(w, m, v, parts, *prev)


def _silu(v):
    return v / (1.0 + jnp.exp(-v))


def _pad_rows(a, rows):
    return jnp.pad(a, ((0, rows - a.shape[0]), (0, 0)))


def _pad_cols(a, cols):
    return jnp.pad(a, ((0, 0), (0, cols - a.shape[1])))


def kernel(x, c, ada_w, ada_b, norm_mix, norm_mlp, fox_w_in, fox_b_f, fox_w_out, conv_w_in, conv_w, conv_w_out, mlp_w_up, mlp_w_down, final_norm, loss_target, m_ada_w, m_ada_b, m_norm_mix, m_norm_mlp, m_fox_w_in, m_fox_b_f, m_fox_w_out, m_conv_w_in, m_conv_w, m_conv_w_out, m_mlp_w_up, m_mlp_w_down, m_final_norm, v_ada_w, v_ada_b, v_norm_mix, v_norm_mlp, v_fox_w_in, v_fox_b_f, v_fox_w_out, v_conv_w_in, v_conv_w, v_conv_w_out, v_mlp_w_up, v_mlp_w_down, v_final_norm):
    S, D = x.shape[1], x.shape[2]
    H = D // HEAD_DIM
    FF = mlp_w_up.shape[2] * NDEV
    depth = ada_w.shape[0]
    n_mod = 6
    assert depth == 2 and fox_w_in.shape[0] == 1 and conv_w_in.shape[0] == 1 and H <= LANES
    me = _my_index()
    x0, target = x[0], loss_target[0]
    row = lambda vec: vec.reshape(1, -1)

    def tied(vec, token):
        return vec + token[0, 0]

    bf = lambda w: w.astype(BF16)
    gather_groups = {
        "fox": [bf(fox_w_in[0]).T, bf(fox_w_out[0])],
        "mlp0": [bf(mlp_w_up[0]).T, bf(mlp_w_down[0])],
        "conv": [bf(conv_w_in[0]).T, conv_w[0], bf(conv_w_out[0])],
        "mlp1": [bf(mlp_w_up[1]).T, bf(mlp_w_down[1])],
    }

    def start_gather(group, after):
        return _exchange_start(gather_groups[group], f"gather_{group}_start", False, after, relay=True)

    def relay_gather(handle, group, after):
        handle = _exchange_wait(handle, f"gather_{group}_arrivals", after, arrivals_only=True)
        return _relay_forward_start(handle, f"gather_{group}_forward")

    def finish_gather(handle, group, after):
        return _exchange_wait(handle, f"gather_{group}_wait", after)

    landed = lambda handle: handle[4][0]

    c_all = _all_gather([c], "gather_cond")[0].reshape(NDEV, D)
    ncol = ada_w.shape[2]
    ada_b_mine = lax.dynamic_slice_in_dim(ada_b, me * ncol, ncol, axis=1)
    mod_cols = jnp.stack([
        _matmul(c_all, ada_w.reshape(depth * D, ncol), mode="nn", name=f"ada_fwd_{i}", out_dtypes=[F32], tm=NDEV,
                tn=ncol // 2, tk=D, b_first_block=i,
                a_pre=_silu, precision=HIGHEST, epilogue=lambda acc, b: (acc + b,), extras=[(ada_b_mine[i:i + 1], "row")])
        for i in range(depth)])
    mod_all = _all_gather([mod_cols], "gather_mod")[0]
    mod = lax.dynamic_index_in_dim(mod_all, me, axis=2, keepdims=False)
    fox_handle, token = start_gather("fox", mod_all)
    mod = tied(mod, token).transpose(1, 0, 2).reshape(depth, n_mod, 1, D)
    sh_mix, sc_mix, g_mix, sh_mlp, sc_mlp, g_mlp = (mod[:, k] for k in range(n_mod))
    b_f = _pad_cols(fox_b_f, LANES)

    def residual(acc, x_in, gate):
        return (x_in + gate * acc, acc)

    def mlp_fwd(i, x_in, handle, relay_next=None):
        h, inv = _rms_mod_fwd(x_in, row(norm_mlp[i]), sh_mlp[i], sc_mlp[i], f"mlp_norm_{i}")
        w_up_t, w_down = finish_gather(handle, f"mlp{i}", h)
        w_up_t, w_down = w_up_t.reshape(FF, D), w_down.reshape(FF, D)
        r = _matmul(h, w_up_t, mode="nt", name=f"mlp_up_{i}", out_dtypes=[BF16], tm=1024, tn=1024, tk=D,
                    epilogue=lambda acc: (jnp.maximum(acc, 0.0),))
        next_handle = relay_gather(relay_next[1], relay_next[0], r) if relay_next else None
        x_out, y = _matmul(r, w_down, mode="nn", name=f"mlp_down_{i}", out_dtypes=[F32, BF16], tm=512, tn=512, tk=FF,
                           n_outer=True, a_pre=jnp.square, epilogue=residual, extras=[(x_in, "tile"), (g_mlp[i], "row")],
                           after=[landed(next_handle)] if relay_next else [])
        return x_out, (x_in, h, inv, r, y, w_up_t, w_down), next_handle

    def mlp_bwd(i, dx, dy, dgate, saved, following, after):
        x_in, h, inv, r, y, w_up_t, w_down = saved
        du = _matmul(dy, w_down, mode="nt", name=f"mlp_down_bwd_{i}", out_dtypes=[BF16], tm=1024, tn=1024, tk=D,
                     epilogue=lambda acc, rv: (acc * (2.0 * rv.astype(F32)),), extras=[(r, "tile")], after=after)
        d_down = _matmul(r, dy, mode="tn", name=f"mlp_down_wgrad_{i}", out_dtypes=[BF16], tm=512, tn=1024, tk=S,
                         a_pre=jnp.square)
        dh = _matmul(du, w_up_t, mode="nn", name=f"mlp_up_bwd_{i}", out_dtypes=[F32], tm=512, tn=512, tk=FF, n_outer=True)
        d_up = _matmul(h, du, mode="tn", name=f"mlp_up_wgrad_{i}", out_dtypes=[BF16], tm=512, tn=FF // NDEV, tk=S,
                       out_shards=True)
        dx, dsh, dsc, dgain, dy_next, dgate_next = _rms_mod_bwd(dh, x_in, inv, dx, row(norm_mlp[i]), sc_mlp[i],
                                                                f"mlp_norm_bwd_{i}", following)
        handle, token = _exchange_start([d_up, d_down.reshape(NDEV, FF // NDEV, D)], f"scatter_mlp{i}_start", True, dx)
        return dx, (dsh, dsc, dgate, dgain), handle, token, dy_next, dgate_next

    h0, inv0 = _rms_mod_fwd(x0, row(norm_mix[0]), sh_mix[0], sc_mix[0], "fox_norm")
    w_in_t, w_fox_out = finish_gather(relay_gather(fox_handle, "fox", h0), "fox", h0)
    mlp0_handle, token = start_gather("mlp0", w_in_t)
    w_in_t = w_in_t.reshape(3 * D + H, D)
    w_f_t = _pad_rows(w_in_t[3 * D:], LANES)
    w_fox_out = w_fox_out.reshape(D, D)
    column_scale = jnp.concatenate([jnp.ones((1, D), F32), jnp.full((1, D), KEY_SCALE, F32), jnp.ones((1, D), F32)], axis=1)
    qkv = _matmul(h0, w_in_t, mode="nt", name="fox_qkv", out_dtypes=[BF16], tm=1024, tn=1024, tk=D, n=3 * D, after=[token],
                  epilogue=lambda acc, mult: (acc * mult,), extras=[(column_scale, "row")])
    f_logit = _matmul(h0, w_f_t, mode="nt", name="fox_forget_logits", out_dtypes=[F32], tm=1024, tn=LANES, tk=D)
    f_cum = _forget_cumsum(f_logit, b_f, "fox_forget_cumsum")
    f_heads = f_cum[:, :H].T
    f_col, f_row = f_heads.reshape(H, S, 1), f_heads.reshape(H, 1, S)
    o, lse = _attn_fwd(qkv, f_col, f_row, "fox_attention")
    mlp0_handle = relay_gather(mlp0_handle, "mlp0", o)
    conv_handle, token = start_gather("conv", landed(mlp0_handle))
    mlp1_handle, token = start_gather("mlp1", token)
    x1, mix0 = _matmul(o, w_fox_out, mode="nn", name="fox_out", out_dtypes=[F32, BF16], tm=512, tn=1024, tk=D,
                       epilogue=residual, extras=[(x0, "tile"), (g_mix[0], "row")], after=[token])
    x2, mlp0, conv_handle = mlp_fwd(0, x1, mlp0_handle, ("conv", conv_handle))

    h1, inv1 = _rms_mod_fwd(x2, row(norm_mix[1]), sh_mix[1], sc_mix[1], "conv_norm")
    w_conv_in_t, w_taps, w_conv_out = finish_gather(conv_handle, "conv", h1)
    w_conv_in_t = w_conv_in_t.reshape(3 * D, D)
    w_taps = _pad_rows(w_taps.transpose(1, 0, 2).reshape(CONV_WIDTH, D), 8)
    w_conv_out = w_conv_out.reshape(D, D)
    proj = _matmul(h1, w_conv_in_t, mode="nt", name="conv_in", out_dtypes=[BF16], tm=1024, tn=1024, tk=D)
    mlp1_handle = relay_gather(mlp1_handle, "mlp1", proj)
    z = _conv_fwd(proj, w_taps, "conv_mix")
    x3, mix1 = _matmul(z, w_conv_out, mode="nn", name="conv_out", out_dtypes=[F32, BF16], tm=512, tn=1024, tk=D,
                       epilogue=residual, extras=[(x2, "tile"), (g_mix[1], "row")], after=[landed(mlp1_handle)])
    x4, mlp1, _ = mlp_fwd(1, x3, mlp1_handle)

    dx, d_final, loss_lanes, dy, dgate = _final_loss_bwd(x4, target, row(final_norm), (mlp1[4], g_mlp[1]), "loss_head")

    dx, dmod_mlp1, mlp1_scatter, token, dmix, dg_mix1 = mlp_bwd(1, dx, dy, dgate, mlp1, (mix1, g_mix[1]), [])
    dz = _matmul(dmix, w_conv_out, mode="nt", name="conv_out_bwd", out_dtypes=[BF16], tm=1024, tn=1024, tk=D,
                 after=[token])
    d_conv_out = _matmul(z, dmix, mode="tn", name="conv_out_wgrad", out_dtypes=[BF16], tm=512, tn=1024, tk=S)
    db, dc, du, d_taps = _conv_bwd(proj, dz, w_taps, "conv_mix_bwd")
    dproj = jnp.concatenate([db, dc, du], axis=1)
    dh1 = _matmul(dproj, w_conv_in_t, mode="nn", name="conv_in_bwd", out_dtypes=[F32], tm=512, tn=512, tk=3 * D, n_outer=True)
    d_conv_in = _matmul(h1, dproj, mode="tn", name="conv_in_wgrad", out_dtypes=[BF16], tm=512, tn=3 * D // NDEV, tk=S,
                        out_shards=True)
    dx, dsh1, dsc1, dgain_mix1, dy, dgate = _rms_mod_bwd(dh1, x2, inv1, dx, row(norm_mix[1]), sc_mix[1], "conv_norm_bwd",
                                                         (mlp0[4], g_mlp[0]))
    d_taps_split = d_taps[:CONV_WIDTH].reshape(CONV_WIDTH, NDEV, -1).transpose(1, 0, 2)
    conv_scatter, token = _exchange_start([d_conv_in, d_taps_split, d_conv_out.reshape(NDEV, D // NDEV, D)],
                                          "scatter_conv_start", True, dx)

    dx, dmod_mlp0, mlp0_scatter, token, dmix, dg_mix0 = mlp_bwd(0, dx, dy, dgate, mlp0, (mix0, g_mix[0]), [token])
    do = _matmul(dmix, w_fox_out, mode="nt", name="fox_out_bwd", out_dtypes=[BF16], tm=1024, tn=1024, tk=D,
                 after=[token])
    d_fox_out = _matmul(o, dmix, mode="tn", name="fox_out_wgrad", out_dtypes=[BF16], tm=512, tn=1024, tk=S)
    dq, dk, dv, dfk = _attn_bwd(qkv, do, f_col, f_row, lse, "fox_attention_bwd")
    dqkv = [dq, dk, dv]
    dfk_lanes = _pad_cols(dfk.reshape(H, S).T, LANES)
    df_logit, db_f = _forget_bwd(dfk_lanes, f_logit, b_f, "fox_forget_bwd")
    d_qkv_t = _matmul(dqkv, h0, mode="tn", name="fox_qkv_wgrad", out_dtypes=[BF16], tm=512, tn=512, tk=S)
    d_f_t = _matmul(df_logit, h0, mode="tn", name="fox_forget_wgrad", out_dtypes=[BF16], tm=LANES, tn=1024, tk=S)
    d_fox_in = jnp.concatenate([d_qkv_t, d_f_t[:H]], axis=0).reshape(NDEV, -1, D)
    fox_scatter, token = _exchange_start([d_fox_in, d_fox_out.reshape(NDEV, D // NDEV, D)], "scatter_fox_start", True,
                                         d_fox_in)
    dh0_f = _matmul(df_logit, w_f_t, mode="nn", name="fox_forget_logits_bwd", out_dtypes=[F32], tm=1024, tn=1024, tk=LANES,
                    after=[token])
    dh0 = _matmul(dqkv, w_in_t, mode="nn", name="fox_qkv_bwd", out_dtypes=[F32], tm=512, tn=512, tk=3 * D, n_outer=True,
                  epilogue=lambda acc, extra: (acc + extra,), extras=[(dh0_f, "tile")])
    dx, dsh0, dsc0, dgain_mix0 = _rms_mod_bwd(dh0, x0, inv0, dx, row(norm_mix[0]), sc_mix[0], "fox_norm_bwd")
    grad_x = dx.reshape(1, S, D)

    dmod = jnp.concatenate([
        jnp.concatenate([dsh0, dsc0, dg_mix0, dmod_mlp0[0], dmod_mlp0[1], dmod_mlp0[2]], axis=1),
        jnp.concatenate([dsh1, dsc1, dg_mix1, dmod_mlp1[0], dmod_mlp1[1], dmod_mlp1[2]], axis=1)], axis=0)
    small_sizes = [depth * n_mod * D, depth * D, depth * D, H, D, 1]
    n_small = sum(small_sizes)
    n_rows = -(-n_small // (8 * LANES)) * 8

    def pack(parts):
        flat = jnp.concatenate([p.reshape(-1) for p in parts])
        return jnp.pad(flat, (0, n_rows * LANES - n_small)).reshape(n_rows, LANES)

    def unpack(packed, shapes):
        flat, out, at = packed.reshape(-1), [], 0
        for size, shape in zip(small_sizes, shapes):
            out.append(flat[at:at + size].reshape(shape))
            at += size
        return out

    small_partial = pack([dmod, jnp.concatenate([dgain_mix0, dgain_mix1], axis=0),
                          jnp.concatenate([dmod_mlp0[3], dmod_mlp1[3]], axis=0), db_f[0, :H], d_final, loss_lanes[0, :1]])
    small_handle, token = _exchange_start([small_partial], "gather_small_start", False, dx)

    up1, down1 = _exchange_wait(mlp1_scatter, "scatter_mlp1_wait", token)
    up_out = _adamw(mlp_w_up, m_mlp_w_up, v_mlp_w_up, up1, "adamw_mlp_w_up_1", layer=1)
    down_out = _adamw(mlp_w_down, m_mlp_w_down, v_mlp_w_down, down1, "adamw_mlp_w_down_1", layer=1)
    cin, taps, cout = _exchange_wait(conv_scatter, "scatter_conv_wait", down_out[0])
    conv_in_out = _adamw(conv_w_in, m_conv_w_in, v_conv_w_in, cin, "adamw_conv_w_in")
    conv_w_res = _adamw(conv_w, m_conv_w, v_conv_w, taps, "adamw_conv_w")
    conv_out_out = _adamw(conv_w_out, m_conv_w_out, v_conv_w_out, cout, "adamw_conv_w_out")
    up0, down0 = _exchange_wait(mlp0_scatter, "scatter_mlp0_wait", conv_out_out[0])
    up_out = _adamw(mlp_w_up, m_mlp_w_up, v_mlp_w_up, up0, "adamw_mlp_w_up_0", layer=0, prev=up_out)
    down_out = _adamw(mlp_w_down, m_mlp_w_down, v_mlp_w_down, down0, "adamw_mlp_w_down_0", layer=0, prev=down_out)

    small_parts = _exchange_wait(small_handle, "gather_small_wait", down_out[0])[0]
    small_shapes = [ada_b.shape, norm_mix.shape, norm_mlp.shape, fox_b_f.shape, final_norm.shape]
    loss = jnp.sum(small_parts.reshape(NDEV, -1)[:, n_small - 1])
    unused = jnp.zeros((1,), F32)
    small_out = _adamw(pack([ada_b, norm_mix, norm_mlp, fox_b_f, final_norm, unused])[None],
                       pack([m_ada_b, m_norm_mix, m_norm_mlp, m_fox_b_f, m_final_norm, unused])[None],
                       pack([v_ada_b, v_norm_mix, v_norm_mlp, v_fox_b_f, v_final_norm, unused])[None], small_parts,
                       "adamw_small")
    small_out = [unpack(t, small_shapes) for t in small_out]

    dmod_all = small_parts.reshape(NDEV, -1)[:, :depth * n_mod * D].reshape(NDEV, depth, n_mod * D)
    dmod_mine = lax.dynamic_slice_in_dim(dmod_all, me * ncol, ncol, axis=2)
    ada_out = None
    for i in range(depth):
        d_ada = _matmul(c_all, dmod_mine[:, i], mode="tn", name=f"ada_wgrad_{i}", out_dtypes=[F32], tm=1024, tn=ncol // 2,
                        tk=NDEV, a_pre=_silu, precision=HIGHEST)
        ada_out = _adamw(ada_w, m_ada_w, v_ada_w, d_ada[None], f"adamw_ada_w_{i}", layer=i, prev=ada_out)

    fin, fout = _exchange_wait(fox_scatter, "scatter_fox_wait", ada_out[0])
    swap = lambda t: jnp.swapaxes(t, 1, 2)
    fox_in_out = [swap(t) for t in _adamw(swap(fox_w_in), swap(m_fox_w_in), swap(v_fox_w_in), fin, "adamw_fox_w_in")]
    fox_out_out = _adamw(fox_w_out, m_fox_w_out, v_fox_w_out, fout, "adamw_fox_w_out")

    outputs = [loss, grad_x]
    for kind in range(4):
        sm = small_out[kind]
        outputs += [ada_out[kind], sm[0], sm[1], sm[2], fox_in_out[kind], sm[3], fox_out_out[kind], conv_in_out[kind],
                    conv_w_res[kind], conv_out_out[kind], up_out[kind], down_out[kind], sm[4]]
    return tuple(outputs)
```

```python
import math

import jax
import jax.numpy as jnp
from jax import lax
from jax.experimental import pallas as pl
from jax.experimental.pallas import tpu as pltpu

F32 = jnp.float32
BF16 = jnp.bfloat16
MESH = pl.DeviceIdType.MESH
NDEV = 8
HEAD_DIM = 128
LANES = 128
CONV_WIDTH = 3
RMS_EPS = 1e-6
ADAM_LR, ADAM_B1, ADAM_B2, ADAM_EPS, ADAM_WD, ADAM_STEP = 0.001, 0.9, 0.999, 1e-08, 0.01, 10
NEG = -1e30
V7X_VMEM_BYTES = 64 * 1024 * 1024
VMEM_HEADROOM = 12 * 1024 * 1024
HBM = pl.BlockSpec(memory_space=pltpu.HBM)
HIGHEST = lax.Precision.HIGHEST


def _nbytes(shape, dtype):
    return math.prod(shape) * jnp.dtype(dtype).itemsize


def _params(semantics, block_bytes, temp_bytes=0):
    limit = min(2 * block_bytes + temp_bytes + VMEM_HEADROOM, V7X_VMEM_BYTES - 4 * 1024 * 1024)
    return pltpu.CompilerParams(dimension_semantics=semantics, vmem_limit_bytes=int(limit))


def _my_index():
    return lax.axis_index("x") * 4 + lax.axis_index("y") * 2 + lax.axis_index("c")


def _peer(r):
    x, y, c = lax.axis_index("x"), lax.axis_index("y"), lax.axis_index("c")
    px = 1 - x if (r >> 2) & 1 else x
    py = 1 - y if (r >> 1) & 1 else y
    pc = 1 - c if r & 1 else c
    return (px, py, pc), px * 4 + py * 2 + pc


def _exchange(arrays, name, scatter, after=None):
    n = len(arrays)
    after = [] if after is None else list(after)

    def body(*refs):
        ins, outs = refs[:n], refs[n + len(after):2 * n + len(after)]
        send_sems, recv_sems, local_sems = refs[2 * n + len(after):]
        me = _my_index()
        local = []
        for a in range(n):
            src = ins[a].at[me] if scatter else ins[a]
            local.append(pltpu.make_async_copy(src, outs[a].at[me], local_sems.at[a]))
            local[-1].start()
        sends = []
        for r in range(1, NDEV):
            peer, pidx = _peer(r)
            for a in range(n):
                src = ins[a].at[pidx] if scatter else ins[a]
                cp = pltpu.make_async_remote_copy(
                    src_ref=src, dst_ref=outs[a].at[me],
                    send_sem=send_sems.at[a * (NDEV - 1) + r - 1], recv_sem=recv_sems.at[a * (NDEV - 1) + r - 1],
                    device_id=peer, device_id_type=MESH)
                cp.start()
                sends.append(cp)
        for r in range(1, NDEV):
            peer, pidx = _peer(r)
            for a in range(n):
                src = ins[a].at[pidx] if scatter else ins[a]
                pltpu.make_async_remote_copy(
                    src_ref=src, dst_ref=outs[a].at[pidx],
                    send_sem=send_sems.at[a * (NDEV - 1) + r - 1], recv_sem=recv_sems.at[a * (NDEV - 1) + r - 1],
                    device_id=peer, device_id_type=MESH).wait_recv()
        for cp in sends:
            cp.wait_send()
        for cp in local:
            cp.wait()

    out_shape = [jax.ShapeDtypeStruct(a.shape if scatter else (NDEV,) + a.shape, a.dtype) for a in arrays]
    return pl.pallas_call(
        body, name=name, out_shape=out_shape, in_specs=[HBM] * n + [ANY] * len(after), out_specs=[HBM] * n,
        scratch_shapes=[pltpu.SemaphoreType.DMA((n * (NDEV - 1),)), pltpu.SemaphoreType.DMA((n * (NDEV - 1),)),
                        pltpu.SemaphoreType.DMA((n,))],
    )(*arrays, *after)


def _all_gather(arrays, name, after=None):
    return _exchange(arrays, name, scatter=False, after=after)


SEM = pl.BlockSpec(memory_space=pltpu.SEMAPHORE)
ANY = pl.BlockSpec(memory_space=pl.ANY)
DATAFLOW = pltpu.SideEffectType.DATAFLOW_SIDE_EFFECTING
TOKEN_SHAPE = (8, LANES)


SIBLING = 1
OTHER_CHIPS = (4, 2, 6)


def _exchange_start(arrays, name, scatter, after, relay=False):
    n = len(arrays)
    n_sems = n * (NDEV - 1)
    assert not (relay and scatter)

    def body(*refs):
        ins = refs[:n]
        send_sems, recv_sems = refs[n + 1], refs[n + 2]
        lands, token = refs[2 * n + 3:3 * n + 3], refs[3 * n + 3]
        me = _my_index()
        for r in (SIBLING, *OTHER_CHIPS) if relay else range(1, NDEV):
            peer, pidx = _peer(r)
            for a in range(n):
                src = ins[a].at[pidx] if scatter else ins[a]
                pltpu.make_async_remote_copy(
                    src_ref=src, dst_ref=lands[a].at[me],
                    send_sem=send_sems.at[a * (NDEV - 1) + r - 1], recv_sem=recv_sems.at[a * (NDEV - 1) + r - 1],
                    device_id=peer, device_id_type=MESH).start()
        token[...] = jnp.zeros(TOKEN_SHAPE, F32)

    land_shapes = [a.shape if scatter else (NDEV,) + a.shape for a in arrays]
    srcs = [pltpu.with_memory_space_constraint(a, pltpu.HBM) for a in arrays]
    outs = pl.pallas_call(
        body, name=name,
        out_shape=(pltpu.SemaphoreType.DMA((n_sems,)), pltpu.SemaphoreType.DMA((n_sems,)),
                   *[pltpu.HBM(a.shape, a.dtype) for a in arrays], *[pltpu.HBM(s, a.dtype) for s, a in zip(land_shapes, arrays)],
                   jax.ShapeDtypeStruct(TOKEN_SHAPE, F32)),
        in_specs=[HBM] * n + [ANY],
        out_specs=(SEM, SEM, *[HBM] * (2 * n), pl.BlockSpec(memory_space=pltpu.VMEM)),
        input_output_aliases={i: 2 + i for i in range(n)},
        compiler_params=pltpu.CompilerParams(has_side_effects=DATAFLOW),
    )(*srcs, after)
    return (scatter, relay, [(outs[0], outs[1])], list(outs[2:2 + n]), list(outs[2 + n:2 + 2 * n])), outs[-1]


def _relay_forward_start(handle, name):
    scatter, relay, sems, srcs, lands = handle
    n = len(lands)
    n_sems = n * len(OTHER_CHIPS)

    def body(*refs):
        land_refs, send_sems, recv_sems = refs[:n], refs[n], refs[n + 1]
        sibling, _ = _peer(SIBLING)
        for j, r in enumerate(OTHER_CHIPS):
            _, pidx = _peer(r)
            for a in range(n):
                pltpu.make_async_remote_copy(
                    src_ref=land_refs[a].at[pidx], dst_ref=land_refs[a].at[pidx],
                    send_sem=send_sems.at[a * len(OTHER_CHIPS) + j], recv_sem=recv_sems.at[a * len(OTHER_CHIPS) + j],
                    device_id=sibling, device_id_type=MESH).start()

    outs = pl.pallas_call(
        body, name=name,
        out_shape=(pltpu.SemaphoreType.DMA((n_sems,)), pltpu.SemaphoreType.DMA((n_sems,)),
                   *[pltpu.HBM(t.shape, t.dtype) for t in lands]),
        in_specs=[HBM] * n, out_specs=(SEM, SEM, *[HBM] * n),
        input_output_aliases={i: 2 + i for i in range(n)},
        compiler_params=pltpu.CompilerParams(has_side_effects=DATAFLOW),
    )(*lands)
    return (scatter, relay, sems + [(outs[0], outs[1])], srcs, list(outs[2:]))


def _exchange_wait(handle, name, after, arrivals_only=False):
    scatter, relay, sems, srcs, lands = handle
    n = len(srcs)
    forwarded = len(sems) == 2
    assert not arrivals_only or (relay and not forwarded)

    def body(*refs):
        src_refs, land_refs = refs[:n], refs[n:2 * n]
        send_sems, recv_sems = refs[2 * n], refs[2 * n + 1]
        for r in (SIBLING, *OTHER_CHIPS) if relay else range(1, NDEV):
            peer, pidx = _peer(r)
            for a in range(n):
                src = src_refs[a].at[pidx] if scatter else src_refs[a]
                cp = pltpu.make_async_remote_copy(
                    src_ref=src, dst_ref=land_refs[a].at[pidx],
                    send_sem=send_sems.at[a * (NDEV - 1) + r - 1], recv_sem=recv_sems.at[a * (NDEV - 1) + r - 1],
                    device_id=peer, device_id_type=MESH)
                if arrivals_only:
                    if r in OTHER_CHIPS:
                        cp.wait_recv()
                else:
                    cp.wait_send()
                    if not (relay and r in OTHER_CHIPS):
                        cp.wait_recv()
        if forwarded:
            fwd_send, fwd_recv = refs[2 * n + 2], refs[2 * n + 3]
            sibling, _ = _peer(SIBLING)
            for j, r in enumerate(OTHER_CHIPS):
                _, pidx = _peer(r ^ SIBLING)
                for a in range(n):
                    cp = pltpu.make_async_remote_copy(
                        src_ref=src_refs[a], dst_ref=land_refs[a].at[pidx],
                        send_sem=fwd_send.at[a * len(OTHER_CHIPS) + j], recv_sem=fwd_recv.at[a * len(OTHER_CHIPS) + j],
                        device_id=sibling, device_id_type=MESH)
                    cp.wait_send()
                    cp.wait_recv()

    flat_sems = [s for pair in sems for s in pair]
    outs = pl.pallas_call(
        body, name=name,
        out_shape=tuple(pltpu.HBM(t.shape, t.dtype) for t in (*srcs, *lands)),
        in_specs=[HBM] * (2 * n) + [SEM] * len(flat_sems) + [ANY], out_specs=tuple([HBM] * (2 * n)),
        input_output_aliases={i: i for i in range(2 * n)},
        compiler_params=pltpu.CompilerParams(has_side_effects=DATAFLOW),
    )(*srcs, *lands, *flat_sems, after)
    if arrivals_only:
        return (scatter, relay, sems, list(outs[:n]), list(outs[n:]))
    me = _my_index()
    mine = [lax.dynamic_index_in_dim(s, me, 0, keepdims=False) if scatter else s for s in outs[:n]]
    return [lax.dynamic_update_index_in_dim(land, own, me, 0) for land, own in zip(outs[n:], mine)]


def _matmul(a, b, *, mode, name, out_dtypes, tm, tn, tk, epilogue=None, extras=(), a_pre=None,
            out_shards=False, n_outer=False, precision=None, after=(), n=None, b_first_block=0):
    after = list(after)
    n_after = len(after)
    parts = list(a) if isinstance(a, (list, tuple)) else [a]
    n_parts = len(parts)
    rows, cols = parts[0].shape
    K, M = (rows, cols * n_parts) if mode == "tn" else (cols * n_parts, rows)
    N = n if n is not None else (b.shape[0] if mode == "nt" else b.shape[1])
    tm, tn, tk = min(tm, M), min(tn, N), min(tk, K)
    assert M % tm == 0 and N % tn == 0 and K % tk == 0, (name, M, N, K, tm, tn, tk)
    nm, nn, nk = M // tm, N // tn, K // tk
    assert n_parts == 1 or (nk == 1 and a_pre is None and mode in ("nn", "tn") and cols % tm == 0), name
    blocks_per_part = cols // tm if mode == "tn" else 1
    n_out, n_ext = len(out_dtypes), len(extras)
    contract = {"nn": ((1,), (0,)), "nt": ((1,), (1,)), "tn": ((0,), (0,))}[mode]

    def body(*refs):
        a_refs, b_ref = refs[:n_parts], refs[n_parts]
        ext_refs = refs[n_parts + 1:n_parts + 1 + n_ext]
        first_out = n_parts + 1 + n_ext + n_after
        out_refs = refs[first_out:first_out + n_out]
        acc_ref = refs[first_out + n_out] if nk > 1 else None

        def product(a_ref, bv):
            av = a_ref[...] if a_pre is None else a_pre(a_ref[...])
            if precision is None:
                av, bv = av.astype(BF16), bv.astype(BF16)
            return lax.dot_general(av, bv, (contract, ((), ())), preferred_element_type=F32, precision=precision)

        def finish(acc):
            vals = (acc,) if epilogue is None else epilogue(acc, *[r[...] for r in ext_refs])
            for r, v in zip(out_refs, vals):
                r[...] = v.astype(r.dtype)

        if n_parts > 1 and mode == "tn":
            i = pl.program_id(1 if n_outer else 0)
            for p in range(n_parts):
                @pl.when(i // blocks_per_part == p)
                def _(p=p):
                    finish(product(a_refs[p], b_ref[...]))
            return
        part = product(a_refs[0], b_ref[...] if n_parts == 1 else b_ref[0:cols, :])
        for p in range(1, n_parts):
            part = part + product(a_refs[p], b_ref[p * cols:(p + 1) * cols, :])

        if nk == 1:
            finish(part)
        else:
            k = pl.program_id(2)

            @pl.when(k == 0)
            def _():
                acc_ref[...] = part

            @pl.when(k > 0)
            def _():
                acc_ref[...] += part

            @pl.when(k == nk - 1)
            def _():
                finish(acc_ref[...])

    def at(index):
        return (lambda j, i, k: index(i, j, k)) if n_outer else index

    if n_parts == 1:
        a_specs = [pl.BlockSpec((tk, tm), at(lambda i, j, k: (k, i))) if mode == "tn"
                   else pl.BlockSpec((tm, tk), at(lambda i, j, k: (i, k)))]
    elif mode == "tn":
        a_specs = [pl.BlockSpec((tk, tm), at(lambda i, j, k, p=p: (k, jnp.clip(i - p * blocks_per_part, 0, blocks_per_part - 1))))
                   for p in range(n_parts)]
    else:
        a_specs = [pl.BlockSpec((tm, cols), at(lambda i, j, k: (i, 0))) for _ in parts]
    b_spec = (pl.BlockSpec((tn, tk), at(lambda i, j, k: (j, k))) if mode == "nt"
              else pl.BlockSpec((tk, tn), at(lambda i, j, k: (k + b_first_block, j))))
    in_specs = a_specs + [b_spec]
    block_bytes = _nbytes((tm, tk), parts[0].dtype) * (n_parts if mode == "tn" else 1) + _nbytes((tk, tn), b.dtype)
    for arr, kind in extras:
        if kind == "tile":
            assert arr.shape == (M, N), (name, arr.shape)
            in_specs.append(pl.BlockSpec((tm, tn), at(lambda i, j, k: (i, j))))
            block_bytes += _nbytes((tm, tn), arr.dtype)
        else:
            assert arr.shape == (1, N), (name, arr.shape)
            in_specs.append(pl.BlockSpec((1, tn), at(lambda i, j, k: (0, j))))
    in_specs += [ANY] * n_after
    if out_shards:
        assert n_out == 1 and tn * NDEV == N
        out_shape = [jax.ShapeDtypeStruct((NDEV, M, tn), out_dtypes[0])]
        out_specs = [pl.BlockSpec((None, tm, tn), at(lambda i, j, k: (j, i, 0)))]
    else:
        out_shape = [jax.ShapeDtypeStruct((M, N), d) for d in out_dtypes]
        out_specs = [pl.BlockSpec((tm, tn), at(lambda i, j, k: (i, j))) for _ in out_dtypes]
    block_bytes += sum(_nbytes((tm, tn), d) for d in out_dtypes)
    scratch = [pltpu.VMEM((tm, tn), F32)] if nk > 1 else []
    outs = pl.pallas_call(
        body, name=name, grid=(nn, nm, nk) if n_outer else (nm, nn, nk), in_specs=in_specs, out_specs=out_specs,
        out_shape=out_shape, scratch_shapes=scratch,
        compiler_params=_params(("parallel", "parallel", "arbitrary"), block_bytes, 2 * tm * tn * 4),
    )(*parts, b, *[arr for arr, _ in extras], *after)
    return outs[0] if n_out == 1 else outs


def _rowwise(fn, tiled, smalls, out_tiles, out_sums, *, name, ts=256):
    S = tiled[0].shape[0]
    ts = min(ts, S)
    assert S % ts == 0
    nt, ns, no, na = len(tiled), len(smalls), len(out_tiles), len(out_sums)

    def body(*refs):
        t_refs, s_refs = refs[:nt], refs[nt:nt + ns]
        o_refs, a_refs = refs[nt + ns:nt + ns + no], refs[nt + ns + no:]
        tile_vals, sum_vals = fn([r[...] for r in t_refs], [r[...] for r in s_refs])
        for r, v in zip(o_refs, tile_vals):
            r[...] = v.astype(r.dtype)

        @pl.when(pl.program_id(0) == 0)
        def _():
            for r in a_refs:
                r[...] = jnp.zeros_like(r)

        for r, v in zip(a_refs, sum_vals):
            r[...] += v

    in_specs = [pl.BlockSpec((ts, t.shape[1]), lambda i: (i, 0)) for t in tiled]
    in_specs += [pl.BlockSpec(s.shape, lambda i: (0, 0)) for s in smalls]
    out_specs = [pl.BlockSpec((ts, w), lambda i: (i, 0)) for w, _ in out_tiles]
    out_specs += [pl.BlockSpec((1, w), lambda i: (0, 0)) for w in out_sums]
    out_shape = [jax.ShapeDtypeStruct((S, w), d) for w, d in out_tiles]
    out_shape += [jax.ShapeDtypeStruct((1, w), F32) for w in out_sums]
    block_bytes = sum(_nbytes((ts, t.shape[1]), t.dtype) for t in tiled) + sum(_nbytes((ts, w), d) for w, d in out_tiles)
    width = max(t.shape[1] for t in tiled)
    outs = pl.pallas_call(
        body, name=name, grid=(S // ts,), in_specs=in_specs, out_specs=out_specs, out_shape=out_shape,
        compiler_params=_params(("arbitrary",), block_bytes, 6 * ts * width * 4),
    )(*tiled, *smalls)
    return outs[:no], outs[no:]


def _colsum(v):
    return jnp.sum(v, axis=0, keepdims=True)


def _rms_mod_fwd(x, gain, shift, scale, name):
    def fn(tiles, smalls):
        (xv,), (g, sh, sc) = tiles, smalls
        inv = lax.rsqrt(jnp.mean(xv * xv, axis=-1, keepdims=True) + RMS_EPS)
        h = (xv * inv) * g * (1.0 + sc) + sh
        return (h, inv), ()

    D = x.shape[1]
    (h, inv), _ = _rowwise(fn, [x], [gain, shift, scale], [(D, BF16), (1, F32)], [], name=name)
    return h, inv


def _gated(dxv, following):
    yv, gate = following
    return dxv * gate, _colsum(dxv * yv)


def _rms_mod_bwd(dh, x, inv, dx_res, gain, scale, name, following=None):
    def fn(tiles, smalls):
        dhv, xv, iv, dres = tiles[:4]
        g, sc = smalls[:2]
        dhv = dhv.astype(F32)
        xhat = xv * iv
        dr = dhv * (1.0 + sc)
        dxhat = dr * g
        dxv = dres + iv * (dxhat - xhat * jnp.mean(dxhat * xhat, axis=-1, keepdims=True))
        sums = (_colsum(dhv), _colsum(dhv * (xhat * g)), _colsum(dr * xhat))
        if following is None:
            return (dxv,), sums
        dy, dgate = _gated(dxv, (tiles[4], smalls[2]))
        return (dxv, dy), (*sums, dgate)

    D = x.shape[1]
    extra = [] if following is None else [following]
    tiles, sums = _rowwise(fn, [dh, x, inv, dx_res] + [f[0] for f in extra], [gain, scale] + [f[1] for f in extra],
                           [(D, F32)] + [(D, BF16)] * len(extra), [D] * (3 + len(extra)), name=name)
    return (tiles[0], *sums[:3]) if following is None else (tiles[0], *sums[:3], tiles[1], sums[3])


def _final_loss_bwd(x, target, gain, following, name):
    D = x.shape[1]

    def fn(tiles, smalls):
        xv, tv, g = tiles[0], tiles[1], smalls[0]
        inv = lax.rsqrt(jnp.mean(xv * xv, axis=-1, keepdims=True) + RMS_EPS)
        xhat = xv * inv
        err = xhat * g - tv
        loss = 0.5 * jnp.sum(jnp.mean(err * err, axis=-1, keepdims=True), axis=0, keepdims=True)
        dout = err * (1.0 / D)
        dxhat = dout * g
        dxv = inv * (dxhat - xhat * jnp.mean(dxhat * xhat, axis=-1, keepdims=True))
        dy, dgate = _gated(dxv, (tiles[2], smalls[1]))
        return (dxv, dy), (_colsum(dout * xhat), jnp.broadcast_to(loss, (1, LANES)), dgate)

    (dx, dy), (dgain, loss, dgate) = _rowwise(fn, [x, target, following[0]], [gain, following[1]],
                                             [(D, F32), (D, BF16)], [D, LANES, D], name=name)
    return dx, dgain, loss, dy, dgate


SCAN_BLOCK = 256


def _triangle(n, lower):
    r = lax.broadcasted_iota(jnp.int32, (n, n), 0)
    c = lax.broadcasted_iota(jnp.int32, (n, n), 1)
    return (r >= c if lower else r <= c).astype(F32)


def _forget_cumsum(logits, bias, name):
    S = logits.shape[0]
    blk = min(SCAN_BLOCK, S)
    nb = S // blk

    def body(z_ref, b_ref, f_ref):
        z = z_ref[...] + b_ref[...]
        f_ref[...] = jnp.minimum(z, 0.0) - jnp.log(1.0 + jnp.exp(-jnp.abs(z)))
        tri = _triangle(blk, lower=True)

        def step(i, carry):
            off = pl.multiple_of(i * blk, blk)
            cs = jnp.dot(tri, f_ref[pl.ds(off, blk), :], preferred_element_type=F32, precision=HIGHEST) + carry
            f_ref[pl.ds(off, blk), :] = cs
            return cs[blk - 1:blk, :]

        lax.fori_loop(0, nb, step, jnp.zeros((1, LANES), F32))

    return pl.pallas_call(body, name=name, out_shape=jax.ShapeDtypeStruct((S, LANES), F32))(logits, bias)


def _forget_bwd(dfk, logits, bias, name):
    S = logits.shape[0]
    blk = min(SCAN_BLOCK, S)
    nb = S // blk

    def body(d_ref, z_ref, b_ref, o_ref, db_ref):
        tri = _triangle(blk, lower=False)

        def step(t, carry):
            off = pl.multiple_of((nb - 1 - t) * blk, blk)
            cs = jnp.dot(tri, d_ref[pl.ds(off, blk), :], preferred_element_type=F32, precision=HIGHEST) + carry
            o_ref[pl.ds(off, blk), :] = cs
            return cs[0:1, :]

        lax.fori_loop(0, nb, step, jnp.zeros((1, LANES), F32))
        z = z_ref[...] + b_ref[...]
        dz = -o_ref[...] / (1.0 + jnp.exp(z))
        o_ref[...] = dz
        db_ref[...] = _colsum(dz)

    return pl.pallas_call(
        body, name=name,
        out_shape=(jax.ShapeDtypeStruct((S, LANES), F32), jax.ShapeDtypeStruct((1, LANES), F32)),
    )(dfk, logits, bias)


KEY_SCALE = HEAD_DIM ** -0.5
ATTN_BLOCK = 512
_NT = (((1,), (1,)), ((), ()))


def _loop_in_pairs(step, count, init):
    carry = lax.fori_loop(0, count // 2, lambda t, c: step(2 * t + 1, step(2 * t, c)), init)
    return lax.fori_loop(count // 2 * 2, count, step, carry)


def _attn_specs(S, H, tb):
    q_blk = lambda part: pl.BlockSpec((tb, HEAD_DIM), lambda h, i: (i, part * H + h))
    q_all = lambda part: pl.BlockSpec((S, HEAD_DIM), lambda h, i: (0, part * H + h))
    col_blk = pl.BlockSpec((None, tb, 1), lambda h, i: (h, i, 0))
    row_all = pl.BlockSpec((None, 1, S), lambda h, i: (h, 0, 0))
    return q_blk, q_all, col_blk, row_all


FWD_HEADS = 2


def _attn_fwd(qkv, f_col, f_row, name):
    S, H = qkv.shape[0], qkv.shape[1] // (3 * HEAD_DIM)
    tb = min(ATTN_BLOCK, S)
    hp = FWD_HEADS if H % FWD_HEADS == 0 else 1
    groups, wide = H // hp, hp * HEAD_DIM
    lanes = lambda u: pl.ds(u * HEAD_DIM, HEAD_DIM)

    def body(q_ref, k_ref, v_ref, fc_ref, fr_ref, o_ref, lse_ref):
        i = pl.program_id(1)

        def step(j, carry, diagonal):
            off = pl.multiple_of(j * tb, tb)
            out = []
            for u in range(hp):
                m, l, acc = carry[u]
                k, v = k_ref[pl.ds(off, tb), lanes(u)], v_ref[pl.ds(off, tb), lanes(u)]
                s = lax.dot_general(q_ref[:, lanes(u)], k, _NT, preferred_element_type=F32)
                s = s + (fc_ref[u] - fr_ref[u, :, pl.ds(off, tb)])
                if diagonal:
                    row = lax.broadcasted_iota(jnp.int32, (tb, tb), 0)
                    col = lax.broadcasted_iota(jnp.int32, (tb, tb), 1)
                    s = jnp.where(col <= row, s, NEG)
                m_new = jnp.maximum(m, jnp.max(s, axis=-1, keepdims=True))
                p = jnp.exp(s - m_new)
                alpha = jnp.exp(m - m_new)
                l = alpha * l + jnp.sum(p, axis=-1, keepdims=True)
                acc = alpha * acc + jnp.dot(p.astype(BF16), v, preferred_element_type=F32)
                out.append((m_new, l, acc))
            return tuple(out)

        init = (jnp.full((tb, 1), NEG, F32), jnp.zeros((tb, 1), F32), jnp.zeros((tb, HEAD_DIM), F32))
        carry = _loop_in_pairs(lambda j, c: step(j, c, False), i, (init,) * hp)
        for u, (m, l, acc) in enumerate(step(i, carry, True)):
            o_ref[:, lanes(u)] = (acc / l).astype(o_ref.dtype)
            lse_ref[u] = m + jnp.log(l)

    part = lambda p, rows: pl.BlockSpec((rows, wide), lambda g, i: (i if rows == tb else 0, p * groups + g))
    col_blk = pl.BlockSpec((hp, tb, 1), lambda g, i: (g, i, 0))
    return pl.pallas_call(
        body, name=name, grid=(groups, S // tb),
        in_specs=[part(0, tb), part(1, S), part(2, S), col_blk, pl.BlockSpec((hp, 1, S), lambda g, i: (g, 0, 0))],
        out_specs=[pl.BlockSpec((tb, wide), lambda g, i: (i, g)), col_blk],
        out_shape=[jax.ShapeDtypeStruct((S, H * HEAD_DIM), BF16), jax.ShapeDtypeStruct((H, S, 1), F32)],
        compiler_params=_params(("parallel", "parallel"), 4 * S * wide * 2, 10 * hp * tb * tb * 4),
    )(qkv, qkv, qkv, f_col, f_row)


_TN = (((0,), (0,)), ((), ()))


def _attn_bwd(qkv, do, f_col, f_row, lse_col, name):
    S, H = qkv.shape[0], qkv.shape[1] // (3 * HEAD_DIM)
    tb = min(ATTN_BLOCK, S)
    nq = S // tb
    q_blk, q_all, col_blk, row_all = _attn_specs(S, H, tb)
    head_blk = pl.BlockSpec((tb, HEAD_DIM), lambda h, i: (i, h))
    head_all = pl.BlockSpec((S, HEAD_DIM), lambda h, i: (0, h))

    def body(q_ref, k_ref, v_ref, do_ref, fc_ref, fr_ref, lse_ref, dq_ref, dk_ref, dv_ref, dfk_ref,
             p_buf, dp_buf, dk_acc, dv_acc, dfk_acc):
        i = pl.program_id(1)
        q, do = q_ref[...], do_ref[...]
        fc_lse = fc_ref[...] - lse_ref[...]

        @pl.when(i == 0)
        def _():
            dk_acc[...] = jnp.zeros_like(dk_acc)
            dv_acc[...] = jnp.zeros_like(dv_acc)
            dfk_acc[...] = jnp.zeros_like(dfk_acc)

        def scores(j, delta, diagonal):
            off = pl.multiple_of(j * tb, tb)
            k, v = k_ref[pl.ds(off, tb), :], v_ref[pl.ds(off, tb), :]
            s = (lax.dot_general(q, k, _NT, preferred_element_type=F32) + fc_lse) - fr_ref[:, pl.ds(off, tb)]
            if diagonal:
                row = lax.broadcasted_iota(jnp.int32, (tb, tb), 0)
                col = lax.broadcasted_iota(jnp.int32, (tb, tb), 1)
                s = jnp.where(col <= row, s, NEG)
            p = jnp.exp(s)
            dp = lax.dot_general(do, v, _NT, preferred_element_type=F32)
            p_buf[j] = p
            dp_buf[j] = dp
            return delta + jnp.sum(p * dp, axis=-1, keepdims=True)

        delta = _loop_in_pairs(lambda j, c: scores(j, c, False), i, jnp.zeros((tb, 1), F32))
        delta = scores(i, delta, True)

        def grad(j, dq):
            off = pl.multiple_of(j * tb, tb)
            p = p_buf[j]
            ds = p * (dp_buf[j] - delta)
            ds_lo = ds.astype(BF16)
            dk_acc[pl.ds(off, tb), :] += lax.dot_general(ds_lo, q, _TN, preferred_element_type=F32)
            dv_acc[pl.ds(off, tb), :] += lax.dot_general(p.astype(BF16), do, _TN, preferred_element_type=F32)
            dfk_acc[:, pl.ds(off, tb)] += jnp.sum(ds, axis=0, keepdims=True)
            return dq + jnp.dot(ds_lo, k_ref[pl.ds(off, tb), :], preferred_element_type=F32)

        dq = _loop_in_pairs(grad, i + 1, jnp.zeros((tb, HEAD_DIM), F32))
        dq_ref[...] = dq.astype(dq_ref.dtype)

        @pl.when(i == nq - 1)
        def _():
            dk_ref[...] = (dk_acc[...] * KEY_SCALE).astype(dk_ref.dtype)
            dv_ref[...] = dv_acc[...].astype(dv_ref.dtype)
            dfk_ref[...] = dfk_acc[...]

    wide = jax.ShapeDtypeStruct((S, H * HEAD_DIM), BF16)
    return pl.pallas_call(
        body, name=name, grid=(H, nq),
        in_specs=[q_blk(0), q_all(1), q_all(2), head_blk, col_blk, row_all, col_blk],
        out_specs=[head_blk, head_all, head_all, row_all],
        out_shape=[wide, wide, wide, jax.ShapeDtypeStruct((H, 1, S), F32)],
        scratch_shapes=[pltpu.VMEM((nq, tb, tb), F32), pltpu.VMEM((nq, tb, tb), F32),
                        pltpu.VMEM((S, HEAD_DIM), F32), pltpu.VMEM((S, HEAD_DIM), F32), pltpu.VMEM((1, S), F32)],
        compiler_params=_params(("parallel", "arbitrary"), 6 * S * HEAD_DIM * 2,
                                2 * nq * tb * tb * 4 + 2 * S * HEAD_DIM * 4 + 10 * tb * tb * 4),
    )(qkv, qkv, qkv, do, f_col, f_row, lse_col)


CONV_TILE = 128


def _shift_down(v, n):
    row = lax.broadcasted_iota(jnp.int32, v.shape, 0)
    return jnp.where(row >= n, pltpu.roll(v, n, 0), 0.0)


def _shift_up(v, n):
    S = v.shape[0]
    row = lax.broadcasted_iota(jnp.int32, v.shape, 0)
    return jnp.where(row < S - n, pltpu.roll(v, S - n, 0), 0.0)


def _conv_specs(S, D, tc):
    nb = D // tc
    part = lambda p: pl.BlockSpec((S, tc), lambda j: (0, p * nb + j))
    return part, pl.BlockSpec((S, tc), lambda j: (0, j)), pl.BlockSpec((8, tc), lambda j: (0, j))


def _conv_fwd(proj, conv_w8, name):
    S, D = proj.shape[0], proj.shape[1] // 3
    tc = min(CONV_TILE, D)
    part, chan, taps = _conv_specs(S, D, tc)

    def body(b_ref, c_ref, u_ref, w_ref, z_ref):
        cu = c_ref[...].astype(F32) * u_ref[...].astype(F32)
        w = w_ref[...]
        y = w[0:1, :] * _shift_down(cu, 2) + w[1:2, :] * _shift_down(cu, 1) + w[2:3, :] * cu
        z_ref[...] = (b_ref[...].astype(F32) * y).astype(z_ref.dtype)

    return pl.pallas_call(
        body, name=name, grid=(D // tc,), in_specs=[part(0), part(1), part(2), taps], out_specs=chan,
        out_shape=jax.ShapeDtypeStruct((S, D), BF16),
        compiler_params=_params(("parallel",), 3 * _nbytes((S, tc), proj.dtype) + S * tc * 2, 6 * S * tc * 4),
    )(proj, proj, proj, conv_w8)


def _conv_bwd(proj, dz, conv_w8, name):
    S, D = proj.shape[0], proj.shape[1] // 3
    tc = min(CONV_TILE, D)
    part, chan, taps = _conv_specs(S, D, tc)

    def body(b_ref, c_ref, u_ref, dz_ref, w_ref, db_ref, dc_ref, du_ref, dw_ref):
        cv, uv = c_ref[...].astype(F32), u_ref[...].astype(F32)
        dzv, w = dz_ref[...].astype(F32), w_ref[...]
        cu = cv * uv
        cu1, cu2 = _shift_down(cu, 1), _shift_down(cu, 2)
        y = w[0:1, :] * cu2 + w[1:2, :] * cu1 + w[2:3, :] * cu
        db_ref[...] = (dzv * y).astype(db_ref.dtype)
        dy = dzv * b_ref[...].astype(F32)
        dcu = w[2:3, :] * dy + w[1:2, :] * _shift_up(dy, 1) + w[0:1, :] * _shift_up(dy, 2)
        dc_ref[...] = (dcu * uv).astype(dc_ref.dtype)
        du_ref[...] = (dcu * cv).astype(du_ref.dtype)
        dw_ref[...] = jnp.concatenate(
            [_colsum(dy * cu2), _colsum(dy * cu1), _colsum(dy * cu), jnp.zeros((8 - CONV_WIDTH, tc), F32)], axis=0)

    return pl.pallas_call(
        body, name=name, grid=(D // tc,), in_specs=[part(0), part(1), part(2), chan, taps],
        out_specs=[chan, chan, chan, taps],
        out_shape=[jax.ShapeDtypeStruct((S, D), BF16)] * 3 + [jax.ShapeDtypeStruct((8, D), F32)],
        compiler_params=_params(("parallel",), 3 * _nbytes((S, tc), proj.dtype) + _nbytes((S, tc), dz.dtype)
                                + 3 * S * tc * 2, 10 * S * tc * 4),
    )(proj, proj, proj, dz, conv_w8)


def _adamw(w, m, v, parts, name, layer=0, prev=None):
    L, R, C = w.shape
    P = parts.shape[0]
    assert parts.shape[1:] == (R, C), (name, parts.shape, w.shape)
    elem_bytes = 12 + 16 + P * parts.dtype.itemsize
    budget = 16 << 20
    tr, tc = R, C
    if R * C * elem_bytes > budget:
        if R % 8 == 0:
            tr = max(8, (budget // (C * elem_bytes)) // 8 * 8)
            while R % tr:
                tr -= 8
        else:
            tc = LANES
            while C % (2 * tc) == 0 and R * 2 * tc * elem_bytes <= budget:
                tc *= 2
            assert C % tc == 0, (name, R, C)
    c1, c2 = 1.0 - ADAM_B1 ** ADAM_STEP, 1.0 - ADAM_B2 ** ADAM_STEP

    def body(w_ref, m_ref, v_ref, p_ref, *rest):
        g_ref, d_ref, nm_ref, nv_ref = rest[-4:]
        g = p_ref[0].astype(F32)
        for p in range(1, P):
            g = g + p_ref[p].astype(F32)
        nm = ADAM_B1 * m_ref[...] + (1.0 - ADAM_B1) * g
        nv = ADAM_B2 * v_ref[...] + (1.0 - ADAM_B2) * (g * g)
        g_ref[...] = g
        nm_ref[...] = nm
        nv_ref[...] = nv
        d_ref[...] = -ADAM_LR * ((nm / c1) / (jnp.sqrt(nv / c2) + ADAM_EPS) + ADAM_WD * w_ref[...])

    blk = pl.BlockSpec((None, tr, tc), lambda i, j: (layer, i, j))
    prev = [] if prev is None else list(prev)
    return pl.pallas_call(
        body, name=name, grid=(R // tr, C // tc),
        in_specs=[blk, blk, blk, pl.BlockSpec((P, tr, tc), lambda i, j: (0, i, j))] + [ANY] * len(prev),
        out_specs=[blk] * 4, out_shape=[jax.ShapeDtypeStruct((L, R, C), F32)] * 4,
        input_output_aliases={4 + k: k for k in range(len(prev))},
        compiler_params=_params(("parallel", "parallel"), tr * tc * elem_bytes),
    )(w, m, v, parts, *prev)


def _silu(v):
    return v / (1.0 + jnp.exp(-v))


def _pad_rows(a, rows):
    return jnp.pad(a, ((0, rows - a.shape[0]), (0, 0)))


def _pad_cols(a, cols):
    return jnp.pad(a, ((0, 0), (0, cols - a.shape[1])))


def kernel(x, c, ada_w, ada_b, norm_mix, norm_mlp, fox_w_in, fox_b_f, fox_w_out, conv_w_in, conv_w, conv_w_out, mlp_w_up, mlp_w_down, final_norm, loss_target, m_ada_w, m_ada_b, m_norm_mix, m_norm_mlp, m_fox_w_in, m_fox_b_f, m_fox_w_out, m_conv_w_in, m_conv_w, m_conv_w_out, m_mlp_w_up, m_mlp_w_down, m_final_norm, v_ada_w, v_ada_b, v_norm_mix, v_norm_mlp, v_fox_w_in, v_fox_b_f, v_fox_w_out, v_conv_w_in, v_conv_w, v_conv_w_out, v_mlp_w_up, v_mlp_w_down, v_final_norm):
    S, D = x.shape[1], x.shape[2]
    H = D // HEAD_DIM
    FF = mlp_w_up.shape[2] * NDEV
    depth = ada_w.shape[0]
    n_mod = 6
    assert depth == 2 and fox_w_in.shape[0] == 1 and conv_w_in.shape[0] == 1 and H <= LANES
    me = _my_index()
    x0, target = x[0], loss_target[0]
    row = lambda vec: vec.reshape(1, -1)

    def tied(vec, token):
        return vec + token[0, 0]

    bf = lambda w: w.astype(BF16)
    gather_groups = {
        "fox": [bf(fox_w_in[0]).T, bf(fox_w_out[0])],
        "mlp0": [bf(mlp_w_up[0]).T, bf(mlp_w_down[0])],
        "conv": [bf(conv_w_in[0]).T, conv_w[0], bf(conv_w_out[0])],
        "mlp1": [bf(mlp_w_up[1]).T, bf(mlp_w_down[1])],
    }

    def start_gather(group, after):
        return _exchange_start(gather_groups[group], f"gather_{group}_start", False, after, relay=True)

    def relay_gather(handle, group, after):
        handle = _exchange_wait(handle, f"gather_{group}_arrivals", after, arrivals_only=True)
        return _relay_forward_start(handle, f"gather_{group}_forward")

    def finish_gather(handle, group, after):
        return _exchange_wait(handle, f"gather_{group}_wait", after)

    landed = lambda handle: handle[4][0]

    c_all = _all_gather([c], "gather_cond")[0].reshape(NDEV, D)
    ncol = ada_w.shape[2]
    ada_b_mine = lax.dynamic_slice_in_dim(ada_b, me * ncol, ncol, axis=1)
    mod_cols = jnp.stack([
        _matmul(c_all, ada_w.reshape(depth * D, ncol), mode="nn", name=f"ada_fwd_{i}", out_dtypes=[F32], tm=NDEV,
                tn=ncol // 2, tk=D, b_first_block=i,
                a_pre=_silu, precision=HIGHEST, epilogue=lambda acc, b: (acc + b,), extras=[(ada_b_mine[i:i + 1], "row")])
        for i in range(depth)])
    mod_all = _all_gather([mod_cols], "gather_mod")[0]
    mod = lax.dynamic_index_in_dim(mod_all, me, axis=2, keepdims=False)
    fox_handle, token = start_gather("fox", mod_all)
    mod = tied(mod, token).transpose(1, 0, 2).reshape(depth, n_mod, 1, D)
    sh_mix, sc_mix, g_mix, sh_mlp, sc_mlp, g_mlp = (mod[:, k] for k in range(n_mod))
    b_f = _pad_cols(fox_b_f, LANES)

    def residual(acc, x_in, gate):
        return (x_in + gate * acc, acc)

    def mlp_fwd(i, x_in, handle, relay_next=None):
        h, inv = _rms_mod_fwd(x_in, row(norm_mlp[i]), sh_mlp[i], sc_mlp[i], f"mlp_norm_{i}")
        w_up_t, w_down = finish_gather(handle, f"mlp{i}", h)
        w_up_t, w_down = w_up_t.reshape(FF, D), w_down.reshape(FF, D)
        r = _matmul(h, w_up_t, mode="nt", name=f"mlp_up_{i}", out_dtypes=[BF16], tm=1024, tn=1024, tk=D,
                    epilogue=lambda acc: (jnp.maximum(acc, 0.0),))
        next_handle = relay_gather(relay_next[1], relay_next[0], r) if relay_next else None
        x_out, y = _matmul(r, w_down, mode="nn", name=f"mlp_down_{i}", out_dtypes=[F32, BF16], tm=512, tn=512, tk=FF,
                           n_outer=True, a_pre=jnp.square, epilogue=residual, extras=[(x_in, "tile"), (g_mlp[i], "row")],
                           after=[landed(next_handle)] if relay_next else [])
        return x_out, (x_in, h, inv, r, y, w_up_t, w_down), next_handle

    def mlp_bwd(i, dx, dy, dgate, saved, following, after):
        x_in, h, inv, r, y, w_up_t, w_down = saved
        du = _matmul(dy, w_down, mode="nt", name=f"mlp_down_bwd_{i}", out_dtypes=[BF16], tm=1024, tn=1024, tk=D,
                     epilogue=lambda acc, rv: (acc * (2.0 * rv.astype(F32)),), extras=[(r, "tile")], after=after)
        d_down = _matmul(r, dy, mode="tn", name=f"mlp_down_wgrad_{i}", out_dtypes=[BF16], tm=512, tn=1024, tk=S,
                         a_pre=jnp.square)
        dh = _matmul(du, w_up_t, mode="nn", name=f"mlp_up_bwd_{i}", out_dtypes=[F32], tm=512, tn=512, tk=FF, n_outer=True)
        d_up = _matmul(h, du, mode="tn", name=f"mlp_up_wgrad_{i}", out_dtypes=[BF16], tm=512, tn=FF // NDEV, tk=S,
                       out_shards=True)
        dx, dsh, dsc, dgain, dy_next, dgate_next = _rms_mod_bwd(dh, x_in, inv, dx, row(norm_mlp[i]), sc_mlp[i],
                                                                f"mlp_norm_bwd_{i}", following)
        handle, token = _exchange_start([d_up, d_down.reshape(NDEV, FF // NDEV, D)], f"scatter_mlp{i}_start", True, dx)
        return dx, (dsh, dsc, dgate, dgain), handle, token, dy_next, dgate_next

    h0, inv0 = _rms_mod_fwd(x0, row(norm_mix[0]), sh_mix[0], sc_mix[0], "fox_norm")
    w_in_t, w_fox_out = finish_gather(relay_gather(fox_handle, "fox", h0), "fox", h0)
    mlp0_handle, token = start_gather("mlp0", w_in_t)
    w_in_t = w_in_t.reshape(3 * D + H, D)
    w_f_t = _pad_rows(w_in_t[3 * D:], LANES)
    w_fox_out = w_fox_out.reshape(D, D)
    column_scale = jnp.concatenate([jnp.ones((1, D), F32), jnp.full((1, D), KEY_SCALE, F32), jnp.ones((1, D), F32)], axis=1)
    qkv = _matmul(h0, w_in_t, mode="nt", name="fox_qkv", out_dtypes=[BF16], tm=1024, tn=1024, tk=D, n=3 * D, after=[token],
                  epilogue=lambda acc, mult: (acc * mult,), extras=[(column_scale, "row")])
    f_logit = _matmul(h0, w_f_t, mode="nt", name="fox_forget_logits", out_dtypes=[F32], tm=1024, tn=LANES, tk=D)
    f_cum = _forget_cumsum(f_logit, b_f, "fox_forget_cumsum")
    f_heads = f_cum[:, :H].T
    f_col, f_row = f_heads.reshape(H, S, 1), f_heads.reshape(H, 1, S)
    o, lse = _attn_fwd(qkv, f_col, f_row, "fox_attention")
    mlp0_handle = relay_gather(mlp0_handle, "mlp0", o)
    conv_handle, token = start_gather("conv", landed(mlp0_handle))
    mlp1_handle, token = start_gather("mlp1", token)
    x1, mix0 = _matmul(o, w_fox_out, mode="nn", name="fox_out", out_dtypes=[F32, BF16], tm=512, tn=1024, tk=D,
                       epilogue=residual, extras=[(x0, "tile"), (g_mix[0], "row")], after=[token])
    x2, mlp0, conv_handle = mlp_fwd(0, x1, mlp0_handle, ("conv", conv_handle))

    h1, inv1 = _rms_mod_fwd(x2, row(norm_mix[1]), sh_mix[1], sc_mix[1], "conv_norm")
    w_conv_in_t, w_taps, w_conv_out = finish_gather(conv_handle, "conv", h1)
    w_conv_in_t = w_conv_in_t.reshape(3 * D, D)
    w_taps = _pad_rows(w_taps.transpose(1, 0, 2).reshape(CONV_WIDTH, D), 8)
    w_conv_out = w_conv_out.reshape(D, D)
    proj = _matmul(h1, w_conv_in_t, mode="nt", name="conv_in", out_dtypes=[BF16], tm=1024, tn=1024, tk=D)
    mlp1_handle = relay_gather(mlp1_handle, "mlp1", proj)
    z = _conv_fwd(proj, w_taps, "conv_mix")
    x3, mix1 = _matmul(z, w_conv_out, mode="nn", name="conv_out", out_dtypes=[F32, BF16], tm=512, tn=1024, tk=D,
                       epilogue=residual, extras=[(x2, "tile"), (g_mix[1], "row")], after=[landed(mlp1_handle)])
    x4, mlp1, _ = mlp_fwd(1, x3, mlp1_handle)

    dx, d_final, loss_lanes, dy, dgate = _final_loss_bwd(x4, target, row(final_norm), (mlp1[4], g_mlp[1]), "loss_head")

    dx, dmod_mlp1, mlp1_scatter, token, dmix, dg_mix1 = mlp_bwd(1, dx, dy, dgate, mlp1, (mix1, g_mix[1]), [])
    dz = _matmul(dmix, w_conv_out, mode="nt", name="conv_out_bwd", out_dtypes=[BF16], tm=1024, tn=1024, tk=D,
                 after=[token])
    d_conv_out = _matmul(z, dmix, mode="tn", name="conv_out_wgrad", out_dtypes=[BF16], tm=512, tn=1024, tk=S)
    db, dc, du, d_taps = _conv_bwd(proj, dz, w_taps, "conv_mix_bwd")
    dproj = jnp.concatenate([db, dc, du], axis=1)
    dh1 = _matmul(dproj, w_conv_in_t, mode="nn", name="conv_in_bwd", out_dtypes=[F32], tm=512, tn=512, tk=3 * D, n_outer=True)
    d_conv_in = _matmul(h1, dproj, mode="tn", name="conv_in_wgrad", out_dtypes=[BF16], tm=512, tn=3 * D // NDEV, tk=S,
                        out_shards=True)
    dx, dsh1, dsc1, dgain_mix1, dy, dgate = _rms_mod_bwd(dh1, x2, inv1, dx, row(norm_mix[1]), sc_mix[1], "conv_norm_bwd",
                                                         (mlp0[4], g_mlp[0]))
    d_taps_split = d_taps[:CONV_WIDTH].reshape(CONV_WIDTH, NDEV, -1).transpose(1, 0, 2)
    conv_scatter, token = _exchange_start([d_conv_in, d_taps_split, d_conv_out.reshape(NDEV, D // NDEV, D)],
                                          "scatter_conv_start", True, dx)

    dx, dmod_mlp0, mlp0_scatter, token, dmix, dg_mix0 = mlp_bwd(0, dx, dy, dgate, mlp0, (mix0, g_mix[0]), [token])
    do = _matmul(dmix, w_fox_out, mode="nt", name="fox_out_bwd", out_dtypes=[BF16], tm=1024, tn=1024, tk=D,
                 after=[token])
    d_fox_out = _matmul(o, dmix, mode="tn", name="fox_out_wgrad", out_dtypes=[BF16], tm=512, tn=1024, tk=S)
    fox_out_scatter, token = _exchange_start([d_fox_out.reshape(NDEV, D // NDEV, D)], "scatter_fox_out_start", True, d_fox_out)
    dq, dk, dv, dfk = _attn_bwd(qkv, do, f_col, tied(f_row, token), lse, "fox_attention_bwd")
    dqkv = [dq, dk, dv]
    dfk_lanes = _pad_cols(dfk.reshape(H, S).T, LANES)
    df_logit, db_f = _forget_bwd(dfk_lanes, f_logit, b_f, "fox_forget_bwd")
    d_qkv_t = _matmul(dqkv, h0, mode="tn", name="fox_qkv_wgrad", out_dtypes=[BF16], tm=512, tn=512, tk=S)
    d_f_t = _matmul(df_logit, h0, mode="tn", name="fox_forget_wgrad", out_dtypes=[BF16], tm=LANES, tn=1024, tk=S)
    d_fox_in = jnp.concatenate([d_qkv_t, d_f_t[:H]], axis=0).reshape(NDEV, -1, D)
    fox_scatter, token = _exchange_start([d_fox_in], "scatter_fox_start", True, d_fox_in)
    dh0_f = _matmul(df_logit, w_f_t, mode="nn", name="fox_forget_logits_bwd", out_dtypes=[F32], tm=1024, tn=1024, tk=LANES,
                    after=[token])
    dh0 = _matmul(dqkv, w_in_t, mode="nn", name="fox_qkv_bwd", out_dtypes=[F32], tm=512, tn=512, tk=3 * D, n_outer=True,
                  epilogue=lambda acc, extra: (acc + extra,), extras=[(dh0_f, "tile")])
    dx, dsh0, dsc0, dgain_mix0 = _rms_mod_bwd(dh0, x0, inv0, dx, row(norm_mix[0]), sc_mix[0], "fox_norm_bwd")
    grad_x = dx.reshape(1, S, D)

    dmod = jnp.concatenate([
        jnp.concatenate([dsh0, dsc0, dg_mix0, dmod_mlp0[0], dmod_mlp0[1], dmod_mlp0[2]], axis=1),
        jnp.concatenate([dsh1, dsc1, dg_mix1, dmod_mlp1[0], dmod_mlp1[1], dmod_mlp1[2]], axis=1)], axis=0)
    small_sizes = [depth * n_mod * D, depth * D, depth * D, H, D, 1]
    n_small = sum(small_sizes)
    n_rows = -(-n_small // (8 * LANES)) * 8

    def pack(parts):
        flat = jnp.concatenate([p.reshape(-1) for p in parts])
        return jnp.pad(flat, (0, n_rows * LANES - n_small)).reshape(n_rows, LANES)

    def unpack(packed, shapes):
        flat, out, at = packed.reshape(-1), [], 0
        for size, shape in zip(small_sizes, shapes):
            out.append(flat[at:at + size].reshape(shape))
            at += size
        return out

    small_partial = pack([dmod, jnp.concatenate([dgain_mix0, dgain_mix1], axis=0),
                          jnp.concatenate([dmod_mlp0[3], dmod_mlp1[3]], axis=0), db_f[0, :H], d_final, loss_lanes[0, :1]])
    small_handle, token = _exchange_start([small_partial], "gather_small_start", False, dx)

    up1, down1 = _exchange_wait(mlp1_scatter, "scatter_mlp1_wait", token)
    up_out = _adamw(mlp_w_up, m_mlp_w_up, v_mlp_w_up, up1, "adamw_mlp_w_up_1", layer=1)
    down_out = _adamw(mlp_w_down, m_mlp_w_down, v_mlp_w_down, down1, "adamw_mlp_w_down_1", layer=1)
    cin, taps, cout = _exchange_wait(conv_scatter, "scatter_conv_wait", down_out[0])
    conv_in_out = _adamw(conv_w_in, m_conv_w_in, v_conv_w_in, cin, "adamw_conv_w_in")
    conv_w_res = _adamw(conv_w, m_conv_w, v_conv_w, taps, "adamw_conv_w")
    conv_out_out = _adamw(conv_w_out, m_conv_w_out, v_conv_w_out, cout, "adamw_conv_w_out")
    up0, down0 = _exchange_wait(mlp0_scatter, "scatter_mlp0_wait", conv_out_out[0])
    up_out = _adamw(mlp_w_up, m_mlp_w_up, v_mlp_w_up, up0, "adamw_mlp_w_up_0", layer=0, prev=up_out)
    down_out = _adamw(mlp_w_down, m_mlp_w_down, v_mlp_w_down, down0, "adamw_mlp_w_down_0", layer=0, prev=down_out)

    small_parts = _exchange_wait(small_handle, "gather_small_wait", down_out[0])[0]
    small_shapes = [ada_b.shape, norm_mix.shape, norm_mlp.shape, fox_b_f.shape, final_norm.shape]
    loss = jnp.sum(small_parts.reshape(NDEV, -1)[:, n_small - 1])
    unused = jnp.zeros((1,), F32)
    small_out = _adamw(pack([ada_b, norm_mix, norm_mlp, fox_b_f, final_norm, unused])[None],
                       pack([m_ada_b, m_norm_mix, m_norm_mlp, m_fox_b_f, m_final_norm, unused])[None],
                       pack([v_ada_b, v_norm_mix, v_norm_mlp, v_fox_b_f, v_final_norm, unused])[None], small_parts,
                       "adamw_small")
    small_out = [unpack(t, small_shapes) for t in small_out]

    dmod_all = small_parts.reshape(NDEV, -1)[:, :depth * n_mod * D].reshape(NDEV, depth, n_mod * D)
    dmod_mine = lax.dynamic_slice_in_dim(dmod_all, me * ncol, ncol, axis=2)
    ada_out = None
    for i in range(depth):
        d_ada = _matmul(c_all, dmod_mine[:, i], mode="tn", name=f"ada_wgrad_{i}", out_dtypes=[F32], tm=1024, tn=ncol // 2,
                        tk=NDEV, a_pre=_silu, precision=HIGHEST)
        ada_out = _adamw(ada_w, m_ada_w, v_ada_w, d_ada[None], f"adamw_ada_w_{i}", layer=i, prev=ada_out)

    (fout,) = _exchange_wait(fox_out_scatter, "scatter_fox_out_wait", ada_out[0])
    fox_out_out = _adamw(fox_w_out, m_fox_w_out, v_fox_w_out, fout, "adamw_fox_w_out")
    (fin,) = _exchange_wait(fox_scatter, "scatter_fox_wait", fox_out_out[0])
    swap = lambda t: jnp.swapaxes(t, 1, 2)
    fox_in_out = [swap(t) for t in _adamw(swap(fox_w_in), swap(m_fox_w_in), swap(v_fox_w_in), fin, "adamw_fox_w_in")]

    outputs = [loss, grad_x]
    for kind in range(4):
        sm = small_out[kind]
        outputs += [ada_out[kind], sm[0], sm[1], sm[2], fox_in_out[kind], sm[3], fox_out_out[kind], conv_in_out[kind],
                    conv_w_res[kind], conv_out_out[kind], up_out[kind], down_out[kind], sm[4]]
    return tuple(outputs)
```

```python
import math

import jax
import jax.numpy as jnp
from jax import lax
from jax.experimental import pallas as pl
from jax.experimental.pallas import tpu as pltpu

F32 = jnp.float32
BF16 = jnp.bfloat16
MESH = pl.DeviceIdType.MESH
NDEV = 8
HEAD_DIM = 128
LANES = 128
CONV_WIDTH = 3
RMS_EPS = 1e-6
ADAM_LR, ADAM_B1, ADAM_B2, ADAM_EPS, ADAM_WD, ADAM_STEP = 0.001, 0.9, 0.999, 1e-08, 0.01, 10
NEG = -1e30
V7X_VMEM_BYTES = 64 * 1024 * 1024
VMEM_HEADROOM = 12 * 1024 * 1024
HBM = pl.BlockSpec(memory_space=pltpu.HBM)
HIGHEST = lax.Precision.HIGHEST


def _nbytes(shape, dtype):
    return math.prod(shape) * jnp.dtype(dtype).itemsize


def _params(semantics, block_bytes, temp_bytes=0):
    limit = min(2 * block_bytes + temp_bytes + VMEM_HEADROOM, V7X_VMEM_BYTES - 4 * 1024 * 1024)
    return pltpu.CompilerParams(dimension_semantics=semantics, vmem_limit_bytes=int(limit))


def _my_index():
    return lax.axis_index("x") * 4 + lax.axis_index("y") * 2 + lax.axis_index("c")


def _peer(r):
    x, y, c = lax.axis_index("x"), lax.axis_index("y"), lax.axis_index("c")
    px = 1 - x if (r >> 2) & 1 else x
    py = 1 - y if (r >> 1) & 1 else y
    pc = 1 - c if r & 1 else c
    return (px, py, pc), px * 4 + py * 2 + pc


def _exchange(arrays, name, scatter, after=None):
    n = len(arrays)
    after = [] if after is None else list(after)

    def body(*refs):
        ins, outs = refs[:n], refs[n + len(after):2 * n + len(after)]
        send_sems, recv_sems, local_sems = refs[2 * n + len(after):]
        me = _my_index()
        local = []
        for a in range(n):
            src = ins[a].at[me] if scatter else ins[a]
            local.append(pltpu.make_async_copy(src, outs[a].at[me], local_sems.at[a]))
            local[-1].start()
        sends = []
        for r in range(1, NDEV):
            peer, pidx = _peer(r)
            for a in range(n):
                src = ins[a].at[pidx] if scatter else ins[a]
                cp = pltpu.make_async_remote_copy(
                    src_ref=src, dst_ref=outs[a].at[me],
                    send_sem=send_sems.at[a * (NDEV - 1) + r - 1], recv_sem=recv_sems.at[a * (NDEV - 1) + r - 1],
                    device_id=peer, device_id_type=MESH)
                cp.start()
                sends.append(cp)
        for r in range(1, NDEV):
            peer, pidx = _peer(r)
            for a in range(n):
                src = ins[a].at[pidx] if scatter else ins[a]
                pltpu.make_async_remote_copy(
                    src_ref=src, dst_ref=outs[a].at[pidx],
                    send_sem=send_sems.at[a * (NDEV - 1) + r - 1], recv_sem=recv_sems.at[a * (NDEV - 1) + r - 1],
                    device_id=peer, device_id_type=MESH).wait_recv()
        for cp in sends:
            cp.wait_send()
        for cp in local:
            cp.wait()

    out_shape = [jax.ShapeDtypeStruct(a.shape if scatter else (NDEV,) + a.shape, a.dtype) for a in arrays]
    return pl.pallas_call(
        body, name=name, out_shape=out_shape, in_specs=[HBM] * n + [ANY] * len(after), out_specs=[HBM] * n,
        scratch_shapes=[pltpu.SemaphoreType.DMA((n * (NDEV - 1),)), pltpu.SemaphoreType.DMA((n * (NDEV - 1),)),
                        pltpu.SemaphoreType.DMA((n,))],
    )(*arrays, *after)


def _all_gather(arrays, name, after=None):
    return _exchange(arrays, name, scatter=False, after=after)


SEM = pl.BlockSpec(memory_space=pltpu.SEMAPHORE)
ANY = pl.BlockSpec(memory_space=pl.ANY)
DATAFLOW = pltpu.SideEffectType.DATAFLOW_SIDE_EFFECTING
TOKEN_SHAPE = (8, LANES)


SIBLING = 1
OTHER_CHIPS = (4, 2, 6)


def _exchange_start(arrays, name, scatter, after, relay=False):
    n = len(arrays)
    n_sems = n * (NDEV - 1)
    assert not (relay and scatter)

    def body(*refs):
        ins = refs[:n]
        send_sems, recv_sems = refs[n + 1], refs[n + 2]
        lands, token = refs[2 * n + 3:3 * n + 3], refs[3 * n + 3]
        me = _my_index()
        for r in (SIBLING, *OTHER_CHIPS) if relay else range(1, NDEV):
            peer, pidx = _peer(r)
            for a in range(n):
                src = ins[a].at[pidx] if scatter else ins[a]
                pltpu.make_async_remote_copy(
                    src_ref=src, dst_ref=lands[a].at[me],
                    send_sem=send_sems.at[a * (NDEV - 1) + r - 1], recv_sem=recv_sems.at[a * (NDEV - 1) + r - 1],
                    device_id=peer, device_id_type=MESH).start()
        token[...] = jnp.zeros(TOKEN_SHAPE, F32)

    land_shapes = [a.shape if scatter else (NDEV,) + a.shape for a in arrays]
    srcs = [pltpu.with_memory_space_constraint(a, pltpu.HBM) for a in arrays]
    outs = pl.pallas_call(
        body, name=name,
        out_shape=(pltpu.SemaphoreType.DMA((n_sems,)), pltpu.SemaphoreType.DMA((n_sems,)),
                   *[pltpu.HBM(a.shape, a.dtype) for a in arrays], *[pltpu.HBM(s, a.dtype) for s, a in zip(land_shapes, arrays)],
                   jax.ShapeDtypeStruct(TOKEN_SHAPE, F32)),
        in_specs=[HBM] * n + [ANY],
        out_specs=(SEM, SEM, *[HBM] * (2 * n), pl.BlockSpec(memory_space=pltpu.VMEM)),
        input_output_aliases={i: 2 + i for i in range(n)},
        compiler_params=pltpu.CompilerParams(has_side_effects=DATAFLOW),
    )(*srcs, after)
    return (scatter, relay, [(outs[0], outs[1])], list(outs[2:2 + n]), list(outs[2 + n:2 + 2 * n])), outs[-1]


def _relay_forward_start(handle, name):
    scatter, relay, sems, srcs, lands = handle
    n = len(lands)
    n_sems = n * len(OTHER_CHIPS)

    def body(*refs):
        land_refs, send_sems, recv_sems = refs[:n], refs[n], refs[n + 1]
        sibling, _ = _peer(SIBLING)
        for j, r in enumerate(OTHER_CHIPS):
            _, pidx = _peer(r)
            for a in range(n):
                pltpu.make_async_remote_copy(
                    src_ref=land_refs[a].at[pidx], dst_ref=land_refs[a].at[pidx],
                    send_sem=send_sems.at[a * len(OTHER_CHIPS) + j], recv_sem=recv_sems.at[a * len(OTHER_CHIPS) + j],
                    device_id=sibling, device_id_type=MESH).start()

    outs = pl.pallas_call(
        body, name=name,
        out_shape=(pltpu.SemaphoreType.DMA((n_sems,)), pltpu.SemaphoreType.DMA((n_sems,)),
                   *[pltpu.HBM(t.shape, t.dtype) for t in lands]),
        in_specs=[HBM] * n, out_specs=(SEM, SEM, *[HBM] * n),
        input_output_aliases={i: 2 + i for i in range(n)},
        compiler_params=pltpu.CompilerParams(has_side_effects=DATAFLOW),
    )(*lands)
    return (scatter, relay, sems + [(outs[0], outs[1])], srcs, list(outs[2:]))


def _exchange_wait(handle, name, after, arrivals_only=False):
    scatter, relay, sems, srcs, lands = handle
    n = len(srcs)
    forwarded = len(sems) == 2
    assert not arrivals_only or (relay and not forwarded)

    def body(*refs):
        src_refs, land_refs = refs[:n], refs[n:2 * n]
        send_sems, recv_sems = refs[2 * n], refs[2 * n + 1]
        for r in (SIBLING, *OTHER_CHIPS) if relay else range(1, NDEV):
            peer, pidx = _peer(r)
            for a in range(n):
                src = src_refs[a].at[pidx] if scatter else src_refs[a]
                cp = pltpu.make_async_remote_copy(
                    src_ref=src, dst_ref=land_refs[a].at[pidx],
                    send_sem=send_sems.at[a * (NDEV - 1) + r - 1], recv_sem=recv_sems.at[a * (NDEV - 1) + r - 1],
                    device_id=peer, device_id_type=MESH)
                if arrivals_only:
                    if r in OTHER_CHIPS:
                        cp.wait_recv()
                else:
                    cp.wait_send()
                    if not (relay and r in OTHER_CHIPS):
                        cp.wait_recv()
        if forwarded:
            fwd_send, fwd_recv = refs[2 * n + 2], refs[2 * n + 3]
            sibling, _ = _peer(SIBLING)
            for j, r in enumerate(OTHER_CHIPS):
                _, pidx = _peer(r ^ SIBLING)
                for a in range(n):
                    cp = pltpu.make_async_remote_copy(
                        src_ref=src_refs[a], dst_ref=land_refs[a].at[pidx],
                        send_sem=fwd_send.at[a * len(OTHER_CHIPS) + j], recv_sem=fwd_recv.at[a * len(OTHER_CHIPS) + j],
                        device_id=sibling, device_id_type=MESH)
                    cp.wait_send()
                    cp.wait_recv()

    flat_sems = [s for pair in sems for s in pair]
    outs = pl.pallas_call(
        body, name=name,
        out_shape=tuple(pltpu.HBM(t.shape, t.dtype) for t in (*srcs, *lands)),
        in_specs=[HBM] * (2 * n) + [SEM] * len(flat_sems) + [ANY], out_specs=tuple([HBM] * (2 * n)),
        input_output_aliases={i: i for i in range(2 * n)},
        compiler_params=pltpu.CompilerParams(has_side_effects=DATAFLOW),
    )(*srcs, *lands, *flat_sems, after)
    if arrivals_only:
        return (scatter, relay, sems, list(outs[:n]), list(outs[n:]))
    me = _my_index()
    mine = [lax.dynamic_index_in_dim(s, me, 0, keepdims=False) if scatter else s for s in outs[:n]]
    return [lax.dynamic_update_index_in_dim(land, own, me, 0) for land, own in zip(outs[n:], mine)]


def _matmul(a, b, *, mode, name, out_dtypes, tm, tn, tk, epilogue=None, extras=(), a_pre=None,
            out_shards=False, n_outer=False, precision=None, after=(), n=None, b_first_block=0):
    after = list(after)
    n_after = len(after)
    parts = list(a) if isinstance(a, (list, tuple)) else [a]
    n_parts = len(parts)
    rows, cols = parts[0].shape
    K, M = (rows, cols * n_parts) if mode == "tn" else (cols * n_parts, rows)
    N = n if n is not None else (b.shape[0] if mode == "nt" else b.shape[1])
    tm, tn, tk = min(tm, M), min(tn, N), min(tk, K)
    assert M % tm == 0 and N % tn == 0 and K % tk == 0, (name, M, N, K, tm, tn, tk)
    nm, nn, nk = M // tm, N // tn, K // tk
    assert n_parts == 1 or (nk == 1 and a_pre is None and mode in ("nn", "tn") and cols % tm == 0), name
    blocks_per_part = cols // tm if mode == "tn" else 1
    n_out, n_ext = len(out_dtypes), len(extras)
    contract = {"nn": ((1,), (0,)), "nt": ((1,), (1,)), "tn": ((0,), (0,))}[mode]

    def body(*refs):
        a_refs, b_ref = refs[:n_parts], refs[n_parts]
        ext_refs = refs[n_parts + 1:n_parts + 1 + n_ext]
        first_out = n_parts + 1 + n_ext + n_after
        out_refs = refs[first_out:first_out + n_out]
        acc_ref = refs[first_out + n_out] if nk > 1 else None

        def product(a_ref, bv):
            av = a_ref[...] if a_pre is None else a_pre(a_ref[...])
            if precision is None:
                av, bv = av.astype(BF16), bv.astype(BF16)
            return lax.dot_general(av, bv, (contract, ((), ())), preferred_element_type=F32, precision=precision)

        def finish(acc):
            vals = (acc,) if epilogue is None else epilogue(acc, *[r[...] for r in ext_refs])
            for r, v in zip(out_refs, vals):
                r[...] = v.astype(r.dtype)

        if n_parts > 1 and mode == "tn":
            i = pl.program_id(1 if n_outer else 0)
            for p in range(n_parts):
                @pl.when(i // blocks_per_part == p)
                def _(p=p):
                    finish(product(a_refs[p], b_ref[...]))
            return
        part = product(a_refs[0], b_ref[...] if n_parts == 1 else b_ref[0:cols, :])
        for p in range(1, n_parts):
            part = part + product(a_refs[p], b_ref[p * cols:(p + 1) * cols, :])

        if nk == 1:
            finish(part)
        else:
            k = pl.program_id(2)

            @pl.when(k == 0)
            def _():
                acc_ref[...] = part

            @pl.when(k > 0)
            def _():
                acc_ref[...] += part

            @pl.when(k == nk - 1)
            def _():
                finish(acc_ref[...])

    def at(index):
        return (lambda j, i, k: index(i, j, k)) if n_outer else index

    if n_parts == 1:
        a_specs = [pl.BlockSpec((tk, tm), at(lambda i, j, k: (k, i))) if mode == "tn"
                   else pl.BlockSpec((tm, tk), at(lambda i, j, k: (i, k)))]
    elif mode == "tn":
        a_specs = [pl.BlockSpec((tk, tm), at(lambda i, j, k, p=p: (k, jnp.clip(i - p * blocks_per_part, 0, blocks_per_part - 1))))
                   for p in range(n_parts)]
    else:
        a_specs = [pl.BlockSpec((tm, cols), at(lambda i, j, k: (i, 0))) for _ in parts]
    b_spec = (pl.BlockSpec((tn, tk), at(lambda i, j, k: (j, k))) if mode == "nt"
              else pl.BlockSpec((tk, tn), at(lambda i, j, k: (k + b_first_block, j))))
    in_specs = a_specs + [b_spec]
    block_bytes = _nbytes((tm, tk), parts[0].dtype) * (n_parts if mode == "tn" else 1) + _nbytes((tk, tn), b.dtype)
    for arr, kind in extras:
        if kind == "tile":
            assert arr.shape == (M, N), (name, arr.shape)
            in_specs.append(pl.BlockSpec((tm, tn), at(lambda i, j, k: (i, j))))
            block_bytes += _nbytes((tm, tn), arr.dtype)
        else:
            assert arr.shape == (1, N), (name, arr.shape)
            in_specs.append(pl.BlockSpec((1, tn), at(lambda i, j, k: (0, j))))
    in_specs += [ANY] * n_after
    if out_shards:
        assert n_out == 1 and tn * NDEV == N
        out_shape = [jax.ShapeDtypeStruct((NDEV, M, tn), out_dtypes[0])]
        out_specs = [pl.BlockSpec((None, tm, tn), at(lambda i, j, k: (j, i, 0)))]
    else:
        out_shape = [jax.ShapeDtypeStruct((M, N), d) for d in out_dtypes]
        out_specs = [pl.BlockSpec((tm, tn), at(lambda i, j, k: (i, j))) for _ in out_dtypes]
    block_bytes += sum(_nbytes((tm, tn), d) for d in out_dtypes)
    scratch = [pltpu.VMEM((tm, tn), F32)] if nk > 1 else []
    outs = pl.pallas_call(
        body, name=name, grid=(nn, nm, nk) if n_outer else (nm, nn, nk), in_specs=in_specs, out_specs=out_specs,
        out_shape=out_shape, scratch_shapes=scratch,
        compiler_params=_params(("parallel", "parallel", "arbitrary"), block_bytes, 2 * tm * tn * 4),
    )(*parts, b, *[arr for arr, _ in extras], *after)
    return outs[0] if n_out == 1 else outs


def _rowwise(fn, tiled, smalls, out_tiles, out_sums, *, name, ts=256):
    S = tiled[0].shape[0]
    ts = min(ts, S)
    assert S % ts == 0
    nt, ns, no, na = len(tiled), len(smalls), len(out_tiles), len(out_sums)

    def body(*refs):
        t_refs, s_refs = refs[:nt], refs[nt:nt + ns]
        o_refs, a_refs = refs[nt + ns:nt + ns + no], refs[nt + ns + no:]
        tile_vals, sum_vals = fn([r[...] for r in t_refs], [r[...] for r in s_refs])
        for r, v in zip(o_refs, tile_vals):
            r[...] = v.astype(r.dtype)

        @pl.when(pl.program_id(0) == 0)
        def _():
            for r in a_refs:
                r[...] = jnp.zeros_like(r)

        for r, v in zip(a_refs, sum_vals):
            r[...] += v

    in_specs = [pl.BlockSpec((ts, t.shape[1]), lambda i: (i, 0)) for t in tiled]
    in_specs += [pl.BlockSpec(s.shape, lambda i: (0, 0)) for s in smalls]
    out_specs = [pl.BlockSpec((ts, w), lambda i: (i, 0)) for w, _ in out_tiles]
    out_specs += [pl.BlockSpec((1, w), lambda i: (0, 0)) for w in out_sums]
    out_shape = [jax.ShapeDtypeStruct((S, w), d) for w, d in out_tiles]
    out_shape += [jax.ShapeDtypeStruct((1, w), F32) for w in out_sums]
    block_bytes = sum(_nbytes((ts, t.shape[1]), t.dtype) for t in tiled) + sum(_nbytes((ts, w), d) for w, d in out_tiles)
    width = max(t.shape[1] for t in tiled)
    outs = pl.pallas_call(
        body, name=name, grid=(S // ts,), in_specs=in_specs, out_specs=out_specs, out_shape=out_shape,
        compiler_params=_params(("arbitrary",), block_bytes, 6 * ts * width * 4),
    )(*tiled, *smalls)
    return outs[:no], outs[no:]


def _colsum(v):
    return jnp.sum(v, axis=0, keepdims=True)


def _rms_mod_fwd(x, gain, shift, scale, name):
    def fn(tiles, smalls):
        (xv,), (g, sh, sc) = tiles, smalls
        inv = lax.rsqrt(jnp.mean(xv * xv, axis=-1, keepdims=True) + RMS_EPS)
        h = (xv * inv) * g * (1.0 + sc) + sh
        return (h, inv), ()

    D = x.shape[1]
    (h, inv), _ = _rowwise(fn, [x], [gain, shift, scale], [(D, BF16), (1, F32)], [], name=name)
    return h, inv


def _gated(dxv, following):
    yv, gate = following
    return dxv * gate, _colsum(dxv * yv)


def _rms_mod_bwd(dh, x, inv, dx_res, gain, scale, name, following=None):
    def fn(tiles, smalls):
        dhv, xv, iv, dres = tiles[:4]
        g, sc = smalls[:2]
        dhv = dhv.astype(F32)
        xhat = xv * iv
        dr = dhv * (1.0 + sc)
        dxhat = dr * g
        dxv = dres + iv * (dxhat - xhat * jnp.mean(dxhat * xhat, axis=-1, keepdims=True))
        sums = (_colsum(dhv), _colsum(dhv * (xhat * g)), _colsum(dr * xhat))
        if following is None:
            return (dxv,), sums
        dy, dgate = _gated(dxv, (tiles[4], smalls[2]))
        return (dxv, dy), (*sums, dgate)

    D = x.shape[1]
    extra = [] if following is None else [following]
    tiles, sums = _rowwise(fn, [dh, x, inv, dx_res] + [f[0] for f in extra], [gain, scale] + [f[1] for f in extra],
                           [(D, F32)] + [(D, BF16)] * len(extra), [D] * (3 + len(extra)), name=name)
    return (tiles[0], *sums[:3]) if following is None else (tiles[0], *sums[:3], tiles[1], sums[3])


def _final_loss_bwd(x, target, gain, following, name):
    D = x.shape[1]

    def fn(tiles, smalls):
        xv, tv, g = tiles[0], tiles[1], smalls[0]
        inv = lax.rsqrt(jnp.mean(xv * xv, axis=-1, keepdims=True) + RMS_EPS)
        xhat = xv * inv
        err = xhat * g - tv
        loss = 0.5 * jnp.sum(jnp.mean(err * err, axis=-1, keepdims=True), axis=0, keepdims=True)
        dout = err * (1.0 / D)
        dxhat = dout * g
        dxv = inv * (dxhat - xhat * jnp.mean(dxhat * xhat, axis=-1, keepdims=True))
        dy, dgate = _gated(dxv, (tiles[2], smalls[1]))
        return (dxv, dy), (_colsum(dout * xhat), jnp.broadcast_to(loss, (1, LANES)), dgate)

    (dx, dy), (dgain, loss, dgate) = _rowwise(fn, [x, target, following[0]], [gain, following[1]],
                                             [(D, F32), (D, BF16)], [D, LANES, D], name=name)
    return dx, dgain, loss, dy, dgate


SCAN_BLOCK = 256


def _triangle(n, lower):
    r = lax.broadcasted_iota(jnp.int32, (n, n), 0)
    c = lax.broadcasted_iota(jnp.int32, (n, n), 1)
    return (r >= c if lower else r <= c).astype(F32)


def _forget_cumsum(logits, bias, name):
    S = logits.shape[0]
    blk = min(SCAN_BLOCK, S)
    nb = S // blk

    def body(z_ref, b_ref, f_ref):
        z = z_ref[...] + b_ref[...]
        f_ref[...] = jnp.minimum(z, 0.0) - jnp.log(1.0 + jnp.exp(-jnp.abs(z)))
        tri = _triangle(blk, lower=True)

        def step(i, carry):
            off = pl.multiple_of(i * blk, blk)
            cs = jnp.dot(tri, f_ref[pl.ds(off, blk), :], preferred_element_type=F32, precision=HIGHEST) + carry
            f_ref[pl.ds(off, blk), :] = cs
            return cs[blk - 1:blk, :]

        lax.fori_loop(0, nb, step, jnp.zeros((1, LANES), F32))

    return pl.pallas_call(body, name=name, out_shape=jax.ShapeDtypeStruct((S, LANES), F32))(logits, bias)


def _forget_bwd(dfk, logits, bias, name):
    S = logits.shape[0]
    blk = min(SCAN_BLOCK, S)
    nb = S // blk

    def body(d_ref, z_ref, b_ref, o_ref, db_ref):
        tri = _triangle(blk, lower=False)

        def step(t, carry):
            off = pl.multiple_of((nb - 1 - t) * blk, blk)
            cs = jnp.dot(tri, d_ref[pl.ds(off, blk), :], preferred_element_type=F32, precision=HIGHEST) + carry
            o_ref[pl.ds(off, blk), :] = cs
            return cs[0:1, :]

        lax.fori_loop(0, nb, step, jnp.zeros((1, LANES), F32))
        z = z_ref[...] + b_ref[...]
        dz = -o_ref[...] / (1.0 + jnp.exp(z))
        o_ref[...] = dz
        db_ref[...] = _colsum(dz)

    return pl.pallas_call(
        body, name=name,
        out_shape=(jax.ShapeDtypeStruct((S, LANES), F32), jax.ShapeDtypeStruct((1, LANES), F32)),
    )(dfk, logits, bias)


KEY_SCALE = HEAD_DIM ** -0.5
ATTN_BLOCK = 512
_NT = (((1,), (1,)), ((), ()))


def _loop_in_pairs(step, count, init):
    carry = lax.fori_loop(0, count // 2, lambda t, c: step(2 * t + 1, step(2 * t, c)), init)
    return lax.fori_loop(count // 2 * 2, count, step, carry)


def _attn_specs(S, H, tb):
    q_blk = lambda part: pl.BlockSpec((tb, HEAD_DIM), lambda h, i: (i, part * H + h))
    q_all = lambda part: pl.BlockSpec((S, HEAD_DIM), lambda h, i: (0, part * H + h))
    col_blk = pl.BlockSpec((None, tb, 1), lambda h, i: (h, i, 0))
    row_all = pl.BlockSpec((None, 1, S), lambda h, i: (h, 0, 0))
    return q_blk, q_all, col_blk, row_all


FWD_HEADS = 2


def _attn_fwd(qkv, f_col, f_row, name):
    S, H = qkv.shape[0], qkv.shape[1] // (3 * HEAD_DIM)
    tb = min(ATTN_BLOCK, S)
    hp = FWD_HEADS if H % FWD_HEADS == 0 else 1
    groups, wide = H // hp, hp * HEAD_DIM
    lanes = lambda u: pl.ds(u * HEAD_DIM, HEAD_DIM)

    def body(q_ref, k_ref, v_ref, fc_ref, fr_ref, o_ref, lse_ref):
        i = pl.program_id(1)

        def step(j, carry, diagonal):
            off = pl.multiple_of(j * tb, tb)
            out = []
            for u in range(hp):
                m, l, acc = carry[u]
                k, v = k_ref[pl.ds(off, tb), lanes(u)], v_ref[pl.ds(off, tb), lanes(u)]
                s = lax.dot_general(q_ref[:, lanes(u)], k, _NT, preferred_element_type=F32)
                s = s + (fc_ref[u] - fr_ref[u, :, pl.ds(off, tb)])
                if diagonal:
                    row = lax.broadcasted_iota(jnp.int32, (tb, tb), 0)
                    col = lax.broadcasted_iota(jnp.int32, (tb, tb), 1)
                    s = jnp.where(col <= row, s, NEG)
                m_new = jnp.maximum(m, jnp.max(s, axis=-1, keepdims=True))
                p = jnp.exp(s - m_new)
                alpha = jnp.exp(m - m_new)
                l = alpha * l + jnp.sum(p, axis=-1, keepdims=True)
                acc = alpha * acc + jnp.dot(p.astype(BF16), v, preferred_element_type=F32)
                out.append((m_new, l, acc))
            return tuple(out)

        init = (jnp.full((tb, 1), NEG, F32), jnp.zeros((tb, 1), F32), jnp.zeros((tb, HEAD_DIM), F32))
        carry = _loop_in_pairs(lambda j, c: step(j, c, False), i, (init,) * hp)
        for u, (m, l, acc) in enumerate(step(i, carry, True)):
            o_ref[:, lanes(u)] = (acc / l).astype(o_ref.dtype)
            lse_ref[u] = m + jnp.log(l)

    part = lambda p, rows: pl.BlockSpec((rows, wide), lambda g, i: (i if rows == tb else 0, p * groups + g))
    col_blk = pl.BlockSpec((hp, tb, 1), lambda g, i: (g, i, 0))
    return pl.pallas_call(
        body, name=name, grid=(groups, S // tb),
        in_specs=[part(0, tb), part(1, S), part(2, S), col_blk, pl.BlockSpec((hp, 1, S), lambda g, i: (g, 0, 0))],
        out_specs=[pl.BlockSpec((tb, wide), lambda g, i: (i, g)), col_blk],
        out_shape=[jax.ShapeDtypeStruct((S, H * HEAD_DIM), BF16), jax.ShapeDtypeStruct((H, S, 1), F32)],
        compiler_params=_params(("parallel", "parallel"), 4 * S * wide * 2, 10 * hp * tb * tb * 4),
    )(qkv, qkv, qkv, f_col, f_row)


_TN = (((0,), (0,)), ((), ()))


def _attn_bwd(qkv, do, f_col, f_row, lse_col, name):
    S, H = qkv.shape[0], qkv.shape[1] // (3 * HEAD_DIM)
    tb = min(ATTN_BLOCK, S)
    nq = S // tb
    q_blk, q_all, col_blk, row_all = _attn_specs(S, H, tb)
    head_blk = pl.BlockSpec((tb, HEAD_DIM), lambda h, i: (i, h))
    head_all = pl.BlockSpec((S, HEAD_DIM), lambda h, i: (0, h))

    def body(q_ref, k_ref, v_ref, do_ref, fc_ref, fr_ref, lse_ref, dq_ref, dk_ref, dv_ref, dfk_ref,
             p_buf, dp_buf, dk_acc, dv_acc, dfk_acc):
        i = pl.program_id(1)
        q, do = q_ref[...], do_ref[...]
        fc_lse = fc_ref[...] - lse_ref[...]

        @pl.when(i == 0)
        def _():
            dk_acc[...] = jnp.zeros_like(dk_acc)
            dv_acc[...] = jnp.zeros_like(dv_acc)
            dfk_acc[...] = jnp.zeros_like(dfk_acc)

        def scores(j, delta, diagonal):
            off = pl.multiple_of(j * tb, tb)
            k, v = k_ref[pl.ds(off, tb), :], v_ref[pl.ds(off, tb), :]
            s = (lax.dot_general(q, k, _NT, preferred_element_type=F32) + fc_lse) - fr_ref[:, pl.ds(off, tb)]
            if diagonal:
                row = lax.broadcasted_iota(jnp.int32, (tb, tb), 0)
                col = lax.broadcasted_iota(jnp.int32, (tb, tb), 1)
                s = jnp.where(col <= row, s, NEG)
            p = jnp.exp(s)
            dp = lax.dot_general(do, v, _NT, preferred_element_type=F32)
            p_buf[j] = p
            dp_buf[j] = dp
            return delta + jnp.sum(p * dp, axis=-1, keepdims=True)

        delta = _loop_in_pairs(lambda j, c: scores(j, c, False), i, jnp.zeros((tb, 1), F32))
        delta = scores(i, delta, True)

        def grad(j, dq):
            off = pl.multiple_of(j * tb, tb)
            p = p_buf[j]
            ds = p * (dp_buf[j] - delta)
            ds_lo = ds.astype(BF16)
            dk_acc[pl.ds(off, tb), :] += lax.dot_general(ds_lo, q, _TN, preferred_element_type=F32)
            dv_acc[pl.ds(off, tb), :] += lax.dot_general(p.astype(BF16), do, _TN, preferred_element_type=F32)
            dfk_acc[:, pl.ds(off, tb)] += jnp.sum(ds, axis=0, keepdims=True)
            return dq + jnp.dot(ds_lo, k_ref[pl.ds(off, tb), :], preferred_element_type=F32)

        dq = _loop_in_pairs(grad, i + 1, jnp.zeros((tb, HEAD_DIM), F32))
        dq_ref[...] = dq.astype(dq_ref.dtype)

        @pl.when(i == nq - 1)
        def _():
            dk_ref[...] = (dk_acc[...] * KEY_SCALE).astype(dk_ref.dtype)
            dv_ref[...] = dv_acc[...].astype(dv_ref.dtype)
            dfk_ref[...] = dfk_acc[...]

    wide = jax.ShapeDtypeStruct((S, H * HEAD_DIM), BF16)
    return pl.pallas_call(
        body, name=name, grid=(H, nq),
        in_specs=[q_blk(0), q_all(1), q_all(2), head_blk, col_blk, row_all, col_blk],
        out_specs=[head_blk, head_all, head_all, row_all],
        out_shape=[wide, wide, wide, jax.ShapeDtypeStruct((H, 1, S), F32)],
        scratch_shapes=[pltpu.VMEM((nq, tb, tb), F32), pltpu.VMEM((nq, tb, tb), F32),
                        pltpu.VMEM((S, HEAD_DIM), F32), pltpu.VMEM((S, HEAD_DIM), F32), pltpu.VMEM((1, S), F32)],
        compiler_params=_params(("parallel", "arbitrary"), 6 * S * HEAD_DIM * 2,
                                2 * nq * tb * tb * 4 + 2 * S * HEAD_DIM * 4 + 10 * tb * tb * 4),
    )(qkv, qkv, qkv, do, f_col, f_row, lse_col)


CONV_TILE = 128


def _shift_down(v, n):
    row = lax.broadcasted_iota(jnp.int32, v.shape, 0)
    return jnp.where(row >= n, pltpu.roll(v, n, 0), 0.0)


def _shift_up(v, n):
    S = v.shape[0]
    row = lax.broadcasted_iota(jnp.int32, v.shape, 0)
    return jnp.where(row < S - n, pltpu.roll(v, S - n, 0), 0.0)


def _conv_specs(S, D, tc):
    nb = D // tc
    part = lambda p: pl.BlockSpec((S, tc), lambda j: (0, p * nb + j))
    return part, pl.BlockSpec((S, tc), lambda j: (0, j)), pl.BlockSpec((8, tc), lambda j: (0, j))


def _conv_fwd(proj, conv_w8, name):
    S, D = proj.shape[0], proj.shape[1] // 3
    tc = min(CONV_TILE, D)
    part, chan, taps = _conv_specs(S, D, tc)

    def body(b_ref, c_ref, u_ref, w_ref, z_ref):
        cu = c_ref[...].astype(F32) * u_ref[...].astype(F32)
        w = w_ref[...]
        y = w[0:1, :] * _shift_down(cu, 2) + w[1:2, :] * _shift_down(cu, 1) + w[2:3, :] * cu
        z_ref[...] = (b_ref[...].astype(F32) * y).astype(z_ref.dtype)

    return pl.pallas_call(
        body, name=name, grid=(D // tc,), in_specs=[part(0), part(1), part(2), taps], out_specs=chan,
        out_shape=jax.ShapeDtypeStruct((S, D), BF16),
        compiler_params=_params(("parallel",), 3 * _nbytes((S, tc), proj.dtype) + S * tc * 2, 6 * S * tc * 4),
    )(proj, proj, proj, conv_w8)


def _conv_bwd(proj, dz, conv_w8, name):
    S, D = proj.shape[0], proj.shape[1] // 3
    tc = min(CONV_TILE, D)
    part, chan, taps = _conv_specs(S, D, tc)

    def body(b_ref, c_ref, u_ref, dz_ref, w_ref, db_ref, dc_ref, du_ref, dw_ref):
        cv, uv = c_ref[...].astype(F32), u_ref[...].astype(F32)
        dzv, w = dz_ref[...].astype(F32), w_ref[...]
        cu = cv * uv
        cu1, cu2 = _shift_down(cu, 1), _shift_down(cu, 2)
        y = w[0:1, :] * cu2 + w[1:2, :] * cu1 + w[2:3, :] * cu
        db_ref[...] = (dzv * y).astype(db_ref.dtype)
        dy = dzv * b_ref[...].astype(F32)
        dcu = w[2:3, :] * dy + w[1:2, :] * _shift_up(dy, 1) + w[0:1, :] * _shift_up(dy, 2)
        dc_ref[...] = (dcu * uv).astype(dc_ref.dtype)
        du_ref[...] = (dcu * cv).astype(du_ref.dtype)
        dw_ref[...] = jnp.concatenate(
            [_colsum(dy * cu2), _colsum(dy * cu1), _colsum(dy * cu), jnp.zeros((8 - CONV_WIDTH, tc), F32)], axis=0)

    return pl.pallas_call(
        body, name=name, grid=(D // tc,), in_specs=[part(0), part(1), part(2), chan, taps],
        out_specs=[chan, chan, chan, taps],
        out_shape=[jax.ShapeDtypeStruct((S, D), BF16)] * 3 + [jax.ShapeDtypeStruct((8, D), F32)],
        compiler_params=_params(("parallel",), 3 * _nbytes((S, tc), proj.dtype) + _nbytes((S, tc), dz.dtype)
                                + 3 * S * tc * 2, 10 * S * tc * 4),
    )(proj, proj, proj, dz, conv_w8)


def _adamw(w, m, v, parts, name, layer=0, prev=None):
    L, R, C = w.shape
    P = parts.shape[0]
    assert parts.shape[1:] == (R, C), (name, parts.shape, w.shape)
    elem_bytes = 12 + 16 + P * parts.dtype.itemsize
    budget = 16 << 20
    tr, tc = R, C
    if R * C * elem_bytes > budget:
        if R % 8 == 0:
            tr = max(8, (budget // (C * elem_bytes)) // 8 * 8)
            while R % tr:
                tr -= 8
        else:
            tc = LANES
            while C % (2 * tc) == 0 and R * 2 * tc * elem_bytes <= budget:
                tc *= 2
            assert C % tc == 0, (name, R, C)
    c1, c2 = 1.0 - ADAM_B1 ** ADAM_STEP, 1.0 - ADAM_B2 ** ADAM_STEP

    def body(w_ref, m_ref, v_ref, p_ref, *rest):
        g_ref, d_ref, nm_ref, nv_ref = rest[-4:]
        g = p_ref[0].astype(F32)
        for p in range(1, P):
            g = g + p_ref[p].astype(F32)
        nm = ADAM_B1 * m_ref[...] + (1.0 - ADAM_B1) * g
        nv = ADAM_B2 * v_ref[...] + (1.0 - ADAM_B2) * (g * g)
        g_ref[...] = g
        nm_ref[...] = nm
        nv_ref[...] = nv
        d_ref[...] = -ADAM_LR * ((nm / c1) / (jnp.sqrt(nv / c2) + ADAM_EPS) + ADAM_WD * w_ref[...])

    blk = pl.BlockSpec((None, tr, tc), lambda i, j: (layer, i, j))
    prev = [] if prev is None else list(prev)
    return pl.pallas_call(
        body, name=name, grid=(R // tr, C // tc),
        in_specs=[blk, blk, blk, pl.BlockSpec((P, tr, tc), lambda i, j: (0, i, j))] + [ANY] * len(prev),
        out_specs=[blk] * 4, out_shape=[jax.ShapeDtypeStruct((L, R, C), F32)] * 4,
        input_output_aliases={4 + k: k for k in range(len(prev))},
        compiler_params=_params(("parallel", "parallel"), tr * tc * elem_bytes),
    )(w, m, v, parts, *prev)


def _silu(v):
    return v / (1.0 + jnp.exp(-v))


def _pad_rows(a, rows):
    return jnp.pad(a, ((0, rows - a.shape[0]), (0, 0)))


def _pad_cols(a, cols):
    return jnp.pad(a, ((0, 0), (0, cols - a.shape[1])))


def kernel(x, c, ada_w, ada_b, norm_mix, norm_mlp, fox_w_in, fox_b_f, fox_w_out, conv_w_in, conv_w, conv_w_out, mlp_w_up, mlp_w_down, final_norm, loss_target, m_ada_w, m_ada_b, m_norm_mix, m_norm_mlp, m_fox_w_in, m_fox_b_f, m_fox_w_out, m_conv_w_in, m_conv_w, m_conv_w_out, m_mlp_w_up, m_mlp_w_down, m_final_norm, v_ada_w, v_ada_b, v_norm_mix, v_norm_mlp, v_fox_w_in, v_fox_b_f, v_fox_w_out, v_conv_w_in, v_conv_w, v_conv_w_out, v_mlp_w_up, v_mlp_w_down, v_final_norm):
    S, D = x.shape[1], x.shape[2]
    H = D // HEAD_DIM
    FF = mlp_w_up.shape[2] * NDEV
    depth = ada_w.shape[0]
    n_mod = 6
    assert depth == 2 and fox_w_in.shape[0] == 1 and conv_w_in.shape[0] == 1 and H <= LANES
    me = _my_index()
    x0, target = x[0], loss_target[0]
    row = lambda vec: vec.reshape(1, -1)

    def tied(vec, token):
        return vec + token[0, 0]

    bf = lambda w: w.astype(BF16)
    gather_groups = {
        "fox": [bf(fox_w_in[0]).T],
        "fox_out": [bf(fox_w_out[0])],
        "mlp0": [bf(mlp_w_up[0]).T, bf(mlp_w_down[0])],
        "conv": [bf(conv_w_in[0]).T, conv_w[0], bf(conv_w_out[0])],
        "mlp1": [bf(mlp_w_up[1]).T, bf(mlp_w_down[1])],
    }

    def start_gather(group, after):
        return _exchange_start(gather_groups[group], f"gather_{group}_start", False, after, relay=True)

    def relay_gather(handle, group, after):
        handle = _exchange_wait(handle, f"gather_{group}_arrivals", after, arrivals_only=True)
        return _relay_forward_start(handle, f"gather_{group}_forward")

    def finish_gather(handle, group, after):
        return _exchange_wait(handle, f"gather_{group}_wait", after)

    landed = lambda handle: handle[4][0]

    c_all = _all_gather([c], "gather_cond")[0].reshape(NDEV, D)
    ncol = ada_w.shape[2]
    ada_b_mine = lax.dynamic_slice_in_dim(ada_b, me * ncol, ncol, axis=1)
    mod_cols = jnp.stack([
        _matmul(c_all, ada_w.reshape(depth * D, ncol), mode="nn", name=f"ada_fwd_{i}", out_dtypes=[F32], tm=NDEV,
                tn=ncol // 2, tk=D, b_first_block=i,
                a_pre=_silu, precision=HIGHEST, epilogue=lambda acc, b: (acc + b,), extras=[(ada_b_mine[i:i + 1], "row")])
        for i in range(depth)])
    mod_all = _all_gather([mod_cols], "gather_mod")[0]
    mod = lax.dynamic_index_in_dim(mod_all, me, axis=2, keepdims=False)
    fox_handle, token = start_gather("fox", mod_all)
    fox_out_handle, token = start_gather("fox_out", token)
    mod = tied(mod, token).transpose(1, 0, 2).reshape(depth, n_mod, 1, D)
    sh_mix, sc_mix, g_mix, sh_mlp, sc_mlp, g_mlp = (mod[:, k] for k in range(n_mod))
    b_f = _pad_cols(fox_b_f, LANES)

    def residual(acc, x_in, gate):
        return (x_in + gate * acc, acc)

    def mlp_fwd(i, x_in, handle, relay_next=None):
        h, inv = _rms_mod_fwd(x_in, row(norm_mlp[i]), sh_mlp[i], sc_mlp[i], f"mlp_norm_{i}")
        w_up_t, w_down = finish_gather(handle, f"mlp{i}", h)
        w_up_t, w_down = w_up_t.reshape(FF, D), w_down.reshape(FF, D)
        r = _matmul(h, w_up_t, mode="nt", name=f"mlp_up_{i}", out_dtypes=[BF16], tm=1024, tn=1024, tk=D,
                    epilogue=lambda acc: (jnp.maximum(acc, 0.0),))
        next_handle = relay_gather(relay_next[1], relay_next[0], r) if relay_next else None
        x_out, y = _matmul(r, w_down, mode="nn", name=f"mlp_down_{i}", out_dtypes=[F32, BF16], tm=512, tn=512, tk=FF,
                           n_outer=True, a_pre=jnp.square, epilogue=residual, extras=[(x_in, "tile"), (g_mlp[i], "row")],
                           after=[landed(next_handle)] if relay_next else [])
        return x_out, (x_in, h, inv, r, y, w_up_t, w_down), next_handle

    def mlp_bwd(i, dx, dy, dgate, saved, following, after):
        x_in, h, inv, r, y, w_up_t, w_down = saved
        du = _matmul(dy, w_down, mode="nt", name=f"mlp_down_bwd_{i}", out_dtypes=[BF16], tm=1024, tn=1024, tk=D,
                     epilogue=lambda acc, rv: (acc * (2.0 * rv.astype(F32)),), extras=[(r, "tile")], after=after)
        d_down = _matmul(r, dy, mode="tn", name=f"mlp_down_wgrad_{i}", out_dtypes=[BF16], tm=512, tn=1024, tk=S,
                         a_pre=jnp.square)
        dh = _matmul(du, w_up_t, mode="nn", name=f"mlp_up_bwd_{i}", out_dtypes=[F32], tm=512, tn=512, tk=FF, n_outer=True)
        d_up = _matmul(h, du, mode="tn", name=f"mlp_up_wgrad_{i}", out_dtypes=[BF16], tm=512, tn=FF // NDEV, tk=S,
                       out_shards=True)
        dx, dsh, dsc, dgain, dy_next, dgate_next = _rms_mod_bwd(dh, x_in, inv, dx, row(norm_mlp[i]), sc_mlp[i],
                                                                f"mlp_norm_bwd_{i}", following)
        handle, token = _exchange_start([d_up, d_down.reshape(NDEV, FF // NDEV, D)], f"scatter_mlp{i}_start", True, dx)
        return dx, (dsh, dsc, dgate, dgain), handle, token, dy_next, dgate_next

    h0, inv0 = _rms_mod_fwd(x0, row(norm_mix[0]), sh_mix[0], sc_mix[0], "fox_norm")
    (w_in_t,) = finish_gather(relay_gather(fox_handle, "fox", h0), "fox", h0)
    mlp0_handle, token = start_gather("mlp0", w_in_t)
    w_in_t = w_in_t.reshape(3 * D + H, D)
    w_f_t = _pad_rows(w_in_t[3 * D:], LANES)
    column_scale = jnp.concatenate([jnp.ones((1, D), F32), jnp.full((1, D), KEY_SCALE, F32), jnp.ones((1, D), F32)], axis=1)
    qkv = _matmul(h0, w_in_t, mode="nt", name="fox_qkv", out_dtypes=[BF16], tm=1024, tn=1024, tk=D, n=3 * D, after=[token],
                  epilogue=lambda acc, mult: (acc * mult,), extras=[(column_scale, "row")])
    fox_out_handle = relay_gather(fox_out_handle, "fox_out", qkv)
    f_logit = _matmul(h0, w_f_t, mode="nt", name="fox_forget_logits", out_dtypes=[F32], tm=1024, tn=LANES, tk=D,
                      after=[landed(fox_out_handle)])
    f_cum = _forget_cumsum(f_logit, b_f, "fox_forget_cumsum")
    f_heads = f_cum[:, :H].T
    f_col, f_row = f_heads.reshape(H, S, 1), f_heads.reshape(H, 1, S)
    o, lse = _attn_fwd(qkv, f_col, f_row, "fox_attention")
    w_fox_out = finish_gather(fox_out_handle, "fox_out", o)[0].reshape(D, D)
    mlp0_handle = relay_gather(mlp0_handle, "mlp0", o)
    conv_handle, token = start_gather("conv", landed(mlp0_handle))
    mlp1_handle, token = start_gather("mlp1", token)
    x1, mix0 = _matmul(o, w_fox_out, mode="nn", name="fox_out", out_dtypes=[F32, BF16], tm=512, tn=1024, tk=D,
                       epilogue=residual, extras=[(x0, "tile"), (g_mix[0], "row")], after=[token])
    x2, mlp0, conv_handle = mlp_fwd(0, x1, mlp0_handle, ("conv", conv_handle))

    h1, inv1 = _rms_mod_fwd(x2, row(norm_mix[1]), sh_mix[1], sc_mix[1], "conv_norm")
    w_conv_in_t, w_taps, w_conv_out = finish_gather(conv_handle, "conv", h1)
    w_conv_in_t = w_conv_in_t.reshape(3 * D, D)
    w_taps = _pad_rows(w_taps.transpose(1, 0, 2).reshape(CONV_WIDTH, D), 8)
    w_conv_out = w_conv_out.reshape(D, D)
    proj = _matmul(h1, w_conv_in_t, mode="nt", name="conv_in", out_dtypes=[BF16], tm=1024, tn=1024, tk=D)
    mlp1_handle = relay_gather(mlp1_handle, "mlp1", proj)
    z = _conv_fwd(proj, w_taps, "conv_mix")
    x3, mix1 = _matmul(z, w_conv_out, mode="nn", name="conv_out", out_dtypes=[F32, BF16], tm=512, tn=1024, tk=D,
                       epilogue=residual, extras=[(x2, "tile"), (g_mix[1], "row")], after=[landed(mlp1_handle)])
    x4, mlp1, _ = mlp_fwd(1, x3, mlp1_handle)

    dx, d_final, loss_lanes, dy, dgate = _final_loss_bwd(x4, target, row(final_norm), (mlp1[4], g_mlp[1]), "loss_head")

    dx, dmod_mlp1, mlp1_scatter, token, dmix, dg_mix1 = mlp_bwd(1, dx, dy, dgate, mlp1, (mix1, g_mix[1]), [])
    dz = _matmul(dmix, w_conv_out, mode="nt", name="conv_out_bwd", out_dtypes=[BF16], tm=1024, tn=1024, tk=D,
                 after=[token])
    d_conv_out = _matmul(z, dmix, mode="tn", name="conv_out_wgrad", out_dtypes=[BF16], tm=512, tn=1024, tk=S)
    db, dc, du, d_taps = _conv_bwd(proj, dz, w_taps, "conv_mix_bwd")
    dproj = jnp.concatenate([db, dc, du], axis=1)
    dh1 = _matmul(dproj, w_conv_in_t, mode="nn", name="conv_in_bwd", out_dtypes=[F32], tm=512, tn=512, tk=3 * D, n_outer=True)
    d_conv_in = _matmul(h1, dproj, mode="tn", name="conv_in_wgrad", out_dtypes=[BF16], tm=512, tn=3 * D // NDEV, tk=S,
                        out_shards=True)
    dx, dsh1, dsc1, dgain_mix1, dy, dgate = _rms_mod_bwd(dh1, x2, inv1, dx, row(norm_mix[1]), sc_mix[1], "conv_norm_bwd",
                                                         (mlp0[4], g_mlp[0]))
    d_taps_split = d_taps[:CONV_WIDTH].reshape(CONV_WIDTH, NDEV, -1).transpose(1, 0, 2)
    conv_scatter, token = _exchange_start([d_conv_in, d_taps_split, d_conv_out.reshape(NDEV, D // NDEV, D)],
                                          "scatter_conv_start", True, dx)

    dx, dmod_mlp0, mlp0_scatter, token, dmix, dg_mix0 = mlp_bwd(0, dx, dy, dgate, mlp0, (mix0, g_mix[0]), [token])
    do = _matmul(dmix, w_fox_out, mode="nt", name="fox_out_bwd", out_dtypes=[BF16], tm=1024, tn=1024, tk=D,
                 after=[token])
    d_fox_out = _matmul(o, dmix, mode="tn", name="fox_out_wgrad", out_dtypes=[BF16], tm=512, tn=1024, tk=S)
    fox_out_scatter, token = _exchange_start([d_fox_out.reshape(NDEV, D // NDEV, D)], "scatter_fox_out_start", True, d_fox_out)
    dq, dk, dv, dfk = _attn_bwd(qkv, do, f_col, tied(f_row, token), lse, "fox_attention_bwd")
    dqkv = [dq, dk, dv]
    dfk_lanes = _pad_cols(dfk.reshape(H, S).T, LANES)
    df_logit, db_f = _forget_bwd(dfk_lanes, f_logit, b_f, "fox_forget_bwd")
    d_qkv_t = _matmul(dqkv, h0, mode="tn", name="fox_qkv_wgrad", out_dtypes=[BF16], tm=512, tn=512, tk=S)
    d_f_t = _matmul(df_logit, h0, mode="tn", name="fox_forget_wgrad", out_dtypes=[BF16], tm=LANES, tn=1024, tk=S)
    d_fox_in = jnp.concatenate([d_qkv_t, d_f_t[:H]], axis=0).reshape(NDEV, -1, D)
    fox_scatter, token = _exchange_start([d_fox_in], "scatter_fox_start", True, d_fox_in)
    dh0_f = _matmul(df_logit, w_f_t, mode="nn", name="fox_forget_logits_bwd", out_dtypes=[F32], tm=1024, tn=1024, tk=LANES,
                    after=[token])
    dh0 = _matmul(dqkv, w_in_t, mode="nn", name="fox_qkv_bwd", out_dtypes=[F32], tm=512, tn=512, tk=3 * D, n_outer=True,
                  epilogue=lambda acc, extra: (acc + extra,), extras=[(dh0_f, "tile")])
    dx, dsh0, dsc0, dgain_mix0 = _rms_mod_bwd(dh0, x0, inv0, dx, row(norm_mix[0]), sc_mix[0], "fox_norm_bwd")
    grad_x = dx.reshape(1, S, D)

    dmod = jnp.concatenate([
        jnp.concatenate([dsh0, dsc0, dg_mix0, dmod_mlp0[0], dmod_mlp0[1], dmod_mlp0[2]], axis=1),
        jnp.concatenate([dsh1, dsc1, dg_mix1, dmod_mlp1[0], dmod_mlp1[1], dmod_mlp1[2]], axis=1)], axis=0)
    small_sizes = [depth * n_mod * D, depth * D, depth * D, H, D, 1]
    n_small = sum(small_sizes)
    n_rows = -(-n_small // (8 * LANES)) * 8

    def pack(parts):
        flat = jnp.concatenate([p.reshape(-1) for p in parts])
        return jnp.pad(flat, (0, n_rows * LANES - n_small)).reshape(n_rows, LANES)

    def unpack(packed, shapes):
        flat, out, at = packed.reshape(-1), [], 0
        for size, shape in zip(small_sizes, shapes):
            out.append(flat[at:at + size].reshape(shape))
            at += size
        return out

    small_partial = pack([dmod, jnp.concatenate([dgain_mix0, dgain_mix1], axis=0),
                          jnp.concatenate([dmod_mlp0[3], dmod_mlp1[3]], axis=0), db_f[0, :H], d_final, loss_lanes[0, :1]])
    small_handle, token = _exchange_start([small_partial], "gather_small_start", False, dx)

    up1, down1 = _exchange_wait(mlp1_scatter, "scatter_mlp1_wait", token)
    up_out = _adamw(mlp_w_up, m_mlp_w_up, v_mlp_w_up, up1, "adamw_mlp_w_up_1", layer=1)
    down_out = _adamw(mlp_w_down, m_mlp_w_down, v_mlp_w_down, down1, "adamw_mlp_w_down_1", layer=1)
    cin, taps, cout = _exchange_wait(conv_scatter, "scatter_conv_wait", down_out[0])
    conv_in_out = _adamw(conv_w_in, m_conv_w_in, v_conv_w_in, cin, "adamw_conv_w_in")
    conv_w_res = _adamw(conv_w, m_conv_w, v_conv_w, taps, "adamw_conv_w")
    conv_out_out = _adamw(conv_w_out, m_conv_w_out, v_conv_w_out, cout, "adamw_conv_w_out")
    up0, down0 = _exchange_wait(mlp0_scatter, "scatter_mlp0_wait", conv_out_out[0])
    up_out = _adamw(mlp_w_up, m_mlp_w_up, v_mlp_w_up, up0, "adamw_mlp_w_up_0", layer=0, prev=up_out)
    down_out = _adamw(mlp_w_down, m_mlp_w_down, v_mlp_w_down, down0, "adamw_mlp_w_down_0", layer=0, prev=down_out)

    small_parts = _exchange_wait(small_handle, "gather_small_wait", down_out[0])[0]
    small_shapes = [ada_b.shape, norm_mix.shape, norm_mlp.shape, fox_b_f.shape, final_norm.shape]
    loss = jnp.sum(small_parts.reshape(NDEV, -1)[:, n_small - 1])
    unused = jnp.zeros((1,), F32)
    small_out = _adamw(pack([ada_b, norm_mix, norm_mlp, fox_b_f, final_norm, unused])[None],
                       pack([m_ada_b, m_norm_mix, m_norm_mlp, m_fox_b_f, m_final_norm, unused])[None],
                       pack([v_ada_b, v_norm_mix, v_norm_mlp, v_fox_b_f, v_final_norm, unused])[None], small_parts,
                       "adamw_small")
    small_out = [unpack(t, small_shapes) for t in small_out]

    dmod_all = small_parts.reshape(NDEV, -1)[:, :depth * n_mod * D].reshape(NDEV, depth, n_mod * D)
    dmod_mine = lax.dynamic_slice_in_dim(dmod_all, me * ncol, ncol, axis=2)
    ada_out = None
    for i in range(depth):
        d_ada = _matmul(c_all, dmod_mine[:, i], mode="tn", name=f"ada_wgrad_{i}", out_dtypes=[F32], tm=1024, tn=ncol // 2,
                        tk=NDEV, a_pre=_silu, precision=HIGHEST)
        ada_out = _adamw(ada_w, m_ada_w, v_ada_w, d_ada[None], f"adamw_ada_w_{i}", layer=i, prev=ada_out)

    (fout,) = _exchange_wait(fox_out_scatter, "scatter_fox_out_wait", ada_out[0])
    fox_out_out = _adamw(fox_w_out, m_fox_w_out, v_fox_w_out, fout, "adamw_fox_w_out")
    (fin,) = _exchange_wait(fox_scatter, "scatter_fox_wait", fox_out_out[0])
    swap = lambda t: jnp.swapaxes(t, 1, 2)
    fox_in_out = [swap(t) for t in _adamw(swap(fox_w_in), swap(m_fox_w_in), swap(v_fox_w_in), fin, "adamw_fox_w_in")]

    outputs = [loss, grad_x]
    for kind in range(4):
        sm = small_out[kind]
        outputs += [ada_out[kind], sm[0], sm[1], sm[2], fox_in_out[kind], sm[3], fox_out_out[kind], conv_in_out[kind],
                    conv_w_res[kind], conv_out_out[kind], up_out[kind], down_out[kind], sm[4]]
    return tuple(outputs)
```

```python
import math

import jax
import jax.numpy as jnp
from jax import lax
from jax.experimental import pallas as pl
from jax.experimental.pallas import tpu as pltpu

F32 = jnp.float32
BF16 = jnp.bfloat16
MESH = pl.DeviceIdType.MESH
NDEV = 8
HEAD_DIM = 128
LANES = 128
CONV_WIDTH = 3
RMS_EPS = 1e-6
ADAM_LR, ADAM_B1, ADAM_B2, ADAM_EPS, ADAM_WD, ADAM_STEP = 0.001, 0.9, 0.999, 1e-08, 0.01, 10
NEG = -1e30
V7X_VMEM_BYTES = 64 * 1024 * 1024
VMEM_HEADROOM = 12 * 1024 * 1024
HBM = pl.BlockSpec(memory_space=pltpu.HBM)
HIGHEST = lax.Precision.HIGHEST


def _nbytes(shape, dtype):
    return math.prod(shape) * jnp.dtype(dtype).itemsize


def _params(semantics, block_bytes, temp_bytes=0):
    limit = min(2 * block_bytes + temp_bytes + VMEM_HEADROOM, V7X_VMEM_BYTES - 4 * 1024 * 1024)
    return pltpu.CompilerParams(dimension_semantics=semantics, vmem_limit_bytes=int(limit))


def _my_index():
    return lax.axis_index("x") * 4 + lax.axis_index("y") * 2 + lax.axis_index("c")


def _peer(r):
    x, y, c = lax.axis_index("x"), lax.axis_index("y"), lax.axis_index("c")
    px = 1 - x if (r >> 2) & 1 else x
    py = 1 - y if (r >> 1) & 1 else y
    pc = 1 - c if r & 1 else c
    return (px, py, pc), px * 4 + py * 2 + pc


def _exchange(arrays, name, scatter, after=None):
    n = len(arrays)
    after = [] if after is None else list(after)

    def body(*refs):
        ins, outs = refs[:n], refs[n + len(after):2 * n + len(after)]
        send_sems, recv_sems, local_sems = refs[2 * n + len(after):]
        me = _my_index()
        local = []
        for a in range(n):
            src = ins[a].at[me] if scatter else ins[a]
            local.append(pltpu.make_async_copy(src, outs[a].at[me], local_sems.at[a]))
            local[-1].start()
        sends = []
        for r in range(1, NDEV):
            peer, pidx = _peer(r)
            for a in range(n):
                src = ins[a].at[pidx] if scatter else ins[a]
                cp = pltpu.make_async_remote_copy(
                    src_ref=src, dst_ref=outs[a].at[me],
                    send_sem=send_sems.at[a * (NDEV - 1) + r - 1], recv_sem=recv_sems.at[a * (NDEV - 1) + r - 1],
                    device_id=peer, device_id_type=MESH)
                cp.start()
                sends.append(cp)
        for r in range(1, NDEV):
            peer, pidx = _peer(r)
            for a in range(n):
                src = ins[a].at[pidx] if scatter else ins[a]
                pltpu.make_async_remote_copy(
                    src_ref=src, dst_ref=outs[a].at[pidx],
                    send_sem=send_sems.at[a * (NDEV - 1) + r - 1], recv_sem=recv_sems.at[a * (NDEV - 1) + r - 1],
                    device_id=peer, device_id_type=MESH).wait_recv()
        for cp in sends:
            cp.wait_send()
        for cp in local:
            cp.wait()

    out_shape = [jax.ShapeDtypeStruct(a.shape if scatter else (NDEV,) + a.shape, a.dtype) for a in arrays]
    return pl.pallas_call(
        body, name=name, out_shape=out_shape, in_specs=[HBM] * n + [ANY] * len(after), out_specs=[HBM] * n,
        scratch_shapes=[pltpu.SemaphoreType.DMA((n * (NDEV - 1),)), pltpu.SemaphoreType.DMA((n * (NDEV - 1),)),
                        pltpu.SemaphoreType.DMA((n,))],
    )(*arrays, *after)


def _all_gather(arrays, name, after=None):
    return _exchange(arrays, name, scatter=False, after=after)


SEM = pl.BlockSpec(memory_space=pltpu.SEMAPHORE)
ANY = pl.BlockSpec(memory_space=pl.ANY)
DATAFLOW = pltpu.SideEffectType.DATAFLOW_SIDE_EFFECTING
TOKEN_SHAPE = (8, LANES)


SIBLING = 1
OTHER_CHIPS = (4, 2, 6)


def _exchange_start(arrays, name, scatter, after, relay=False):
    n = len(arrays)
    n_sems = n * (NDEV - 1)
    assert not (relay and scatter)

    def body(*refs):
        ins = refs[:n]
        send_sems, recv_sems = refs[n + 1], refs[n + 2]
        lands, token = refs[2 * n + 3:3 * n + 3], refs[3 * n + 3]
        me = _my_index()
        for r in (SIBLING, *OTHER_CHIPS) if relay else range(1, NDEV):
            peer, pidx = _peer(r)
            for a in range(n):
                src = ins[a].at[pidx] if scatter else ins[a]
                pltpu.make_async_remote_copy(
                    src_ref=src, dst_ref=lands[a].at[me],
                    send_sem=send_sems.at[a * (NDEV - 1) + r - 1], recv_sem=recv_sems.at[a * (NDEV - 1) + r - 1],
                    device_id=peer, device_id_type=MESH).start()
        token[...] = jnp.zeros(TOKEN_SHAPE, F32)

    land_shapes = [a.shape if scatter else (NDEV,) + a.shape for a in arrays]
    srcs = [pltpu.with_memory_space_constraint(a, pltpu.HBM) for a in arrays]
    outs = pl.pallas_call(
        body, name=name,
        out_shape=(pltpu.SemaphoreType.DMA((n_sems,)), pltpu.SemaphoreType.DMA((n_sems,)),
                   *[pltpu.HBM(a.shape, a.dtype) for a in arrays], *[pltpu.HBM(s, a.dtype) for s, a in zip(land_shapes, arrays)],
                   jax.ShapeDtypeStruct(TOKEN_SHAPE, F32)),
        in_specs=[HBM] * n + [ANY],
        out_specs=(SEM, SEM, *[HBM] * (2 * n), pl.BlockSpec(memory_space=pltpu.VMEM)),
        input_output_aliases={i: 2 + i for i in range(n)},
        compiler_params=pltpu.CompilerParams(has_side_effects=DATAFLOW),
    )(*srcs, after)
    return (scatter, relay, [(outs[0], outs[1])], list(outs[2:2 + n]), list(outs[2 + n:2 + 2 * n])), outs[-1]


def _relay_forward_start(handle, name):
    scatter, relay, sems, srcs, lands = handle
    n = len(lands)
    n_sems = n * len(OTHER_CHIPS)

    def body(*refs):
        land_refs, send_sems, recv_sems = refs[:n], refs[n], refs[n + 1]
        sibling, _ = _peer(SIBLING)
        for j, r in enumerate(OTHER_CHIPS):
            _, pidx = _peer(r)
            for a in range(n):
                pltpu.make_async_remote_copy(
                    src_ref=land_refs[a].at[pidx], dst_ref=land_refs[a].at[pidx],
                    send_sem=send_sems.at[a * len(OTHER_CHIPS) + j], recv_sem=recv_sems.at[a * len(OTHER_CHIPS) + j],
                    device_id=sibling, device_id_type=MESH).start()

    outs = pl.pallas_call(
        body, name=name,
        out_shape=(pltpu.SemaphoreType.DMA((n_sems,)), pltpu.SemaphoreType.DMA((n_sems,)),
                   *[pltpu.HBM(t.shape, t.dtype) for t in lands]),
        in_specs=[HBM] * n, out_specs=(SEM, SEM, *[HBM] * n),
        input_output_aliases={i: 2 + i for i in range(n)},
        compiler_params=pltpu.CompilerParams(has_side_effects=DATAFLOW),
    )(*lands)
    return (scatter, relay, sems + [(outs[0], outs[1])], srcs, list(outs[2:]))


def _exchange_wait(handle, name, after, arrivals_only=False):
    scatter, relay, sems, srcs, lands = handle
    n = len(srcs)
    forwarded = len(sems) == 2
    assert not arrivals_only or (relay and not forwarded)

    def body(*refs):
        src_refs, land_refs = refs[:n], refs[n:2 * n]
        send_sems, recv_sems = refs[2 * n], refs[2 * n + 1]
        for r in (SIBLING, *OTHER_CHIPS) if relay else range(1, NDEV):
            peer, pidx = _peer(r)
            for a in range(n):
                src = src_refs[a].at[pidx] if scatter else src_refs[a]
                cp = pltpu.make_async_remote_copy(
                    src_ref=src, dst_ref=land_refs[a].at[pidx],
                    send_sem=send_sems.at[a * (NDEV - 1) + r - 1], recv_sem=recv_sems.at[a * (NDEV - 1) + r - 1],
                    device_id=peer, device_id_type=MESH)
                if arrivals_only:
                    if r in OTHER_CHIPS:
                        cp.wait_recv()
                else:
                    cp.wait_send()
                    if not (relay and r in OTHER_CHIPS):
                        cp.wait_recv()
        if forwarded:
            fwd_send, fwd_recv = refs[2 * n + 2], refs[2 * n + 3]
            sibling, _ = _peer(SIBLING)
            for j, r in enumerate(OTHER_CHIPS):
                _, pidx = _peer(r ^ SIBLING)
                for a in range(n):
                    cp = pltpu.make_async_remote_copy(
                        src_ref=src_refs[a], dst_ref=land_refs[a].at[pidx],
                        send_sem=fwd_send.at[a * len(OTHER_CHIPS) + j], recv_sem=fwd_recv.at[a * len(OTHER_CHIPS) + j],
                        device_id=sibling, device_id_type=MESH)
                    cp.wait_send()
                    cp.wait_recv()

    flat_sems = [s for pair in sems for s in pair]
    outs = pl.pallas_call(
        body, name=name,
        out_shape=tuple(pltpu.HBM(t.shape, t.dtype) for t in (*srcs, *lands)),
        in_specs=[HBM] * (2 * n) + [SEM] * len(flat_sems) + [ANY], out_specs=tuple([HBM] * (2 * n)),
        input_output_aliases={i: i for i in range(2 * n)},
        compiler_params=pltpu.CompilerParams(has_side_effects=DATAFLOW),
    )(*srcs, *lands, *flat_sems, after)
    if arrivals_only:
        return (scatter, relay, sems, list(outs[:n]), list(outs[n:]))
    me = _my_index()
    mine = [lax.dynamic_index_in_dim(s, me, 0, keepdims=False) if scatter else s for s in outs[:n]]
    return [lax.dynamic_update_index_in_dim(land, own, me, 0) for land, own in zip(outs[n:], mine)]


def _matmul(a, b, *, mode, name, out_dtypes, tm, tn, tk, epilogue=None, extras=(), a_pre=None,
            out_shards=False, n_outer=False, precision=None, after=(), n=None, b_first_block=0):
    after = list(after)
    n_after = len(after)
    parts = list(a) if isinstance(a, (list, tuple)) else [a]
    n_parts = len(parts)
    rows, cols = parts[0].shape
    K, M = (rows, cols * n_parts) if mode == "tn" else (cols * n_parts, rows)
    N = n if n is not None else (b.shape[0] if mode == "nt" else b.shape[1])
    tm, tn, tk = min(tm, M), min(tn, N), min(tk, K)
    assert M % tm == 0 and N % tn == 0 and K % tk == 0, (name, M, N, K, tm, tn, tk)
    nm, nn, nk = M // tm, N // tn, K // tk
    assert n_parts == 1 or (nk == 1 and a_pre is None and mode in ("nn", "tn") and cols % tm == 0), name
    blocks_per_part = cols // tm if mode == "tn" else 1
    n_out, n_ext = len(out_dtypes), len(extras)
    contract = {"nn": ((1,), (0,)), "nt": ((1,), (1,)), "tn": ((0,), (0,))}[mode]

    def body(*refs):
        a_refs, b_ref = refs[:n_parts], refs[n_parts]
        ext_refs = refs[n_parts + 1:n_parts + 1 + n_ext]
        first_out = n_parts + 1 + n_ext + n_after
        out_refs = refs[first_out:first_out + n_out]
        acc_ref = refs[first_out + n_out] if nk > 1 else None

        def product(a_ref, bv):
            av = a_ref[...] if a_pre is None else a_pre(a_ref[...])
            if precision is None:
                av, bv = av.astype(BF16), bv.astype(BF16)
            return lax.dot_general(av, bv, (contract, ((), ())), preferred_element_type=F32, precision=precision)

        def finish(acc):
            vals = (acc,) if epilogue is None else epilogue(acc, *[r[...] for r in ext_refs])
            for r, v in zip(out_refs, vals):
                r[...] = v.astype(r.dtype)

        if n_parts > 1 and mode == "tn":
            i = pl.program_id(1 if n_outer else 0)
            for p in range(n_parts):
                @pl.when(i // blocks_per_part == p)
                def _(p=p):
                    finish(product(a_refs[p], b_ref[...]))
            return
        part = product(a_refs[0], b_ref[...] if n_parts == 1 else b_ref[0:cols, :])
        for p in range(1, n_parts):
            part = part + product(a_refs[p], b_ref[p * cols:(p + 1) * cols, :])

        if nk == 1:
            finish(part)
        else:
            k = pl.program_id(2)

            @pl.when(k == 0)
            def _():
                acc_ref[...] = part

            @pl.when(k > 0)
            def _():
                acc_ref[...] += part

            @pl.when(k == nk - 1)
            def _():
                finish(acc_ref[...])

    def at(index):
        return (lambda j, i, k: index(i, j, k)) if n_outer else index

    if n_parts == 1:
        a_specs = [pl.BlockSpec((tk, tm), at(lambda i, j, k: (k, i))) if mode == "tn"
                   else pl.BlockSpec((tm, tk), at(lambda i, j, k: (i, k)))]
    elif mode == "tn":
        a_specs = [pl.BlockSpec((tk, tm), at(lambda i, j, k, p=p: (k, jnp.clip(i - p * blocks_per_part, 0, blocks_per_part - 1))))
                   for p in range(n_parts)]
    else:
        a_specs = [pl.BlockSpec((tm, cols), at(lambda i, j, k: (i, 0))) for _ in parts]
    b_spec = (pl.BlockSpec((tn, tk), at(lambda i, j, k: (j, k))) if mode == "nt"
              else pl.BlockSpec((tk, tn), at(lambda i, j, k: (k + b_first_block, j))))
    in_specs = a_specs + [b_spec]
    block_bytes = _nbytes((tm, tk), parts[0].dtype) * (n_parts if mode == "tn" else 1) + _nbytes((tk, tn), b.dtype)
    for arr, kind in extras:
        if kind == "tile":
            assert arr.shape == (M, N), (name, arr.shape)
            in_specs.append(pl.BlockSpec((tm, tn), at(lambda i, j, k: (i, j))))
            block_bytes += _nbytes((tm, tn), arr.dtype)
        else:
            assert arr.shape == (1, N), (name, arr.shape)
            in_specs.append(pl.BlockSpec((1, tn), at(lambda i, j, k: (0, j))))
    in_specs += [ANY] * n_after
    if out_shards:
        assert n_out == 1 and tn * NDEV == N
        out_shape = [jax.ShapeDtypeStruct((NDEV, M, tn), out_dtypes[0])]
        out_specs = [pl.BlockSpec((None, tm, tn), at(lambda i, j, k: (j, i, 0)))]
    else:
        out_shape = [jax.ShapeDtypeStruct((M, N), d) for d in out_dtypes]
        out_specs = [pl.BlockSpec((tm, tn), at(lambda i, j, k: (i, j))) for _ in out_dtypes]
    block_bytes += sum(_nbytes((tm, tn), d) for d in out_dtypes)
    scratch = [pltpu.VMEM((tm, tn), F32)] if nk > 1 else []
    outs = pl.pallas_call(
        body, name=name, grid=(nn, nm, nk) if n_outer else (nm, nn, nk), in_specs=in_specs, out_specs=out_specs,
        out_shape=out_shape, scratch_shapes=scratch,
        compiler_params=_params(("parallel", "parallel", "arbitrary"), block_bytes, 2 * tm * tn * 4),
    )(*parts, b, *[arr for arr, _ in extras], *after)
    return outs[0] if n_out == 1 else outs


def _rowwise(fn, tiled, smalls, out_tiles, out_sums, *, name, ts=256):
    S = tiled[0].shape[0]
    ts = min(ts, S)
    assert S % ts == 0
    nt, ns, no, na = len(tiled), len(smalls), len(out_tiles), len(out_sums)

    def body(*refs):
        t_refs, s_refs = refs[:nt], refs[nt:nt + ns]
        o_refs, a_refs = refs[nt + ns:nt + ns + no], refs[nt + ns + no:]
        tile_vals, sum_vals = fn([r[...] for r in t_refs], [r[...] for r in s_refs])
        for r, v in zip(o_refs, tile_vals):
            r[...] = v.astype(r.dtype)

        @pl.when(pl.program_id(0) == 0)
        def _():
            for r in a_refs:
                r[...] = jnp.zeros_like(r)

        for r, v in zip(a_refs, sum_vals):
            r[...] += v

    in_specs = [pl.BlockSpec((ts, t.shape[1]), lambda i: (i, 0)) for t in tiled]
    in_specs += [pl.BlockSpec(s.shape, lambda i: (0, 0)) for s in smalls]
    out_specs = [pl.BlockSpec((ts, w), lambda i: (i, 0)) for w, _ in out_tiles]
    out_specs += [pl.BlockSpec((1, w), lambda i: (0, 0)) for w in out_sums]
    out_shape = [jax.ShapeDtypeStruct((S, w), d) for w, d in out_tiles]
    out_shape += [jax.ShapeDtypeStruct((1, w), F32) for w in out_sums]
    block_bytes = sum(_nbytes((ts, t.shape[1]), t.dtype) for t in tiled) + sum(_nbytes((ts, w), d) for w, d in out_tiles)
    width = max(t.shape[1] for t in tiled)
    outs = pl.pallas_call(
        body, name=name, grid=(S // ts,), in_specs=in_specs, out_specs=out_specs, out_shape=out_shape,
        compiler_params=_params(("arbitrary",), block_bytes, 6 * ts * width * 4),
    )(*tiled, *smalls)
    return outs[:no], outs[no:]


def _colsum(v):
    return jnp.sum(v, axis=0, keepdims=True)


def _rms_mod_fwd(x, gain, shift, scale, name):
    def fn(tiles, smalls):
        (xv,), (g, sh, sc) = tiles, smalls
        inv = lax.rsqrt(jnp.mean(xv * xv, axis=-1, keepdims=True) + RMS_EPS)
        h = (xv * inv) * g * (1.0 + sc) + sh
        return (h, inv), ()

    D = x.shape[1]
    (h, inv), _ = _rowwise(fn, [x], [gain, shift, scale], [(D, BF16), (1, F32)], [], name=name)
    return h, inv


def _gated(dxv, following):
    yv, gate = following
    return dxv * gate, _colsum(dxv * yv)


def _rms_mod_bwd(dh, x, inv, dx_res, gain, scale, name, following=None):
    def fn(tiles, smalls):
        dhv, xv, iv, dres = tiles[:4]
        g, sc = smalls[:2]
        dhv = dhv.astype(F32)
        xhat = xv * iv
        dr = dhv * (1.0 + sc)
        dxhat = dr * g
        dxv = dres + iv * (dxhat - xhat * jnp.mean(dxhat * xhat, axis=-1, keepdims=True))
        sums = (_colsum(dhv), _colsum(dhv * (xhat * g)), _colsum(dr * xhat))
        if following is None:
            return (dxv,), sums
        dy, dgate = _gated(dxv, (tiles[4], smalls[2]))
        return (dxv, dy), (*sums, dgate)

    D = x.shape[1]
    extra = [] if following is None else [following]
    tiles, sums = _rowwise(fn, [dh, x, inv, dx_res] + [f[0] for f in extra], [gain, scale] + [f[1] for f in extra],
                           [(D, F32)] + [(D, BF16)] * len(extra), [D] * (3 + len(extra)), name=name)
    return (tiles[0], *sums[:3]) if following is None else (tiles[0], *sums[:3], tiles[1], sums[3])


def _final_loss_bwd(x, target, gain, following, name):
    D = x.shape[1]

    def fn(tiles, smalls):
        xv, tv, g = tiles[0], tiles[1], smalls[0]
        inv = lax.rsqrt(jnp.mean(xv * xv, axis=-1, keepdims=True) + RMS_EPS)
        xhat = xv * inv
        err = xhat * g - tv
        loss = 0.5 * jnp.sum(jnp.mean(err * err, axis=-1, keepdims=True), axis=0, keepdims=True)
        dout = err * (1.0 / D)
        dxhat = dout * g
        dxv = inv * (dxhat - xhat * jnp.mean(dxhat * xhat, axis=-1, keepdims=True))
        dy, dgate = _gated(dxv, (tiles[2], smalls[1]))
        return (dxv, dy), (_colsum(dout * xhat), jnp.broadcast_to(loss, (1, LANES)), dgate)

    (dx, dy), (dgain, loss, dgate) = _rowwise(fn, [x, target, following[0]], [gain, following[1]],
                                             [(D, F32), (D, BF16)], [D, LANES, D], name=name)
    return dx, dgain, loss, dy, dgate


SCAN_BLOCK = 256


def _triangle(n, lower):
    r = lax.broadcasted_iota(jnp.int32, (n, n), 0)
    c = lax.broadcasted_iota(jnp.int32, (n, n), 1)
    return (r >= c if lower else r <= c).astype(F32)


def _forget_cumsum(logits, bias, name):
    S = logits.shape[0]
    blk = min(SCAN_BLOCK, S)
    nb = S // blk

    def body(z_ref, b_ref, f_ref):
        z = z_ref[...] + b_ref[...]
        f_ref[...] = jnp.minimum(z, 0.0) - jnp.log(1.0 + jnp.exp(-jnp.abs(z)))
        tri = _triangle(blk, lower=True)

        def step(i, carry):
            off = pl.multiple_of(i * blk, blk)
            cs = jnp.dot(tri, f_ref[pl.ds(off, blk), :], preferred_element_type=F32, precision=HIGHEST) + carry
            f_ref[pl.ds(off, blk), :] = cs
            return cs[blk - 1:blk, :]

        lax.fori_loop(0, nb, step, jnp.zeros((1, LANES), F32))

    return pl.pallas_call(body, name=name, out_shape=jax.ShapeDtypeStruct((S, LANES), F32))(logits, bias)


def _forget_bwd(dfk, logits, bias, name):
    S = logits.shape[0]
    blk = min(SCAN_BLOCK, S)
    nb = S // blk

    def body(d_ref, z_ref, b_ref, o_ref, db_ref):
        tri = _triangle(blk, lower=False)

        def step(t, carry):
            off = pl.multiple_of((nb - 1 - t) * blk, blk)
            cs = jnp.dot(tri, d_ref[pl.ds(off, blk), :], preferred_element_type=F32, precision=HIGHEST) + carry
            o_ref[pl.ds(off, blk), :] = cs
            return cs[0:1, :]

        lax.fori_loop(0, nb, step, jnp.zeros((1, LANES), F32))
        z = z_ref[...] + b_ref[...]
        dz = -o_ref[...] / (1.0 + jnp.exp(z))
        o_ref[...] = dz
        db_ref[...] = _colsum(dz)

    return pl.pallas_call(
        body, name=name,
        out_shape=(jax.ShapeDtypeStruct((S, LANES), F32), jax.ShapeDtypeStruct((1, LANES), F32)),
    )(dfk, logits, bias)


KEY_SCALE = HEAD_DIM ** -0.5
ATTN_BLOCK = 512
_NT = (((1,), (1,)), ((), ()))


def _loop_in_pairs(step, count, init):
    carry = lax.fori_loop(0, count // 2, lambda t, c: step(2 * t + 1, step(2 * t, c)), init)
    return lax.fori_loop(count // 2 * 2, count, step, carry)


def _attn_specs(S, H, tb):
    q_blk = lambda part: pl.BlockSpec((tb, HEAD_DIM), lambda h, i: (i, part * H + h))
    q_all = lambda part: pl.BlockSpec((S, HEAD_DIM), lambda h, i: (0, part * H + h))
    col_blk = pl.BlockSpec((None, tb, 1), lambda h, i: (h, i, 0))
    row_all = pl.BlockSpec((None, 1, S), lambda h, i: (h, 0, 0))
    return q_blk, q_all, col_blk, row_all


FWD_HEADS = 2


def _attn_fwd(qkv, f_col, f_row, name):
    S, H = qkv.shape[0], qkv.shape[1] // (3 * HEAD_DIM)
    tb = min(ATTN_BLOCK, S)
    hp = FWD_HEADS if H % FWD_HEADS == 0 else 1
    groups, wide = H // hp, hp * HEAD_DIM
    lanes = lambda u: pl.ds(u * HEAD_DIM, HEAD_DIM)

    def body(q_ref, k_ref, v_ref, fc_ref, fr_ref, o_ref, lse_ref):
        i = pl.program_id(1)

        def step(j, carry, diagonal):
            off = pl.multiple_of(j * tb, tb)
            out = []
            for u in range(hp):
                m, l, acc = carry[u]
                k, v = k_ref[pl.ds(off, tb), lanes(u)], v_ref[pl.ds(off, tb), lanes(u)]
                s = lax.dot_general(q_ref[:, lanes(u)], k, _NT, preferred_element_type=F32)
                s = s + (fc_ref[u] - fr_ref[u, :, pl.ds(off, tb)])
                if diagonal:
                    row = lax.broadcasted_iota(jnp.int32, (tb, tb), 0)
                    col = lax.broadcasted_iota(jnp.int32, (tb, tb), 1)
                    s = jnp.where(col <= row, s, NEG)
                m_new = jnp.maximum(m, jnp.max(s, axis=-1, keepdims=True))
                p = jnp.exp(s - m_new)
                alpha = jnp.exp(m - m_new)
                l = alpha * l + jnp.sum(p, axis=-1, keepdims=True)
                acc = alpha * acc + jnp.dot(p.astype(BF16), v, preferred_element_type=F32)
                out.append((m_new, l, acc))
            return tuple(out)

        init = (jnp.full((tb, 1), NEG, F32), jnp.zeros((tb, 1), F32), jnp.zeros((tb, HEAD_DIM), F32))
        carry = _loop_in_pairs(lambda j, c: step(j, c, False), i, (init,) * hp)
        for u, (m, l, acc) in enumerate(step(i, carry, True)):
            o_ref[:, lanes(u)] = (acc / l).astype(o_ref.dtype)
            lse_ref[u] = m + jnp.log(l)

    part = lambda p, rows: pl.BlockSpec((rows, wide), lambda g, i: (i if rows == tb else 0, p * groups + g))
    col_blk = pl.BlockSpec((hp, tb, 1), lambda g, i: (g, i, 0))
    return pl.pallas_call(
        body, name=name, grid=(groups, S // tb),
        in_specs=[part(0, tb), part(1, S), part(2, S), col_blk, pl.BlockSpec((hp, 1, S), lambda g, i: (g, 0, 0))],
        out_specs=[pl.BlockSpec((tb, wide), lambda g, i: (i, g)), col_blk],
        out_shape=[jax.ShapeDtypeStruct((S, H * HEAD_DIM), BF16), jax.ShapeDtypeStruct((H, S, 1), F32)],
        compiler_params=_params(("parallel", "parallel"), 4 * S * wide * 2, 10 * hp * tb * tb * 4),
    )(qkv, qkv, qkv, f_col, f_row)


_TN = (((0,), (0,)), ((), ()))


def _attn_bwd(qkv, do, f_col, f_row, lse_col, name):
    S, H = qkv.shape[0], qkv.shape[1] // (3 * HEAD_DIM)
    tb = min(ATTN_BLOCK, S)
    nq = S // tb
    q_blk, q_all, col_blk, row_all = _attn_specs(S, H, tb)
    head_blk = pl.BlockSpec((tb, HEAD_DIM), lambda h, i: (i, h))
    head_all = pl.BlockSpec((S, HEAD_DIM), lambda h, i: (0, h))

    def body(q_ref, k_ref, v_ref, do_ref, fc_ref, fr_ref, lse_ref, dq_ref, dk_ref, dv_ref, dfk_ref,
             p_buf, dp_buf, dk_acc, dv_acc, dfk_acc):
        i = pl.program_id(1)
        q, do = q_ref[...], do_ref[...]
        fc_lse = fc_ref[...] - lse_ref[...]

        @pl.when(i == 0)
        def _():
            dk_acc[...] = jnp.zeros_like(dk_acc)
            dv_acc[...] = jnp.zeros_like(dv_acc)
            dfk_acc[...] = jnp.zeros_like(dfk_acc)

        def scores(j, delta, diagonal):
            off = pl.multiple_of(j * tb, tb)
            k, v = k_ref[pl.ds(off, tb), :], v_ref[pl.ds(off, tb), :]
            s = (lax.dot_general(q, k, _NT, preferred_element_type=F32) + fc_lse) - fr_ref[:, pl.ds(off, tb)]
            if diagonal:
                row = lax.broadcasted_iota(jnp.int32, (tb, tb), 0)
                col = lax.broadcasted_iota(jnp.int32, (tb, tb), 1)
                s = jnp.where(col <= row, s, NEG)
            p = jnp.exp(s)
            dp = lax.dot_general(do, v, _NT, preferred_element_type=F32)
            p_buf[j] = p
            dp_buf[j] = dp
            return delta + jnp.sum(p * dp, axis=-1, keepdims=True)

        delta = _loop_in_pairs(lambda j, c: scores(j, c, False), i, jnp.zeros((tb, 1), F32))
        delta = scores(i, delta, True)

        def grad(j, dq):
            off = pl.multiple_of(j * tb, tb)
            p = p_buf[j]
            ds = p * (dp_buf[j] - delta)
            ds_lo = ds.astype(BF16)
            dk_acc[pl.ds(off, tb), :] += lax.dot_general(ds_lo, q, _TN, preferred_element_type=F32)
            dv_acc[pl.ds(off, tb), :] += lax.dot_general(p.astype(BF16), do, _TN, preferred_element_type=F32)
            dfk_acc[:, pl.ds(off, tb)] += jnp.sum(ds, axis=0, keepdims=True)
            return dq + jnp.dot(ds_lo, k_ref[pl.ds(off, tb), :], preferred_element_type=F32)

        dq = _loop_in_pairs(grad, i + 1, jnp.zeros((tb, HEAD_DIM), F32))
        dq_ref[...] = dq.astype(dq_ref.dtype)

        @pl.when(i == nq - 1)
        def _():
            dk_ref[...] = (dk_acc[...] * KEY_SCALE).astype(dk_ref.dtype)
            dv_ref[...] = dv_acc[...].astype(dv_ref.dtype)
            dfk_ref[...] = dfk_acc[...]

    wide = jax.ShapeDtypeStruct((S, H * HEAD_DIM), BF16)
    return pl.pallas_call(
        body, name=name, grid=(H, nq),
        in_specs=[q_blk(0), q_all(1), q_all(2), head_blk, col_blk, row_all, col_blk],
        out_specs=[head_blk, head_all, head_all, row_all],
        out_shape=[wide, wide, wide, jax.ShapeDtypeStruct((H, 1, S), F32)],
        scratch_shapes=[pltpu.VMEM((nq, tb, tb), F32), pltpu.VMEM((nq, tb, tb), F32),
                        pltpu.VMEM((S, HEAD_DIM), F32), pltpu.VMEM((S, HEAD_DIM), F32), pltpu.VMEM((1, S), F32)],
        compiler_params=_params(("parallel", "arbitrary"), 6 * S * HEAD_DIM * 2,
                                2 * nq * tb * tb * 4 + 2 * S * HEAD_DIM * 4 + 10 * tb * tb * 4),
    )(qkv, qkv, qkv, do, f_col, f_row, lse_col)


CONV_TILE = 128


def _shift_down(v, n):
    row = lax.broadcasted_iota(jnp.int32, v.shape, 0)
    return jnp.where(row >= n, pltpu.roll(v, n, 0), 0.0)


def _shift_up(v, n):
    S = v.shape[0]
    row = lax.broadcasted_iota(jnp.int32, v.shape, 0)
    return jnp.where(row < S - n, pltpu.roll(v, S - n, 0), 0.0)


def _conv_specs(S, D, tc):
    nb = D // tc
    part = lambda p: pl.BlockSpec((S, tc), lambda j: (0, p * nb + j))
    return part, pl.BlockSpec((S, tc), lambda j: (0, j)), pl.BlockSpec((8, tc), lambda j: (0, j))


def _conv_fwd(proj, conv_w8, name):
    S, D = proj.shape[0], proj.shape[1] // 3
    tc = min(CONV_TILE, D)
    part, chan, taps = _conv_specs(S, D, tc)

    def body(b_ref, c_ref, u_ref, w_ref, z_ref):
        cu = c_ref[...].astype(F32) * u_ref[...].astype(F32)
        w = w_ref[...]
        y = w[0:1, :] * _shift_down(cu, 2) + w[1:2, :] * _shift_down(cu, 1) + w[2:3, :] * cu
        z_ref[...] = (b_ref[...].astype(F32) * y).astype(z_ref.dtype)

    return pl.pallas_call(
        body, name=name, grid=(D // tc,), in_specs=[part(0), part(1), part(2), taps], out_specs=chan,
        out_shape=jax.ShapeDtypeStruct((S, D), BF16),
        compiler_params=_params(("parallel",), 3 * _nbytes((S, tc), proj.dtype) + S * tc * 2, 6 * S * tc * 4),
    )(proj, proj, proj, conv_w8)


def _conv_bwd(proj, dz, conv_w8, name):
    S, D = proj.shape[0], proj.shape[1] // 3
    tc = min(CONV_TILE, D)
    part, chan, taps = _conv_specs(S, D, tc)

    def body(b_ref, c_ref, u_ref, dz_ref, w_ref, db_ref, dc_ref, du_ref, dw_ref):
        cv, uv = c_ref[...].astype(F32), u_ref[...].astype(F32)
        dzv, w = dz_ref[...].astype(F32), w_ref[...]
        cu = cv * uv
        cu1, cu2 = _shift_down(cu, 1), _shift_down(cu, 2)
        y = w[0:1, :] * cu2 + w[1:2, :] * cu1 + w[2:3, :] * cu
        db_ref[...] = (dzv * y).astype(db_ref.dtype)
        dy = dzv * b_ref[...].astype(F32)
        dcu = w[2:3, :] * dy + w[1:2, :] * _shift_up(dy, 1) + w[0:1, :] * _shift_up(dy, 2)
        dc_ref[...] = (dcu * uv).astype(dc_ref.dtype)
        du_ref[...] = (dcu * cv).astype(du_ref.dtype)
        dw_ref[...] = jnp.concatenate(
            [_colsum(dy * cu2), _colsum(dy * cu1), _colsum(dy * cu), jnp.zeros((8 - CONV_WIDTH, tc), F32)], axis=0)

    return pl.pallas_call(
        body, name=name, grid=(D // tc,), in_specs=[part(0), part(1), part(2), chan, taps],
        out_specs=[chan, chan, chan, taps],
        out_shape=[jax.ShapeDtypeStruct((S, D), BF16)] * 3 + [jax.ShapeDtypeStruct((8, D), F32)],
        compiler_params=_params(("parallel",), 3 * _nbytes((S, tc), proj.dtype) + _nbytes((S, tc), dz.dtype)
                                + 3 * S * tc * 2, 10 * S * tc * 4),
    )(proj, proj, proj, dz, conv_w8)


def _adamw(w, m, v, parts, name, layer=0, prev=None):
    L, R, C = w.shape
    P = parts.shape[0]
    assert parts.shape[1:] == (R, C), (name, parts.shape, w.shape)
    elem_bytes = 12 + 16 + P * parts.dtype.itemsize
    budget = 16 << 20
    tr, tc = R, C
    if R * C * elem_bytes > budget:
        if R % 8 == 0:
            tr = max(8, (budget // (C * elem_bytes)) // 8 * 8)
            while R % tr:
                tr -= 8
        else:
            tc = LANES
            while C % (2 * tc) == 0 and R * 2 * tc * elem_bytes <= budget:
                tc *= 2
            assert C % tc == 0, (name, R, C)
    c1, c2 = 1.0 - ADAM_B1 ** ADAM_STEP, 1.0 - ADAM_B2 ** ADAM_STEP

    def body(w_ref, m_ref, v_ref, p_ref, *rest):
        g_ref, d_ref, nm_ref, nv_ref = rest[-4:]
        g = p_ref[0].astype(F32)
        for p in range(1, P):
            g = g + p_ref[p].astype(F32)
        nm = ADAM_B1 * m_ref[...] + (1.0 - ADAM_B1) * g
        nv = ADAM_B2 * v_ref[...] + (1.0 - ADAM_B2) * (g * g)
        g_ref[...] = g
        nm_ref[...] = nm
        nv_ref[...] = nv
        d_ref[...] = -ADAM_LR * ((nm / c1) / (jnp.sqrt(nv / c2) + ADAM_EPS) + ADAM_WD * w_ref[...])

    blk = pl.BlockSpec((None, tr, tc), lambda i, j: (layer, i, j))
    prev = [] if prev is None else list(prev)
    return pl.pallas_call(
        body, name=name, grid=(R // tr, C // tc),
        in_specs=[blk, blk, blk, pl.BlockSpec((P, tr, tc), lambda i, j: (0, i, j))] + [ANY] * len(prev),
        out_specs=[blk] * 4, out_shape=[jax.ShapeDtypeStruct((L, R, C), F32)] * 4,
        input_output_aliases={4 + k: k for k in range(len(prev))},
        compiler_params=_params(("parallel", "parallel"), tr * tc * elem_bytes),
    )(w, m, v, parts, *prev)


def _silu(v):
    return v / (1.0 + jnp.exp(-v))


def _pad_rows(a, rows):
    return jnp.pad(a, ((0, rows - a.shape[0]), (0, 0)))


def _pad_cols(a, cols):
    return jnp.pad(a, ((0, 0), (0, cols - a.shape[1])))


def kernel(x, c, ada_w, ada_b, norm_mix, norm_mlp, fox_w_in, fox_b_f, fox_w_out, conv_w_in, conv_w, conv_w_out, mlp_w_up, mlp_w_down, final_norm, loss_target, m_ada_w, m_ada_b, m_norm_mix, m_norm_mlp, m_fox_w_in, m_fox_b_f, m_fox_w_out, m_conv_w_in, m_conv_w, m_conv_w_out, m_mlp_w_up, m_mlp_w_down, m_final_norm, v_ada_w, v_ada_b, v_norm_mix, v_norm_mlp, v_fox_w_in, v_fox_b_f, v_fox_w_out, v_conv_w_in, v_conv_w, v_conv_w_out, v_mlp_w_up, v_mlp_w_down, v_final_norm):
    S, D = x.shape[1], x.shape[2]
    H = D // HEAD_DIM
    FF = mlp_w_up.shape[2] * NDEV
    depth = ada_w.shape[0]
    n_mod = 6
    assert depth == 2 and fox_w_in.shape[0] == 1 and conv_w_in.shape[0] == 1 and H <= LANES
    me = _my_index()
    x0, target = x[0], loss_target[0]
    row = lambda vec: vec.reshape(1, -1)

    def tied(vec, token):
        return vec + token[0, 0]

    bf = lambda w: w.astype(BF16)
    gather_groups = {
        "fox": [bf(fox_w_in[0]).T],
        "fox_out": [bf(fox_w_out[0])],
        "mlp0": [bf(mlp_w_up[0]).T, bf(mlp_w_down[0])],
        "conv": [bf(conv_w_in[0]).T, conv_w[0], bf(conv_w_out[0])],
        "mlp1": [bf(mlp_w_up[1]).T, bf(mlp_w_down[1])],
    }

    def start_gather(group, after):
        return _exchange_start(gather_groups[group], f"gather_{group}_start", False, after, relay=True)

    def relay_gather(handle, group, after):
        handle = _exchange_wait(handle, f"gather_{group}_arrivals", after, arrivals_only=True)
        return _relay_forward_start(handle, f"gather_{group}_forward")

    def finish_gather(handle, group, after):
        return _exchange_wait(handle, f"gather_{group}_wait", after)

    landed = lambda handle: handle[4][0]

    c_all = _all_gather([c], "gather_cond")[0].reshape(NDEV, D)
    ncol = ada_w.shape[2]
    ada_b_mine = lax.dynamic_slice_in_dim(ada_b, me * ncol, ncol, axis=1)
    mod_cols = jnp.stack([
        _matmul(c_all, ada_w.reshape(depth * D, ncol), mode="nn", name=f"ada_fwd_{i}", out_dtypes=[F32], tm=NDEV,
                tn=ncol // 2, tk=D, b_first_block=i,
                a_pre=_silu, precision=HIGHEST, epilogue=lambda acc, b: (acc + b,), extras=[(ada_b_mine[i:i + 1], "row")])
        for i in range(depth)])
    mod_all = _all_gather([mod_cols], "gather_mod")[0]
    mod = lax.dynamic_index_in_dim(mod_all, me, axis=2, keepdims=False)
    fox_handle, token = start_gather("fox", mod_all)
    fox_out_handle, token = start_gather("fox_out", token)
    mod = tied(mod, token).transpose(1, 0, 2).reshape(depth, n_mod, 1, D)
    sh_mix, sc_mix, g_mix, sh_mlp, sc_mlp, g_mlp = (mod[:, k] for k in range(n_mod))
    b_f = _pad_cols(fox_b_f, LANES)

    def residual(acc, x_in, gate):
        return (x_in + gate * acc, acc)

    def mlp_fwd(i, x_in, handle, relay_next=None):
        h, inv = _rms_mod_fwd(x_in, row(norm_mlp[i]), sh_mlp[i], sc_mlp[i], f"mlp_norm_{i}")
        w_up_t, w_down = finish_gather(handle, f"mlp{i}", h)
        w_up_t, w_down = w_up_t.reshape(FF, D), w_down.reshape(FF, D)
        r = _matmul(h, w_up_t, mode="nt", name=f"mlp_up_{i}", out_dtypes=[BF16], tm=1024, tn=1024, tk=D,
                    epilogue=lambda acc: (jnp.maximum(acc, 0.0),))
        next_handle = relay_gather(relay_next[1], relay_next[0], r) if relay_next else None
        x_out, y = _matmul(r, w_down, mode="nn", name=f"mlp_down_{i}", out_dtypes=[F32, BF16], tm=512, tn=512, tk=FF,
                           n_outer=True, a_pre=jnp.square, epilogue=residual, extras=[(x_in, "tile"), (g_mlp[i], "row")],
                           after=[landed(next_handle)] if relay_next else [])
        return x_out, (x_in, h, inv, r, y, w_up_t, w_down), next_handle

    def mlp_bwd(i, dx, dy, dgate, saved, following, after):
        x_in, h, inv, r, y, w_up_t, w_down = saved
        du = _matmul(dy, w_down, mode="nt", name=f"mlp_down_bwd_{i}", out_dtypes=[BF16], tm=1024, tn=1024, tk=D,
                     epilogue=lambda acc, rv: (acc * (2.0 * rv.astype(F32)),), extras=[(r, "tile")], after=after)
        d_down = _matmul(r, dy, mode="tn", name=f"mlp_down_wgrad_{i}", out_dtypes=[BF16], tm=512, tn=1024, tk=S,
                         a_pre=jnp.square)
        dh = _matmul(du, w_up_t, mode="nn", name=f"mlp_up_bwd_{i}", out_dtypes=[F32], tm=512, tn=512, tk=FF, n_outer=True)
        d_up = _matmul(h, du, mode="tn", name=f"mlp_up_wgrad_{i}", out_dtypes=[BF16], tm=512, tn=FF // NDEV, tk=S,
                       out_shards=True)
        dx, dsh, dsc, dgain, dy_next, dgate_next = _rms_mod_bwd(dh, x_in, inv, dx, row(norm_mlp[i]), sc_mlp[i],
                                                                f"mlp_norm_bwd_{i}", following)
        handle, token = _exchange_start([d_up, d_down.reshape(NDEV, FF // NDEV, D)], f"scatter_mlp{i}_start", True, dx)
        return dx, (dsh, dsc, dgate, dgain), handle, token, dy_next, dgate_next

    h0, inv0 = _rms_mod_fwd(x0, row(norm_mix[0]), sh_mix[0], sc_mix[0], "fox_norm")
    (w_in_t,) = finish_gather(relay_gather(fox_handle, "fox", h0), "fox", h0)
    mlp0_handle, token = start_gather("mlp0", w_in_t)
    w_in_t = w_in_t.reshape(3 * D + H, D)
    w_f_t = _pad_rows(w_in_t[3 * D:], LANES)
    column_scale = jnp.concatenate([jnp.ones((1, D), F32), jnp.full((1, D), KEY_SCALE, F32), jnp.ones((1, D), F32)], axis=1)
    qkv = _matmul(h0, w_in_t, mode="nt", name="fox_qkv", out_dtypes=[BF16], tm=1024, tn=1024, tk=D, n=3 * D, after=[token],
                  epilogue=lambda acc, mult: (acc * mult,), extras=[(column_scale, "row")])
    fox_out_handle = relay_gather(fox_out_handle, "fox_out", qkv)
    f_logit = _matmul(h0, w_f_t, mode="nt", name="fox_forget_logits", out_dtypes=[F32], tm=1024, tn=LANES, tk=D,
                      after=[landed(fox_out_handle)])
    f_cum = _forget_cumsum(f_logit, b_f, "fox_forget_cumsum")
    f_heads = f_cum[:, :H].T
    f_col, f_row = f_heads.reshape(H, S, 1), f_heads.reshape(H, 1, S)
    o, lse = _attn_fwd(qkv, f_col, f_row, "fox_attention")
    w_fox_out = finish_gather(fox_out_handle, "fox_out", o)[0].reshape(D, D)
    mlp0_handle = relay_gather(mlp0_handle, "mlp0", o)
    conv_handle, token = start_gather("conv", landed(mlp0_handle))
    mlp1_handle, token = start_gather("mlp1", token)
    x1, mix0 = _matmul(o, w_fox_out, mode="nn", name="fox_out", out_dtypes=[F32, BF16], tm=512, tn=1024, tk=D,
                       epilogue=residual, extras=[(x0, "tile"), (g_mix[0], "row")], after=[token])
    x2, mlp0, conv_handle = mlp_fwd(0, x1, mlp0_handle, ("conv", conv_handle))

    h1, inv1 = _rms_mod_fwd(x2, row(norm_mix[1]), sh_mix[1], sc_mix[1], "conv_norm")
    w_conv_in_t, w_taps, w_conv_out = finish_gather(conv_handle, "conv", h1)
    w_conv_in_t = w_conv_in_t.reshape(3 * D, D)
    w_taps = _pad_rows(w_taps.transpose(1, 0, 2).reshape(CONV_WIDTH, D), 8)
    w_conv_out = w_conv_out.reshape(D, D)
    proj = _matmul(h1, w_conv_in_t, mode="nt", name="conv_in", out_dtypes=[BF16], tm=1024, tn=1024, tk=D)
    mlp1_handle = relay_gather(mlp1_handle, "mlp1", proj)
    z = _conv_fwd(proj, w_taps, "conv_mix")
    x3, mix1 = _matmul(z, w_conv_out, mode="nn", name="conv_out", out_dtypes=[F32, BF16], tm=512, tn=1024, tk=D,
                       epilogue=residual, extras=[(x2, "tile"), (g_mix[1], "row")], after=[landed(mlp1_handle)])
    x4, mlp1, _ = mlp_fwd(1, x3, mlp1_handle)

    dx, d_final, loss_lanes, dy, dgate = _final_loss_bwd(x4, target, row(final_norm), (mlp1[4], g_mlp[1]), "loss_head")

    dx, dmod_mlp1, mlp1_scatter, token, dmix, dg_mix1 = mlp_bwd(1, dx, dy, dgate, mlp1, (mix1, g_mix[1]), [])
    dz = _matmul(dmix, w_conv_out, mode="nt", name="conv_out_bwd", out_dtypes=[BF16], tm=1024, tn=1024, tk=D,
                 after=[token])
    d_conv_out = _matmul(z, dmix, mode="tn", name="conv_out_wgrad", out_dtypes=[BF16], tm=512, tn=1024, tk=S)
    db, dc, du, d_taps = _conv_bwd(proj, dz, w_taps, "conv_mix_bwd")
    dproj = jnp.concatenate([db, dc, du], axis=1)
    dh1 = _matmul(dproj, w_conv_in_t, mode="nn", name="conv_in_bwd", out_dtypes=[F32], tm=512, tn=512, tk=3 * D, n_outer=True)
    d_conv_in = _matmul(h1, dproj, mode="tn", name="conv_in_wgrad", out_dtypes=[BF16], tm=512, tn=3 * D // NDEV, tk=S,
                        out_shards=True)
    dx, dsh1, dsc1, dgain_mix1, dy, dgate = _rms_mod_bwd(dh1, x2, inv1, dx, row(norm_mix[1]), sc_mix[1], "conv_norm_bwd",
                                                         (mlp0[4], g_mlp[0]))
    d_taps_split = d_taps[:CONV_WIDTH].reshape(CONV_WIDTH, NDEV, -1).transpose(1, 0, 2)
    conv_scatter, token = _exchange_start([d_conv_in, d_taps_split, d_conv_out.reshape(NDEV, D // NDEV, D)],
                                          "scatter_conv_start", True, dx)

    dx, dmod_mlp0, mlp0_scatter, token, dmix, dg_mix0 = mlp_bwd(0, dx, dy, dgate, mlp0, (mix0, g_mix[0]), [token])
    do = _matmul(dmix, w_fox_out, mode="nt", name="fox_out_bwd", out_dtypes=[BF16], tm=1024, tn=1024, tk=D,
                 after=[token])
    d_fox_out = _matmul(o, dmix, mode="tn", name="fox_out_wgrad", out_dtypes=[BF16], tm=512, tn=1024, tk=S)
    fox_out_scatter, token = _exchange_start([d_fox_out.reshape(NDEV, D // NDEV, D)], "scatter_fox_out_start", True, d_fox_out)
    dq, dk, dv, dfk = _attn_bwd(qkv, do, f_col, tied(f_row, token), lse, "fox_attention_bwd")
    dqkv = [dq, dk, dv]
    dfk_lanes = _pad_cols(dfk.reshape(H, S).T, LANES)
    df_logit, db_f = _forget_bwd(dfk_lanes, f_logit, b_f, "fox_forget_bwd")
    d_qkv_t = _matmul(dqkv, h0, mode="tn", name="fox_qkv_wgrad", out_dtypes=[BF16], tm=512, tn=1024, tk=S)
    d_f_t = _matmul(df_logit, h0, mode="tn", name="fox_forget_wgrad", out_dtypes=[BF16], tm=LANES, tn=1024, tk=S)
    d_fox_in = jnp.concatenate([d_qkv_t, d_f_t[:H]], axis=0).reshape(NDEV, -1, D)
    fox_scatter, token = _exchange_start([d_fox_in], "scatter_fox_start", True, d_fox_in)
    dh0_f = _matmul(df_logit, w_f_t, mode="nn", name="fox_forget_logits_bwd", out_dtypes=[F32], tm=1024, tn=1024, tk=LANES,
                    after=[token])
    dh0 = _matmul(dqkv, w_in_t, mode="nn", name="fox_qkv_bwd", out_dtypes=[F32], tm=512, tn=512, tk=3 * D, n_outer=True,
                  epilogue=lambda acc, extra: (acc + extra,), extras=[(dh0_f, "tile")])
    dx, dsh0, dsc0, dgain_mix0 = _rms_mod_bwd(dh0, x0, inv0, dx, row(norm_mix[0]), sc_mix[0], "fox_norm_bwd")
    grad_x = dx.reshape(1, S, D)

    dmod = jnp.concatenate([
        jnp.concatenate([dsh0, dsc0, dg_mix0, dmod_mlp0[0], dmod_mlp0[1], dmod_mlp0[2]], axis=1),
        jnp.concatenate([dsh1, dsc1, dg_mix1, dmod_mlp1[0], dmod_mlp1[1], dmod_mlp1[2]], axis=1)], axis=0)
    small_sizes = [depth * n_mod * D, depth * D, depth * D, H, D, 1]
    n_small = sum(small_sizes)
    n_rows = -(-n_small // (8 * LANES)) * 8

    def pack(parts):
        flat = jnp.concatenate([p.reshape(-1) for p in parts])
        return jnp.pad(flat, (0, n_rows * LANES - n_small)).reshape(n_rows, LANES)

    def unpack(packed, shapes):
        flat, out, at = packed.reshape(-1), [], 0
        for size, shape in zip(small_sizes, shapes):
            out.append(flat[at:at + size].reshape(shape))
            at += size
        return out

    small_partial = pack([dmod, jnp.concatenate([dgain_mix0, dgain_mix1], axis=0),
                          jnp.concatenate([dmod_mlp0[3], dmod_mlp1[3]], axis=0), db_f[0, :H], d_final, loss_lanes[0, :1]])
    small_handle, token = _exchange_start([small_partial], "gather_small_start", False, dx)

    up1, down1 = _exchange_wait(mlp1_scatter, "scatter_mlp1_wait", token)
    up_out = _adamw(mlp_w_up, m_mlp_w_up, v_mlp_w_up, up1, "adamw_mlp_w_up_1", layer=1)
    down_out = _adamw(mlp_w_down, m_mlp_w_down, v_mlp_w_down, down1, "adamw_mlp_w_down_1", layer=1)
    cin, taps, cout = _exchange_wait(conv_scatter, "scatter_conv_wait", down_out[0])
    conv_in_out = _adamw(conv_w_in, m_conv_w_in, v_conv_w_in, cin, "adamw_conv_w_in")
    conv_w_res = _adamw(conv_w, m_conv_w, v_conv_w, taps, "adamw_conv_w")
    conv_out_out = _adamw(conv_w_out, m_conv_w_out, v_conv_w_out, cout, "adamw_conv_w_out")
    up0, down0 = _exchange_wait(mlp0_scatter, "scatter_mlp0_wait", conv_out_out[0])
    up_out = _adamw(mlp_w_up, m_mlp_w_up, v_mlp_w_up, up0, "adamw_mlp_w_up_0", layer=0, prev=up_out)
    down_out = _adamw(mlp_w_down, m_mlp_w_down, v_mlp_w_down, down0, "adamw_mlp_w_down_0", layer=0, prev=down_out)

    small_parts = _exchange_wait(small_handle, "gather_small_wait", down_out[0])[0]
    small_shapes = [ada_b.shape, norm_mix.shape, norm_mlp.shape, fox_b_f.shape, final_norm.shape]
    loss = jnp.sum(small_parts.reshape(NDEV, -1)[:, n_small - 1])
    unused = jnp.zeros((1,), F32)
    small_out = _adamw(pack([ada_b, norm_mix, norm_mlp, fox_b_f, final_norm, unused])[None],
                       pack([m_ada_b, m_norm_mix, m_norm_mlp, m_fox_b_f, m_final_norm, unused])[None],
                       pack([v_ada_b, v_norm_mix, v_norm_mlp, v_fox_b_f, v_final_norm, unused])[None], small_parts,
                       "adamw_small")
    small_out = [unpack(t, small_shapes) for t in small_out]

    dmod_all = small_parts.reshape(NDEV, -1)[:, :depth * n_mod * D].reshape(NDEV, depth, n_mod * D)
    dmod_mine = lax.dynamic_slice_in_dim(dmod_all, me * ncol, ncol, axis=2)
    ada_out = None
    for i in range(depth):
        d_ada = _matmul(c_all, dmod_mine[:, i], mode="tn", name=f"ada_wgrad_{i}", out_dtypes=[F32], tm=1024, tn=ncol // 2,
                        tk=NDEV, a_pre=_silu, precision=HIGHEST)
        ada_out = _adamw(ada_w, m_ada_w, v_ada_w, d_ada[None], f"adamw_ada_w_{i}", layer=i, prev=ada_out)

    (fout,) = _exchange_wait(fox_out_scatter, "scatter_fox_out_wait", ada_out[0])
    fox_out_out = _adamw(fox_w_out, m_fox_w_out, v_fox_w_out, fout, "adamw_fox_w_out")
    (fin,) = _exchange_wait(fox_scatter, "scatter_fox_wait", fox_out_out[0])
    swap = lambda t: jnp.swapaxes(t, 1, 2)
    fox_in_out = [swap(t) for t in _adamw(swap(fox_w_in), swap(m_fox_w_in), swap(v_fox_w_in), fin, "adamw_fox_w_in")]

    outputs = [loss, grad_x]
    for kind in range(4):
        sm = small_out[kind]
        outputs += [ada_out[kind], sm[0], sm[1], sm[2], fox_in_out[kind], sm[3], fox_out_out[kind], conv_in_out[kind],
                    conv_w_res[kind], conv_out_out[kind], up_out[kind], down_out[kind], sm[4]]
    return tuple(outputs)
```

```python
import math

import jax
import jax.numpy as jnp
from jax import lax
from jax.experimental import pallas as pl
from jax.experimental.pallas import tpu as pltpu

F32 = jnp.float32
BF16 = jnp.bfloat16
MESH = pl.DeviceIdType.MESH
NDEV = 8
HEAD_DIM = 128
LANES = 128
CONV_WIDTH = 3
RMS_EPS = 1e-6
ADAM_LR, ADAM_B1, ADAM_B2, ADAM_EPS, ADAM_WD, ADAM_STEP = 0.001, 0.9, 0.999, 1e-08, 0.01, 10
NEG = -1e30
V7X_VMEM_BYTES = 64 * 1024 * 1024
VMEM_HEADROOM = 12 * 1024 * 1024
HBM = pl.BlockSpec(memory_space=pltpu.HBM)
HIGHEST = lax.Precision.HIGHEST


def _nbytes(shape, dtype):
    return math.prod(shape) * jnp.dtype(dtype).itemsize


def _params(semantics, block_bytes, temp_bytes=0):
    limit = min(2 * block_bytes + temp_bytes + VMEM_HEADROOM, V7X_VMEM_BYTES - 4 * 1024 * 1024)
    return pltpu.CompilerParams(dimension_semantics=semantics, vmem_limit_bytes=int(limit))


def _my_index():
    return lax.axis_index("x") * 4 + lax.axis_index("y") * 2 + lax.axis_index("c")


def _peer(r):
    x, y, c = lax.axis_index("x"), lax.axis_index("y"), lax.axis_index("c")
    px = 1 - x if (r >> 2) & 1 else x
    py = 1 - y if (r >> 1) & 1 else y
    pc = 1 - c if r & 1 else c
    return (px, py, pc), px * 4 + py * 2 + pc


def _exchange(arrays, name, scatter, after=None):
    n = len(arrays)
    after = [] if after is None else list(after)

    def body(*refs):
        ins, outs = refs[:n], refs[n + len(after):2 * n + len(after)]
        send_sems, recv_sems, local_sems = refs[2 * n + len(after):]
        me = _my_index()
        local = []
        for a in range(n):
            src = ins[a].at[me] if scatter else ins[a]
            local.append(pltpu.make_async_copy(src, outs[a].at[me], local_sems.at[a]))
            local[-1].start()
        sends = []
        for r in range(1, NDEV):
            peer, pidx = _peer(r)
            for a in range(n):
                src = ins[a].at[pidx] if scatter else ins[a]
                cp = pltpu.make_async_remote_copy(
                    src_ref=src, dst_ref=outs[a].at[me],
                    send_sem=send_sems.at[a * (NDEV - 1) + r - 1], recv_sem=recv_sems.at[a * (NDEV - 1) + r - 1],
                    device_id=peer, device_id_type=MESH)
                cp.start()
                sends.append(cp)
        for r in range(1, NDEV):
            peer, pidx = _peer(r)
            for a in range(n):
                src = ins[a].at[pidx] if scatter else ins[a]
                pltpu.make_async_remote_copy(
                    src_ref=src, dst_ref=outs[a].at[pidx],
                    send_sem=send_sems.at[a * (NDEV - 1) + r - 1], recv_sem=recv_sems.at[a * (NDEV - 1) + r - 1],
                    device_id=peer, device_id_type=MESH).wait_recv()
        for cp in sends:
            cp.wait_send()
        for cp in local:
            cp.wait()

    out_shape = [jax.ShapeDtypeStruct(a.shape if scatter else (NDEV,) + a.shape, a.dtype) for a in arrays]
    return pl.pallas_call(
        body, name=name, out_shape=out_shape, in_specs=[HBM] * n + [ANY] * len(after), out_specs=[HBM] * n,
        scratch_shapes=[pltpu.SemaphoreType.DMA((n * (NDEV - 1),)), pltpu.SemaphoreType.DMA((n * (NDEV - 1),)),
                        pltpu.SemaphoreType.DMA((n,))],
    )(*arrays, *after)


def _all_gather(arrays, name, after=None):
    return _exchange(arrays, name, scatter=False, after=after)


SEM = pl.BlockSpec(memory_space=pltpu.SEMAPHORE)
ANY = pl.BlockSpec(memory_space=pl.ANY)
DATAFLOW = pltpu.SideEffectType.DATAFLOW_SIDE_EFFECTING
TOKEN_SHAPE = (8, LANES)


SIBLING = 1
OTHER_CHIPS = (4, 2, 6)


def _exchange_start(arrays, name, scatter, after, relay=False):
    n = len(arrays)
    n_sems = n * (NDEV - 1)
    assert not (relay and scatter)

    def body(*refs):
        ins = refs[:n]
        send_sems, recv_sems = refs[n + 1], refs[n + 2]
        lands, token = refs[2 * n + 3:3 * n + 3], refs[3 * n + 3]
        me = _my_index()
        for r in (SIBLING, *OTHER_CHIPS) if relay else range(1, NDEV):
            peer, pidx = _peer(r)
            for a in range(n):
                src = ins[a].at[pidx] if scatter else ins[a]
                pltpu.make_async_remote_copy(
                    src_ref=src, dst_ref=lands[a].at[me],
                    send_sem=send_sems.at[a * (NDEV - 1) + r - 1], recv_sem=recv_sems.at[a * (NDEV - 1) + r - 1],
                    device_id=peer, device_id_type=MESH).start()
        token[...] = jnp.zeros(TOKEN_SHAPE, F32)

    land_shapes = [a.shape if scatter else (NDEV,) + a.shape for a in arrays]
    srcs = [pltpu.with_memory_space_constraint(a, pltpu.HBM) for a in arrays]
    outs = pl.pallas_call(
        body, name=name,
        out_shape=(pltpu.SemaphoreType.DMA((n_sems,)), pltpu.SemaphoreType.DMA((n_sems,)),
                   *[pltpu.HBM(a.shape, a.dtype) for a in arrays], *[pltpu.HBM(s, a.dtype) for s, a in zip(land_shapes, arrays)],
                   jax.ShapeDtypeStruct(TOKEN_SHAPE, F32)),
        in_specs=[HBM] * n + [ANY],
        out_specs=(SEM, SEM, *[HBM] * (2 * n), pl.BlockSpec(memory_space=pltpu.VMEM)),
        input_output_aliases={i: 2 + i for i in range(n)},
        compiler_params=pltpu.CompilerParams(has_side_effects=DATAFLOW),
    )(*srcs, after)
    return (scatter, relay, [(outs[0], outs[1])], list(outs[2:2 + n]), list(outs[2 + n:2 + 2 * n])), outs[-1]


def _relay_forward_start(handle, name):
    scatter, relay, sems, srcs, lands = handle
    n = len(lands)
    n_sems = n * len(OTHER_CHIPS)

    def body(*refs):
        land_refs, send_sems, recv_sems = refs[:n], refs[n], refs[n + 1]
        sibling, _ = _peer(SIBLING)
        for j, r in enumerate(OTHER_CHIPS):
            _, pidx = _peer(r)
            for a in range(n):
                pltpu.make_async_remote_copy(
                    src_ref=land_refs[a].at[pidx], dst_ref=land_refs[a].at[pidx],
                    send_sem=send_sems.at[a * len(OTHER_CHIPS) + j], recv_sem=recv_sems.at[a * len(OTHER_CHIPS) + j],
                    device_id=sibling, device_id_type=MESH).start()

    outs = pl.pallas_call(
        body, name=name,
        out_shape=(pltpu.SemaphoreType.DMA((n_sems,)), pltpu.SemaphoreType.DMA((n_sems,)),
                   *[pltpu.HBM(t.shape, t.dtype) for t in lands]),
        in_specs=[HBM] * n, out_specs=(SEM, SEM, *[HBM] * n),
        input_output_aliases={i: 2 + i for i in range(n)},
        compiler_params=pltpu.CompilerParams(has_side_effects=DATAFLOW),
    )(*lands)
    return (scatter, relay, sems + [(outs[0], outs[1])], srcs, list(outs[2:]))


def _exchange_wait(handle, name, after, arrivals_only=False):
    scatter, relay, sems, srcs, lands = handle
    n = len(srcs)
    forwarded = len(sems) == 2
    assert not arrivals_only or (relay and not forwarded)

    def body(*refs):
        src_refs, land_refs = refs[:n], refs[n:2 * n]
        send_sems, recv_sems = refs[2 * n], refs[2 * n + 1]
        for r in (SIBLING, *OTHER_CHIPS) if relay else range(1, NDEV):
            peer, pidx = _peer(r)
            for a in range(n):
                src = src_refs[a].at[pidx] if scatter else src_refs[a]
                cp = pltpu.make_async_remote_copy(
                    src_ref=src, dst_ref=land_refs[a].at[pidx],
                    send_sem=send_sems.at[a * (NDEV - 1) + r - 1], recv_sem=recv_sems.at[a * (NDEV - 1) + r - 1],
                    device_id=peer, device_id_type=MESH)
                if arrivals_only:
                    if r in OTHER_CHIPS:
                        cp.wait_recv()
                else:
                    cp.wait_send()
                    if not (relay and r in OTHER_CHIPS):
                        cp.wait_recv()
        if forwarded:
            fwd_send, fwd_recv = refs[2 * n + 2], refs[2 * n + 3]
            sibling, _ = _peer(SIBLING)
            for j, r in enumerate(OTHER_CHIPS):
                _, pidx = _peer(r ^ SIBLING)
                for a in range(n):
                    cp = pltpu.make_async_remote_copy(
                        src_ref=src_refs[a], dst_ref=land_refs[a].at[pidx],
                        send_sem=fwd_send.at[a * len(OTHER_CHIPS) + j], recv_sem=fwd_recv.at[a * len(OTHER_CHIPS) + j],
                        device_id=sibling, device_id_type=MESH)
                    cp.wait_send()
                    cp.wait_recv()

    flat_sems = [s for pair in sems for s in pair]
    outs = pl.pallas_call(
        body, name=name,
        out_shape=tuple(pltpu.HBM(t.shape, t.dtype) for t in (*srcs, *lands)),
        in_specs=[HBM] * (2 * n) + [SEM] * len(flat_sems) + [ANY], out_specs=tuple([HBM] * (2 * n)),
        input_output_aliases={i: i for i in range(2 * n)},
        compiler_params=pltpu.CompilerParams(has_side_effects=DATAFLOW),
    )(*srcs, *lands, *flat_sems, after)
    if arrivals_only:
        return (scatter, relay, sems, list(outs[:n]), list(outs[n:]))
    me = _my_index()
    mine = [lax.dynamic_index_in_dim(s, me, 0, keepdims=False) if scatter else s for s in outs[:n]]
    return [lax.dynamic_update_index_in_dim(land, own, me, 0) for land, own in zip(outs[n:], mine)]


def _matmul(a, b, *, mode, name, out_dtypes, tm, tn, tk, epilogue=None, extras=(), a_pre=None,
            out_shards=False, n_outer=False, precision=None, after=(), n=None, b_first_block=0, halves=1):
    after = list(after)
    n_after = len(after)
    parts = list(a) if isinstance(a, (list, tuple)) else [a]
    n_parts = len(parts)
    rows, cols = parts[0].shape
    K, M = (rows, cols * n_parts) if mode == "tn" else (cols * n_parts, rows)
    N = n if n is not None else (b.shape[0] if mode == "nt" else b.shape[1])
    tm, tn, tk = min(tm, M), min(tn, N), min(tk, K)
    assert M % tm == 0 and N % tn == 0 and K % tk == 0, (name, M, N, K, tm, tn, tk)
    nm, nn, nk = M // tm, N // tn, K // tk
    assert n_parts == 1 or (nk == 1 and a_pre is None and mode in ("nn", "tn") and cols % tm == 0), name
    assert halves == 1 or (nk == 1 and n_parts == 1 and tn % (halves * LANES) == 0), name
    blocks_per_part = cols // tm if mode == "tn" else 1
    n_out, n_ext = len(out_dtypes), len(extras)
    contract = {"nn": ((1,), (0,)), "nt": ((1,), (1,)), "tn": ((0,), (0,))}[mode]

    def body(*refs):
        a_refs, b_ref = refs[:n_parts], refs[n_parts]
        ext_refs = refs[n_parts + 1:n_parts + 1 + n_ext]
        first_out = n_parts + 1 + n_ext + n_after
        out_refs = refs[first_out:first_out + n_out]
        acc_ref = refs[first_out + n_out] if nk > 1 else None

        def product(a_ref, bv):
            av = a_ref[...] if a_pre is None else a_pre(a_ref[...])
            if precision is None:
                av, bv = av.astype(BF16), bv.astype(BF16)
            return lax.dot_general(av, bv, (contract, ((), ())), preferred_element_type=F32, precision=precision)

        def finish(acc):
            vals = (acc,) if epilogue is None else epilogue(acc, *[r[...] for r in ext_refs])
            for r, v in zip(out_refs, vals):
                r[...] = v.astype(r.dtype)

        if n_parts > 1 and mode == "tn":
            i = pl.program_id(1 if n_outer else 0)
            for p in range(n_parts):
                @pl.when(i // blocks_per_part == p)
                def _(p=p):
                    finish(product(a_refs[p], b_ref[...]))
            return
        if halves > 1:
            width = tn // halves
            for h in range(halves):
                at_cols = pl.ds(h * width, width)
                acc = product(a_refs[0], b_ref[at_cols, :] if mode == "nt" else b_ref[:, at_cols])
                vals = (acc,) if epilogue is None else epilogue(acc, *[r[:, at_cols] for r in ext_refs])
                for r, v in zip(out_refs, vals):
                    r[:, at_cols] = v.astype(r.dtype)
            return
        part = product(a_refs[0], b_ref[...] if n_parts == 1 else b_ref[0:cols, :])
        for p in range(1, n_parts):
            part = part + product(a_refs[p], b_ref[p * cols:(p + 1) * cols, :])

        if nk == 1:
            finish(part)
        else:
            k = pl.program_id(2)

            @pl.when(k == 0)
            def _():
                acc_ref[...] = part

            @pl.when(k > 0)
            def _():
                acc_ref[...] += part

            @pl.when(k == nk - 1)
            def _():
                finish(acc_ref[...])

    def at(index):
        return (lambda j, i, k: index(i, j, k)) if n_outer else index

    if n_parts == 1:
        a_specs = [pl.BlockSpec((tk, tm), at(lambda i, j, k: (k, i))) if mode == "tn"
                   else pl.BlockSpec((tm, tk), at(lambda i, j, k: (i, k)))]
    elif mode == "tn":
        a_specs = [pl.BlockSpec((tk, tm), at(lambda i, j, k, p=p: (k, jnp.clip(i - p * blocks_per_part, 0, blocks_per_part - 1))))
                   for p in range(n_parts)]
    else:
        a_specs = [pl.BlockSpec((tm, cols), at(lambda i, j, k: (i, 0))) for _ in parts]
    b_spec = (pl.BlockSpec((tn, tk), at(lambda i, j, k: (j, k))) if mode == "nt"
              else pl.BlockSpec((tk, tn), at(lambda i, j, k: (k + b_first_block, j))))
    in_specs = a_specs + [b_spec]
    block_bytes = _nbytes((tm, tk), parts[0].dtype) * (n_parts if mode == "tn" else 1) + _nbytes((tk, tn), b.dtype)
    for arr, kind in extras:
        if kind == "tile":
            assert arr.shape == (M, N), (name, arr.shape)
            in_specs.append(pl.BlockSpec((tm, tn), at(lambda i, j, k: (i, j))))
            block_bytes += _nbytes((tm, tn), arr.dtype)
        else:
            assert arr.shape == (1, N), (name, arr.shape)
            in_specs.append(pl.BlockSpec((1, tn), at(lambda i, j, k: (0, j))))
    in_specs += [ANY] * n_after
    if out_shards:
        assert n_out == 1 and tn * NDEV == N
        out_shape = [jax.ShapeDtypeStruct((NDEV, M, tn), out_dtypes[0])]
        out_specs = [pl.BlockSpec((None, tm, tn), at(lambda i, j, k: (j, i, 0)))]
    else:
        out_shape = [jax.ShapeDtypeStruct((M, N), d) for d in out_dtypes]
        out_specs = [pl.BlockSpec((tm, tn), at(lambda i, j, k: (i, j))) for _ in out_dtypes]
    block_bytes += sum(_nbytes((tm, tn), d) for d in out_dtypes)
    scratch = [pltpu.VMEM((tm, tn), F32)] if nk > 1 else []
    outs = pl.pallas_call(
        body, name=name, grid=(nn, nm, nk) if n_outer else (nm, nn, nk), in_specs=in_specs, out_specs=out_specs,
        out_shape=out_shape, scratch_shapes=scratch,
        compiler_params=_params(("parallel", "parallel", "arbitrary"), block_bytes, 2 * tm * tn * 4),
    )(*parts, b, *[arr for arr, _ in extras], *after)
    return outs[0] if n_out == 1 else outs


def _rowwise(fn, tiled, smalls, out_tiles, out_sums, *, name, ts=256):
    S = tiled[0].shape[0]
    ts = min(ts, S)
    assert S % ts == 0
    nt, ns, no, na = len(tiled), len(smalls), len(out_tiles), len(out_sums)

    def body(*refs):
        t_refs, s_refs = refs[:nt], refs[nt:nt + ns]
        o_refs, a_refs = refs[nt + ns:nt + ns + no], refs[nt + ns + no:]
        tile_vals, sum_vals = fn([r[...] for r in t_refs], [r[...] for r in s_refs])
        for r, v in zip(o_refs, tile_vals):
            r[...] = v.astype(r.dtype)

        @pl.when(pl.program_id(0) == 0)
        def _():
            for r in a_refs:
                r[...] = jnp.zeros_like(r)

        for r, v in zip(a_refs, sum_vals):
            r[...] += v

    in_specs = [pl.BlockSpec((ts, t.shape[1]), lambda i: (i, 0)) for t in tiled]
    in_specs += [pl.BlockSpec(s.shape, lambda i: (0, 0)) for s in smalls]
    out_specs = [pl.BlockSpec((ts, w), lambda i: (i, 0)) for w, _ in out_tiles]
    out_specs += [pl.BlockSpec((1, w), lambda i: (0, 0)) for w in out_sums]
    out_shape = [jax.ShapeDtypeStruct((S, w), d) for w, d in out_tiles]
    out_shape += [jax.ShapeDtypeStruct((1, w), F32) for w in out_sums]
    block_bytes = sum(_nbytes((ts, t.shape[1]), t.dtype) for t in tiled) + sum(_nbytes((ts, w), d) for w, d in out_tiles)
    width = max(t.shape[1] for t in tiled)
    outs = pl.pallas_call(
        body, name=name, grid=(S // ts,), in_specs=in_specs, out_specs=out_specs, out_shape=out_shape,
        compiler_params=_params(("arbitrary",), block_bytes, 6 * ts * width * 4),
    )(*tiled, *smalls)
    return outs[:no], outs[no:]


def _colsum(v):
    return jnp.sum(v, axis=0, keepdims=True)


def _rms_mod_fwd(x, gain, shift, scale, name):
    def fn(tiles, smalls):
        (xv,), (g, sh, sc) = tiles, smalls
        inv = lax.rsqrt(jnp.mean(xv * xv, axis=-1, keepdims=True) + RMS_EPS)
        h = (xv * inv) * g * (1.0 + sc) + sh
        return (h, inv), ()

    D = x.shape[1]
    (h, inv), _ = _rowwise(fn, [x], [gain, shift, scale], [(D, BF16), (1, F32)], [], name=name)
    return h, inv


def _gated(dxv, following):
    yv, gate = following
    return dxv * gate, _colsum(dxv * yv)


def _rms_mod_bwd(dh, x, inv, dx_res, gain, scale, name, following=None):
    def fn(tiles, smalls):
        dhv, xv, iv, dres = tiles[:4]
        g, sc = smalls[:2]
        dhv = dhv.astype(F32)
        xhat = xv * iv
        dr = dhv * (1.0 + sc)
        dxhat = dr * g
        dxv = dres + iv * (dxhat - xhat * jnp.mean(dxhat * xhat, axis=-1, keepdims=True))
        sums = (_colsum(dhv), _colsum(dhv * (xhat * g)), _colsum(dr * xhat))
        if following is None:
            return (dxv,), sums
        dy, dgate = _gated(dxv, (tiles[4], smalls[2]))
        return (dxv, dy), (*sums, dgate)

    D = x.shape[1]
    extra = [] if following is None else [following]
    tiles, sums = _rowwise(fn, [dh, x, inv, dx_res] + [f[0] for f in extra], [gain, scale] + [f[1] for f in extra],
                           [(D, F32)] + [(D, BF16)] * len(extra), [D] * (3 + len(extra)), name=name)
    return (tiles[0], *sums[:3]) if following is None else (tiles[0], *sums[:3], tiles[1], sums[3])


def _final_loss_bwd(x, target, gain, following, name):
    D = x.shape[1]

    def fn(tiles, smalls):
        xv, tv, g = tiles[0], tiles[1], smalls[0]
        inv = lax.rsqrt(jnp.mean(xv * xv, axis=-1, keepdims=True) + RMS_EPS)
        xhat = xv * inv
        err = xhat * g - tv
        loss = 0.5 * jnp.sum(jnp.mean(err * err, axis=-1, keepdims=True), axis=0, keepdims=True)
        dout = err * (1.0 / D)
        dxhat = dout * g
        dxv = inv * (dxhat - xhat * jnp.mean(dxhat * xhat, axis=-1, keepdims=True))
        dy, dgate = _gated(dxv, (tiles[2], smalls[1]))
        return (dxv, dy), (_colsum(dout * xhat), jnp.broadcast_to(loss, (1, LANES)), dgate)

    (dx, dy), (dgain, loss, dgate) = _rowwise(fn, [x, target, following[0]], [gain, following[1]],
                                             [(D, F32), (D, BF16)], [D, LANES, D], name=name)
    return dx, dgain, loss, dy, dgate


SCAN_BLOCK = 256


def _triangle(n, lower):
    r = lax.broadcasted_iota(jnp.int32, (n, n), 0)
    c = lax.broadcasted_iota(jnp.int32, (n, n), 1)
    return (r >= c if lower else r <= c).astype(F32)


def _forget_cumsum(logits, bias, name):
    S = logits.shape[0]
    blk = min(SCAN_BLOCK, S)
    nb = S // blk

    def body(z_ref, b_ref, f_ref):
        z = z_ref[...] + b_ref[...]
        f_ref[...] = jnp.minimum(z, 0.0) - jnp.log(1.0 + jnp.exp(-jnp.abs(z)))
        tri = _triangle(blk, lower=True)

        def step(i, carry):
            off = pl.multiple_of(i * blk, blk)
            cs = jnp.dot(tri, f_ref[pl.ds(off, blk), :], preferred_element_type=F32, precision=HIGHEST) + carry
            f_ref[pl.ds(off, blk), :] = cs
            return cs[blk - 1:blk, :]

        lax.fori_loop(0, nb, step, jnp.zeros((1, LANES), F32))

    return pl.pallas_call(body, name=name, out_shape=jax.ShapeDtypeStruct((S, LANES), F32))(logits, bias)


def _forget_bwd(dfk, logits, bias, name):
    S = logits.shape[0]
    blk = min(SCAN_BLOCK, S)
    nb = S // blk

    def body(d_ref, z_ref, b_ref, o_ref, db_ref):
        tri = _triangle(blk, lower=False)

        def step(t, carry):
            off = pl.multiple_of((nb - 1 - t) * blk, blk)
            cs = jnp.dot(tri, d_ref[pl.ds(off, blk), :], preferred_element_type=F32, precision=HIGHEST) + carry
            o_ref[pl.ds(off, blk), :] = cs
            return cs[0:1, :]

        lax.fori_loop(0, nb, step, jnp.zeros((1, LANES), F32))
        z = z_ref[...] + b_ref[...]
        dz = -o_ref[...] / (1.0 + jnp.exp(z))
        o_ref[...] = dz
        db_ref[...] = _colsum(dz)

    return pl.pallas_call(
        body, name=name,
        out_shape=(jax.ShapeDtypeStruct((S, LANES), F32), jax.ShapeDtypeStruct((1, LANES), F32)),
    )(dfk, logits, bias)


KEY_SCALE = HEAD_DIM ** -0.5
ATTN_BLOCK = 512
_NT = (((1,), (1,)), ((), ()))


def _loop_in_pairs(step, count, init):
    carry = lax.fori_loop(0, count // 2, lambda t, c: step(2 * t + 1, step(2 * t, c)), init)
    return lax.fori_loop(count // 2 * 2, count, step, carry)


def _attn_specs(S, H, tb):
    q_blk = lambda part: pl.BlockSpec((tb, HEAD_DIM), lambda h, i: (i, part * H + h))
    q_all = lambda part: pl.BlockSpec((S, HEAD_DIM), lambda h, i: (0, part * H + h))
    col_blk = pl.BlockSpec((None, tb, 1), lambda h, i: (h, i, 0))
    row_all = pl.BlockSpec((None, 1, S), lambda h, i: (h, 0, 0))
    return q_blk, q_all, col_blk, row_all


FWD_HEADS = 2


def _attn_fwd(qkv, f_col, f_row, name):
    S, H = qkv.shape[0], qkv.shape[1] // (3 * HEAD_DIM)
    tb = min(ATTN_BLOCK, S)
    hp = FWD_HEADS if H % FWD_HEADS == 0 else 1
    groups, wide = H // hp, hp * HEAD_DIM
    lanes = lambda u: pl.ds(u * HEAD_DIM, HEAD_DIM)

    def body(q_ref, k_ref, v_ref, fc_ref, fr_ref, o_ref, lse_ref):
        i = pl.program_id(1)

        def step(j, carry, diagonal):
            off = pl.multiple_of(j * tb, tb)
            out = []
            for u in range(hp):
                m, l, acc = carry[u]
                k, v = k_ref[pl.ds(off, tb), lanes(u)], v_ref[pl.ds(off, tb), lanes(u)]
                s = lax.dot_general(q_ref[:, lanes(u)], k, _NT, preferred_element_type=F32)
                s = s + (fc_ref[u] - fr_ref[u, :, pl.ds(off, tb)])
                if diagonal:
                    row = lax.broadcasted_iota(jnp.int32, (tb, tb), 0)
                    col = lax.broadcasted_iota(jnp.int32, (tb, tb), 1)
                    s = jnp.where(col <= row, s, NEG)
                m_new = jnp.maximum(m, jnp.max(s, axis=-1, keepdims=True))
                p = jnp.exp(s - m_new)
                alpha = jnp.exp(m - m_new)
                l = alpha * l + jnp.sum(p, axis=-1, keepdims=True)
                acc = alpha * acc + jnp.dot(p.astype(BF16), v, preferred_element_type=F32)
                out.append((m_new, l, acc))
            return tuple(out)

        init = (jnp.full((tb, 1), NEG, F32), jnp.zeros((tb, 1), F32), jnp.zeros((tb, HEAD_DIM), F32))
        carry = _loop_in_pairs(lambda j, c: step(j, c, False), i, (init,) * hp)
        for u, (m, l, acc) in enumerate(step(i, carry, True)):
            o_ref[:, lanes(u)] = (acc / l).astype(o_ref.dtype)
            lse_ref[u] = m + jnp.log(l)

    part = lambda p, rows: pl.BlockSpec((rows, wide), lambda g, i: (i if rows == tb else 0, p * groups + g))
    col_blk = pl.BlockSpec((hp, tb, 1), lambda g, i: (g, i, 0))
    return pl.pallas_call(
        body, name=name, grid=(groups, S // tb),
        in_specs=[part(0, tb), part(1, S), part(2, S), col_blk, pl.BlockSpec((hp, 1, S), lambda g, i: (g, 0, 0))],
        out_specs=[pl.BlockSpec((tb, wide), lambda g, i: (i, g)), col_blk],
        out_shape=[jax.ShapeDtypeStruct((S, H * HEAD_DIM), BF16), jax.ShapeDtypeStruct((H, S, 1), F32)],
        compiler_params=_params(("parallel", "parallel"), 4 * S * wide * 2, 10 * hp * tb * tb * 4),
    )(qkv, qkv, qkv, f_col, f_row)


_TN = (((0,), (0,)), ((), ()))


def _attn_bwd(qkv, do, f_col, f_row, lse_col, name):
    S, H = qkv.shape[0], qkv.shape[1] // (3 * HEAD_DIM)
    tb = min(ATTN_BLOCK, S)
    nq = S // tb
    q_blk, q_all, col_blk, row_all = _attn_specs(S, H, tb)
    head_blk = pl.BlockSpec((tb, HEAD_DIM), lambda h, i: (i, h))
    head_all = pl.BlockSpec((S, HEAD_DIM), lambda h, i: (0, h))

    def body(q_ref, k_ref, v_ref, do_ref, fc_ref, fr_ref, lse_ref, dq_ref, dk_ref, dv_ref, dfk_ref,
             p_buf, dp_buf, dk_acc, dv_acc, dfk_acc):
        i = pl.program_id(1)
        q, do = q_ref[...], do_ref[...]
        fc_lse = fc_ref[...] - lse_ref[...]

        @pl.when(i == 0)
        def _():
            dk_acc[...] = jnp.zeros_like(dk_acc)
            dv_acc[...] = jnp.zeros_like(dv_acc)
            dfk_acc[...] = jnp.zeros_like(dfk_acc)

        def scores(j, delta, diagonal):
            off = pl.multiple_of(j * tb, tb)
            k, v = k_ref[pl.ds(off, tb), :], v_ref[pl.ds(off, tb), :]
            s = (lax.dot_general(q, k, _NT, preferred_element_type=F32) + fc_lse) - fr_ref[:, pl.ds(off, tb)]
            if diagonal:
                row = lax.broadcasted_iota(jnp.int32, (tb, tb), 0)
                col = lax.broadcasted_iota(jnp.int32, (tb, tb), 1)
                s = jnp.where(col <= row, s, NEG)
            p = jnp.exp(s)
            dp = lax.dot_general(do, v, _NT, preferred_element_type=F32)
            p_buf[j] = p
            dp_buf[j] = dp
            return delta + jnp.sum(p * dp, axis=-1, keepdims=True)

        delta = _loop_in_pairs(lambda j, c: scores(j, c, False), i, jnp.zeros((tb, 1), F32))
        delta = scores(i, delta, True)

        def grad(j, dq):
            off = pl.multiple_of(j * tb, tb)
            p = p_buf[j]
            ds = p * (dp_buf[j] - delta)
            ds_lo = ds.astype(BF16)
            dk_acc[pl.ds(off, tb), :] += lax.dot_general(ds_lo, q, _TN, preferred_element_type=F32)
            dv_acc[pl.ds(off, tb), :] += lax.dot_general(p.astype(BF16), do, _TN, preferred_element_type=F32)
            dfk_acc[:, pl.ds(off, tb)] += jnp.sum(ds, axis=0, keepdims=True)
            return dq + jnp.dot(ds_lo, k_ref[pl.ds(off, tb), :], preferred_element_type=F32)

        dq = _loop_in_pairs(grad, i + 1, jnp.zeros((tb, HEAD_DIM), F32))
        dq_ref[...] = dq.astype(dq_ref.dtype)

        @pl.when(i == nq - 1)
        def _():
            dk_ref[...] = (dk_acc[...] * KEY_SCALE).astype(dk_ref.dtype)
            dv_ref[...] = dv_acc[...].astype(dv_ref.dtype)
            dfk_ref[...] = dfk_acc[...]

    wide = jax.ShapeDtypeStruct((S, H * HEAD_DIM), BF16)
    return pl.pallas_call(
        body, name=name, grid=(H, nq),
        in_specs=[q_blk(0), q_all(1), q_all(2), head_blk, col_blk, row_all, col_blk],
        out_specs=[head_blk, head_all, head_all, row_all],
        out_shape=[wide, wide, wide, jax.ShapeDtypeStruct((H, 1, S), F32)],
        scratch_shapes=[pltpu.VMEM((nq, tb, tb), F32), pltpu.VMEM((nq, tb, tb), F32),
                        pltpu.VMEM((S, HEAD_DIM), F32), pltpu.VMEM((S, HEAD_DIM), F32), pltpu.VMEM((1, S), F32)],
        compiler_params=_params(("parallel", "arbitrary"), 6 * S * HEAD_DIM * 2,
                                2 * nq * tb * tb * 4 + 2 * S * HEAD_DIM * 4 + 10 * tb * tb * 4),
    )(qkv, qkv, qkv, do, f_col, f_row, lse_col)


CONV_TILE = 128


def _shift_down(v, n):
    row = lax.broadcasted_iota(jnp.int32, v.shape, 0)
    return jnp.where(row >= n, pltpu.roll(v, n, 0), 0.0)


def _shift_up(v, n):
    S = v.shape[0]
    row = lax.broadcasted_iota(jnp.int32, v.shape, 0)
    return jnp.where(row < S - n, pltpu.roll(v, S - n, 0), 0.0)


def _conv_specs(S, D, tc):
    nb = D // tc
    part = lambda p: pl.BlockSpec((S, tc), lambda j: (0, p * nb + j))
    return part, pl.BlockSpec((S, tc), lambda j: (0, j)), pl.BlockSpec((8, tc), lambda j: (0, j))


def _conv_fwd(proj, conv_w8, name):
    S, D = proj.shape[0], proj.shape[1] // 3
    tc = min(CONV_TILE, D)
    part, chan, taps = _conv_specs(S, D, tc)

    def body(b_ref, c_ref, u_ref, w_ref, z_ref):
        cu = c_ref[...].astype(F32) * u_ref[...].astype(F32)
        w = w_ref[...]
        y = w[0:1, :] * _shift_down(cu, 2) + w[1:2, :] * _shift_down(cu, 1) + w[2:3, :] * cu
        z_ref[...] = (b_ref[...].astype(F32) * y).astype(z_ref.dtype)

    return pl.pallas_call(
        body, name=name, grid=(D // tc,), in_specs=[part(0), part(1), part(2), taps], out_specs=chan,
        out_shape=jax.ShapeDtypeStruct((S, D), BF16),
        compiler_params=_params(("parallel",), 3 * _nbytes((S, tc), proj.dtype) + S * tc * 2, 6 * S * tc * 4),
    )(proj, proj, proj, conv_w8)


def _conv_bwd(proj, dz, conv_w8, name):
    S, D = proj.shape[0], proj.shape[1] // 3
    tc = min(CONV_TILE, D)
    part, chan, taps = _conv_specs(S, D, tc)

    def body(b_ref, c_ref, u_ref, dz_ref, w_ref, db_ref, dc_ref, du_ref, dw_ref):
        cv, uv = c_ref[...].astype(F32), u_ref[...].astype(F32)
        dzv, w = dz_ref[...].astype(F32), w_ref[...]
        cu = cv * uv
        cu1, cu2 = _shift_down(cu, 1), _shift_down(cu, 2)
        y = w[0:1, :] * cu2 + w[1:2, :] * cu1 + w[2:3, :] * cu
        db_ref[...] = (dzv * y).astype(db_ref.dtype)
        dy = dzv * b_ref[...].astype(F32)
        dcu = w[2:3, :] * dy + w[1:2, :] * _shift_up(dy, 1) + w[0:1, :] * _shift_up(dy, 2)
        dc_ref[...] = (dcu * uv).astype(dc_ref.dtype)
        du_ref[...] = (dcu * cv).astype(du_ref.dtype)
        dw_ref[...] = jnp.concatenate(
            [_colsum(dy * cu2), _colsum(dy * cu1), _colsum(dy * cu), jnp.zeros((8 - CONV_WIDTH, tc), F32)], axis=0)

    return pl.pallas_call(
        body, name=name, grid=(D // tc,), in_specs=[part(0), part(1), part(2), chan, taps],
        out_specs=[chan, chan, chan, taps],
        out_shape=[jax.ShapeDtypeStruct((S, D), BF16)] * 3 + [jax.ShapeDtypeStruct((8, D), F32)],
        compiler_params=_params(("parallel",), 3 * _nbytes((S, tc), proj.dtype) + _nbytes((S, tc), dz.dtype)
                                + 3 * S * tc * 2, 10 * S * tc * 4),
    )(proj, proj, proj, dz, conv_w8)


def _adamw(w, m, v, parts, name, layer=0, prev=None):
    L, R, C = w.shape
    P = parts.shape[0]
    assert parts.shape[1:] == (R, C), (name, parts.shape, w.shape)
    elem_bytes = 12 + 16 + P * parts.dtype.itemsize
    budget = 16 << 20
    tr, tc = R, C
    if R * C * elem_bytes > budget:
        if R % 8 == 0:
            tr = max(8, (budget // (C * elem_bytes)) // 8 * 8)
            while R % tr:
                tr -= 8
        else:
            tc = LANES
            while C % (2 * tc) == 0 and R * 2 * tc * elem_bytes <= budget:
                tc *= 2
            assert C % tc == 0, (name, R, C)
    c1, c2 = 1.0 - ADAM_B1 ** ADAM_STEP, 1.0 - ADAM_B2 ** ADAM_STEP

    def body(w_ref, m_ref, v_ref, p_ref, *rest):
        g_ref, d_ref, nm_ref, nv_ref = rest[-4:]
        g = p_ref[0].astype(F32)
        for p in range(1, P):
            g = g + p_ref[p].astype(F32)
        nm = ADAM_B1 * m_ref[...] + (1.0 - ADAM_B1) * g
        nv = ADAM_B2 * v_ref[...] + (1.0 - ADAM_B2) * (g * g)
        g_ref[...] = g
        nm_ref[...] = nm
        nv_ref[...] = nv
        d_ref[...] = -ADAM_LR * ((nm / c1) / (jnp.sqrt(nv / c2) + ADAM_EPS) + ADAM_WD * w_ref[...])

    blk = pl.BlockSpec((None, tr, tc), lambda i, j: (layer, i, j))
    prev = [] if prev is None else list(prev)
    return pl.pallas_call(
        body, name=name, grid=(R // tr, C // tc),
        in_specs=[blk, blk, blk, pl.BlockSpec((P, tr, tc), lambda i, j: (0, i, j))] + [ANY] * len(prev),
        out_specs=[blk] * 4, out_shape=[jax.ShapeDtypeStruct((L, R, C), F32)] * 4,
        input_output_aliases={4 + k: k for k in range(len(prev))},
        compiler_params=_params(("parallel", "parallel"), tr * tc * elem_bytes),
    )(w, m, v, parts, *prev)


def _silu(v):
    return v / (1.0 + jnp.exp(-v))


def _pad_rows(a, rows):
    return jnp.pad(a, ((0, rows - a.shape[0]), (0, 0)))


def _pad_cols(a, cols):
    return jnp.pad(a, ((0, 0), (0, cols - a.shape[1])))


def kernel(x, c, ada_w, ada_b, norm_mix, norm_mlp, fox_w_in, fox_b_f, fox_w_out, conv_w_in, conv_w, conv_w_out, mlp_w_up, mlp_w_down, final_norm, loss_target, m_ada_w, m_ada_b, m_norm_mix, m_norm_mlp, m_fox_w_in, m_fox_b_f, m_fox_w_out, m_conv_w_in, m_conv_w, m_conv_w_out, m_mlp_w_up, m_mlp_w_down, m_final_norm, v_ada_w, v_ada_b, v_norm_mix, v_norm_mlp, v_fox_w_in, v_fox_b_f, v_fox_w_out, v_conv_w_in, v_conv_w, v_conv_w_out, v_mlp_w_up, v_mlp_w_down, v_final_norm):
    S, D = x.shape[1], x.shape[2]
    H = D // HEAD_DIM
    FF = mlp_w_up.shape[2] * NDEV
    depth = ada_w.shape[0]
    n_mod = 6
    assert depth == 2 and fox_w_in.shape[0] == 1 and conv_w_in.shape[0] == 1 and H <= LANES
    me = _my_index()
    x0, target = x[0], loss_target[0]
    row = lambda vec: vec.reshape(1, -1)

    def tied(vec, token):
        return vec + token[0, 0]

    bf = lambda w: w.astype(BF16)
    gather_groups = {
        "fox": [bf(fox_w_in[0]).T],
        "fox_out": [bf(fox_w_out[0])],
        "mlp0": [bf(mlp_w_up[0]).T, bf(mlp_w_down[0])],
        "conv": [bf(conv_w_in[0]).T, conv_w[0], bf(conv_w_out[0])],
        "mlp1": [bf(mlp_w_up[1]).T, bf(mlp_w_down[1])],
    }

    def start_gather(group, after):
        return _exchange_start(gather_groups[group], f"gather_{group}_start", False, after, relay=True)

    def relay_gather(handle, group, after):
        handle = _exchange_wait(handle, f"gather_{group}_arrivals", after, arrivals_only=True)
        return _relay_forward_start(handle, f"gather_{group}_forward")

    def finish_gather(handle, group, after):
        return _exchange_wait(handle, f"gather_{group}_wait", after)

    landed = lambda handle: handle[4][0]

    c_all = _all_gather([c], "gather_cond")[0].reshape(NDEV, D)
    ncol = ada_w.shape[2]
    ada_b_mine = lax.dynamic_slice_in_dim(ada_b, me * ncol, ncol, axis=1)
    mod_cols = jnp.stack([
        _matmul(c_all, ada_w.reshape(depth * D, ncol), mode="nn", name=f"ada_fwd_{i}", out_dtypes=[F32], tm=NDEV,
                tn=ncol // 2, tk=D, b_first_block=i,
                a_pre=_silu, precision=HIGHEST, epilogue=lambda acc, b: (acc + b,), extras=[(ada_b_mine[i:i + 1], "row")])
        for i in range(depth)])
    mod_all = _all_gather([mod_cols], "gather_mod")[0]
    mod = lax.dynamic_index_in_dim(mod_all, me, axis=2, keepdims=False)
    fox_handle, token = start_gather("fox", mod_all)
    fox_out_handle, token = start_gather("fox_out", token)
    mod = tied(mod, token).transpose(1, 0, 2).reshape(depth, n_mod, 1, D)
    sh_mix, sc_mix, g_mix, sh_mlp, sc_mlp, g_mlp = (mod[:, k] for k in range(n_mod))
    b_f = _pad_cols(fox_b_f, LANES)

    def residual(acc, x_in, gate):
        return (x_in + gate * acc, acc)

    def mlp_fwd(i, x_in, handle, relay_next=None):
        h, inv = _rms_mod_fwd(x_in, row(norm_mlp[i]), sh_mlp[i], sc_mlp[i], f"mlp_norm_{i}")
        w_up_t, w_down = finish_gather(handle, f"mlp{i}", h)
        w_up_t, w_down = w_up_t.reshape(FF, D), w_down.reshape(FF, D)
        r = _matmul(h, w_up_t, mode="nt", name=f"mlp_up_{i}", out_dtypes=[BF16], tm=1024, tn=2048, tk=D, halves=2,
                    epilogue=lambda acc: (jnp.maximum(acc, 0.0),))
        next_handle = relay_gather(relay_next[1], relay_next[0], r) if relay_next else None
        x_out, y = _matmul(r, w_down, mode="nn", name=f"mlp_down_{i}", out_dtypes=[F32, BF16], tm=512, tn=512, tk=FF,
                           n_outer=True, a_pre=jnp.square, epilogue=residual, extras=[(x_in, "tile"), (g_mlp[i], "row")],
                           after=[landed(next_handle)] if relay_next else [])
        return x_out, (x_in, h, inv, r, y, w_up_t, w_down), next_handle

    def mlp_bwd(i, dx, dy, dgate, saved, following, after):
        x_in, h, inv, r, y, w_up_t, w_down = saved
        du = _matmul(dy, w_down, mode="nt", name=f"mlp_down_bwd_{i}", out_dtypes=[BF16], tm=1024, tn=1024, tk=D,
                     epilogue=lambda acc, rv: (acc * (2.0 * rv.astype(F32)),), extras=[(r, "tile")], after=after)
        d_down = _matmul(r, dy, mode="tn", name=f"mlp_down_wgrad_{i}", out_dtypes=[BF16], tm=512, tn=1024, tk=S,
                         a_pre=jnp.square)
        dh = _matmul(du, w_up_t, mode="nn", name=f"mlp_up_bwd_{i}", out_dtypes=[F32], tm=512, tn=512, tk=FF, n_outer=True)
        d_up = _matmul(h, du, mode="tn", name=f"mlp_up_wgrad_{i}", out_dtypes=[BF16], tm=512, tn=FF // NDEV, tk=S,
                       out_shards=True)
        dx, dsh, dsc, dgain, dy_next, dgate_next = _rms_mod_bwd(dh, x_in, inv, dx, row(norm_mlp[i]), sc_mlp[i],
                                                                f"mlp_norm_bwd_{i}", following)
        handle, token = _exchange_start([d_up, d_down.reshape(NDEV, FF // NDEV, D)], f"scatter_mlp{i}_start", True, dx)
        return dx, (dsh, dsc, dgate, dgain), handle, token, dy_next, dgate_next

    h0, inv0 = _rms_mod_fwd(x0, row(norm_mix[0]), sh_mix[0], sc_mix[0], "fox_norm")
    (w_in_t,) = finish_gather(relay_gather(fox_handle, "fox", h0), "fox", h0)
    mlp0_handle, token = start_gather("mlp0", w_in_t)
    w_in_t = w_in_t.reshape(3 * D + H, D)
    w_f_t = _pad_rows(w_in_t[3 * D:], LANES)
    column_scale = jnp.concatenate([jnp.ones((1, D), F32), jnp.full((1, D), KEY_SCALE, F32), jnp.ones((1, D), F32)], axis=1)
    qkv = _matmul(h0, w_in_t, mode="nt", name="fox_qkv", out_dtypes=[BF16], tm=1024, tn=2048, tk=D, halves=2, n=3 * D, after=[token],
                  epilogue=lambda acc, mult: (acc * mult,), extras=[(column_scale, "row")])
    fox_out_handle = relay_gather(fox_out_handle, "fox_out", qkv)
    f_logit = _matmul(h0, w_f_t, mode="nt", name="fox_forget_logits", out_dtypes=[F32], tm=1024, tn=LANES, tk=D,
                      after=[landed(fox_out_handle)])
    f_cum = _forget_cumsum(f_logit, b_f, "fox_forget_cumsum")
    f_heads = f_cum[:, :H].T
    f_col, f_row = f_heads.reshape(H, S, 1), f_heads.reshape(H, 1, S)
    o, lse = _attn_fwd(qkv, f_col, f_row, "fox_attention")
    w_fox_out = finish_gather(fox_out_handle, "fox_out", o)[0].reshape(D, D)
    mlp0_handle = relay_gather(mlp0_handle, "mlp0", o)
    conv_handle, token = start_gather("conv", landed(mlp0_handle))
    mlp1_handle, token = start_gather("mlp1", token)
    x1, mix0 = _matmul(o, w_fox_out, mode="nn", name="fox_out", out_dtypes=[F32, BF16], tm=512, tn=1024, tk=D,
                       epilogue=residual, extras=[(x0, "tile"), (g_mix[0], "row")], after=[token])
    x2, mlp0, conv_handle = mlp_fwd(0, x1, mlp0_handle, ("conv", conv_handle))

    h1, inv1 = _rms_mod_fwd(x2, row(norm_mix[1]), sh_mix[1], sc_mix[1], "conv_norm")
    w_conv_in_t, w_taps, w_conv_out = finish_gather(conv_handle, "conv", h1)
    w_conv_in_t = w_conv_in_t.reshape(3 * D, D)
    w_taps = _pad_rows(w_taps.transpose(1, 0, 2).reshape(CONV_WIDTH, D), 8)
    w_conv_out = w_conv_out.reshape(D, D)
    proj = _matmul(h1, w_conv_in_t, mode="nt", name="conv_in", out_dtypes=[BF16], tm=1024, tn=2048, tk=D, halves=2)
    mlp1_handle = relay_gather(mlp1_handle, "mlp1", proj)
    z = _conv_fwd(proj, w_taps, "conv_mix")
    x3, mix1 = _matmul(z, w_conv_out, mode="nn", name="conv_out", out_dtypes=[F32, BF16], tm=512, tn=1024, tk=D,
                       epilogue=residual, extras=[(x2, "tile"), (g_mix[1], "row")], after=[landed(mlp1_handle)])
    x4, mlp1, _ = mlp_fwd(1, x3, mlp1_handle)

    dx, d_final, loss_lanes, dy, dgate = _final_loss_bwd(x4, target, row(final_norm), (mlp1[4], g_mlp[1]), "loss_head")

    dx, dmod_mlp1, mlp1_scatter, token, dmix, dg_mix1 = mlp_bwd(1, dx, dy, dgate, mlp1, (mix1, g_mix[1]), [])
    dz = _matmul(dmix, w_conv_out, mode="nt", name="conv_out_bwd", out_dtypes=[BF16], tm=1024, tn=1024, tk=D,
                 after=[token])
    d_conv_out = _matmul(z, dmix, mode="tn", name="conv_out_wgrad", out_dtypes=[BF16], tm=512, tn=1024, tk=S)
    db, dc, du, d_taps = _conv_bwd(proj, dz, w_taps, "conv_mix_bwd")
    dproj = jnp.concatenate([db, dc, du], axis=1)
    dh1 = _matmul(dproj, w_conv_in_t, mode="nn", name="conv_in_bwd", out_dtypes=[F32], tm=512, tn=512, tk=3 * D, n_outer=True)
    d_conv_in = _matmul(h1, dproj, mode="tn", name="conv_in_wgrad", out_dtypes=[BF16], tm=512, tn=3 * D // NDEV, tk=S,
                        out_shards=True)
    dx, dsh1, dsc1, dgain_mix1, dy, dgate = _rms_mod_bwd(dh1, x2, inv1, dx, row(norm_mix[1]), sc_mix[1], "conv_norm_bwd",
                                                         (mlp0[4], g_mlp[0]))
    d_taps_split = d_taps[:CONV_WIDTH].reshape(CONV_WIDTH, NDEV, -1).transpose(1, 0, 2)
    conv_scatter, token = _exchange_start([d_conv_in, d_taps_split, d_conv_out.reshape(NDEV, D // NDEV, D)],
                                          "scatter_conv_start", True, dx)

    dx, dmod_mlp0, mlp0_scatter, token, dmix, dg_mix0 = mlp_bwd(0, dx, dy, dgate, mlp0, (mix0, g_mix[0]), [token])
    do = _matmul(dmix, w_fox_out, mode="nt", name="fox_out_bwd", out_dtypes=[BF16], tm=1024, tn=1024, tk=D,
                 after=[token])
    d_fox_out = _matmul(o, dmix, mode="tn", name="fox_out_wgrad", out_dtypes=[BF16], tm=512, tn=1024, tk=S)
    fox_out_scatter, token = _exchange_start([d_fox_out.reshape(NDEV, D // NDEV, D)], "scatter_fox_out_start", True, d_fox_out)
    dq, dk, dv, dfk = _attn_bwd(qkv, do, f_col, tied(f_row, token), lse, "fox_attention_bwd")
    dqkv = [dq, dk, dv]
    dfk_lanes = _pad_cols(dfk.reshape(H, S).T, LANES)
    df_logit, db_f = _forget_bwd(dfk_lanes, f_logit, b_f, "fox_forget_bwd")
    d_qkv_t = _matmul(dqkv, h0, mode="tn", name="fox_qkv_wgrad", out_dtypes=[BF16], tm=512, tn=1024, tk=S)
    d_f_t = _matmul(df_logit, h0, mode="tn", name="fox_forget_wgrad", out_dtypes=[BF16], tm=LANES, tn=1024, tk=S)
    d_fox_in = jnp.concatenate([d_qkv_t, d_f_t[:H]], axis=0).reshape(NDEV, -1, D)
    fox_scatter, token = _exchange_start([d_fox_in], "scatter_fox_start", True, d_fox_in)
    dh0_f = _matmul(df_logit, w_f_t, mode="nn", name="fox_forget_logits_bwd", out_dtypes=[F32], tm=1024, tn=1024, tk=LANES,
                    after=[token])
    dh0 = _matmul(dqkv, w_in_t, mode="nn", name="fox_qkv_bwd", out_dtypes=[F32], tm=512, tn=512, tk=3 * D, n_outer=True,
                  epilogue=lambda acc, extra: (acc + extra,), extras=[(dh0_f, "tile")])
    dx, dsh0, dsc0, dgain_mix0 = _rms_mod_bwd(dh0, x0, inv0, dx, row(norm_mix[0]), sc_mix[0], "fox_norm_bwd")
    grad_x = dx.reshape(1, S, D)

    dmod = jnp.concatenate([
        jnp.concatenate([dsh0, dsc0, dg_mix0, dmod_mlp0[0], dmod_mlp0[1], dmod_mlp0[2]], axis=1),
        jnp.concatenate([dsh1, dsc1, dg_mix1, dmod_mlp1[0], dmod_mlp1[1], dmod_mlp1[2]], axis=1)], axis=0)
    small_sizes = [depth * n_mod * D, depth * D, depth * D, H, D, 1]
    n_small = sum(small_sizes)
    n_rows = -(-n_small // (8 * LANES)) * 8

    def pack(parts):
        flat = jnp.concatenate([p.reshape(-1) for p in parts])
        return jnp.pad(flat, (0, n_rows * LANES - n_small)).reshape(n_rows, LANES)

    def unpack(packed, shapes):
        flat, out, at = packed.reshape(-1), [], 0
        for size, shape in zip(small_sizes, shapes):
            out.append(flat[at:at + size].reshape(shape))
            at += size
        return out

    small_partial = pack([dmod, jnp.concatenate([dgain_mix0, dgain_mix1], axis=0),
                          jnp.concatenate([dmod_mlp0[3], dmod_mlp1[3]], axis=0), db_f[0, :H], d_final, loss_lanes[0, :1]])
    small_handle, token = _exchange_start([small_partial], "gather_small_start", False, dx)

    up1, down1 = _exchange_wait(mlp1_scatter, "scatter_mlp1_wait", token)
    up_out = _adamw(mlp_w_up, m_mlp_w_up, v_mlp_w_up, up1, "adamw_mlp_w_up_1", layer=1)
    down_out = _adamw(mlp_w_down, m_mlp_w_down, v_mlp_w_down, down1, "adamw_mlp_w_down_1", layer=1)
    cin, taps, cout = _exchange_wait(conv_scatter, "scatter_conv_wait", down_out[0])
    conv_in_out = _adamw(conv_w_in, m_conv_w_in, v_conv_w_in, cin, "adamw_conv_w_in")
    conv_w_res = _adamw(conv_w, m_conv_w, v_conv_w, taps, "adamw_conv_w")
    conv_out_out = _adamw(conv_w_out, m_conv_w_out, v_conv_w_out, cout, "adamw_conv_w_out")
    up0, down0 = _exchange_wait(mlp0_scatter, "scatter_mlp0_wait", conv_out_out[0])
    up_out = _adamw(mlp_w_up, m_mlp_w_up, v_mlp_w_up, up0, "adamw_mlp_w_up_0", layer=0, prev=up_out)
    down_out = _adamw(mlp_w_down, m_mlp_w_down, v_mlp_w_down, down0, "adamw_mlp_w_down_0", layer=0, prev=down_out)

    small_parts = _exchange_wait(small_handle, "gather_small_wait", down_out[0])[0]
    small_shapes = [ada_b.shape, norm_mix.shape, norm_mlp.shape, fox_b_f.shape, final_norm.shape]
    loss = jnp.sum(small_parts.reshape(NDEV, -1)[:, n_small - 1])
    unused = jnp.zeros((1,), F32)
    small_out = _adamw(pack([ada_b, norm_mix, norm_mlp, fox_b_f, final_norm, unused])[None],
                       pack([m_ada_b, m_norm_mix, m_norm_mlp, m_fox_b_f, m_final_norm, unused])[None],
                       pack([v_ada_b, v_norm_mix, v_norm_mlp, v_fox_b_f, v_final_norm, unused])[None], small_parts,
                       "adamw_small")
    small_out = [unpack(t, small_shapes) for t in small_out]

    dmod_all = small_parts.reshape(NDEV, -1)[:, :depth * n_mod * D].reshape(NDEV, depth, n_mod * D)
    dmod_mine = lax.dynamic_slice_in_dim(dmod_all, me * ncol, ncol, axis=2)
    ada_out = None
    for i in range(depth):
        d_ada = _matmul(c_all, dmod_mine[:, i], mode="tn", name=f"ada_wgrad_{i}", out_dtypes=[F32], tm=1024, tn=ncol // 2,
                        tk=NDEV, a_pre=_silu, precision=HIGHEST)
        ada_out = _adamw(ada_w, m_ada_w, v_ada_w, d_ada[None], f"adamw_ada_w_{i}", layer=i, prev=ada_out)

    (fout,) = _exchange_wait(fox_out_scatter, "scatter_fox_out_wait", ada_out[0])
    fox_out_out = _adamw(fox_w_out, m_fox_w_out, v_fox_w_out, fout, "adamw_fox_w_out")
    (fin,) = _exchange_wait(fox_scatter, "scatter_fox_wait", fox_out_out[0])
    swap = lambda t: jnp.swapaxes(t, 1, 2)
    fox_in_out = [swap(t) for t in _adamw(swap(fox_w_in), swap(m_fox_w_in), swap(v_fox_w_in), fin, "adamw_fox_w_in")]

    outputs = [loss, grad_x]
    for kind in range(4):
        sm = small_out[kind]
        outputs += [ada_out[kind], sm[0], sm[1], sm[2], fox_in_out[kind], sm[3], fox_out_out[kind], conv_in_out[kind],
                    conv_w_res[kind], conv_out_out[kind], up_out[kind], down_out[kind], sm[4]]
    return tuple(outputs)
```
